```python
import math
import jax
import jax.numpy as jnp
from jax import lax
import numpy as np

D_MODEL = 4096
BATCH = 4
SEQ = 2048
DEPTH = 1
DEC_BATCH = 32
DEC_SEQ = 4
PAST_LEN = 8192
PAGE_SIZE = 128

RWKV_WIDTH = D_MODEL // 2
NSA_WIDTH = D_MODEL - RWKV_WIDTH

RWKV_HEAD_DIM = 64
RWKV_HEADS = RWKV_WIDTH // RWKV_HEAD_DIM
DECAY_LORA = max(32, int(round(1.8 * RWKV_WIDTH ** 0.5 / 32)) * 32)
AAA_LORA = max(32, int(round(1.8 * RWKV_WIDTH ** 0.5 / 32)) * 32)
GATE_LORA = max(32, int(round(0.6 * RWKV_WIDTH ** 0.8 / 32)) * 32)
RWKV_PROJ = 3 * RWKV_WIDTH + DECAY_LORA + AAA_LORA + GATE_LORA
RWKV_SPLITS = (RWKV_WIDTH, 2 * RWKV_WIDTH, 3 * RWKV_WIDTH, 3 * RWKV_WIDTH + DECAY_LORA,
               3 * RWKV_WIDTH + DECAY_LORA + AAA_LORA)
LNX_EPS = 64e-5

NSA_HEAD_DIM = 128
NSA_HEADS = NSA_WIDTH // NSA_HEAD_DIM
NSA_KV_HEADS = 4
NSA_GROUP = NSA_HEADS // NSA_KV_HEADS
CMP_BLOCK = 32
SEL_BLOCK = 64
N_SEL = 16
WINDOW = 512
Q_CHUNK = 32
FORCE_SCORE = 1e4
NSA_KV_WIDTH = 2 * NSA_KV_HEADS * NSA_HEAD_DIM
NSA_PROJ = NSA_WIDTH + 3 * NSA_KV_WIDTH + 3 * NSA_HEADS
IN_PROJ = RWKV_PROJ + NSA_PROJ

NUM_BUCKETS = 32
MAX_DISTANCE = 128

N_EXPERTS = 64
N_GROUPS = 8
TOPK_GROUPS = 4
TOP_K = 8
EXPERT_FF = 512
SHARED_FF = 512
ROUTED_SCALE = 2.5
MOE_BLOCK = 128

RMS_EPS = 1e-6

kernel_name = 'hymba_rwkv7_nsa_moe_step'


def rms_norm(x, g):
    xf = x.astype(jnp.float32)
    y = xf * lax.rsqrt(jnp.mean(xf * xf, axis=-1, keepdims=True) + RMS_EPS)
    return (y * g.astype(jnp.float32)).astype(x.dtype)


def rel_bucket(dist):
    max_exact = NUM_BUCKETS // 2
    n = jnp.maximum(dist, 0)
    nf = jnp.maximum(n, max_exact).astype(jnp.float32)
    large = max_exact + (jnp.log(nf / max_exact) / math.log(MAX_DISTANCE / max_exact)
                         * (NUM_BUCKETS - max_exact)).astype(jnp.int32)
    return jnp.where(n < max_exact, n, jnp.minimum(large, NUM_BUCKETS - 1))


def masked_softmax(logits, mask):
    logits = jnp.where(mask, logits.astype(jnp.float32), -jnp.inf)
    m = jnp.max(logits, axis=-1, keepdims=True)
    p = jnp.exp(logits - jnp.where(jnp.isfinite(m), m, 0.0))
    s = jnp.sum(p, axis=-1, keepdims=True)
    return p / jnp.where(s > 0, s, 1.0)


def pad_rows(a, length):
    return jnp.pad(a, [(0, 0), (0, length - a.shape[1])] + [(0, 0)] * (a.ndim - 2))


def rwkv7_mix(pr, shift_prev, wkv0, lp):
    B, T, _ = pr.shape
    f32 = jnp.float32
    prev = jnp.concatenate([shift_prev[:, None, :].astype(pr.dtype), pr[:, :-1]], axis=1)
    m = pr + (prev - pr) * lp['rwkv_mu']
    r, k, v, xw, xa, xg = jnp.split(m, RWKV_SPLITS, axis=-1)
    w_log = -jax.nn.softplus(-(lp['rwkv_w0'] + jnp.tanh(xw) @ lp['rwkv_w_up']).astype(f32)) - 0.5
    decay = jnp.exp(-jnp.exp(w_log))
    a = jax.nn.sigmoid((lp['rwkv_a0'] + xa @ lp['rwkv_a_up']).astype(f32))
    g = jax.nn.sigmoid(xg) @ lp['rwkv_g_up']

    def heads(t):
        return t.astype(f32).reshape(B, T, RWKV_HEADS, RWKV_HEAD_DIM)

    r, k, v, decay, a = heads(r), heads(k), heads(v), heads(decay), heads(a)
    kk = k * lp['rwkv_k_k'].astype(f32).reshape(RWKV_HEADS, RWKV_HEAD_DIM)
    kk = kk / jnp.maximum(jnp.linalg.norm(kk, axis=-1, keepdims=True), 1e-12)
    k = k * (1.0 + (a - 1.0) * lp['rwkv_k_a'].astype(f32).reshape(RWKV_HEADS, RWKV_HEAD_DIM))

    def step(S, inp):
        r_t, w_t, k_t, v_t, a_t, b_t = inp
        sa = jnp.einsum('bhvk,bhk->bhv', S, a_t)
        S = S * w_t[:, :, None, :] + sa[..., None] * b_t[:, :, None, :] + v_t[..., None] * k_t[:, :, None, :]
        return S, jnp.einsum('bhvk,bhk->bhv', S, r_t)

    xs = tuple(jnp.moveaxis(t, 1, 0) for t in (r, decay, k, v, -kk, kk * a))
    s_final, ys = lax.scan(step, wkv0.astype(f32), xs)
    y = jnp.moveaxis(ys, 0, 1)
    mu = jnp.mean(y, axis=-1, keepdims=True)
    var = jnp.mean(jnp.square(y - mu), axis=-1, keepdims=True)
    y = ((y - mu) * lax.rsqrt(var + LNX_EPS)).reshape(B, T, RWKV_WIDTH)
    y = y * lp['rwkv_lnx_w'].astype(f32) + lp['rwkv_lnx_b'].astype(f32)
    bonus = (jnp.sum(r * k * lp['rwkv_r_k'].astype(f32), axis=-1, keepdims=True) * v).reshape(B, T, RWKV_WIDTH)
    out = ((y + bonus) * g.astype(f32)).astype(pr.dtype)
    return out, pr[:, -1], s_final.astype(wkv0.dtype)


def compress_blocks(rows, pe, w):
    B, L = rows.shape[:2]
    blk = rows.reshape(B, L // CMP_BLOCK, CMP_BLOCK, 2, NSA_KV_HEADS, NSA_HEAD_DIM)
    blk = blk + jnp.swapaxes(pe, 0, 1)[None, None, :, :, None, :]
    return jnp.einsum('bncjkd,jcde->bnjke', blk, w)


def nsa_attend(q, q_pos, gates, kc, vc, c_end, n_sel_blocks, gather_sel, kw, vw, w_pos, rel_bias):
    B, Tq = q.shape[:2]
    f32 = jnp.float32
    scale = NSA_HEAD_DIM ** -0.5
    qg = q.reshape(B, Tq, NSA_KV_HEADS, NSA_GROUP, NSA_HEAD_DIM)
    bias_tab = rel_bias.astype(f32).reshape(NUM_BUCKETS, NSA_KV_HEADS, NSA_GROUP)

    dc = q_pos[:, None] - c_end[None, :]
    bias_c = bias_tab[rel_bucket(dc)].transpose(0, 2, 3, 1)
    lc = jnp.einsum('btkgd,bnkd->btkgn', qg, kc).astype(f32) * scale + bias_c
    pc = masked_softmax(lc, (dc >= 0)[:, None, None, :])
    oc = jnp.einsum('btkgn,bnkd->btkgd', pc.astype(vc.dtype), vc)

    imp = pc.sum(3).reshape(B, Tq, NSA_KV_HEADS, n_sel_blocks, SEL_BLOCK // CMP_BLOCK).sum(-1)
    blk = jnp.arange(n_sel_blocks)
    cur = (q_pos // SEL_BLOCK)[:, None]
    force = (blk == 0) | (blk == cur) | (blk == cur - 1)
    score = jnp.where(force[:, None, :], FORCE_SCORE, imp)
    score = jnp.where((blk <= cur)[:, None, :], score, -jnp.inf)
    _, idx = lax.top_k(jnp.moveaxis(score, 2, 1), min(N_SEL, n_sel_blocks))

    sel = gather_sel(idx)
    ks, vs = sel[..., 0, :], sel[..., 1, :]
    spos = idx[..., None] * SEL_BLOCK + jnp.arange(SEL_BLOCK)
    ds = q_pos[None, None, :, None, None] - spos
    hi = jnp.arange(NSA_KV_HEADS)[None, :, None, None, None]
    bias_s = jnp.moveaxis(bias_tab[rel_bucket(ds), hi], -1, 3)
    ls = jnp.einsum('btkgd,bktjsd->bktgjs', qg, ks).astype(f32) * scale + bias_s
    ms = (ds >= 0)[:, :, :, None]
    ps = masked_softmax(ls.reshape(ls.shape[:4] + (-1,)), ms.reshape(ms.shape[:4] + (-1,))).reshape(ls.shape)
    osel = jnp.einsum('bktgjs,bktjsd->btkgd', ps.astype(vs.dtype), vs)

    dw = q_pos[:, None] - w_pos[None, :]
    mw = (dw >= 0) & (dw <= WINDOW) & (w_pos >= 0)[None, :]
    bias_w = bias_tab[rel_bucket(dw)].transpose(0, 2, 3, 1)
    lw = jnp.einsum('btkgd,blkd->btkgl', qg, kw).astype(f32) * scale + bias_w
    pw = masked_softmax(lw, mw[:, None, None, :])
    ow = jnp.einsum('btkgl,blkd->btkgd', pw.astype(vw.dtype), vw)

    gt = gates.reshape(B, Tq, NSA_KV_HEADS, NSA_GROUP, 3)
    o = gt[..., 0:1] * oc + gt[..., 1:2] * osel + gt[..., 2:3] * ow
    return o.reshape(B, Tq, NSA_WIDTH)


def split_proj(h, lp):
    B, T, _ = h.shape
    proj = h @ lp['w_in']
    pr = proj[..., :RWKV_PROJ]
    pn = proj[..., RWKV_PROJ:]
    q = pn[..., :NSA_WIDTH].reshape(B, T, NSA_HEADS, NSA_HEAD_DIM)
    kv = pn[..., NSA_WIDTH:NSA_WIDTH + 3 * NSA_KV_WIDTH].reshape(B, T, 3, 2, NSA_KV_HEADS, NSA_HEAD_DIM)
    gates = jax.nn.sigmoid(pn[..., NSA_WIDTH + 3 * NSA_KV_WIDTH:].astype(jnp.float32))
    gates = gates.reshape(B, T, NSA_HEADS, 3).astype(h.dtype)
    return pr, q, kv[:, :, 0], kv[:, :, 1], kv[:, :, 2], gates


def merge_groups(o_rwkv, o_nsa, lp):
    o = jnp.concatenate([o_rwkv, rms_norm(o_nsa, lp['nsa_out_g'])], axis=-1)
    return o @ lp['w_out']


def mixer_prompt(h, lp, rel_bias):
    B, T, _ = h.shape
    pr, q, kv_c, kv_s, kv_w, gates = split_proj(h, lp)
    o_r, shift_last, wkv = rwkv7_mix(pr, jnp.zeros((B, RWKV_PROJ), h.dtype),
                                     jnp.zeros((B, RWKV_HEADS, RWKV_HEAD_DIM, RWKV_HEAD_DIM), jnp.float32), lp)
    lp_len = -(-T // SEL_BLOCK) * SEL_BLOCK
    comp = compress_blocks(pad_rows(kv_c, lp_len), lp['cmp_pe'], lp['cmp_w'])
    kc, vc = comp[:, :, 0], comp[:, :, 1]
    c_end = jnp.arange(lp_len // CMP_BLOCK) * CMP_BLOCK + (CMP_BLOCK - 1)
    n_sb = lp_len // SEL_BLOCK
    sel_blocks = pad_rows(kv_s, lp_len).reshape(B, n_sb, SEL_BLOCK, 2, NSA_KV_HEADS, NSA_HEAD_DIM)
    bi = jnp.arange(B)[:, None, None, None]
    hi = jnp.arange(NSA_KV_HEADS)[None, :, None, None]

    def gather_sel(idx):
        return sel_blocks[bi, idx, :, :, hi, :]

    kw_pad = jnp.pad(kv_w, ((0, 0), (WINDOW, 0), (0, 0), (0, 0), (0, 0)))

    def chunk(i):
        t0 = i * Q_CHUNK
        qc = lax.dynamic_slice_in_dim(q, t0, Q_CHUNK, axis=1)
        gc = lax.dynamic_slice_in_dim(gates, t0, Q_CHUNK, axis=1)
        wc = lax.dynamic_slice_in_dim(kw_pad, t0, WINDOW + Q_CHUNK, axis=1)
        q_pos = t0 + jnp.arange(Q_CHUNK)
        w_pos = t0 - WINDOW + jnp.arange(WINDOW + Q_CHUNK)
        return nsa_attend(qc, q_pos, gc, kc, vc, c_end, n_sb, gather_sel,
                          wc[:, :, 0], wc[:, :, 1], w_pos, rel_bias)

    o_n = lax.map(chunk, jnp.arange(T // Q_CHUNK))
    o_n = jnp.moveaxis(o_n, 0, 1).reshape(B, T, NSA_WIDTH)
    win_keep = min(WINDOW, T)
    return merge_groups(o_r, o_n, lp), (kv_c, kv_s, kv_w[:, T - win_keep:], wkv, shift_last)


def mixer_sample(h, lp, rel_bias, pool_cmp, pool_sel, win_buf, wkv0, shift0, page_table):
    B, T, _ = h.shape
    n_pages = page_table.shape[1]
    page = pool_cmp.shape[1]
    past = n_pages * page
    pr, q, kv_c, kv_s, kv_w, gates = split_proj(h, lp)
    o_r, shift_last, wkv = rwkv7_mix(pr, shift0, wkv0, lp)
    lp_len = -(-(past + T) // SEL_BLOCK) * SEL_BLOCK
    q_pos = past + jnp.arange(T)

    past_c = pool_cmp[page_table].reshape(B, past, 2, NSA_KV_HEADS, NSA_HEAD_DIM)
    rows_c = pad_rows(jnp.concatenate([past_c, kv_c], axis=1), lp_len)
    comp = compress_blocks(rows_c, lp['cmp_pe'], lp['cmp_w'])
    kc, vc = comp[:, :, 0], comp[:, :, 1]
    c_end = jnp.arange(lp_len // CMP_BLOCK) * CMP_BLOCK + (CMP_BLOCK - 1)

    sub = page // SEL_BLOCK
    ns_past = past // SEL_BLOCK
    new_len = lp_len - past
    new_blocks = pad_rows(kv_s, new_len).reshape(B, new_len // SEL_BLOCK, SEL_BLOCK, 2, NSA_KV_HEADS, NSA_HEAD_DIM)
    pool_blk = pool_sel.reshape(pool_sel.shape[0], sub, SEL_BLOCK, 2, NSA_KV_HEADS, NSA_HEAD_DIM)
    bi = jnp.arange(B)[:, None, None, None]
    hi = jnp.arange(NSA_KV_HEADS)[None, :, None, None]

    def gather_sel(idx):
        ip = jnp.minimum(idx, ns_past - 1)
        phys = page_table[bi, ip // sub]
        from_pool = pool_blk[phys, ip % sub, :, :, hi, :]
        inew = jnp.clip(idx - ns_past, 0, new_blocks.shape[1] - 1)
        from_new = new_blocks[bi, inew, :, :, hi, :]
        return jnp.where((idx < ns_past)[..., None, None, None], from_pool, from_new)

    wrows = jnp.concatenate([win_buf, kv_w], axis=1)
    wb = win_buf.shape[1]
    w_pos = past - wb + jnp.arange(wb + T)
    o_n = nsa_attend(q, q_pos, gates, kc, vc, c_end, lp_len // SEL_BLOCK, gather_sel,
                     wrows[:, :, 0], wrows[:, :, 1], w_pos, rel_bias)
    return merge_groups(o_r, o_n, lp), (kv_c, kv_s, wrows[:, T:], wkv, shift_last)


def moe_ffn(h, lp):
    B, T, D = h.shape
    x = h.reshape(B * T, D)
    n = x.shape[0]
    f32 = jnp.float32
    scores = jax.nn.sigmoid((x @ lp['router_w']).astype(f32))
    biased = scores + lp['router_bias'].astype(f32)
    grp = biased.reshape(n, N_GROUPS, N_EXPERTS // N_GROUPS)
    grp_score = lax.top_k(grp, 2)[0].sum(-1)
    _, gidx = lax.top_k(grp_score, TOPK_GROUPS)
    gmask = jax.nn.one_hot(gidx, N_GROUPS, dtype=f32).sum(1)
    emask = jnp.repeat(gmask, N_EXPERTS // N_GROUPS, axis=1) > 0
    _, eidx = lax.top_k(jnp.where(emask, biased, -jnp.inf), TOP_K)
    w = jnp.take_along_axis(scores, eidx, axis=1)
    w = w / jnp.sum(w, axis=-1, keepdims=True) * ROUTED_SCALE

    nk = n * TOP_K
    flat_e = eidx.reshape(nk)
    order = jnp.argsort(flat_e)
    se = flat_e[order]
    counts = jnp.bincount(flat_e, length=N_EXPERTS)
    padded = (counts + MOE_BLOCK - 1) // MOE_BLOCK * MOE_BLOCK
    pad_end = jnp.cumsum(padded)
    dest = (pad_end - padded)[se] + jnp.arange(nk) - (jnp.cumsum(counts) - counts)[se]
    n_blocks = -(-(nk + N_EXPERTS * (MOE_BLOCK - 1)) // MOE_BLOCK)
    rows = n_blocks * MOE_BLOCK
    row_tok = jnp.zeros((rows,), jnp.int32).at[dest].set((order // TOP_K).astype(jnp.int32))
    row_w = jnp.zeros((rows,), f32).at[dest].set(w.reshape(nk)[order])
    blk_exp = jnp.minimum(jnp.searchsorted(pad_end, jnp.arange(n_blocks) * MOE_BLOCK, side='right'),
                          N_EXPERTS - 1)

    def expert_block(args):
        tok, wt, e = args
        xb = x[tok]
        hb = jax.nn.silu(xb @ lp['exp_w_gate'][e]) * (xb @ lp['exp_w_up'][e])
        return (hb @ lp['exp_w_down'][e]).astype(f32) * wt[:, None]

    out = lax.map(expert_block, (row_tok.reshape(n_blocks, MOE_BLOCK), row_w.reshape(n_blocks, MOE_BLOCK), blk_exp))
    routed = jnp.zeros((n, D), f32).at[row_tok].add(out.reshape(rows, D))
    shared = (jax.nn.silu(x @ lp['sh_w_gate']) * (x @ lp['sh_w_up'])) @ lp['sh_w_down']
    return (routed + shared.astype(f32)).astype(h.dtype).reshape(B, T, D)


def trunk_layer(x, c, lp, mixer):
    B, _, D = x.shape
    mod = (jax.nn.silu(c) @ lp['w_ada'] + lp['b_ada']).reshape(B, 6, 1, D)
    sh1, sc1, gt1, sh2, sc2, gt2 = (mod[:, i] for i in range(6))
    g = lp['norm_g']
    h = rms_norm(x, g[0]) * (1 + sc1) + sh1
    o, st = mixer(h)
    x = x + gt1 * rms_norm(o, g[1])
    h = rms_norm(x, g[2]) * (1 + sc2) + sh2
    x = x + gt2 * rms_norm(moe_ffn(h, lp), g[3])
    return x, st


def setup_inputs(seed: int = 0) -> dict:
    key = jax.random.key(seed)
    keys = iter(jax.random.split(key, 48))

    def nrm(shape, scale=1.0):
        return jax.random.normal(next(keys), shape, jnp.float32) * scale

    def uni(shape, lo, hi):
        return jax.random.uniform(next(keys), shape, jnp.float32, lo, hi)

    n_pages = PAST_LEN // PAGE_SIZE
    n_phys = (5 * DEC_BATCH * n_pages) // 4
    perm = jax.random.permutation(next(keys), n_phys)
    page_table = perm[:DEC_BATCH * n_pages].reshape(DEC_BATCH, n_pages).astype(jnp.int32)
    win_buf = min(WINDOW, PAST_LEN)
    kvt = (2, NSA_KV_HEADS, NSA_HEAD_DIM)
    L, d = DEPTH, D_MODEL
    return {
        'x_prompt': nrm((BATCH, SEQ, d)),
        'x_sample': nrm((DEC_BATCH, DEC_SEQ, d)),
        'c_prompt': nrm((BATCH, d)),
        'c_sample': nrm((DEC_BATCH, d)),
        'cache_cmp': nrm((L, n_phys, PAGE_SIZE) + kvt),
        'cache_sel': nrm((L, n_phys, PAGE_SIZE) + kvt),
        'state_win': nrm((L, DEC_BATCH, win_buf) + kvt),
        'state_wkv': nrm((L, DEC_BATCH, RWKV_HEADS, RWKV_HEAD_DIM, RWKV_HEAD_DIM), 0.3),
        'state_shift': nrm((L, DEC_BATCH, RWKV_PROJ)),
        'page_table': page_table,
        'rel_bias': nrm((NUM_BUCKETS, NSA_HEADS), 0.5),
        'w_ada': nrm((L, d, 6 * d), 0.3 * d ** -0.5),
        'b_ada': nrm((L, 6 * d), 0.05),
        'norm_g': 1.0 + nrm((L, 4, d), 0.05),
        'w_in': nrm((L, d, IN_PROJ), d ** -0.5),
        'w_out': nrm((L, d, d), d ** -0.5),
        'rwkv_mu': uni((L, RWKV_PROJ), 0.0, 1.0),
        'rwkv_w0': uni((L, RWKV_WIDTH), -6.0, -1.0),
        'rwkv_w_up': nrm((L, DECAY_LORA, RWKV_WIDTH), 0.1),
        'rwkv_a0': nrm((L, RWKV_WIDTH), 0.1),
        'rwkv_a_up': nrm((L, AAA_LORA, RWKV_WIDTH), 0.5 * AAA_LORA ** -0.5),
        'rwkv_g_up': nrm((L, GATE_LORA, RWKV_WIDTH), GATE_LORA ** -0.5),
        'rwkv_k_k': 0.85 + nrm((L, RWKV_WIDTH), 0.05),
        'rwkv_k_a': 1.0 + nrm((L, RWKV_WIDTH), 0.05),
        'rwkv_r_k': nrm((L, RWKV_HEADS, RWKV_HEAD_DIM), 0.1),
        'rwkv_lnx_w': 1.0 + nrm((L, RWKV_WIDTH), 0.05),
        'rwkv_lnx_b': nrm((L, RWKV_WIDTH), 0.01),
        'cmp_pe': nrm((L, 2, CMP_BLOCK, NSA_HEAD_DIM), 0.1),
        'cmp_w': nrm((L, 2, CMP_BLOCK, NSA_HEAD_DIM, NSA_HEAD_DIM), (CMP_BLOCK * NSA_HEAD_DIM) ** -0.5),
        'nsa_out_g': 1.0 + nrm((L, NSA_WIDTH), 0.05),
        'router_w': nrm((L, d, N_EXPERTS), d ** -0.5),
        'router_bias': nrm((L, N_EXPERTS), 0.01),
        'exp_w_gate': nrm((L, N_EXPERTS, d, EXPERT_FF), d ** -0.5),
        'exp_w_up': nrm((L, N_EXPERTS, d, EXPERT_FF), d ** -0.5),
        'exp_w_down': nrm((L, N_EXPERTS, EXPERT_FF, d), EXPERT_FF ** -0.5),
        'sh_w_gate': nrm((L, d, SHARED_FF), d ** -0.5),
        'sh_w_up': nrm((L, d, SHARED_FF), d ** -0.5),
        'sh_w_down': nrm((L, SHARED_FF, d), SHARED_FF ** -0.5),
    }


def reference(x_prompt, x_sample, c_prompt, c_sample, cache_cmp, cache_sel, state_win, state_wkv,
              state_shift, page_table, rel_bias, w_ada, b_ada, norm_g, w_in, w_out, rwkv_mu, rwkv_w0,
              rwkv_w_up, rwkv_a0, rwkv_a_up, rwkv_g_up, rwkv_k_k, rwkv_k_a, rwkv_r_k, rwkv_lnx_w,
              rwkv_lnx_b, cmp_pe, cmp_w, nsa_out_g, router_w, router_bias, exp_w_gate, exp_w_up,
              exp_w_down, sh_w_gate, sh_w_up, sh_w_down):
    yp, ys = x_prompt, x_sample
    p_states, s_states = [], []
    for l in range(DEPTH):
        lp = dict(w_ada=w_ada[l], b_ada=b_ada[l], norm_g=norm_g[l], w_in=w_in[l], w_out=w_out[l],
                  rwkv_mu=rwkv_mu[l], rwkv_w0=rwkv_w0[l], rwkv_w_up=rwkv_w_up[l], rwkv_a0=rwkv_a0[l],
                  rwkv_a_up=rwkv_a_up[l], rwkv_g_up=rwkv_g_up[l], rwkv_k_k=rwkv_k_k[l],
                  rwkv_k_a=rwkv_k_a[l], rwkv_r_k=rwkv_r_k[l], rwkv_lnx_w=rwkv_lnx_w[l],
                  rwkv_lnx_b=rwkv_lnx_b[l], cmp_pe=cmp_pe[l], cmp_w=cmp_w[l], nsa_out_g=nsa_out_g[l],
                  router_w=router_w[l], router_bias=router_bias[l], exp_w_gate=exp_w_gate[l],
                  exp_w_up=exp_w_up[l], exp_w_down=exp_w_down[l], sh_w_gate=sh_w_gate[l],
                  sh_w_up=sh_w_up[l], sh_w_down=sh_w_down[l])
        yp, st_p = trunk_layer(yp, c_prompt, lp, lambda h: mixer_prompt(h, lp, rel_bias))
        ys, st_s = trunk_layer(ys, c_sample, lp, lambda h: mixer_sample(
            h, lp, rel_bias, cache_cmp[l], cache_sel[l], state_win[l], state_wkv[l], state_shift[l], page_table))
        p_states.append(st_p)
        s_states.append(st_s)
    p_cmp, p_sel, p_win, p_wkv, p_shift = [jnp.stack(a) for a in zip(*p_states)]
    s_cmp, s_sel, s_win, s_wkv, s_shift = [jnp.stack(a) for a in zip(*s_states)]
    return (yp, ys, p_cmp, p_sel, p_win, p_wkv, p_shift, s_cmp, s_sel, s_win, s_wkv, s_shift)
```

```python
import functools
import math

import jax
import jax.numpy as jnp
from jax import lax
from jax.experimental import pallas as pl
from jax.experimental.pallas import tpu as pltpu

D_MODEL = 4096
RWKV_WIDTH = D_MODEL // 2
NSA_WIDTH = D_MODEL - RWKV_WIDTH
RWKV_HEAD_DIM = 64
RWKV_HEADS = RWKV_WIDTH // RWKV_HEAD_DIM
DECAY_LORA = 96
AAA_LORA = 96
GATE_LORA = 256
RWKV_PROJ = 3 * RWKV_WIDTH + DECAY_LORA + AAA_LORA + GATE_LORA
RWKV_SPLITS = (RWKV_WIDTH, 2 * RWKV_WIDTH, 3 * RWKV_WIDTH, 3 * RWKV_WIDTH + DECAY_LORA,
               3 * RWKV_WIDTH + DECAY_LORA + AAA_LORA)
LNX_EPS = 64e-5
NSA_HEAD_DIM = 128
NSA_HEADS = NSA_WIDTH // NSA_HEAD_DIM
NSA_KV_HEADS = 4
NSA_GROUP = NSA_HEADS // NSA_KV_HEADS
CMP_BLOCK = 32
SEL_BLOCK = 64
N_SEL = 16
WINDOW = 512
Q_CHUNK = 32
FORCE_SCORE = 1e4
NSA_KV_WIDTH = 2 * NSA_KV_HEADS * NSA_HEAD_DIM
NUM_BUCKETS = 32
MAX_DISTANCE = 128
N_EXPERTS = 64
N_GROUPS = 8
TOPK_GROUPS = 4
TOP_K = 8
ROUTED_SCALE = 2.5
MOE_BLOCK = 128
RMS_EPS = 1e-6

VMEM_LIMIT_BYTES = 48 * 1024 * 1024


def _matmul_kernel(x_ref, w_ref, o_ref):
    o_ref[...] = jnp.dot(x_ref[...].astype(jnp.bfloat16), w_ref[...].astype(jnp.bfloat16),
                         preferred_element_type=jnp.float32)


def pallas_matmul(x, w, tm=512, tn=512):
    m, k = x.shape
    n = w.shape[1]
    mp = -(-m // 8) * 8
    tm = min(tm, mp)
    mp = -(-mp // tm) * tm
    np_ = -(-n // tn) * tn
    if mp != m:
        x = jnp.pad(x, ((0, mp - m), (0, 0)))
    if np_ != n:
        w = jnp.pad(w, ((0, 0), (0, np_ - n)))
    out = pl.pallas_call(
        _matmul_kernel,
        grid=(np_ // tn, mp // tm),
        in_specs=[pl.BlockSpec((tm, k), lambda j, i: (i, 0)),
                  pl.BlockSpec((k, tn), lambda j, i: (0, j))],
        out_specs=pl.BlockSpec((tm, tn), lambda j, i: (i, j)),
        out_shape=jax.ShapeDtypeStruct((mp, np_), jnp.float32),
        compiler_params=pltpu.CompilerParams(
            dimension_semantics=("arbitrary", "arbitrary"), vmem_limit_bytes=VMEM_LIMIT_BYTES),
        name="matmul",
    )(x, w)
    return out[:m, :n]


def mm3(h, w):
    b, t, d = h.shape
    return pallas_matmul(h.reshape(b * t, d), w).reshape(b, t, w.shape[1])


def rms_norm(x, g):
    xf = x.astype(jnp.float32)
    y = xf * lax.rsqrt(jnp.mean(xf * xf, axis=-1, keepdims=True) + RMS_EPS)
    return (y * g.astype(jnp.float32)).astype(x.dtype)


def rel_bucket(dist):
    max_exact = NUM_BUCKETS // 2
    n = jnp.maximum(dist, 0)
    nf = jnp.maximum(n, max_exact).astype(jnp.float32)
    large = max_exact + (jnp.log(nf / max_exact) / math.log(MAX_DISTANCE / max_exact)
                         * (NUM_BUCKETS - max_exact)).astype(jnp.int32)
    return jnp.where(n < max_exact, n, jnp.minimum(large, NUM_BUCKETS - 1))


def masked_softmax(logits, mask):
    logits = jnp.where(mask, logits.astype(jnp.float32), -jnp.inf)
    m = jnp.max(logits, axis=-1, keepdims=True)
    p = jnp.exp(logits - jnp.where(jnp.isfinite(m), m, 0.0))
    s = jnp.sum(p, axis=-1, keepdims=True)
    return p / jnp.where(s > 0, s, 1.0)


def pad_rows(a, length):
    return jnp.pad(a, [(0, 0), (0, length - a.shape[1])] + [(0, 0)] * (a.ndim - 2))


def rwkv7_mix(pr, shift_prev, wkv0, lp):
    B, T, _ = pr.shape
    f32 = jnp.float32
    prev = jnp.concatenate([shift_prev[:, None, :].astype(pr.dtype), pr[:, :-1]], axis=1)
    m = pr + (prev - pr) * lp['rwkv_mu']
    r, k, v, xw, xa, xg = jnp.split(m, RWKV_SPLITS, axis=-1)
    w_log = -jax.nn.softplus(-(lp['rwkv_w0'] + jnp.tanh(xw) @ lp['rwkv_w_up']).astype(f32)) - 0.5
    decay = jnp.exp(-jnp.exp(w_log))
    a = jax.nn.sigmoid((lp['rwkv_a0'] + xa @ lp['rwkv_a_up']).astype(f32))
    g = jax.nn.sigmoid(xg) @ lp['rwkv_g_up']

    def heads(t):
        return t.astype(f32).reshape(B, T, RWKV_HEADS, RWKV_HEAD_DIM)

    r, k, v, decay, a = heads(r), heads(k), heads(v), heads(decay), heads(a)
    kk = k * lp['rwkv_k_k'].astype(f32).reshape(RWKV_HEADS, RWKV_HEAD_DIM)
    kk = kk / jnp.maximum(jnp.linalg.norm(kk, axis=-1, keepdims=True), 1e-12)
    k = k * (1.0 + (a - 1.0) * lp['rwkv_k_a'].astype(f32).reshape(RWKV_HEADS, RWKV_HEAD_DIM))

    def step(S, inp):
        r_t, w_t, k_t, v_t, a_t, b_t = inp
        sa = jnp.einsum('bhvk,bhk->bhv', S, a_t)
        S = S * w_t[:, :, None, :] + sa[..., None] * b_t[:, :, None, :] + v_t[..., None] * k_t[:, :, None, :]
        return S, jnp.einsum('bhvk,bhk->bhv', S, r_t)

    xs = tuple(jnp.moveaxis(t, 1, 0) for t in (r, decay, k, v, -kk, kk * a))
    s_final, ys = lax.scan(step, wkv0.astype(f32), xs)
    y = jnp.moveaxis(ys, 0, 1)
    mu = jnp.mean(y, axis=-1, keepdims=True)
    var = jnp.mean(jnp.square(y - mu), axis=-1, keepdims=True)
    y = ((y - mu) * lax.rsqrt(var + LNX_EPS)).reshape(B, T, RWKV_WIDTH)
    y = y * lp['rwkv_lnx_w'].astype(f32) + lp['rwkv_lnx_b'].astype(f32)
    bonus = (jnp.sum(r * k * lp['rwkv_r_k'].astype(f32), axis=-1, keepdims=True) * v).reshape(B, T, RWKV_WIDTH)
    out = ((y + bonus) * g.astype(f32)).astype(pr.dtype)
    return out, pr[:, -1], s_final.astype(wkv0.dtype)


def compress_blocks(rows, pe, w):
    B, L = rows.shape[:2]
    blk = rows.reshape(B, L // CMP_BLOCK, CMP_BLOCK, 2, NSA_KV_HEADS, NSA_HEAD_DIM)
    blk = blk + jnp.swapaxes(pe, 0, 1)[None, None, :, :, None, :]
    return jnp.einsum('bncjkd,jcde->bnjke', blk, w)


def nsa_attend(q, q_pos, gates, kc, vc, c_end, n_sel_blocks, gather_sel, kw, vw, w_pos, rel_bias):
    B, Tq = q.shape[:2]
    f32 = jnp.float32
    scale = NSA_HEAD_DIM ** -0.5
    qg = q.reshape(B, Tq, NSA_KV_HEADS, NSA_GROUP, NSA_HEAD_DIM)
    bias_tab = rel_bias.astype(f32).reshape(NUM_BUCKETS, NSA_KV_HEADS, NSA_GROUP)

    dc = q_pos[:, None] - c_end[None, :]
    bias_c = bias_tab[rel_bucket(dc)].transpose(0, 2, 3, 1)
    lc = jnp.einsum('btkgd,bnkd->btkgn', qg, kc).astype(f32) * scale + bias_c
    pc = masked_softmax(lc, (dc >= 0)[:, None, None, :])
    oc = jnp.einsum('btkgn,bnkd->btkgd', pc.astype(vc.dtype), vc)

    imp = pc.sum(3).reshape(B, Tq, NSA_KV_HEADS, n_sel_blocks, SEL_BLOCK // CMP_BLOCK).sum(-1)
    blk = jnp.arange(n_sel_blocks)
    cur = (q_pos // SEL_BLOCK)[:, None]
    force = (blk == 0) | (blk == cur) | (blk == cur - 1)
    score = jnp.where(force[:, None, :], FORCE_SCORE, imp)
    score = jnp.where((blk <= cur)[:, None, :], score, -jnp.inf)
    _, idx = lax.top_k(jnp.moveaxis(score, 2, 1), min(N_SEL, n_sel_blocks))

    sel = gather_sel(idx)
    ks, vs = sel[..., 0, :], sel[..., 1, :]
    spos = idx[..., None] * SEL_BLOCK + jnp.arange(SEL_BLOCK)
    ds = q_pos[None, None, :, None, None] - spos
    hi = jnp.arange(NSA_KV_HEADS)[None, :, None, None, None]
    bias_s = jnp.moveaxis(bias_tab[rel_bucket(ds), hi], -1, 3)
    ls = jnp.einsum('btkgd,bktjsd->bktgjs', qg, ks).astype(f32) * scale + bias_s
    ms = (ds >= 0)[:, :, :, None]
    ps = masked_softmax(ls.reshape(ls.shape[:4] + (-1,)), ms.reshape(ms.shape[:4] + (-1,))).reshape(ls.shape)
    osel = jnp.einsum('bktgjs,bktjsd->btkgd', ps.astype(vs.dtype), vs)

    dw = q_pos[:, None] - w_pos[None, :]
    mw = (dw >= 0) & (dw <= WINDOW) & (w_pos >= 0)[None, :]
    bias_w = bias_tab[rel_bucket(dw)].transpose(0, 2, 3, 1)
    lw = jnp.einsum('btkgd,blkd->btkgl', qg, kw).astype(f32) * scale + bias_w
    pw = masked_softmax(lw, mw[:, None, None, :])
    ow = jnp.einsum('btkgl,blkd->btkgd', pw.astype(vw.dtype), vw)

    gt = gates.reshape(B, Tq, NSA_KV_HEADS, NSA_GROUP, 3)
    o = gt[..., 0:1] * oc + gt[..., 1:2] * osel + gt[..., 2:3] * ow
    return o.reshape(B, Tq, NSA_WIDTH)


def split_proj(h, lp):
    B, T, _ = h.shape
    proj = mm3(h, lp['w_in'])
    pr = proj[..., :RWKV_PROJ]
    pn = proj[..., RWKV_PROJ:]
    q = pn[..., :NSA_WIDTH].reshape(B, T, NSA_HEADS, NSA_HEAD_DIM)
    kv = pn[..., NSA_WIDTH:NSA_WIDTH + 3 * NSA_KV_WIDTH].reshape(B, T, 3, 2, NSA_KV_HEADS, NSA_HEAD_DIM)
    gates = jax.nn.sigmoid(pn[..., NSA_WIDTH + 3 * NSA_KV_WIDTH:].astype(jnp.float32))
    gates = gates.reshape(B, T, NSA_HEADS, 3).astype(h.dtype)
    return pr, q, kv[:, :, 0], kv[:, :, 1], kv[:, :, 2], gates


def merge_groups(o_rwkv, o_nsa, lp):
    o = jnp.concatenate([o_rwkv, rms_norm(o_nsa, lp['nsa_out_g'])], axis=-1)
    return mm3(o, lp['w_out'])


def mixer_prompt(h, lp, rel_bias):
    B, T, _ = h.shape
    pr, q, kv_c, kv_s, kv_w, gates = split_proj(h, lp)
    o_r, shift_last, wkv = rwkv7_mix(pr, jnp.zeros((B, RWKV_PROJ), h.dtype),
                                     jnp.zeros((B, RWKV_HEADS, RWKV_HEAD_DIM, RWKV_HEAD_DIM), jnp.float32), lp)
    lp_len = -(-T // SEL_BLOCK) * SEL_BLOCK
    comp = compress_blocks(pad_rows(kv_c, lp_len), lp['cmp_pe'], lp['cmp_w'])
    kc, vc = comp[:, :, 0], comp[:, :, 1]
    c_end = jnp.arange(lp_len // CMP_BLOCK) * CMP_BLOCK + (CMP_BLOCK - 1)
    n_sb = lp_len // SEL_BLOCK
    sel_blocks = pad_rows(kv_s, lp_len).reshape(B, n_sb, SEL_BLOCK, 2, NSA_KV_HEADS, NSA_HEAD_DIM)
    bi = jnp.arange(B)[:, None, None, None]
    hi = jnp.arange(NSA_KV_HEADS)[None, :, None, None]

    def gather_sel(idx):
        return sel_blocks[bi, idx, :, :, hi, :]

    kw_pad = jnp.pad(kv_w, ((0, 0), (WINDOW, 0), (0, 0), (0, 0), (0, 0)))

    def chunk(i):
        t0 = i * Q_CHUNK
        qc = lax.dynamic_slice_in_dim(q, t0, Q_CHUNK, axis=1)
        gc = lax.dynamic_slice_in_dim(gates, t0, Q_CHUNK, axis=1)
        wc = lax.dynamic_slice_in_dim(kw_pad, t0, WINDOW + Q_CHUNK, axis=1)
        q_pos = t0 + jnp.arange(Q_CHUNK)
        w_pos = t0 - WINDOW + jnp.arange(WINDOW + Q_CHUNK)
        return nsa_attend(qc, q_pos, gc, kc, vc, c_end, n_sb, gather_sel,
                          wc[:, :, 0], wc[:, :, 1], w_pos, rel_bias)

    o_n = lax.map(chunk, jnp.arange(T // Q_CHUNK))
    o_n = jnp.moveaxis(o_n, 0, 1).reshape(B, T, NSA_WIDTH)
    win_keep = min(WINDOW, T)
    return merge_groups(o_r, o_n, lp), (kv_c, kv_s, kv_w[:, T - win_keep:], wkv, shift_last)


def mixer_sample(h, lp, rel_bias, pool_cmp, pool_sel, win_buf, wkv0, shift0, page_table):
    B, T, _ = h.shape
    n_pages = page_table.shape[1]
    page = pool_cmp.shape[1]
    past = n_pages * page
    pr, q, kv_c, kv_s, kv_w, gates = split_proj(h, lp)
    o_r, shift_last, wkv = rwkv7_mix(pr, shift0, wkv0, lp)
    lp_len = -(-(past + T) // SEL_BLOCK) * SEL_BLOCK
    q_pos = past + jnp.arange(T)

    past_c = pool_cmp[page_table].reshape(B, past, 2, NSA_KV_HEADS, NSA_HEAD_DIM)
    rows_c = pad_rows(jnp.concatenate([past_c, kv_c], axis=1), lp_len)
    comp = compress_blocks(rows_c, lp['cmp_pe'], lp['cmp_w'])
    kc, vc = comp[:, :, 0], comp[:, :, 1]
    c_end = jnp.arange(lp_len // CMP_BLOCK) * CMP_BLOCK + (CMP_BLOCK - 1)

    sub = page // SEL_BLOCK
    ns_past = past // SEL_BLOCK
    new_len = lp_len - past
    new_blocks = pad_rows(kv_s, new_len).reshape(B, new_len // SEL_BLOCK, SEL_BLOCK, 2, NSA_KV_HEADS, NSA_HEAD_DIM)
    pool_blk = pool_sel.reshape(pool_sel.shape[0], sub, SEL_BLOCK, 2, NSA_KV_HEADS, NSA_HEAD_DIM)
    bi = jnp.arange(B)[:, None, None, None]
    hi = jnp.arange(NSA_KV_HEADS)[None, :, None, None]

    def gather_sel(idx):
        ip = jnp.minimum(idx, ns_past - 1)
        phys = page_table[bi, ip // sub]
        from_pool = pool_blk[phys, ip % sub, :, :, hi, :]
        inew = jnp.clip(idx - ns_past, 0, new_blocks.shape[1] - 1)
        from_new = new_blocks[bi, inew, :, :, hi, :]
        return jnp.where((idx < ns_past)[..., None, None, None], from_pool, from_new)

    wrows = jnp.concatenate([win_buf, kv_w], axis=1)
    wb = win_buf.shape[1]
    w_pos = past - wb + jnp.arange(wb + T)
    o_n = nsa_attend(q, q_pos, gates, kc, vc, c_end, lp_len // SEL_BLOCK, gather_sel,
                     wrows[:, :, 0], wrows[:, :, 1], w_pos, rel_bias)
    return merge_groups(o_r, o_n, lp), (kv_c, kv_s, wrows[:, T:], wkv, shift_last)


def moe_ffn(h, lp):
    B, T, D = h.shape
    x = h.reshape(B * T, D)
    n = x.shape[0]
    f32 = jnp.float32
    scores = jax.nn.sigmoid((x @ lp['router_w']).astype(f32))
    biased = scores + lp['router_bias'].astype(f32)
    grp = biased.reshape(n, N_GROUPS, N_EXPERTS // N_GROUPS)
    grp_score = lax.top_k(grp, 2)[0].sum(-1)
    _, gidx = lax.top_k(grp_score, TOPK_GROUPS)
    gmask = jax.nn.one_hot(gidx, N_GROUPS, dtype=f32).sum(1)
    emask = jnp.repeat(gmask, N_EXPERTS // N_GROUPS, axis=1) > 0
    _, eidx = lax.top_k(jnp.where(emask, biased, -jnp.inf), TOP_K)
    w = jnp.take_along_axis(scores, eidx, axis=1)
    w = w / jnp.sum(w, axis=-1, keepdims=True) * ROUTED_SCALE

    nk = n * TOP_K
    flat_e = eidx.reshape(nk)
    order = jnp.argsort(flat_e)
    se = flat_e[order]
    counts = jnp.bincount(flat_e, length=N_EXPERTS)
    padded = (counts + MOE_BLOCK - 1) // MOE_BLOCK * MOE_BLOCK
    pad_end = jnp.cumsum(padded)
    dest = (pad_end - padded)[se] + jnp.arange(nk) - (jnp.cumsum(counts) - counts)[se]
    n_blocks = -(-(nk + N_EXPERTS * (MOE_BLOCK - 1)) // MOE_BLOCK)
    rows = n_blocks * MOE_BLOCK
    row_tok = jnp.zeros((rows,), jnp.int32).at[dest].set((order // TOP_K).astype(jnp.int32))
    row_w = jnp.zeros((rows,), f32).at[dest].set(w.reshape(nk)[order])
    blk_exp = jnp.minimum(jnp.searchsorted(pad_end, jnp.arange(n_blocks) * MOE_BLOCK, side='right'),
                          N_EXPERTS - 1)

    def expert_block(args):
        tok, wt, e = args
        xb = x[tok]
        hb = jax.nn.silu(xb @ lp['exp_w_gate'][e]) * (xb @ lp['exp_w_up'][e])
        return (hb @ lp['exp_w_down'][e]).astype(f32) * wt[:, None]

    out = lax.map(expert_block, (row_tok.reshape(n_blocks, MOE_BLOCK), row_w.reshape(n_blocks, MOE_BLOCK), blk_exp))
    routed = jnp.zeros((n, D), f32).at[row_tok].add(out.reshape(rows, D))
    shared = (jax.nn.silu(x @ lp['sh_w_gate']) * (x @ lp['sh_w_up'])) @ lp['sh_w_down']
    return (routed + shared.astype(f32)).astype(h.dtype).reshape(B, T, D)


def trunk_layer(x, c, lp, mixer):
    B, _, D = x.shape
    mod = (pallas_matmul(jax.nn.silu(c), lp['w_ada']) + lp['b_ada']).reshape(B, 6, 1, D)
    sh1, sc1, gt1, sh2, sc2, gt2 = (mod[:, i] for i in range(6))
    g = lp['norm_g']
    h = rms_norm(x, g[0]) * (1 + sc1) + sh1
    o, st = mixer(h)
    x = x + gt1 * rms_norm(o, g[1])
    h = rms_norm(x, g[2]) * (1 + sc2) + sh2
    x = x + gt2 * rms_norm(moe_ffn(h, lp), g[3])
    return x, st


def kernel(x_prompt, x_sample, c_prompt, c_sample, cache_cmp, cache_sel, state_win, state_wkv,
           state_shift, page_table, rel_bias, w_ada, b_ada, norm_g, w_in, w_out, rwkv_mu, rwkv_w0,
           rwkv_w_up, rwkv_a0, rwkv_a_up, rwkv_g_up, rwkv_k_k, rwkv_k_a, rwkv_r_k, rwkv_lnx_w,
           rwkv_lnx_b, cmp_pe, cmp_w, nsa_out_g, router_w, router_bias, exp_w_gate, exp_w_up,
           exp_w_down, sh_w_gate, sh_w_up, sh_w_down):
    l = 0
    lp = dict(w_ada=w_ada[l], b_ada=b_ada[l], norm_g=norm_g[l], w_in=w_in[l], w_out=w_out[l],
              rwkv_mu=rwkv_mu[l], rwkv_w0=rwkv_w0[l], rwkv_w_up=rwkv_w_up[l], rwkv_a0=rwkv_a0[l],
              rwkv_a_up=rwkv_a_up[l], rwkv_g_up=rwkv_g_up[l], rwkv_k_k=rwkv_k_k[l],
              rwkv_k_a=rwkv_k_a[l], rwkv_r_k=rwkv_r_k[l], rwkv_lnx_w=rwkv_lnx_w[l],
              rwkv_lnx_b=rwkv_lnx_b[l], cmp_pe=cmp_pe[l], cmp_w=cmp_w[l], nsa_out_g=nsa_out_g[l],
              router_w=router_w[l], router_bias=router_bias[l], exp_w_gate=exp_w_gate[l],
              exp_w_up=exp_w_up[l], exp_w_down=exp_w_down[l], sh_w_gate=sh_w_gate[l],
              sh_w_up=sh_w_up[l], sh_w_down=sh_w_down[l])
    yp, st_p = trunk_layer(x_prompt, c_prompt, lp, lambda h: mixer_prompt(h, lp, rel_bias))
    ys, st_s = trunk_layer(x_sample, c_sample, lp, lambda h: mixer_sample(
        h, lp, rel_bias, cache_cmp[l], cache_sel[l], state_win[l], state_wkv[l], state_shift[l], page_table))
    p_cmp, p_sel, p_win, p_wkv, p_shift = [a[None] for a in st_p]
    s_cmp, s_sel, s_win, s_wkv, s_shift = [a[None] for a in st_s]
    return (yp, ys, p_cmp, p_sel, p_win, p_wkv, p_shift, s_cmp, s_sel, s_win, s_wkv, s_shift)
```

```python
import functools
import math

import jax
import jax.numpy as jnp
from jax import lax
from jax.experimental import pallas as pl
from jax.experimental.pallas import tpu as pltpu

D_MODEL = 4096
RWKV_WIDTH = D_MODEL // 2
NSA_WIDTH = D_MODEL - RWKV_WIDTH
RWKV_HEAD_DIM = 64
RWKV_HEADS = RWKV_WIDTH // RWKV_HEAD_DIM
DECAY_LORA = 96
AAA_LORA = 96
GATE_LORA = 256
RWKV_PROJ = 3 * RWKV_WIDTH + DECAY_LORA + AAA_LORA + GATE_LORA
RWKV_SPLITS = (RWKV_WIDTH, 2 * RWKV_WIDTH, 3 * RWKV_WIDTH, 3 * RWKV_WIDTH + DECAY_LORA,
               3 * RWKV_WIDTH + DECAY_LORA + AAA_LORA)
LNX_EPS = 64e-5
NSA_HEAD_DIM = 128
NSA_HEADS = NSA_WIDTH // NSA_HEAD_DIM
NSA_KV_HEADS = 4
NSA_GROUP = NSA_HEADS // NSA_KV_HEADS
CMP_BLOCK = 32
SEL_BLOCK = 64
N_SEL = 16
WINDOW = 512
Q_CHUNK = 32
FORCE_SCORE = 1e4
NSA_KV_WIDTH = 2 * NSA_KV_HEADS * NSA_HEAD_DIM
NUM_BUCKETS = 32
MAX_DISTANCE = 128
N_EXPERTS = 64
N_GROUPS = 8
TOPK_GROUPS = 4
TOP_K = 8
ROUTED_SCALE = 2.5
MOE_BLOCK = 128
RMS_EPS = 1e-6

VMEM_LIMIT_BYTES = 48 * 1024 * 1024


def _matmul_kernel(x_ref, w_ref, o_ref):
    o_ref[...] = jnp.dot(x_ref[...].astype(jnp.bfloat16), w_ref[...].astype(jnp.bfloat16),
                         preferred_element_type=jnp.float32)


def pallas_matmul(x, w, tm=512, tn=512):
    m, k = x.shape
    n = w.shape[1]
    mp = -(-m // 8) * 8
    tm = min(tm, mp)
    mp = -(-mp // tm) * tm
    np_ = -(-n // tn) * tn
    if mp != m:
        x = jnp.pad(x, ((0, mp - m), (0, 0)))
    if np_ != n:
        w = jnp.pad(w, ((0, 0), (0, np_ - n)))
    out = pl.pallas_call(
        _matmul_kernel,
        grid=(np_ // tn, mp // tm),
        in_specs=[pl.BlockSpec((tm, k), lambda j, i: (i, 0)),
                  pl.BlockSpec((k, tn), lambda j, i: (0, j))],
        out_specs=pl.BlockSpec((tm, tn), lambda j, i: (i, j)),
        out_shape=jax.ShapeDtypeStruct((mp, np_), jnp.float32),
        compiler_params=pltpu.CompilerParams(
            dimension_semantics=("arbitrary", "arbitrary"), vmem_limit_bytes=VMEM_LIMIT_BYTES),
        name="matmul",
    )(x, w)
    return out[:m, :n]


def mm3(h, w):
    b, t, d = h.shape
    return pallas_matmul(h.reshape(b * t, d), w).reshape(b, t, w.shape[1])


def rms_norm(x, g):
    xf = x.astype(jnp.float32)
    y = xf * lax.rsqrt(jnp.mean(xf * xf, axis=-1, keepdims=True) + RMS_EPS)
    return (y * g.astype(jnp.float32)).astype(x.dtype)


def rel_bucket(dist):
    max_exact = NUM_BUCKETS // 2
    n = jnp.maximum(dist, 0)
    nf = jnp.maximum(n, max_exact).astype(jnp.float32)
    large = max_exact + (jnp.log(nf / max_exact) / math.log(MAX_DISTANCE / max_exact)
                         * (NUM_BUCKETS - max_exact)).astype(jnp.int32)
    return jnp.where(n < max_exact, n, jnp.minimum(large, NUM_BUCKETS - 1))


def masked_softmax(logits, mask):
    logits = jnp.where(mask, logits.astype(jnp.float32), -jnp.inf)
    m = jnp.max(logits, axis=-1, keepdims=True)
    p = jnp.exp(logits - jnp.where(jnp.isfinite(m), m, 0.0))
    s = jnp.sum(p, axis=-1, keepdims=True)
    return p / jnp.where(s > 0, s, 1.0)


LANE = 128
P_XW = 3 * RWKV_WIDTH
P_XA = P_XW + LANE
P_XG = P_XA + LANE
P_Q = P_XG + GATE_LORA
P_KVC = P_Q + NSA_WIDTH
P_KVS = P_KVC + NSA_KV_WIDTH
P_KVW = P_KVS + NSA_KV_WIDTH
P_GATES = P_KVW + NSA_KV_WIDTH
P_TOTAL = P_GATES + NSA_KV_HEADS * LANE
N_GATE_COLS = 3 * NSA_GROUP


def pad_w_in(w_in):
    d = w_in.shape[0]
    z = lambda n: jnp.zeros((d, n), w_in.dtype)
    o_xw, o_xa, o_xg = RWKV_SPLITS[2], RWKV_SPLITS[3], RWKV_SPLITS[4]
    o_g = RWKV_PROJ + NSA_WIDTH + 3 * NSA_KV_WIDTH
    parts = [w_in[:, :o_xw], w_in[:, o_xw:o_xa], z(LANE - DECAY_LORA), w_in[:, o_xa:o_xg], z(LANE - AAA_LORA),
             w_in[:, o_xg:o_g]]
    for kh in range(NSA_KV_HEADS):
        parts += [w_in[:, o_g + kh * N_GATE_COLS:o_g + (kh + 1) * N_GATE_COLS], z(LANE - N_GATE_COLS)]
    return jnp.concatenate(parts, axis=1).astype(jnp.bfloat16)


def unpad_rwkv(proj_p):
    return jnp.concatenate([proj_p[..., :P_XW], proj_p[..., P_XW:P_XW + DECAY_LORA],
                            proj_p[..., P_XA:P_XA + AAA_LORA], proj_p[..., P_XG:P_XG + GATE_LORA]], axis=-1)


def bucket_table(n):
    import numpy as np
    d = np.arange(n)
    max_exact = NUM_BUCKETS // 2
    nf = np.maximum(d, max_exact).astype(np.float64)
    large = max_exact + (np.log(nf / max_exact) / math.log(MAX_DISTANCE / max_exact)
                         * (NUM_BUCKETS - max_exact)).astype(np.int64)
    return np.where(d < max_exact, d, np.minimum(large, NUM_BUCKETS - 1)).astype(np.int32)


def _compress_kernel(x_ref, w_ref, pe_ref, o_ref, *, nblk):
    half = nblk // 2
    acc = jnp.zeros((nblk, NSA_HEAD_DIM), jnp.float32)
    for c in range(CMP_BLOCK):
        xe = x_ref[pl.ds(c, half, stride=2 * CMP_BLOCK), :]
        xo = x_ref[pl.ds(CMP_BLOCK + c, half, stride=2 * CMP_BLOCK), :]
        lhs = jnp.concatenate([xe, xo], axis=0) + pe_ref[c:c + 1, :]
        acc = acc + jnp.dot(lhs.astype(jnp.bfloat16), w_ref[c].astype(jnp.bfloat16),
                            preferred_element_type=jnp.float32)
    o_ref[...] = acc


def compress_prompt(proj_p, cmp_pe, cmp_w, B, T):
    nblk = T // CMP_BLOCK
    hd, KVH = NSA_HEAD_DIM, NSA_KV_HEADS
    return pl.pallas_call(
        functools.partial(_compress_kernel, nblk=nblk),
        grid=(B, 2, KVH),
        in_specs=[pl.BlockSpec((T, hd), lambda b, j, kh: (b, P_KVC // hd + j * KVH + kh)),
                  pl.BlockSpec((None, CMP_BLOCK, hd, hd), lambda b, j, kh: (j, 0, 0, 0)),
                  pl.BlockSpec((None, CMP_BLOCK, hd), lambda b, j, kh: (j, 0, 0))],
        out_specs=pl.BlockSpec((None, None, None, nblk, hd), lambda b, j, kh: (b, j, kh, 0, 0)),
        out_shape=jax.ShapeDtypeStruct((B, 2, KVH, nblk, hd), jnp.float32),
        compiler_params=pltpu.CompilerParams(dimension_semantics=("arbitrary",) * 3, vmem_limit_bytes=VMEM_LIMIT_BYTES),
        name="compress_prompt",
    )(proj_p, cmp_w, cmp_pe)


NSA_TQ = 128
NEG_BIG = -1e30


def _nsa_prompt_kernel(q_ref, gate_ref, ks_ref, vs_ref, kw_ref, vw_ref, kc_ref, vc_ref, bt_ref, bc_ref, es_ref,
                       o_ref, *, n_sb):
    f32, bf16 = jnp.float32, jnp.bfloat16
    tq, G, hd = NSA_TQ, NSA_GROUP, NSA_HEAD_DIM
    qi = pl.program_id(2)
    t0 = qi * tq
    scale = NSA_HEAD_DIM ** -0.5
    q = jnp.concatenate([q_ref[:, g * hd:(g + 1) * hd] for g in range(G)], axis=0).astype(bf16)
    rows = G * tq
    t_row = t0 + (lax.broadcasted_iota(jnp.int32, (rows, 1), 0) & (tq - 1))

    def qk(k):
        return lax.dot_general(q, k.astype(bf16), (((1,), (1,)), ((), ())), preferred_element_type=f32)

    ncmp = kc_ref.shape[0]
    lane = lax.broadcasted_iota(jnp.int32, (1, ncmp), 1)
    blk_id = jnp.where(lane < ncmp // 2, 2 * lane, 2 * (lane - ncmp // 2) + 1)
    c_end = blk_id * CMP_BLOCK + (CMP_BLOCK - 1)
    vis = c_end <= t_row
    lc = jnp.where(vis, qk(kc_ref[...]) * scale + bc_ref[...], -jnp.inf)
    mc = jnp.max(lc, axis=-1, keepdims=True)
    pc = jnp.exp(lc - jnp.where(mc > -jnp.inf, mc, 0.0))
    sc = jnp.sum(pc, axis=-1, keepdims=True)
    pc = pc / jnp.where(sc > 0, sc, 1.0)
    oc = jnp.dot(pc.astype(bf16), vc_ref[...].astype(bf16), preferred_element_type=f32)

    pg = pc[0:tq]
    for g in range(1, G):
        pg = pg + pc[g * tq:(g + 1) * tq]
    imp = pg[:, :n_sb] + pg[:, n_sb:]
    tpos = t0 + lax.broadcasted_iota(jnp.int32, (tq, 1), 0)
    cur = tpos // SEL_BLOCK
    bl = lax.broadcasted_iota(jnp.int32, (tq, n_sb), 1)
    force = (bl == 0) | (bl == cur) | (bl == cur - 1)
    score = jnp.where(force, FORCE_SCORE, imp)
    score = jnp.where(bl <= cur, score, -jnp.inf)
    rank = jnp.zeros((tq, n_sb), jnp.int32)
    for i in range(n_sb):
        si = score[:, i:i + 1]
        ahead = (si > score) | ((si == score) & (i < bl))
        rank = rank + ahead.astype(jnp.int32)
    selmask = (rank < N_SEL).astype(bf16)
    selmask = jnp.concatenate([selmask] * G, axis=0)

    col = lax.broadcasted_iota(jnp.int32, (1, tq), 1)

    def attend(c, carry, k_ref, v_ref, selected):
        m, l, acc = carry
        s0 = pl.multiple_of(c * tq, tq)
        s = qk(k_ref[pl.ds(s0, tq), :]) * scale + bt_ref[jnp.minimum(qi - c, 2)]
        dist = t_row - (s0 + col)
        if selected:
            ok = (jnp.dot(selmask, es_ref[c], preferred_element_type=f32) > 0.5) & (dist >= 0)
        else:
            ok = (dist >= 0) & (dist <= WINDOW)
        s = jnp.where(ok, s, NEG_BIG)
        m_new = jnp.maximum(m, jnp.max(s, axis=-1, keepdims=True))
        p = jnp.where(ok, jnp.exp(s - m_new), 0.0)
        alpha = jnp.exp(m - m_new)
        l = alpha * l + jnp.sum(p, axis=-1, keepdims=True)
        acc = alpha * acc + jnp.dot(p.astype(bf16), v_ref[pl.ds(s0, tq), :].astype(bf16), preferred_element_type=f32)
        return m_new, l, acc

    init = (jnp.full((rows, 1), NEG_BIG, f32), jnp.zeros((rows, 1), f32), jnp.zeros((rows, hd), f32))
    _, l_s, acc_s = lax.fori_loop(0, qi + 1, functools.partial(attend, k_ref=ks_ref, v_ref=vs_ref, selected=True), init)
    c_lo = jnp.maximum(qi - WINDOW // tq, 0)
    _, l_w, acc_w = lax.fori_loop(c_lo, qi + 1, functools.partial(attend, k_ref=kw_ref, v_ref=vw_ref, selected=False), init)
    osel = acc_s / l_s
    owin = acc_w / l_w

    gate = jax.nn.sigmoid(gate_ref[...])
    for g in range(G):
        r = slice(g * tq, (g + 1) * tq)
        o_ref[:, g * hd:(g + 1) * hd] = (gate[:, 3 * g:3 * g + 1] * oc[r] + gate[:, 3 * g + 1:3 * g + 2] * osel[r]
                                         + gate[:, 3 * g + 2:3 * g + 3] * owin[r])


def nsa_prompt(proj_p, kvc, rel_bias, B, T):
    import numpy as np
    tq, G, hd, KVH = NSA_TQ, NSA_GROUP, NSA_HEAD_DIM, NSA_KV_HEADS
    nq = T // tq
    n_sb = T // SEL_BLOCK
    ncmp = T // CMP_BLOCK
    assert T % tq == 0 and T % SEL_BLOCK == 0 and ncmp == 2 * n_sb
    bucket = bucket_table(max(T, 3 * tq))
    tab = rel_bias.astype(jnp.float32)
    ii, jj = np.meshgrid(np.arange(tq), np.arange(tq), indexing="ij")
    didx = np.stack([bucket[np.maximum(ii - jj, 0)], bucket[tq + ii - jj], np.full((tq, tq), NUM_BUCKETS - 1)])
    assert bucket[tq + 1] == NUM_BUCKETS - 1
    bt = tab[didx].reshape(3, tq, tq, KVH, G).transpose(3, 0, 4, 1, 2).reshape(KVH, 3, G * tq, tq)
    blk = np.concatenate([np.arange(0, ncmp, 2), np.arange(1, ncmp, 2)])
    dc = np.arange(T)[:, None] - (blk * CMP_BLOCK + CMP_BLOCK - 1)[None, :]
    bc = tab[bucket[np.maximum(dc, 0)]].reshape(nq, tq, ncmp, KVH, G).transpose(3, 0, 4, 1, 2).reshape(KVH, nq, G * tq, ncmp)
    es = (np.arange(n_sb)[None, :, None] == (np.arange(nq)[:, None, None] * (tq // SEL_BLOCK)
                                              + np.arange(tq)[None, None, :] // SEL_BLOCK))
    es = jnp.asarray(es, jnp.bfloat16)
    col = lambda off: off // hd
    kv_spec = lambda off: pl.BlockSpec((T, hd), lambda b, kh, i: (b, col(off) + kh))
    return pl.pallas_call(
        functools.partial(_nsa_prompt_kernel, n_sb=n_sb),
        grid=(B, KVH, nq),
        in_specs=[pl.BlockSpec((tq, G * hd), lambda b, kh, i: (b * nq + i, P_Q // (G * hd) + kh)),
                  pl.BlockSpec((tq, LANE), lambda b, kh, i: (b * nq + i, P_GATES // LANE + kh)),
                  kv_spec(P_KVS), kv_spec(P_KVS + KVH * hd), kv_spec(P_KVW), kv_spec(P_KVW + KVH * hd),
                  pl.BlockSpec((None, None, None, ncmp, hd), lambda b, kh, i: (b, 0, kh, 0, 0)),
                  pl.BlockSpec((None, None, None, ncmp, hd), lambda b, kh, i: (b, 1, kh, 0, 0)),
                  pl.BlockSpec((None, 3, G * tq, tq), lambda b, kh, i: (kh, 0, 0, 0)),
                  pl.BlockSpec((None, None, G * tq, ncmp), lambda b, kh, i: (kh, i, 0, 0)),
                  pl.BlockSpec((nq, n_sb, tq), lambda b, kh, i: (0, 0, 0))],
        out_specs=pl.BlockSpec((tq, G * hd), lambda b, kh, i: (b * nq + i, kh)),
        out_shape=jax.ShapeDtypeStruct((B * T, NSA_WIDTH), jnp.float32),
        compiler_params=pltpu.CompilerParams(dimension_semantics=("arbitrary",) * 3, vmem_limit_bytes=VMEM_LIMIT_BYTES),
        name="nsa_prompt",
    )(proj_p, proj_p, proj_p, proj_p, proj_p, proj_p, kvc, kvc, bt, bc, es)


def pad_rows(a, length):
    return jnp.pad(a, [(0, 0), (0, length - a.shape[1])] + [(0, 0)] * (a.ndim - 2))


def rwkv7_mix(pr, shift_prev, wkv0, lp):
    B, T, _ = pr.shape
    f32 = jnp.float32
    prev = jnp.concatenate([shift_prev[:, None, :].astype(pr.dtype), pr[:, :-1]], axis=1)
    m = pr + (prev - pr) * lp['rwkv_mu']
    r, k, v, xw, xa, xg = jnp.split(m, RWKV_SPLITS, axis=-1)
    w_log = -jax.nn.softplus(-(lp['rwkv_w0'] + jnp.tanh(xw) @ lp['rwkv_w_up']).astype(f32)) - 0.5
    decay = jnp.exp(-jnp.exp(w_log))
    a = jax.nn.sigmoid((lp['rwkv_a0'] + xa @ lp['rwkv_a_up']).astype(f32))
    g = jax.nn.sigmoid(xg) @ lp['rwkv_g_up']

    def heads(t):
        return t.astype(f32).reshape(B, T, RWKV_HEADS, RWKV_HEAD_DIM)

    r, k, v, decay, a = heads(r), heads(k), heads(v), heads(decay), heads(a)
    kk = k * lp['rwkv_k_k'].astype(f32).reshape(RWKV_HEADS, RWKV_HEAD_DIM)
    kk = kk / jnp.maximum(jnp.linalg.norm(kk, axis=-1, keepdims=True), 1e-12)
    k = k * (1.0 + (a - 1.0) * lp['rwkv_k_a'].astype(f32).reshape(RWKV_HEADS, RWKV_HEAD_DIM))

    def step(S, inp):
        r_t, w_t, k_t, v_t, a_t, b_t = inp
        sa = jnp.einsum('bhvk,bhk->bhv', S, a_t)
        S = S * w_t[:, :, None, :] + sa[..., None] * b_t[:, :, None, :] + v_t[..., None] * k_t[:, :, None, :]
        return S, jnp.einsum('bhvk,bhk->bhv', S, r_t)

    xs = tuple(jnp.moveaxis(t, 1, 0) for t in (r, decay, k, v, -kk, kk * a))
    s_final, ys = lax.scan(step, wkv0.astype(f32), xs)
    y = jnp.moveaxis(ys, 0, 1)
    mu = jnp.mean(y, axis=-1, keepdims=True)
    var = jnp.mean(jnp.square(y - mu), axis=-1, keepdims=True)
    y = ((y - mu) * lax.rsqrt(var + LNX_EPS)).reshape(B, T, RWKV_WIDTH)
    y = y * lp['rwkv_lnx_w'].astype(f32) + lp['rwkv_lnx_b'].astype(f32)
    bonus = (jnp.sum(r * k * lp['rwkv_r_k'].astype(f32), axis=-1, keepdims=True) * v).reshape(B, T, RWKV_WIDTH)
    out = ((y + bonus) * g.astype(f32)).astype(pr.dtype)
    return out, pr[:, -1], s_final.astype(wkv0.dtype)


def compress_blocks(rows, pe, w):
    B, L = rows.shape[:2]
    blk = rows.reshape(B, L // CMP_BLOCK, CMP_BLOCK, 2, NSA_KV_HEADS, NSA_HEAD_DIM)
    blk = blk + jnp.swapaxes(pe, 0, 1)[None, None, :, :, None, :]
    return jnp.einsum('bncjkd,jcde->bnjke', blk, w)


def nsa_attend(q, q_pos, gates, kc, vc, c_end, n_sel_blocks, gather_sel, kw, vw, w_pos, rel_bias):
    B, Tq = q.shape[:2]
    f32 = jnp.float32
    scale = NSA_HEAD_DIM ** -0.5
    qg = q.reshape(B, Tq, NSA_KV_HEADS, NSA_GROUP, NSA_HEAD_DIM)
    bias_tab = rel_bias.astype(f32).reshape(NUM_BUCKETS, NSA_KV_HEADS, NSA_GROUP)

    dc = q_pos[:, None] - c_end[None, :]
    bias_c = bias_tab[rel_bucket(dc)].transpose(0, 2, 3, 1)
    lc = jnp.einsum('btkgd,bnkd->btkgn', qg, kc).astype(f32) * scale + bias_c
    pc = masked_softmax(lc, (dc >= 0)[:, None, None, :])
    oc = jnp.einsum('btkgn,bnkd->btkgd', pc.astype(vc.dtype), vc)

    imp = pc.sum(3).reshape(B, Tq, NSA_KV_HEADS, n_sel_blocks, SEL_BLOCK // CMP_BLOCK).sum(-1)
    blk = jnp.arange(n_sel_blocks)
    cur = (q_pos // SEL_BLOCK)[:, None]
    force = (blk == 0) | (blk == cur) | (blk == cur - 1)
    score = jnp.where(force[:, None, :], FORCE_SCORE, imp)
    score = jnp.where((blk <= cur)[:, None, :], score, -jnp.inf)
    _, idx = lax.top_k(jnp.moveaxis(score, 2, 1), min(N_SEL, n_sel_blocks))

    sel = gather_sel(idx)
    ks, vs = sel[..., 0, :], sel[..., 1, :]
    spos = idx[..., None] * SEL_BLOCK + jnp.arange(SEL_BLOCK)
    ds = q_pos[None, None, :, None, None] - spos
    hi = jnp.arange(NSA_KV_HEADS)[None, :, None, None, None]
    bias_s = jnp.moveaxis(bias_tab[rel_bucket(ds), hi], -1, 3)
    ls = jnp.einsum('btkgd,bktjsd->bktgjs', qg, ks).astype(f32) * scale + bias_s
    ms = (ds >= 0)[:, :, :, None]
    ps = masked_softmax(ls.reshape(ls.shape[:4] + (-1,)), ms.reshape(ms.shape[:4] + (-1,))).reshape(ls.shape)
    osel = jnp.einsum('bktgjs,bktjsd->btkgd', ps.astype(vs.dtype), vs)

    dw = q_pos[:, None] - w_pos[None, :]
    mw = (dw >= 0) & (dw <= WINDOW) & (w_pos >= 0)[None, :]
    bias_w = bias_tab[rel_bucket(dw)].transpose(0, 2, 3, 1)
    lw = jnp.einsum('btkgd,blkd->btkgl', qg, kw).astype(f32) * scale + bias_w
    pw = masked_softmax(lw, mw[:, None, None, :])
    ow = jnp.einsum('btkgl,blkd->btkgd', pw.astype(vw.dtype), vw)

    gt = gates.reshape(B, Tq, NSA_KV_HEADS, NSA_GROUP, 3)
    o = gt[..., 0:1] * oc + gt[..., 1:2] * osel + gt[..., 2:3] * ow
    return o.reshape(B, Tq, NSA_WIDTH)


def split_proj(h, lp):
    B, T, _ = h.shape
    proj = mm3(h, lp['w_in'])
    pr = proj[..., :RWKV_PROJ]
    pn = proj[..., RWKV_PROJ:]
    q = pn[..., :NSA_WIDTH].reshape(B, T, NSA_HEADS, NSA_HEAD_DIM)
    kv = pn[..., NSA_WIDTH:NSA_WIDTH + 3 * NSA_KV_WIDTH].reshape(B, T, 3, 2, NSA_KV_HEADS, NSA_HEAD_DIM)
    gates = jax.nn.sigmoid(pn[..., NSA_WIDTH + 3 * NSA_KV_WIDTH:].astype(jnp.float32))
    gates = gates.reshape(B, T, NSA_HEADS, 3).astype(h.dtype)
    return pr, q, kv[:, :, 0], kv[:, :, 1], kv[:, :, 2], gates


def merge_groups(o_rwkv, o_nsa, lp):
    o = jnp.concatenate([o_rwkv, rms_norm(o_nsa, lp['nsa_out_g'])], axis=-1)
    return mm3(o, lp['w_out'])


def mixer_prompt(h, lp, rel_bias):
    B, T, d = h.shape
    proj_p = pallas_matmul(h.reshape(B * T, d), lp['w_in_p'])
    pr = unpad_rwkv(proj_p).reshape(B, T, RWKV_PROJ)
    o_r, shift_last, wkv = rwkv7_mix(pr, jnp.zeros((B, RWKV_PROJ), h.dtype),
                                     jnp.zeros((B, RWKV_HEADS, RWKV_HEAD_DIM, RWKV_HEAD_DIM), jnp.float32), lp)
    kvc = compress_prompt(proj_p, lp['cmp_pe'], lp['cmp_w'], B, T)
    o_n = nsa_prompt(proj_p, kvc, rel_bias, B, T).reshape(B, T, NSA_WIDTH)
    kv = proj_p[:, P_KVC:P_GATES].reshape(B, T, 3, 2, NSA_KV_HEADS, NSA_HEAD_DIM)
    kv_c, kv_s, kv_w = kv[:, :, 0], kv[:, :, 1], kv[:, :, 2]
    win_keep = min(WINDOW, T)
    return merge_groups(o_r, o_n, lp), (kv_c, kv_s, kv_w[:, T - win_keep:], wkv, shift_last)


def mixer_prompt_gather(h, lp, rel_bias):
    B, T, _ = h.shape
    pr, q, kv_c, kv_s, kv_w, gates = split_proj(h, lp)
    o_r, shift_last, wkv = rwkv7_mix(pr, jnp.zeros((B, RWKV_PROJ), h.dtype),
                                     jnp.zeros((B, RWKV_HEADS, RWKV_HEAD_DIM, RWKV_HEAD_DIM), jnp.float32), lp)
    lp_len = -(-T // SEL_BLOCK) * SEL_BLOCK
    comp = compress_blocks(pad_rows(kv_c, lp_len), lp['cmp_pe'], lp['cmp_w'])
    kc, vc = comp[:, :, 0], comp[:, :, 1]
    c_end = jnp.arange(lp_len // CMP_BLOCK) * CMP_BLOCK + (CMP_BLOCK - 1)
    n_sb = lp_len // SEL_BLOCK
    sel_blocks = pad_rows(kv_s, lp_len).reshape(B, n_sb, SEL_BLOCK, 2, NSA_KV_HEADS, NSA_HEAD_DIM)
    bi = jnp.arange(B)[:, None, None, None]
    hi = jnp.arange(NSA_KV_HEADS)[None, :, None, None]

    def gather_sel(idx):
        return sel_blocks[bi, idx, :, :, hi, :]

    kw_pad = jnp.pad(kv_w, ((0, 0), (WINDOW, 0), (0, 0), (0, 0), (0, 0)))

    def chunk(i):
        t0 = i * Q_CHUNK
        qc = lax.dynamic_slice_in_dim(q, t0, Q_CHUNK, axis=1)
        gc = lax.dynamic_slice_in_dim(gates, t0, Q_CHUNK, axis=1)
        wc = lax.dynamic_slice_in_dim(kw_pad, t0, WINDOW + Q_CHUNK, axis=1)
        q_pos = t0 + jnp.arange(Q_CHUNK)
        w_pos = t0 - WINDOW + jnp.arange(WINDOW + Q_CHUNK)
        return nsa_attend(qc, q_pos, gc, kc, vc, c_end, n_sb, gather_sel,
                          wc[:, :, 0], wc[:, :, 1], w_pos, rel_bias)

    o_n = lax.map(chunk, jnp.arange(T // Q_CHUNK))
    o_n = jnp.moveaxis(o_n, 0, 1).reshape(B, T, NSA_WIDTH)
    win_keep = min(WINDOW, T)
    return merge_groups(o_r, o_n, lp), (kv_c, kv_s, kv_w[:, T - win_keep:], wkv, shift_last)


def mixer_sample(h, lp, rel_bias, pool_cmp, pool_sel, win_buf, wkv0, shift0, page_table):
    B, T, _ = h.shape
    n_pages = page_table.shape[1]
    page = pool_cmp.shape[1]
    past = n_pages * page
    pr, q, kv_c, kv_s, kv_w, gates = split_proj(h, lp)
    o_r, shift_last, wkv = rwkv7_mix(pr, shift0, wkv0, lp)
    lp_len = -(-(past + T) // SEL_BLOCK) * SEL_BLOCK
    q_pos = past + jnp.arange(T)

    past_c = pool_cmp[page_table].reshape(B, past, 2, NSA_KV_HEADS, NSA_HEAD_DIM)
    rows_c = pad_rows(jnp.concatenate([past_c, kv_c], axis=1), lp_len)
    comp = compress_blocks(rows_c, lp['cmp_pe'], lp['cmp_w'])
    kc, vc = comp[:, :, 0], comp[:, :, 1]
    c_end = jnp.arange(lp_len // CMP_BLOCK) * CMP_BLOCK + (CMP_BLOCK - 1)

    sub = page // SEL_BLOCK
    ns_past = past // SEL_BLOCK
    new_len = lp_len - past
    new_blocks = pad_rows(kv_s, new_len).reshape(B, new_len // SEL_BLOCK, SEL_BLOCK, 2, NSA_KV_HEADS, NSA_HEAD_DIM)
    pool_blk = pool_sel.reshape(pool_sel.shape[0], sub, SEL_BLOCK, 2, NSA_KV_HEADS, NSA_HEAD_DIM)
    bi = jnp.arange(B)[:, None, None, None]
    hi = jnp.arange(NSA_KV_HEADS)[None, :, None, None]

    def gather_sel(idx):
        ip = jnp.minimum(idx, ns_past - 1)
        phys = page_table[bi, ip // sub]
        from_pool = pool_blk[phys, ip % sub, :, :, hi, :]
        inew = jnp.clip(idx - ns_past, 0, new_blocks.shape[1] - 1)
        from_new = new_blocks[bi, inew, :, :, hi, :]
        return jnp.where((idx < ns_past)[..., None, None, None], from_pool, from_new)

    wrows = jnp.concatenate([win_buf, kv_w], axis=1)
    wb = win_buf.shape[1]
    w_pos = past - wb + jnp.arange(wb + T)
    o_n = nsa_attend(q, q_pos, gates, kc, vc, c_end, lp_len // SEL_BLOCK, gather_sel,
                     wrows[:, :, 0], wrows[:, :, 1], w_pos, rel_bias)
    return merge_groups(o_r, o_n, lp), (kv_c, kv_s, wrows[:, T:], wkv, shift_last)


def moe_ffn(h, lp):
    B, T, D = h.shape
    x = h.reshape(B * T, D)
    n = x.shape[0]
    f32 = jnp.float32
    scores = jax.nn.sigmoid((x @ lp['router_w']).astype(f32))
    biased = scores + lp['router_bias'].astype(f32)
    grp = biased.reshape(n, N_GROUPS, N_EXPERTS // N_GROUPS)
    grp_score = lax.top_k(grp, 2)[0].sum(-1)
    _, gidx = lax.top_k(grp_score, TOPK_GROUPS)
    gmask = jax.nn.one_hot(gidx, N_GROUPS, dtype=f32).sum(1)
    emask = jnp.repeat(gmask, N_EXPERTS // N_GROUPS, axis=1) > 0
    _, eidx = lax.top_k(jnp.where(emask, biased, -jnp.inf), TOP_K)
    w = jnp.take_along_axis(scores, eidx, axis=1)
    w = w / jnp.sum(w, axis=-1, keepdims=True) * ROUTED_SCALE

    nk = n * TOP_K
    flat_e = eidx.reshape(nk)
    order = jnp.argsort(flat_e)
    se = flat_e[order]
    counts = jnp.bincount(flat_e, length=N_EXPERTS)
    padded = (counts + MOE_BLOCK - 1) // MOE_BLOCK * MOE_BLOCK
    pad_end = jnp.cumsum(padded)
    dest = (pad_end - padded)[se] + jnp.arange(nk) - (jnp.cumsum(counts) - counts)[se]
    n_blocks = -(-(nk + N_EXPERTS * (MOE_BLOCK - 1)) // MOE_BLOCK)
    rows = n_blocks * MOE_BLOCK
    row_tok = jnp.zeros((rows,), jnp.int32).at[dest].set((order // TOP_K).astype(jnp.int32))
    row_w = jnp.zeros((rows,), f32).at[dest].set(w.reshape(nk)[order])
    blk_exp = jnp.minimum(jnp.searchsorted(pad_end, jnp.arange(n_blocks) * MOE_BLOCK, side='right'),
                          N_EXPERTS - 1)

    def expert_block(args):
        tok, wt, e = args
        xb = x[tok]
        hb = jax.nn.silu(xb @ lp['exp_w_gate'][e]) * (xb @ lp['exp_w_up'][e])
        return (hb @ lp['exp_w_down'][e]).astype(f32) * wt[:, None]

    out = lax.map(expert_block, (row_tok.reshape(n_blocks, MOE_BLOCK), row_w.reshape(n_blocks, MOE_BLOCK), blk_exp))
    routed = jnp.zeros((n, D), f32).at[row_tok].add(out.reshape(rows, D))
    shared = (jax.nn.silu(x @ lp['sh_w_gate']) * (x @ lp['sh_w_up'])) @ lp['sh_w_down']
    return (routed + shared.astype(f32)).astype(h.dtype).reshape(B, T, D)


def trunk_layer(x, c, lp, mixer):
    B, _, D = x.shape
    mod = (pallas_matmul(jax.nn.silu(c), lp['w_ada']) + lp['b_ada']).reshape(B, 6, 1, D)
    sh1, sc1, gt1, sh2, sc2, gt2 = (mod[:, i] for i in range(6))
    g = lp['norm_g']
    h = rms_norm(x, g[0]) * (1 + sc1) + sh1
    o, st = mixer(h)
    x = x + gt1 * rms_norm(o, g[1])
    h = rms_norm(x, g[2]) * (1 + sc2) + sh2
    x = x + gt2 * rms_norm(moe_ffn(h, lp), g[3])
    return x, st


def kernel(x_prompt, x_sample, c_prompt, c_sample, cache_cmp, cache_sel, state_win, state_wkv,
           state_shift, page_table, rel_bias, w_ada, b_ada, norm_g, w_in, w_out, rwkv_mu, rwkv_w0,
           rwkv_w_up, rwkv_a0, rwkv_a_up, rwkv_g_up, rwkv_k_k, rwkv_k_a, rwkv_r_k, rwkv_lnx_w,
           rwkv_lnx_b, cmp_pe, cmp_w, nsa_out_g, router_w, router_bias, exp_w_gate, exp_w_up,
           exp_w_down, sh_w_gate, sh_w_up, sh_w_down):
    l = 0
    lp = dict(w_ada=w_ada[l], b_ada=b_ada[l], norm_g=norm_g[l], w_in=w_in[l], w_out=w_out[l],
              rwkv_mu=rwkv_mu[l], rwkv_w0=rwkv_w0[l], rwkv_w_up=rwkv_w_up[l], rwkv_a0=rwkv_a0[l],
              rwkv_a_up=rwkv_a_up[l], rwkv_g_up=rwkv_g_up[l], rwkv_k_k=rwkv_k_k[l],
              rwkv_k_a=rwkv_k_a[l], rwkv_r_k=rwkv_r_k[l], rwkv_lnx_w=rwkv_lnx_w[l],
              rwkv_lnx_b=rwkv_lnx_b[l], cmp_pe=cmp_pe[l], cmp_w=cmp_w[l], nsa_out_g=nsa_out_g[l],
              router_w=router_w[l], router_bias=router_bias[l], exp_w_gate=exp_w_gate[l],
              exp_w_up=exp_w_up[l], exp_w_down=exp_w_down[l], sh_w_gate=sh_w_gate[l],
              sh_w_up=sh_w_up[l], sh_w_down=sh_w_down[l])
    lp['w_in_p'] = pad_w_in(w_in[l])
    yp, st_p = trunk_layer(x_prompt, c_prompt, lp, lambda h: mixer_prompt(h, lp, rel_bias))
    ys, st_s = trunk_layer(x_sample, c_sample, lp, lambda h: mixer_sample(
        h, lp, rel_bias, cache_cmp[l], cache_sel[l], state_win[l], state_wkv[l], state_shift[l], page_table))
    p_cmp, p_sel, p_win, p_wkv, p_shift = [a[None] for a in st_p]
    s_cmp, s_sel, s_win, s_wkv, s_shift = [a[None] for a in st_s]
    return (yp, ys, p_cmp, p_sel, p_win, p_wkv, p_shift, s_cmp, s_sel, s_win, s_wkv, s_shift)
```

```python
import functools
import math

import jax
import jax.numpy as jnp
from jax import lax
from jax.experimental import pallas as pl
from jax.experimental.pallas import tpu as pltpu

D_MODEL = 4096
RWKV_WIDTH = D_MODEL // 2
NSA_WIDTH = D_MODEL - RWKV_WIDTH
RWKV_HEAD_DIM = 64
RWKV_HEADS = RWKV_WIDTH // RWKV_HEAD_DIM
DECAY_LORA = 96
AAA_LORA = 96
GATE_LORA = 256
RWKV_PROJ = 3 * RWKV_WIDTH + DECAY_LORA + AAA_LORA + GATE_LORA
RWKV_SPLITS = (RWKV_WIDTH, 2 * RWKV_WIDTH, 3 * RWKV_WIDTH, 3 * RWKV_WIDTH + DECAY_LORA,
               3 * RWKV_WIDTH + DECAY_LORA + AAA_LORA)
LNX_EPS = 64e-5
NSA_HEAD_DIM = 128
NSA_HEADS = NSA_WIDTH // NSA_HEAD_DIM
NSA_KV_HEADS = 4
NSA_GROUP = NSA_HEADS // NSA_KV_HEADS
CMP_BLOCK = 32
SEL_BLOCK = 64
N_SEL = 16
WINDOW = 512
Q_CHUNK = 32
FORCE_SCORE = 1e4
NSA_KV_WIDTH = 2 * NSA_KV_HEADS * NSA_HEAD_DIM
NUM_BUCKETS = 32
MAX_DISTANCE = 128
N_EXPERTS = 64
N_GROUPS = 8
TOPK_GROUPS = 4
TOP_K = 8
ROUTED_SCALE = 2.5
MOE_BLOCK = 128
RMS_EPS = 1e-6

VMEM_LIMIT_BYTES = 56 * 1024 * 1024


def _matmul_kernel(x_ref, w_ref, o_ref):
    o_ref[...] = jnp.dot(x_ref[...].astype(jnp.bfloat16), w_ref[...].astype(jnp.bfloat16),
                         preferred_element_type=jnp.float32)


def pallas_matmul(x, w, tm=512, tn=512):
    m, k = x.shape
    n = w.shape[1]
    mp = -(-m // 8) * 8
    tm = min(tm, mp)
    mp = -(-mp // tm) * tm
    np_ = -(-n // tn) * tn
    if mp != m:
        x = jnp.pad(x, ((0, mp - m), (0, 0)))
    if np_ != n:
        w = jnp.pad(w, ((0, 0), (0, np_ - n)))
    out = pl.pallas_call(
        _matmul_kernel,
        grid=(np_ // tn, mp // tm),
        in_specs=[pl.BlockSpec((tm, k), lambda j, i: (i, 0)),
                  pl.BlockSpec((k, tn), lambda j, i: (0, j))],
        out_specs=pl.BlockSpec((tm, tn), lambda j, i: (i, j)),
        out_shape=jax.ShapeDtypeStruct((mp, np_), jnp.float32),
        compiler_params=pltpu.CompilerParams(
            dimension_semantics=("arbitrary", "arbitrary"), vmem_limit_bytes=VMEM_LIMIT_BYTES),
        name="matmul",
    )(x, w)
    return out[:m, :n]


def mm3(h, w):
    b, t, d = h.shape
    return pallas_matmul(h.reshape(b * t, d), w).reshape(b, t, w.shape[1])


def rms_norm(x, g):
    xf = x.astype(jnp.float32)
    y = xf * lax.rsqrt(jnp.mean(xf * xf, axis=-1, keepdims=True) + RMS_EPS)
    return (y * g.astype(jnp.float32)).astype(x.dtype)


def rel_bucket(dist):
    max_exact = NUM_BUCKETS // 2
    n = jnp.maximum(dist, 0)
    nf = jnp.maximum(n, max_exact).astype(jnp.float32)
    large = max_exact + (jnp.log(nf / max_exact) / math.log(MAX_DISTANCE / max_exact)
                         * (NUM_BUCKETS - max_exact)).astype(jnp.int32)
    return jnp.where(n < max_exact, n, jnp.minimum(large, NUM_BUCKETS - 1))


def masked_softmax(logits, mask):
    logits = jnp.where(mask, logits.astype(jnp.float32), -jnp.inf)
    m = jnp.max(logits, axis=-1, keepdims=True)
    p = jnp.exp(logits - jnp.where(jnp.isfinite(m), m, 0.0))
    s = jnp.sum(p, axis=-1, keepdims=True)
    return p / jnp.where(s > 0, s, 1.0)


LANE = 128
P_XW = 3 * RWKV_WIDTH
P_XA = P_XW + LANE
P_XG = P_XA + LANE
P_Q = P_XG + GATE_LORA
P_KVC = P_Q + NSA_WIDTH
P_KVS = P_KVC + NSA_KV_WIDTH
P_KVW = P_KVS + NSA_KV_WIDTH
P_GATES = P_KVW + NSA_KV_WIDTH
P_TOTAL = P_GATES + NSA_KV_HEADS * LANE
N_GATE_COLS = 3 * NSA_GROUP


def pad_w_in(w_in):
    d = w_in.shape[0]
    z = lambda n: jnp.zeros((d, n), w_in.dtype)
    o_xw, o_xa, o_xg = RWKV_SPLITS[2], RWKV_SPLITS[3], RWKV_SPLITS[4]
    o_g = RWKV_PROJ + NSA_WIDTH + 3 * NSA_KV_WIDTH
    parts = [w_in[:, :o_xw], w_in[:, o_xw:o_xa], z(LANE - DECAY_LORA), w_in[:, o_xa:o_xg], z(LANE - AAA_LORA),
             w_in[:, o_xg:o_g]]
    for kh in range(NSA_KV_HEADS):
        parts += [w_in[:, o_g + kh * N_GATE_COLS:o_g + (kh + 1) * N_GATE_COLS], z(LANE - N_GATE_COLS)]
    return jnp.concatenate(parts, axis=1).astype(jnp.bfloat16)


def unpad_rwkv(proj_p):
    return jnp.concatenate([proj_p[..., :P_XW], proj_p[..., P_XW:P_XW + DECAY_LORA],
                            proj_p[..., P_XA:P_XA + AAA_LORA], proj_p[..., P_XG:P_XG + GATE_LORA]], axis=-1)


def bucket_table(n):
    import numpy as np
    d = np.arange(n)
    max_exact = NUM_BUCKETS // 2
    nf = np.maximum(d, max_exact).astype(np.float64)
    large = max_exact + (np.log(nf / max_exact) / math.log(MAX_DISTANCE / max_exact)
                         * (NUM_BUCKETS - max_exact)).astype(np.int64)
    return np.where(d < max_exact, d, np.minimum(large, NUM_BUCKETS - 1)).astype(np.int32)


def _compress_kernel(x_ref, w_ref, pe_ref, o_ref, *, nblk):
    half = nblk // 2
    acc = jnp.zeros((nblk, NSA_HEAD_DIM), jnp.float32)
    for c in range(CMP_BLOCK):
        xe = x_ref[pl.ds(c, half, stride=2 * CMP_BLOCK), :]
        xo = x_ref[pl.ds(CMP_BLOCK + c, half, stride=2 * CMP_BLOCK), :]
        lhs = jnp.concatenate([xe, xo], axis=0) + pe_ref[c:c + 1, :]
        acc = acc + jnp.dot(lhs.astype(jnp.bfloat16), w_ref[c].astype(jnp.bfloat16),
                            preferred_element_type=jnp.float32)
    o_ref[...] = acc


def compress_prompt(proj_p, cmp_pe, cmp_w, B, T):
    nblk = T // CMP_BLOCK
    hd, KVH = NSA_HEAD_DIM, NSA_KV_HEADS
    return pl.pallas_call(
        functools.partial(_compress_kernel, nblk=nblk),
        grid=(B, 2, KVH),
        in_specs=[pl.BlockSpec((T, hd), lambda b, j, kh: (b, P_KVC // hd + j * KVH + kh)),
                  pl.BlockSpec((None, CMP_BLOCK, hd, hd), lambda b, j, kh: (j, 0, 0, 0)),
                  pl.BlockSpec((None, CMP_BLOCK, hd), lambda b, j, kh: (j, 0, 0))],
        out_specs=pl.BlockSpec((None, None, None, nblk, hd), lambda b, j, kh: (b, j, kh, 0, 0)),
        out_shape=jax.ShapeDtypeStruct((B, 2, KVH, nblk, hd), jnp.float32),
        compiler_params=pltpu.CompilerParams(dimension_semantics=("arbitrary",) * 3, vmem_limit_bytes=VMEM_LIMIT_BYTES),
        name="compress_prompt",
    )(proj_p, cmp_w, cmp_pe)


NSA_TQ = 128
NEG_BIG = -1e30


def _nsa_prompt_kernel(q_ref, gate_ref, ks_ref, vs_ref, kw_ref, vw_ref, kc_ref, vc_ref, bt_ref, bc_ref, es_ref,
                       o_ref, *, n_sb):
    f32, bf16 = jnp.float32, jnp.bfloat16
    tq, G, hd = NSA_TQ, NSA_GROUP, NSA_HEAD_DIM
    qi = pl.program_id(2)
    t0 = qi * tq
    scale = NSA_HEAD_DIM ** -0.5
    q = jnp.concatenate([q_ref[:, g * hd:(g + 1) * hd] for g in range(G)], axis=0).astype(bf16)
    rows = G * tq
    t_row = t0 + (lax.broadcasted_iota(jnp.int32, (rows, 1), 0) & (tq - 1))

    def qk(k):
        return lax.dot_general(q, k.astype(bf16), (((1,), (1,)), ((), ())), preferred_element_type=f32)

    ncmp = kc_ref.shape[0]
    lane = lax.broadcasted_iota(jnp.int32, (1, ncmp), 1)
    blk_id = jnp.where(lane < ncmp // 2, 2 * lane, 2 * (lane - ncmp // 2) + 1)
    c_end = blk_id * CMP_BLOCK + (CMP_BLOCK - 1)
    vis = c_end <= t_row
    lc = jnp.where(vis, qk(kc_ref[...]) * scale + bc_ref[...], -jnp.inf)
    mc = jnp.max(lc, axis=-1, keepdims=True)
    pc = jnp.exp(lc - jnp.where(mc > -jnp.inf, mc, 0.0))
    sc = jnp.sum(pc, axis=-1, keepdims=True)
    pc = pc / jnp.where(sc > 0, sc, 1.0)
    oc = jnp.dot(pc.astype(bf16), vc_ref[...].astype(bf16), preferred_element_type=f32)

    pg = pc[0:tq]
    for g in range(1, G):
        pg = pg + pc[g * tq:(g + 1) * tq]
    imp = pg[:, :n_sb] + pg[:, n_sb:]
    tpos = t0 + lax.broadcasted_iota(jnp.int32, (tq, 1), 0)
    cur = tpos // SEL_BLOCK
    bl = lax.broadcasted_iota(jnp.int32, (tq, n_sb), 1)
    force = (bl == 0) | (bl == cur) | (bl == cur - 1)
    score = jnp.where(force, FORCE_SCORE, imp)
    score = jnp.where(bl <= cur, score, -jnp.inf)
    rank = jnp.zeros((tq, n_sb), jnp.int32)
    for i in range(n_sb):
        si = score[:, i:i + 1]
        ahead = (si > score) | ((si == score) & (i < bl))
        rank = rank + ahead.astype(jnp.int32)
    selmask = (rank < N_SEL).astype(bf16)
    selmask = jnp.concatenate([selmask] * G, axis=0)

    col = lax.broadcasted_iota(jnp.int32, (1, tq), 1)

    def attend(c, carry, k_ref, v_ref, selected):
        m, l, acc = carry
        s0 = pl.multiple_of(c * tq, tq)
        s = qk(k_ref[pl.ds(s0, tq), :]) * scale + bt_ref[jnp.minimum(qi - c, 2)]
        dist = t_row - (s0 + col)
        if selected:
            ok = (jnp.dot(selmask, es_ref[c], preferred_element_type=f32) > 0.5) & (dist >= 0)
        else:
            ok = (dist >= 0) & (dist <= WINDOW)
        s = jnp.where(ok, s, NEG_BIG)
        m_new = jnp.maximum(m, jnp.max(s, axis=-1, keepdims=True))
        p = jnp.where(ok, jnp.exp(s - m_new), 0.0)
        alpha = jnp.exp(m - m_new)
        l = alpha * l + jnp.sum(p, axis=-1, keepdims=True)
        acc = alpha * acc + jnp.dot(p.astype(bf16), v_ref[pl.ds(s0, tq), :].astype(bf16), preferred_element_type=f32)
        return m_new, l, acc

    init = (jnp.full((rows, 1), NEG_BIG, f32), jnp.zeros((rows, 1), f32), jnp.zeros((rows, hd), f32))
    _, l_s, acc_s = lax.fori_loop(0, qi + 1, functools.partial(attend, k_ref=ks_ref, v_ref=vs_ref, selected=True), init)
    c_lo = jnp.maximum(qi - WINDOW // tq, 0)
    _, l_w, acc_w = lax.fori_loop(c_lo, qi + 1, functools.partial(attend, k_ref=kw_ref, v_ref=vw_ref, selected=False), init)
    osel = acc_s / l_s
    owin = acc_w / l_w

    gate = jax.nn.sigmoid(gate_ref[...])
    for g in range(G):
        r = slice(g * tq, (g + 1) * tq)
        o_ref[:, g * hd:(g + 1) * hd] = (gate[:, 3 * g:3 * g + 1] * oc[r] + gate[:, 3 * g + 1:3 * g + 2] * osel[r]
                                         + gate[:, 3 * g + 2:3 * g + 3] * owin[r])


def nsa_prompt(proj_p, kvc, rel_bias, B, T):
    import numpy as np
    tq, G, hd, KVH = NSA_TQ, NSA_GROUP, NSA_HEAD_DIM, NSA_KV_HEADS
    nq = T // tq
    n_sb = T // SEL_BLOCK
    ncmp = T // CMP_BLOCK
    assert T % tq == 0 and T % SEL_BLOCK == 0 and ncmp == 2 * n_sb
    bucket = bucket_table(max(T, 3 * tq))
    tab = rel_bias.astype(jnp.float32)
    ii, jj = np.meshgrid(np.arange(tq), np.arange(tq), indexing="ij")
    didx = np.stack([bucket[np.maximum(ii - jj, 0)], bucket[tq + ii - jj], np.full((tq, tq), NUM_BUCKETS - 1)])
    assert bucket[tq + 1] == NUM_BUCKETS - 1
    bt = tab[didx].reshape(3, tq, tq, KVH, G).transpose(3, 0, 4, 1, 2).reshape(KVH, 3, G * tq, tq)
    blk = np.concatenate([np.arange(0, ncmp, 2), np.arange(1, ncmp, 2)])
    dc = np.arange(T)[:, None] - (blk * CMP_BLOCK + CMP_BLOCK - 1)[None, :]
    bc = tab[bucket[np.maximum(dc, 0)]].reshape(nq, tq, ncmp, KVH, G).transpose(3, 0, 4, 1, 2).reshape(KVH, nq, G * tq, ncmp)
    es = (np.arange(n_sb)[None, :, None] == (np.arange(nq)[:, None, None] * (tq // SEL_BLOCK)
                                              + np.arange(tq)[None, None, :] // SEL_BLOCK))
    es = jnp.asarray(es, jnp.bfloat16)
    col = lambda off: off // hd
    kv_spec = lambda off: pl.BlockSpec((T, hd), lambda b, kh, i: (b, col(off) + kh))
    return pl.pallas_call(
        functools.partial(_nsa_prompt_kernel, n_sb=n_sb),
        grid=(B, KVH, nq),
        in_specs=[pl.BlockSpec((tq, G * hd), lambda b, kh, i: (b * nq + i, P_Q // (G * hd) + kh)),
                  pl.BlockSpec((tq, LANE), lambda b, kh, i: (b * nq + i, P_GATES // LANE + kh)),
                  kv_spec(P_KVS), kv_spec(P_KVS + KVH * hd), kv_spec(P_KVW), kv_spec(P_KVW + KVH * hd),
                  pl.BlockSpec((None, None, None, ncmp, hd), lambda b, kh, i: (b, 0, kh, 0, 0)),
                  pl.BlockSpec((None, None, None, ncmp, hd), lambda b, kh, i: (b, 1, kh, 0, 0)),
                  pl.BlockSpec((None, 3, G * tq, tq), lambda b, kh, i: (kh, 0, 0, 0)),
                  pl.BlockSpec((None, None, G * tq, ncmp), lambda b, kh, i: (kh, i, 0, 0)),
                  pl.BlockSpec((nq, n_sb, tq), lambda b, kh, i: (0, 0, 0))],
        out_specs=pl.BlockSpec((tq, G * hd), lambda b, kh, i: (b * nq + i, kh)),
        out_shape=jax.ShapeDtypeStruct((B * T, NSA_WIDTH), jnp.float32),
        compiler_params=pltpu.CompilerParams(dimension_semantics=("arbitrary",) * 3, vmem_limit_bytes=VMEM_LIMIT_BYTES),
        name="nsa_prompt",
    )(proj_p, proj_p, proj_p, proj_p, proj_p, proj_p, kvc, kvc, bt, bc, es)


RWKV_PAIR = 2 * RWKV_HEAD_DIM
HIGHEST = lax.Precision.HIGHEST


def _rwkv_kernel(r_ref, k_ref, v_ref, xw_ref, xa_ref, xg_ref,
                 sr_ref, sk_ref, sv_ref, sxw_ref, sxa_ref, sxg_ref,
                 mr_ref, mk_ref, mv_ref, mxw_ref, mxa_ref, mxg_ref,
                 s0_ref, w0_ref, a0_ref, kk_ref, ka_ref, rk_ref, lnw_ref, lnb_ref,
                 wup_ref, aup_ref, gup_ref,
                 o_ref, sout_ref,
                 S_ref, cr_ref, ck_ref, cv_ref, cxw_ref, cxa_ref, cxg_ref, *, C, t_valid, n_chunks):
    f32, bf16 = jnp.float32, jnp.bfloat16
    hd, W = RWKV_HEAD_DIM, RWKV_PAIR
    ci = pl.program_id(2)
    lane = lax.broadcasted_iota(jnp.int32, (1, W), 1)
    h0 = lane < hd
    row = lax.broadcasted_iota(jnp.int32, (C, 1), 0)
    valid = (ci * C + row) < t_valid
    carries = ((cr_ref, sr_ref), (ck_ref, sk_ref), (cv_ref, sv_ref), (cxw_ref, sxw_ref), (cxa_ref, sxa_ref),
               (cxg_ref, sxg_ref))

    @pl.when(ci == 0)
    def _():
        S_ref[...] = jnp.zeros((W, W), f32)
        S_ref[0:hd, 0:hd] = s0_ref[0]
        S_ref[hd:W, hd:W] = s0_ref[1]
        for c_ref, s_ref in carries:
            c_ref[0:1, :] = s_ref[...]

    def shifted(x_ref, c_ref, mu_ref):
        x = x_ref[...]
        prev = jnp.where(row == 0, c_ref[0:1, :], pltpu.roll(x, 1, axis=0))
        c_ref[0:1, :] = x[C - 1:C, :]
        return x + (prev - x) * mu_ref[...]

    def seg_sum(x):
        s_lo = jnp.sum(jnp.where(h0, x, 0.0), axis=-1, keepdims=True)
        s_hi = jnp.sum(jnp.where(h0, 0.0, x), axis=-1, keepdims=True)
        return jnp.where(h0, s_lo, s_hi)

    def stack(x):
        return jnp.concatenate([jnp.where(h0, x, 0.0), jnp.where(h0, 0.0, x)], axis=0)

    mr = shifted(r_ref, cr_ref, mr_ref)
    mk = shifted(k_ref, ck_ref, mk_ref)
    mv = shifted(v_ref, cv_ref, mv_ref)
    mxw = shifted(xw_ref, cxw_ref, mxw_ref)
    mxa = shifted(xa_ref, cxa_ref, mxa_ref)
    mxg = shifted(xg_ref, cxg_ref, mxg_ref)

    z = w0_ref[...] + jnp.dot(jnp.tanh(mxw).astype(bf16), wup_ref[...].astype(bf16), preferred_element_type=f32)
    w_log = -jax.nn.softplus(-z) - 0.5
    lw = -jnp.exp(w_log)
    a = jax.nn.sigmoid(a0_ref[...] + jnp.dot(mxa.astype(bf16), aup_ref[...].astype(bf16), preferred_element_type=f32))
    g = jnp.dot(jax.nn.sigmoid(mxg).astype(bf16), gup_ref[...].astype(bf16), preferred_element_type=f32)
    kk = mk * kk_ref[...]
    kk = kk / jnp.maximum(jnp.sqrt(seg_sum(kk * kk)), 1e-12)
    kp = mk * (1.0 + (a - 1.0) * ka_ref[...])
    lw = jnp.where(valid, lw, 0.0)
    a_t = jnp.where(valid, -kk, 0.0)
    b_t = jnp.where(valid, kk * a, 0.0)
    k_t = jnp.where(valid, kp, 0.0)
    v_t = jnp.where(valid, mv, 0.0)

    ti = lax.broadcasted_iota(jnp.int32, (C, C), 0)
    tj = lax.broadcasted_iota(jnp.int32, (C, C), 1)
    cl = jnp.dot((tj <= ti).astype(f32), lw, precision=HIGHEST, preferred_element_type=f32)
    e_neg = jnp.exp(-cl)
    At = a_t * jnp.exp(cl - lw)
    Bt = b_t * e_neg
    Kt = k_t * e_neg
    Rt = mr * jnp.exp(cl)
    G = jnp.concatenate([stack(At), stack(Rt)], axis=0).astype(bf16)
    Z = jnp.concatenate([Bt, Bt, Kt, Kt], axis=0).astype(bf16)
    nt = (((1,), (1,)), ((), ()))
    M = lax.dot_general(G, Z, nt, preferred_element_type=f32)
    ri = lax.broadcasted_iota(jnp.int32, (2 * C, 2 * C), 0)
    cj = lax.broadcasted_iota(jnp.int32, (2 * C, 2 * C), 1)
    same = (ri >= C) == (cj >= C)
    strict = same & ((cj & (C - 1)) < (ri & (C - 1)))
    incl = same & ((cj & (C - 1)) <= (ri & (C - 1)))
    L = jnp.where(strict, M[0:2 * C, 0:2 * C], 0.0)
    AK = jnp.where(strict, M[0:2 * C, 2 * C:4 * C], 0.0)
    RB = jnp.where(incl, M[2 * C:4 * C, 0:2 * C], 0.0)
    RK = jnp.where(incl, M[2 * C:4 * C, 2 * C:4 * C], 0.0)
    S = S_ref[...]
    GS = lax.dot_general(G, S.astype(bf16), nt, preferred_element_type=f32)
    Vs = stack(v_t)
    U = GS[0:2 * C] + jnp.dot(AK.astype(bf16), Vs.astype(bf16), preferred_element_type=f32)
    P = L
    levels = C.bit_length() - 1
    for lvl in range(levels):
        U = U + jnp.dot(P, U, precision=HIGHEST, preferred_element_type=f32)
        if lvl + 1 < levels:
            P = jnp.dot(P, P, precision=HIGHEST, preferred_element_type=f32)
    X = jnp.concatenate([U, Vs], axis=0).astype(bf16)
    Y = GS[2 * C:4 * C] + jnp.dot(jnp.concatenate([RB, RK], axis=1).astype(bf16), X, preferred_element_type=f32)
    y = Y[0:C] + Y[C:2 * C]
    dS = lax.dot_general(X, Z, (((0,), (0,)), ((), ())), preferred_element_type=f32)
    vi = lax.broadcasted_iota(jnp.int32, (W, W), 0)
    kj = lax.broadcasted_iota(jnp.int32, (W, W), 1)
    S_new = jnp.where((vi < hd) == (kj < hd), S + dS, 0.0) * jnp.exp(cl[C - 1:C, :])
    S_ref[...] = S_new

    mean = seg_sum(y) * (1.0 / hd)
    yc = y - mean
    var = seg_sum(yc * yc) * (1.0 / hd)
    yn = yc * lax.rsqrt(var + LNX_EPS) * lnw_ref[...] + lnb_ref[...]
    bonus = seg_sum(mr * kp * rk_ref[...]) * mv
    o_ref[...] = (yn + bonus) * g

    @pl.when(ci == n_chunks - 1)
    def _():
        sout_ref[0] = S_new[0:hd, 0:hd]
        sout_ref[1] = S_new[hd:W, hd:W]


def rwkv_mix(proj_p, shift_prev, wkv0, lp, B, T, t_valid, C):
    f32 = jnp.float32
    W, hd = RWKV_PAIR, RWKV_HEAD_DIM
    n_pairs = RWKV_WIDTH // W
    n_chunks = T // C
    assert T % C == 0 and C & (C - 1) == 0 and C >= 8

    def pad_vec(x):
        z = lambda n: jnp.zeros(x.shape[:-1] + (n,), x.dtype)
        return jnp.concatenate([x[..., :RWKV_SPLITS[2]], x[..., RWKV_SPLITS[2]:RWKV_SPLITS[3]], z(LANE - DECAY_LORA),
                                x[..., RWKV_SPLITS[3]:RWKV_SPLITS[4]], z(LANE - AAA_LORA), x[..., RWKV_SPLITS[4]:]], -1)

    shift_p = pad_vec(shift_prev.astype(f32))[:, None, :]
    mu_p = pad_vec(lp['rwkv_mu'].astype(f32))[None, :]
    pad_rows_to = lambda w: jnp.pad(w, ((0, LANE - w.shape[0]), (0, 0)))
    wup, aup = pad_rows_to(lp['rwkv_w_up']), pad_rows_to(lp['rwkv_a_up'])
    vec = lambda x: x.reshape(1, RWKV_WIDTH).astype(f32)
    nw = RWKV_WIDTH // W
    col_r, col_k, col_v = 0, nw, 2 * nw
    col_xw, col_xa, col_xg = P_XW // LANE, P_XA // LANE, P_XG // GATE_LORA
    row_spec = lambda w, cfn: pl.BlockSpec((C, w), lambda b, p, c: (b * n_chunks + c, cfn(p)))
    sh_spec = lambda w, cfn: pl.BlockSpec((None, 1, w), lambda b, p, c: (b, 0, cfn(p)))
    mu_spec = lambda w, cfn: pl.BlockSpec((1, w), lambda b, p, c: (0, cfn(p)))
    cols = [(W, lambda p: col_r + p), (W, lambda p: col_k + p), (W, lambda p: col_v + p),
            (LANE, lambda p: col_xw), (LANE, lambda p: col_xa), (GATE_LORA, lambda p: col_xg)]
    pvec = pl.BlockSpec((1, W), lambda b, p, c: (0, p))
    state_spec = pl.BlockSpec((None, 2, hd, hd), lambda b, p, c: (b, p, 0, 0))
    in_specs = ([row_spec(w, f) for w, f in cols] + [sh_spec(w, f) for w, f in cols] + [mu_spec(w, f) for w, f in cols]
                + [state_spec] + [pvec] * 7
                + [pl.BlockSpec((LANE, W), lambda b, p, c: (0, p)), pl.BlockSpec((LANE, W), lambda b, p, c: (0, p)),
                   pl.BlockSpec((GATE_LORA, W), lambda b, p, c: (0, p))])
    scratch = [pltpu.VMEM((W, W), f32)] + [pltpu.VMEM((8, w), f32) for w, _ in cols]
    o, s_fin = pl.pallas_call(
        functools.partial(_rwkv_kernel, C=C, t_valid=t_valid, n_chunks=n_chunks),
        grid=(B, n_pairs, n_chunks),
        in_specs=in_specs,
        out_specs=[pl.BlockSpec((C, W), lambda b, p, c: (b * n_chunks + c, p)), state_spec],
        out_shape=[jax.ShapeDtypeStruct((B * T, RWKV_WIDTH), f32),
                   jax.ShapeDtypeStruct((B, RWKV_HEADS, hd, hd), f32)],
        scratch_shapes=scratch,
        compiler_params=pltpu.CompilerParams(dimension_semantics=("arbitrary",) * 3, vmem_limit_bytes=VMEM_LIMIT_BYTES),
        name="rwkv_mix",
    )(*([proj_p] * 6), *([shift_p] * 6), *([mu_p] * 6), wkv0.astype(f32),
      vec(lp['rwkv_w0']), vec(lp['rwkv_a0']), vec(lp['rwkv_k_k']), vec(lp['rwkv_k_a']), vec(lp['rwkv_r_k']),
      vec(lp['rwkv_lnx_w']), vec(lp['rwkv_lnx_b']), wup, aup, lp['rwkv_g_up'])
    return o, s_fin


def _ffn_up_kernel(be_ref, nv_ref, x_ref, wg_ref, wu_ref, h_ref):
    del be_ref
    i = pl.program_id(0)

    @pl.when(i < nv_ref[0])
    def _():
        x = x_ref[...]
        hg = jnp.dot(x, wg_ref[...].astype(jnp.bfloat16), preferred_element_type=jnp.float32)
        hu = jnp.dot(x, wu_ref[...].astype(jnp.bfloat16), preferred_element_type=jnp.float32)
        h_ref[...] = (jax.nn.silu(hg) * hu).astype(h_ref.dtype)

    @pl.when(i >= nv_ref[0])
    def _():
        h_ref[...] = jnp.zeros(h_ref.shape, h_ref.dtype)


def _ffn_down_kernel(be_ref, nv_ref, h_ref, wd_ref, rw_ref, o_ref):
    del be_ref
    i = pl.program_id(0)

    @pl.when(i < nv_ref[0])
    def _():
        o_ref[...] = jnp.dot(h_ref[...].astype(jnp.bfloat16), wd_ref[...].astype(jnp.bfloat16),
                             preferred_element_type=jnp.float32) * rw_ref[...]

    @pl.when(i >= nv_ref[0])
    def _():
        o_ref[...] = jnp.zeros(o_ref.shape, o_ref.dtype)


def expert_ffn(xs, w_gate, w_up, w_down, row_w, blk_exp, n_valid, blk):
    rows, d = xs.shape
    ff = w_gate.shape[-1]
    n_blocks = rows // blk
    params = pltpu.CompilerParams(dimension_semantics=("arbitrary",), vmem_limit_bytes=VMEM_LIMIT_BYTES)
    h = pl.pallas_call(
        _ffn_up_kernel,
        grid_spec=pltpu.PrefetchScalarGridSpec(
            num_scalar_prefetch=2, grid=(n_blocks,),
            in_specs=[pl.BlockSpec((blk, d), lambda i, be, nv: (i, 0)),
                      pl.BlockSpec((None, d, ff), lambda i, be, nv: (be[i], 0, 0)),
                      pl.BlockSpec((None, d, ff), lambda i, be, nv: (be[i], 0, 0))],
            out_specs=pl.BlockSpec((blk, ff), lambda i, be, nv: (i, 0))),
        out_shape=jax.ShapeDtypeStruct((rows, ff), jnp.bfloat16),
        compiler_params=params, name="ffn_up",
    )(blk_exp, n_valid, xs, w_gate, w_up)
    return pl.pallas_call(
        _ffn_down_kernel,
        grid_spec=pltpu.PrefetchScalarGridSpec(
            num_scalar_prefetch=2, grid=(n_blocks,),
            in_specs=[pl.BlockSpec((blk, ff), lambda i, be, nv: (i, 0)),
                      pl.BlockSpec((None, ff, d), lambda i, be, nv: (be[i], 0, 0)),
                      pl.BlockSpec((blk, 1), lambda i, be, nv: (i, 0))],
            out_specs=pl.BlockSpec((blk, d), lambda i, be, nv: (i, 0))),
        out_shape=jax.ShapeDtypeStruct((rows, d), jnp.float32),
        compiler_params=params, name="ffn_down",
    )(blk_exp, n_valid, h, w_down, row_w)


def moe_ffn_pallas(h, lp, blk):
    B, T, D = h.shape
    x = h.reshape(B * T, D)
    n = x.shape[0]
    f32 = jnp.float32
    scores = jax.nn.sigmoid(pallas_matmul(x, lp['router_w']).astype(f32))
    biased = scores + lp['router_bias'].astype(f32)
    grp = biased.reshape(n, N_GROUPS, N_EXPERTS // N_GROUPS)
    grp_score = lax.top_k(grp, 2)[0].sum(-1)
    _, gidx = lax.top_k(grp_score, TOPK_GROUPS)
    gmask = jax.nn.one_hot(gidx, N_GROUPS, dtype=f32).sum(1)
    emask = jnp.repeat(gmask, N_EXPERTS // N_GROUPS, axis=1) > 0
    _, eidx = lax.top_k(jnp.where(emask, biased, -jnp.inf), TOP_K)
    w = jnp.take_along_axis(scores, eidx, axis=1)
    w = w / jnp.sum(w, axis=-1, keepdims=True) * ROUTED_SCALE

    nk = n * TOP_K
    flat_e = eidx.reshape(nk)
    order = jnp.argsort(flat_e)
    se = flat_e[order]
    counts = jnp.bincount(flat_e, length=N_EXPERTS)
    padded = (counts + blk - 1) // blk * blk
    pad_end = jnp.cumsum(padded)
    dest = (pad_end - padded)[se] + jnp.arange(nk) - (jnp.cumsum(counts) - counts)[se]
    n_blocks = -(-(nk + N_EXPERTS * (blk - 1)) // blk)
    rows = n_blocks * blk
    row_tok = jnp.zeros((rows,), jnp.int32).at[dest].set((order // TOP_K).astype(jnp.int32))
    row_w = jnp.zeros((rows,), f32).at[dest].set(w.reshape(nk)[order])
    blk_exp = jnp.minimum(jnp.searchsorted(pad_end, jnp.arange(n_blocks) * blk, side='right'),
                          N_EXPERTS - 1).astype(jnp.int32)
    n_valid = (pad_end[-1] // blk).astype(jnp.int32).reshape(1)
    pos = jnp.zeros((nk,), jnp.int32).at[order].set(dest.astype(jnp.int32)).reshape(n, TOP_K)

    xb = x.astype(jnp.bfloat16)
    ys = expert_ffn(xb[row_tok], lp['exp_w_gate'], lp['exp_w_up'], lp['exp_w_down'], row_w[:, None], blk_exp,
                    n_valid, blk)
    routed = ys[pos].sum(axis=1)
    sblk = min(256, n)
    one = jnp.ones((n, 1), f32)
    shared = expert_ffn(xb, lp['sh_w_gate'][None], lp['sh_w_up'][None], lp['sh_w_down'][None], one,
                        jnp.zeros((n // sblk,), jnp.int32), jnp.full((1,), n // sblk, jnp.int32), sblk)
    return (routed + shared).astype(h.dtype).reshape(B, T, D)


def pad_rows(a, length):
    return jnp.pad(a, [(0, 0), (0, length - a.shape[1])] + [(0, 0)] * (a.ndim - 2))


def rwkv7_mix(pr, shift_prev, wkv0, lp):
    B, T, _ = pr.shape
    f32 = jnp.float32
    prev = jnp.concatenate([shift_prev[:, None, :].astype(pr.dtype), pr[:, :-1]], axis=1)
    m = pr + (prev - pr) * lp['rwkv_mu']
    r, k, v, xw, xa, xg = jnp.split(m, RWKV_SPLITS, axis=-1)
    w_log = -jax.nn.softplus(-(lp['rwkv_w0'] + jnp.tanh(xw) @ lp['rwkv_w_up']).astype(f32)) - 0.5
    decay = jnp.exp(-jnp.exp(w_log))
    a = jax.nn.sigmoid((lp['rwkv_a0'] + xa @ lp['rwkv_a_up']).astype(f32))
    g = jax.nn.sigmoid(xg) @ lp['rwkv_g_up']

    def heads(t):
        return t.astype(f32).reshape(B, T, RWKV_HEADS, RWKV_HEAD_DIM)

    r, k, v, decay, a = heads(r), heads(k), heads(v), heads(decay), heads(a)
    kk = k * lp['rwkv_k_k'].astype(f32).reshape(RWKV_HEADS, RWKV_HEAD_DIM)
    kk = kk / jnp.maximum(jnp.linalg.norm(kk, axis=-1, keepdims=True), 1e-12)
    k = k * (1.0 + (a - 1.0) * lp['rwkv_k_a'].astype(f32).reshape(RWKV_HEADS, RWKV_HEAD_DIM))

    def step(S, inp):
        r_t, w_t, k_t, v_t, a_t, b_t = inp
        sa = jnp.einsum('bhvk,bhk->bhv', S, a_t)
        S = S * w_t[:, :, None, :] + sa[..., None] * b_t[:, :, None, :] + v_t[..., None] * k_t[:, :, None, :]
        return S, jnp.einsum('bhvk,bhk->bhv', S, r_t)

    xs = tuple(jnp.moveaxis(t, 1, 0) for t in (r, decay, k, v, -kk, kk * a))
    s_final, ys = lax.scan(step, wkv0.astype(f32), xs)
    y = jnp.moveaxis(ys, 0, 1)
    mu = jnp.mean(y, axis=-1, keepdims=True)
    var = jnp.mean(jnp.square(y - mu), axis=-1, keepdims=True)
    y = ((y - mu) * lax.rsqrt(var + LNX_EPS)).reshape(B, T, RWKV_WIDTH)
    y = y * lp['rwkv_lnx_w'].astype(f32) + lp['rwkv_lnx_b'].astype(f32)
    bonus = (jnp.sum(r * k * lp['rwkv_r_k'].astype(f32), axis=-1, keepdims=True) * v).reshape(B, T, RWKV_WIDTH)
    out = ((y + bonus) * g.astype(f32)).astype(pr.dtype)
    return out, pr[:, -1], s_final.astype(wkv0.dtype)


def compress_blocks(rows, pe, w):
    B, L = rows.shape[:2]
    blk = rows.reshape(B, L // CMP_BLOCK, CMP_BLOCK, 2, NSA_KV_HEADS, NSA_HEAD_DIM)
    blk = blk + jnp.swapaxes(pe, 0, 1)[None, None, :, :, None, :]
    return jnp.einsum('bncjkd,jcde->bnjke', blk, w)


def nsa_attend(q, q_pos, gates, kc, vc, c_end, n_sel_blocks, gather_sel, kw, vw, w_pos, rel_bias):
    B, Tq = q.shape[:2]
    f32 = jnp.float32
    scale = NSA_HEAD_DIM ** -0.5
    qg = q.reshape(B, Tq, NSA_KV_HEADS, NSA_GROUP, NSA_HEAD_DIM)
    bias_tab = rel_bias.astype(f32).reshape(NUM_BUCKETS, NSA_KV_HEADS, NSA_GROUP)

    dc = q_pos[:, None] - c_end[None, :]
    bias_c = bias_tab[rel_bucket(dc)].transpose(0, 2, 3, 1)
    lc = jnp.einsum('btkgd,bnkd->btkgn', qg, kc).astype(f32) * scale + bias_c
    pc = masked_softmax(lc, (dc >= 0)[:, None, None, :])
    oc = jnp.einsum('btkgn,bnkd->btkgd', pc.astype(vc.dtype), vc)

    imp = pc.sum(3).reshape(B, Tq, NSA_KV_HEADS, n_sel_blocks, SEL_BLOCK // CMP_BLOCK).sum(-1)
    blk = jnp.arange(n_sel_blocks)
    cur = (q_pos // SEL_BLOCK)[:, None]
    force = (blk == 0) | (blk == cur) | (blk == cur - 1)
    score = jnp.where(force[:, None, :], FORCE_SCORE, imp)
    score = jnp.where((blk <= cur)[:, None, :], score, -jnp.inf)
    _, idx = lax.top_k(jnp.moveaxis(score, 2, 1), min(N_SEL, n_sel_blocks))

    sel = gather_sel(idx)
    ks, vs = sel[..., 0, :], sel[..., 1, :]
    spos = idx[..., None] * SEL_BLOCK + jnp.arange(SEL_BLOCK)
    ds = q_pos[None, None, :, None, None] - spos
    hi = jnp.arange(NSA_KV_HEADS)[None, :, None, None, None]
    bias_s = jnp.moveaxis(bias_tab[rel_bucket(ds), hi], -1, 3)
    ls = jnp.einsum('btkgd,bktjsd->bktgjs', qg, ks).astype(f32) * scale + bias_s
    ms = (ds >= 0)[:, :, :, None]
    ps = masked_softmax(ls.reshape(ls.shape[:4] + (-1,)), ms.reshape(ms.shape[:4] + (-1,))).reshape(ls.shape)
    osel = jnp.einsum('bktgjs,bktjsd->btkgd', ps.astype(vs.dtype), vs)

    dw = q_pos[:, None] - w_pos[None, :]
    mw = (dw >= 0) & (dw <= WINDOW) & (w_pos >= 0)[None, :]
    bias_w = bias_tab[rel_bucket(dw)].transpose(0, 2, 3, 1)
    lw = jnp.einsum('btkgd,blkd->btkgl', qg, kw).astype(f32) * scale + bias_w
    pw = masked_softmax(lw, mw[:, None, None, :])
    ow = jnp.einsum('btkgl,blkd->btkgd', pw.astype(vw.dtype), vw)

    gt = gates.reshape(B, Tq, NSA_KV_HEADS, NSA_GROUP, 3)
    o = gt[..., 0:1] * oc + gt[..., 1:2] * osel + gt[..., 2:3] * ow
    return o.reshape(B, Tq, NSA_WIDTH)


def split_proj(h, lp):
    B, T, _ = h.shape
    proj = mm3(h, lp['w_in'])
    pr = proj[..., :RWKV_PROJ]
    pn = proj[..., RWKV_PROJ:]
    q = pn[..., :NSA_WIDTH].reshape(B, T, NSA_HEADS, NSA_HEAD_DIM)
    kv = pn[..., NSA_WIDTH:NSA_WIDTH + 3 * NSA_KV_WIDTH].reshape(B, T, 3, 2, NSA_KV_HEADS, NSA_HEAD_DIM)
    gates = jax.nn.sigmoid(pn[..., NSA_WIDTH + 3 * NSA_KV_WIDTH:].astype(jnp.float32))
    gates = gates.reshape(B, T, NSA_HEADS, 3).astype(h.dtype)
    return pr, q, kv[:, :, 0], kv[:, :, 1], kv[:, :, 2], gates


def merge_groups(o_rwkv, o_nsa, lp):
    o = jnp.concatenate([o_rwkv, rms_norm(o_nsa, lp['nsa_out_g'])], axis=-1)
    return mm3(o, lp['w_out'])


def mixer_prompt(h, lp, rel_bias):
    B, T, d = h.shape
    proj_p = pallas_matmul(h.reshape(B * T, d), lp['w_in_p'])
    o_r, wkv = rwkv_mix(proj_p, jnp.zeros((B, RWKV_PROJ), h.dtype),
                        jnp.zeros((B, RWKV_HEADS, RWKV_HEAD_DIM, RWKV_HEAD_DIM), jnp.float32), lp, B, T, T, 64)
    o_r = o_r.reshape(B, T, RWKV_WIDTH)
    shift_last = unpad_rwkv(proj_p.reshape(B, T, P_TOTAL)[:, -1])
    kvc = compress_prompt(proj_p, lp['cmp_pe'], lp['cmp_w'], B, T)
    o_n = nsa_prompt(proj_p, kvc, rel_bias, B, T).reshape(B, T, NSA_WIDTH)
    kv = proj_p[:, P_KVC:P_GATES].reshape(B, T, 3, 2, NSA_KV_HEADS, NSA_HEAD_DIM)
    kv_c, kv_s, kv_w = kv[:, :, 0], kv[:, :, 1], kv[:, :, 2]
    win_keep = min(WINDOW, T)
    return merge_groups(o_r, o_n, lp), (kv_c, kv_s, kv_w[:, T - win_keep:], wkv, shift_last)


def mixer_prompt_gather(h, lp, rel_bias):
    B, T, _ = h.shape
    pr, q, kv_c, kv_s, kv_w, gates = split_proj(h, lp)
    o_r, shift_last, wkv = rwkv7_mix(pr, jnp.zeros((B, RWKV_PROJ), h.dtype),
                                     jnp.zeros((B, RWKV_HEADS, RWKV_HEAD_DIM, RWKV_HEAD_DIM), jnp.float32), lp)
    lp_len = -(-T // SEL_BLOCK) * SEL_BLOCK
    comp = compress_blocks(pad_rows(kv_c, lp_len), lp['cmp_pe'], lp['cmp_w'])
    kc, vc = comp[:, :, 0], comp[:, :, 1]
    c_end = jnp.arange(lp_len // CMP_BLOCK) * CMP_BLOCK + (CMP_BLOCK - 1)
    n_sb = lp_len // SEL_BLOCK
    sel_blocks = pad_rows(kv_s, lp_len).reshape(B, n_sb, SEL_BLOCK, 2, NSA_KV_HEADS, NSA_HEAD_DIM)
    bi = jnp.arange(B)[:, None, None, None]
    hi = jnp.arange(NSA_KV_HEADS)[None, :, None, None]

    def gather_sel(idx):
        return sel_blocks[bi, idx, :, :, hi, :]

    kw_pad = jnp.pad(kv_w, ((0, 0), (WINDOW, 0), (0, 0), (0, 0), (0, 0)))

    def chunk(i):
        t0 = i * Q_CHUNK
        qc = lax.dynamic_slice_in_dim(q, t0, Q_CHUNK, axis=1)
        gc = lax.dynamic_slice_in_dim(gates, t0, Q_CHUNK, axis=1)
        wc = lax.dynamic_slice_in_dim(kw_pad, t0, WINDOW + Q_CHUNK, axis=1)
        q_pos = t0 + jnp.arange(Q_CHUNK)
        w_pos = t0 - WINDOW + jnp.arange(WINDOW + Q_CHUNK)
        return nsa_attend(qc, q_pos, gc, kc, vc, c_end, n_sb, gather_sel,
                          wc[:, :, 0], wc[:, :, 1], w_pos, rel_bias)

    o_n = lax.map(chunk, jnp.arange(T // Q_CHUNK))
    o_n = jnp.moveaxis(o_n, 0, 1).reshape(B, T, NSA_WIDTH)
    win_keep = min(WINDOW, T)
    return merge_groups(o_r, o_n, lp), (kv_c, kv_s, kv_w[:, T - win_keep:], wkv, shift_last)


def mixer_sample(h, lp, rel_bias, pool_cmp, pool_sel, win_buf, wkv0, shift0, page_table):
    B, T, _ = h.shape
    n_pages = page_table.shape[1]
    page = pool_cmp.shape[1]
    past = n_pages * page
    d = h.shape[-1]
    proj_p = pallas_matmul(h.reshape(B * T, d), lp['w_in_p']).reshape(B, T, P_TOTAL)
    q = proj_p[..., P_Q:P_Q + NSA_WIDTH].reshape(B, T, NSA_HEADS, NSA_HEAD_DIM)
    kv = proj_p[..., P_KVC:P_GATES].reshape(B, T, 3, 2, NSA_KV_HEADS, NSA_HEAD_DIM)
    kv_c, kv_s, kv_w = kv[:, :, 0], kv[:, :, 1], kv[:, :, 2]
    graw = jnp.concatenate([proj_p[..., P_GATES + kh * LANE:P_GATES + kh * LANE + N_GATE_COLS]
                            for kh in range(NSA_KV_HEADS)], axis=-1)
    gates = jax.nn.sigmoid(graw).reshape(B, T, NSA_HEADS, 3)
    t_pad = 8
    proj_pad = jnp.pad(proj_p, ((0, 0), (0, t_pad - T), (0, 0))).reshape(B * t_pad, P_TOTAL)
    o_r, wkv = rwkv_mix(proj_pad, shift0, wkv0, lp, B, t_pad, T, t_pad)
    o_r = o_r.reshape(B, t_pad, RWKV_WIDTH)[:, :T]
    shift_last = unpad_rwkv(proj_p[:, -1])
    lp_len = -(-(past + T) // SEL_BLOCK) * SEL_BLOCK
    q_pos = past + jnp.arange(T)

    past_c = pool_cmp[page_table].reshape(B, past, 2, NSA_KV_HEADS, NSA_HEAD_DIM)
    rows_c = pad_rows(jnp.concatenate([past_c, kv_c], axis=1), lp_len)
    comp = compress_blocks(rows_c, lp['cmp_pe'], lp['cmp_w'])
    kc, vc = comp[:, :, 0], comp[:, :, 1]
    c_end = jnp.arange(lp_len // CMP_BLOCK) * CMP_BLOCK + (CMP_BLOCK - 1)

    sub = page // SEL_BLOCK
    ns_past = past // SEL_BLOCK
    new_len = lp_len - past
    new_blocks = pad_rows(kv_s, new_len).reshape(B, new_len // SEL_BLOCK, SEL_BLOCK, 2, NSA_KV_HEADS, NSA_HEAD_DIM)
    pool_blk = pool_sel.reshape(pool_sel.shape[0], sub, SEL_BLOCK, 2, NSA_KV_HEADS, NSA_HEAD_DIM)
    bi = jnp.arange(B)[:, None, None, None]
    hi = jnp.arange(NSA_KV_HEADS)[None, :, None, None]

    def gather_sel(idx):
        ip = jnp.minimum(idx, ns_past - 1)
        phys = page_table[bi, ip // sub]
        from_pool = pool_blk[phys, ip % sub, :, :, hi, :]
        inew = jnp.clip(idx - ns_past, 0, new_blocks.shape[1] - 1)
        from_new = new_blocks[bi, inew, :, :, hi, :]
        return jnp.where((idx < ns_past)[..., None, None, None], from_pool, from_new)

    wrows = jnp.concatenate([win_buf, kv_w], axis=1)
    wb = win_buf.shape[1]
    w_pos = past - wb + jnp.arange(wb + T)
    o_n = nsa_attend(q, q_pos, gates, kc, vc, c_end, lp_len // SEL_BLOCK, gather_sel,
                     wrows[:, :, 0], wrows[:, :, 1], w_pos, rel_bias)
    return merge_groups(o_r, o_n, lp), (kv_c, kv_s, wrows[:, T:], wkv, shift_last)


def moe_ffn(h, lp):
    B, T, D = h.shape
    x = h.reshape(B * T, D)
    n = x.shape[0]
    f32 = jnp.float32
    scores = jax.nn.sigmoid((x @ lp['router_w']).astype(f32))
    biased = scores + lp['router_bias'].astype(f32)
    grp = biased.reshape(n, N_GROUPS, N_EXPERTS // N_GROUPS)
    grp_score = lax.top_k(grp, 2)[0].sum(-1)
    _, gidx = lax.top_k(grp_score, TOPK_GROUPS)
    gmask = jax.nn.one_hot(gidx, N_GROUPS, dtype=f32).sum(1)
    emask = jnp.repeat(gmask, N_EXPERTS // N_GROUPS, axis=1) > 0
    _, eidx = lax.top_k(jnp.where(emask, biased, -jnp.inf), TOP_K)
    w = jnp.take_along_axis(scores, eidx, axis=1)
    w = w / jnp.sum(w, axis=-1, keepdims=True) * ROUTED_SCALE

    nk = n * TOP_K
    flat_e = eidx.reshape(nk)
    order = jnp.argsort(flat_e)
    se = flat_e[order]
    counts = jnp.bincount(flat_e, length=N_EXPERTS)
    padded = (counts + MOE_BLOCK - 1) // MOE_BLOCK * MOE_BLOCK
    pad_end = jnp.cumsum(padded)
    dest = (pad_end - padded)[se] + jnp.arange(nk) - (jnp.cumsum(counts) - counts)[se]
    n_blocks = -(-(nk + N_EXPERTS * (MOE_BLOCK - 1)) // MOE_BLOCK)
    rows = n_blocks * MOE_BLOCK
    row_tok = jnp.zeros((rows,), jnp.int32).at[dest].set((order // TOP_K).astype(jnp.int32))
    row_w = jnp.zeros((rows,), f32).at[dest].set(w.reshape(nk)[order])
    blk_exp = jnp.minimum(jnp.searchsorted(pad_end, jnp.arange(n_blocks) * MOE_BLOCK, side='right'),
                          N_EXPERTS - 1)

    def expert_block(args):
        tok, wt, e = args
        xb = x[tok]
        hb = jax.nn.silu(xb @ lp['exp_w_gate'][e]) * (xb @ lp['exp_w_up'][e])
        return (hb @ lp['exp_w_down'][e]).astype(f32) * wt[:, None]

    out = lax.map(expert_block, (row_tok.reshape(n_blocks, MOE_BLOCK), row_w.reshape(n_blocks, MOE_BLOCK), blk_exp))
    routed = jnp.zeros((n, D), f32).at[row_tok].add(out.reshape(rows, D))
    shared = (jax.nn.silu(x @ lp['sh_w_gate']) * (x @ lp['sh_w_up'])) @ lp['sh_w_down']
    return (routed + shared.astype(f32)).astype(h.dtype).reshape(B, T, D)


def trunk_layer(x, c, lp, mixer):
    B, _, D = x.shape
    mod = (pallas_matmul(jax.nn.silu(c), lp['w_ada']) + lp['b_ada']).reshape(B, 6, 1, D)
    sh1, sc1, gt1, sh2, sc2, gt2 = (mod[:, i] for i in range(6))
    g = lp['norm_g']
    h = rms_norm(x, g[0]) * (1 + sc1) + sh1
    o, st = mixer(h)
    x = x + gt1 * rms_norm(o, g[1])
    h = rms_norm(x, g[2]) * (1 + sc2) + sh2
    x = x + gt2 * rms_norm(moe_ffn_pallas(h, lp, 256 if x.shape[0] * x.shape[1] >= 4096 else 32), g[3])
    return x, st


def kernel(x_prompt, x_sample, c_prompt, c_sample, cache_cmp, cache_sel, state_win, state_wkv,
           state_shift, page_table, rel_bias, w_ada, b_ada, norm_g, w_in, w_out, rwkv_mu, rwkv_w0,
           rwkv_w_up, rwkv_a0, rwkv_a_up, rwkv_g_up, rwkv_k_k, rwkv_k_a, rwkv_r_k, rwkv_lnx_w,
           rwkv_lnx_b, cmp_pe, cmp_w, nsa_out_g, router_w, router_bias, exp_w_gate, exp_w_up,
           exp_w_down, sh_w_gate, sh_w_up, sh_w_down):
    l = 0
    lp = dict(w_ada=w_ada[l], b_ada=b_ada[l], norm_g=norm_g[l], w_in=w_in[l], w_out=w_out[l],
              rwkv_mu=rwkv_mu[l], rwkv_w0=rwkv_w0[l], rwkv_w_up=rwkv_w_up[l], rwkv_a0=rwkv_a0[l],
              rwkv_a_up=rwkv_a_up[l], rwkv_g_up=rwkv_g_up[l], rwkv_k_k=rwkv_k_k[l],
              rwkv_k_a=rwkv_k_a[l], rwkv_r_k=rwkv_r_k[l], rwkv_lnx_w=rwkv_lnx_w[l],
              rwkv_lnx_b=rwkv_lnx_b[l], cmp_pe=cmp_pe[l], cmp_w=cmp_w[l], nsa_out_g=nsa_out_g[l],
              router_w=router_w[l], router_bias=router_bias[l], exp_w_gate=exp_w_gate[l],
              exp_w_up=exp_w_up[l], exp_w_down=exp_w_down[l], sh_w_gate=sh_w_gate[l],
              sh_w_up=sh_w_up[l], sh_w_down=sh_w_down[l])
    lp['w_in_p'] = pad_w_in(w_in[l])
    yp, st_p = trunk_layer(x_prompt, c_prompt, lp, lambda h: mixer_prompt(h, lp, rel_bias))
    ys, st_s = trunk_layer(x_sample, c_sample, lp, lambda h: mixer_sample(
        h, lp, rel_bias, cache_cmp[l], cache_sel[l], state_win[l], state_wkv[l], state_shift[l], page_table))
    p_cmp, p_sel, p_win, p_wkv, p_shift = [a[None] for a in st_p]
    s_cmp, s_sel, s_win, s_wkv, s_shift = [a[None] for a in st_s]
    return (yp, ys, p_cmp, p_sel, p_win, p_wkv, p_shift, s_cmp, s_sel, s_win, s_wkv, s_shift)
```

```python
import functools
import math

import jax
import jax.numpy as jnp
from jax import lax
from jax.experimental import pallas as pl
from jax.experimental.pallas import tpu as pltpu

D_MODEL = 4096
RWKV_WIDTH = D_MODEL // 2
NSA_WIDTH = D_MODEL - RWKV_WIDTH
RWKV_HEAD_DIM = 64
RWKV_HEADS = RWKV_WIDTH // RWKV_HEAD_DIM
DECAY_LORA = 96
AAA_LORA = 96
GATE_LORA = 256
RWKV_PROJ = 3 * RWKV_WIDTH + DECAY_LORA + AAA_LORA + GATE_LORA
RWKV_SPLITS = (RWKV_WIDTH, 2 * RWKV_WIDTH, 3 * RWKV_WIDTH, 3 * RWKV_WIDTH + DECAY_LORA,
               3 * RWKV_WIDTH + DECAY_LORA + AAA_LORA)
LNX_EPS = 64e-5
NSA_HEAD_DIM = 128
NSA_HEADS = NSA_WIDTH // NSA_HEAD_DIM
NSA_KV_HEADS = 4
NSA_GROUP = NSA_HEADS // NSA_KV_HEADS
CMP_BLOCK = 32
SEL_BLOCK = 64
N_SEL = 16
WINDOW = 512
Q_CHUNK = 32
FORCE_SCORE = 1e4
NSA_KV_WIDTH = 2 * NSA_KV_HEADS * NSA_HEAD_DIM
NUM_BUCKETS = 32
MAX_DISTANCE = 128
N_EXPERTS = 64
N_GROUPS = 8
TOPK_GROUPS = 4
TOP_K = 8
ROUTED_SCALE = 2.5
MOE_BLOCK = 128
RMS_EPS = 1e-6

VMEM_LIMIT_BYTES = 56 * 1024 * 1024


def _matmul_kernel(x_ref, w_ref, o_ref):
    o_ref[...] = jnp.dot(x_ref[...].astype(jnp.bfloat16), w_ref[...].astype(jnp.bfloat16),
                         preferred_element_type=jnp.float32)


def pallas_matmul(x, w, tm=512, tn=512):
    m, k = x.shape
    n = w.shape[1]
    mp = -(-m // 8) * 8
    tm = min(tm, mp)
    mp = -(-mp // tm) * tm
    np_ = -(-n // tn) * tn
    if mp != m:
        x = jnp.pad(x, ((0, mp - m), (0, 0)))
    if np_ != n:
        w = jnp.pad(w, ((0, 0), (0, np_ - n)))
    out = pl.pallas_call(
        _matmul_kernel,
        grid=(np_ // tn, mp // tm),
        in_specs=[pl.BlockSpec((tm, k), lambda j, i: (i, 0)),
                  pl.BlockSpec((k, tn), lambda j, i: (0, j))],
        out_specs=pl.BlockSpec((tm, tn), lambda j, i: (i, j)),
        out_shape=jax.ShapeDtypeStruct((mp, np_), jnp.float32),
        compiler_params=pltpu.CompilerParams(
            dimension_semantics=("arbitrary", "arbitrary"), vmem_limit_bytes=VMEM_LIMIT_BYTES),
        name="matmul",
    )(x, w)
    return out[:m, :n]


def mm3(h, w):
    b, t, d = h.shape
    return pallas_matmul(h.reshape(b * t, d), w).reshape(b, t, w.shape[1])


def rms_norm(x, g):
    xf = x.astype(jnp.float32)
    y = xf * lax.rsqrt(jnp.mean(xf * xf, axis=-1, keepdims=True) + RMS_EPS)
    return (y * g.astype(jnp.float32)).astype(x.dtype)


def rel_bucket(dist):
    max_exact = NUM_BUCKETS // 2
    n = jnp.maximum(dist, 0)
    nf = jnp.maximum(n, max_exact).astype(jnp.float32)
    large = max_exact + (jnp.log(nf / max_exact) / math.log(MAX_DISTANCE / max_exact)
                         * (NUM_BUCKETS - max_exact)).astype(jnp.int32)
    return jnp.where(n < max_exact, n, jnp.minimum(large, NUM_BUCKETS - 1))


def masked_softmax(logits, mask):
    logits = jnp.where(mask, logits.astype(jnp.float32), -jnp.inf)
    m = jnp.max(logits, axis=-1, keepdims=True)
    p = jnp.exp(logits - jnp.where(jnp.isfinite(m), m, 0.0))
    s = jnp.sum(p, axis=-1, keepdims=True)
    return p / jnp.where(s > 0, s, 1.0)


LANE = 128
P_XW = 3 * RWKV_WIDTH
P_XA = P_XW + LANE
P_XG = P_XA + LANE
P_Q = P_XG + GATE_LORA
P_KVC = P_Q + NSA_WIDTH
P_KVS = P_KVC + NSA_KV_WIDTH
P_KVW = P_KVS + NSA_KV_WIDTH
P_GATES = P_KVW + NSA_KV_WIDTH
P_TOTAL = P_GATES + NSA_KV_HEADS * LANE
N_GATE_COLS = 3 * NSA_GROUP


def pad_w_in(w_in):
    d = w_in.shape[0]
    z = lambda n: jnp.zeros((d, n), w_in.dtype)
    o_xw, o_xa, o_xg = RWKV_SPLITS[2], RWKV_SPLITS[3], RWKV_SPLITS[4]
    o_g = RWKV_PROJ + NSA_WIDTH + 3 * NSA_KV_WIDTH
    parts = [w_in[:, :o_xw], w_in[:, o_xw:o_xa], z(LANE - DECAY_LORA), w_in[:, o_xa:o_xg], z(LANE - AAA_LORA),
             w_in[:, o_xg:o_g]]
    for kh in range(NSA_KV_HEADS):
        parts += [w_in[:, o_g + kh * N_GATE_COLS:o_g + (kh + 1) * N_GATE_COLS], z(LANE - N_GATE_COLS)]
    return jnp.concatenate(parts, axis=1).astype(jnp.bfloat16)


def unpad_rwkv(proj_p):
    return jnp.concatenate([proj_p[..., :P_XW], proj_p[..., P_XW:P_XW + DECAY_LORA],
                            proj_p[..., P_XA:P_XA + AAA_LORA], proj_p[..., P_XG:P_XG + GATE_LORA]], axis=-1)


def bucket_table(n):
    import numpy as np
    d = np.arange(n)
    max_exact = NUM_BUCKETS // 2
    nf = np.maximum(d, max_exact).astype(np.float64)
    large = max_exact + (np.log(nf / max_exact) / math.log(MAX_DISTANCE / max_exact)
                         * (NUM_BUCKETS - max_exact)).astype(np.int64)
    return np.where(d < max_exact, d, np.minimum(large, NUM_BUCKETS - 1)).astype(np.int32)


def _compress_kernel(x_ref, w_ref, pe_ref, o_ref, *, nblk):
    half = nblk // 2
    acc = jnp.zeros((nblk, NSA_HEAD_DIM), jnp.float32)
    for c in range(CMP_BLOCK):
        xe = x_ref[pl.ds(c, half, stride=2 * CMP_BLOCK), :]
        xo = x_ref[pl.ds(CMP_BLOCK + c, half, stride=2 * CMP_BLOCK), :]
        lhs = jnp.concatenate([xe, xo], axis=0) + pe_ref[c:c + 1, :]
        acc = acc + jnp.dot(lhs.astype(jnp.bfloat16), w_ref[c].astype(jnp.bfloat16),
                            preferred_element_type=jnp.float32)
    o_ref[...] = acc


def compress_prompt(proj_p, cmp_pe, cmp_w, B, T, col0=P_KVC):
    nblk = T // CMP_BLOCK
    hd, KVH = NSA_HEAD_DIM, NSA_KV_HEADS
    return pl.pallas_call(
        functools.partial(_compress_kernel, nblk=nblk),
        grid=(B, 2, KVH),
        in_specs=[pl.BlockSpec((T, hd), lambda b, j, kh: (b, col0 // hd + j * KVH + kh)),
                  pl.BlockSpec((None, CMP_BLOCK, hd, hd), lambda b, j, kh: (j, 0, 0, 0)),
                  pl.BlockSpec((None, CMP_BLOCK, hd), lambda b, j, kh: (j, 0, 0))],
        out_specs=pl.BlockSpec((None, None, None, nblk, hd), lambda b, j, kh: (b, j, kh, 0, 0)),
        out_shape=jax.ShapeDtypeStruct((B, 2, KVH, nblk, hd), jnp.float32),
        compiler_params=pltpu.CompilerParams(dimension_semantics=("arbitrary",) * 3, vmem_limit_bytes=VMEM_LIMIT_BYTES),
        name="compress_prompt",
    )(proj_p, cmp_w, cmp_pe)


NSA_TQ = 128
NEG_BIG = -1e30


def _nsa_prompt_kernel(q_ref, gate_ref, ks_ref, vs_ref, kw_ref, vw_ref, kc_ref, vc_ref, bt_ref, bc_ref, es_ref,
                       o_ref, *, n_sb):
    f32, bf16 = jnp.float32, jnp.bfloat16
    tq, G, hd = NSA_TQ, NSA_GROUP, NSA_HEAD_DIM
    qi = pl.program_id(2)
    t0 = qi * tq
    scale = NSA_HEAD_DIM ** -0.5
    q = jnp.concatenate([q_ref[:, g * hd:(g + 1) * hd] for g in range(G)], axis=0).astype(bf16)
    rows = G * tq
    t_row = t0 + (lax.broadcasted_iota(jnp.int32, (rows, 1), 0) & (tq - 1))

    def qk(k):
        return lax.dot_general(q, k.astype(bf16), (((1,), (1,)), ((), ())), preferred_element_type=f32)

    ncmp = kc_ref.shape[0]
    lane = lax.broadcasted_iota(jnp.int32, (1, ncmp), 1)
    blk_id = jnp.where(lane < ncmp // 2, 2 * lane, 2 * (lane - ncmp // 2) + 1)
    c_end = blk_id * CMP_BLOCK + (CMP_BLOCK - 1)
    vis = c_end <= t_row
    lc = jnp.where(vis, qk(kc_ref[...]) * scale + bc_ref[...], -jnp.inf)
    mc = jnp.max(lc, axis=-1, keepdims=True)
    pc = jnp.exp(lc - jnp.where(mc > -jnp.inf, mc, 0.0))
    sc = jnp.sum(pc, axis=-1, keepdims=True)
    pc = pc / jnp.where(sc > 0, sc, 1.0)
    oc = jnp.dot(pc.astype(bf16), vc_ref[...].astype(bf16), preferred_element_type=f32)

    pg = pc[0:tq]
    for g in range(1, G):
        pg = pg + pc[g * tq:(g + 1) * tq]
    imp = pg[:, :n_sb] + pg[:, n_sb:]
    tpos = t0 + lax.broadcasted_iota(jnp.int32, (tq, 1), 0)
    cur = tpos // SEL_BLOCK
    bl = lax.broadcasted_iota(jnp.int32, (tq, n_sb), 1)
    force = (bl == 0) | (bl == cur) | (bl == cur - 1)
    score = jnp.where(force, FORCE_SCORE, imp)
    score = jnp.where(bl <= cur, score, -jnp.inf)
    rank = jnp.zeros((tq, n_sb), jnp.int32)
    for i in range(n_sb):
        si = score[:, i:i + 1]
        ahead = (si > score) | ((si == score) & (i < bl))
        rank = rank + ahead.astype(jnp.int32)
    selmask = (rank < N_SEL).astype(bf16)
    selmask = jnp.concatenate([selmask] * G, axis=0)

    col = lax.broadcasted_iota(jnp.int32, (1, tq), 1)

    def attend(c, carry, k_ref, v_ref, selected):
        m, l, acc = carry
        s0 = pl.multiple_of(c * tq, tq)
        s = qk(k_ref[pl.ds(s0, tq), :]) * scale + bt_ref[jnp.minimum(qi - c, 2)]
        dist = t_row - (s0 + col)
        if selected:
            ok = (jnp.dot(selmask, es_ref[c], preferred_element_type=f32) > 0.5) & (dist >= 0)
        else:
            ok = (dist >= 0) & (dist <= WINDOW)
        s = jnp.where(ok, s, NEG_BIG)
        m_new = jnp.maximum(m, jnp.max(s, axis=-1, keepdims=True))
        p = jnp.where(ok, jnp.exp(s - m_new), 0.0)
        alpha = jnp.exp(m - m_new)
        l = alpha * l + jnp.sum(p, axis=-1, keepdims=True)
        acc = alpha * acc + jnp.dot(p.astype(bf16), v_ref[pl.ds(s0, tq), :].astype(bf16), preferred_element_type=f32)
        return m_new, l, acc

    init = (jnp.full((rows, 1), NEG_BIG, f32), jnp.zeros((rows, 1), f32), jnp.zeros((rows, hd), f32))
    _, l_s, acc_s = lax.fori_loop(0, qi + 1, functools.partial(attend, k_ref=ks_ref, v_ref=vs_ref, selected=True), init)
    c_lo = jnp.maximum(qi - WINDOW // tq, 0)
    _, l_w, acc_w = lax.fori_loop(c_lo, qi + 1, functools.partial(attend, k_ref=kw_ref, v_ref=vw_ref, selected=False), init)
    osel = acc_s / l_s
    owin = acc_w / l_w

    gate = jax.nn.sigmoid(gate_ref[...])
    for g in range(G):
        r = slice(g * tq, (g + 1) * tq)
        o_ref[:, g * hd:(g + 1) * hd] = (gate[:, 3 * g:3 * g + 1] * oc[r] + gate[:, 3 * g + 1:3 * g + 2] * osel[r]
                                         + gate[:, 3 * g + 2:3 * g + 3] * owin[r])


def nsa_prompt(proj_p, kvc, rel_bias, B, T):
    import numpy as np
    tq, G, hd, KVH = NSA_TQ, NSA_GROUP, NSA_HEAD_DIM, NSA_KV_HEADS
    nq = T // tq
    n_sb = T // SEL_BLOCK
    ncmp = T // CMP_BLOCK
    assert T % tq == 0 and T % SEL_BLOCK == 0 and ncmp == 2 * n_sb
    bucket = bucket_table(max(T, 3 * tq))
    tab = rel_bias.astype(jnp.float32)
    ii, jj = np.meshgrid(np.arange(tq), np.arange(tq), indexing="ij")
    didx = np.stack([bucket[np.maximum(ii - jj, 0)], bucket[tq + ii - jj], np.full((tq, tq), NUM_BUCKETS - 1)])
    assert bucket[tq + 1] == NUM_BUCKETS - 1
    bt = tab[didx].reshape(3, tq, tq, KVH, G).transpose(3, 0, 4, 1, 2).reshape(KVH, 3, G * tq, tq)
    blk = np.concatenate([np.arange(0, ncmp, 2), np.arange(1, ncmp, 2)])
    dc = np.arange(T)[:, None] - (blk * CMP_BLOCK + CMP_BLOCK - 1)[None, :]
    bc = tab[bucket[np.maximum(dc, 0)]].reshape(nq, tq, ncmp, KVH, G).transpose(3, 0, 4, 1, 2).reshape(KVH, nq, G * tq, ncmp)
    es = (np.arange(n_sb)[None, :, None] == (np.arange(nq)[:, None, None] * (tq // SEL_BLOCK)
                                              + np.arange(tq)[None, None, :] // SEL_BLOCK))
    es = jnp.asarray(es, jnp.bfloat16)
    col = lambda off: off // hd
    kv_spec = lambda off: pl.BlockSpec((T, hd), lambda b, kh, i: (b, col(off) + kh))
    return pl.pallas_call(
        functools.partial(_nsa_prompt_kernel, n_sb=n_sb),
        grid=(B, KVH, nq),
        in_specs=[pl.BlockSpec((tq, G * hd), lambda b, kh, i: (b * nq + i, P_Q // (G * hd) + kh)),
                  pl.BlockSpec((tq, LANE), lambda b, kh, i: (b * nq + i, P_GATES // LANE + kh)),
                  kv_spec(P_KVS), kv_spec(P_KVS + KVH * hd), kv_spec(P_KVW), kv_spec(P_KVW + KVH * hd),
                  pl.BlockSpec((None, None, None, ncmp, hd), lambda b, kh, i: (b, 0, kh, 0, 0)),
                  pl.BlockSpec((None, None, None, ncmp, hd), lambda b, kh, i: (b, 1, kh, 0, 0)),
                  pl.BlockSpec((None, 3, G * tq, tq), lambda b, kh, i: (kh, 0, 0, 0)),
                  pl.BlockSpec((None, None, G * tq, ncmp), lambda b, kh, i: (kh, i, 0, 0)),
                  pl.BlockSpec((nq, n_sb, tq), lambda b, kh, i: (0, 0, 0))],
        out_specs=pl.BlockSpec((tq, G * hd), lambda b, kh, i: (b * nq + i, kh)),
        out_shape=jax.ShapeDtypeStruct((B * T, NSA_WIDTH), jnp.float32),
        compiler_params=pltpu.CompilerParams(dimension_semantics=("arbitrary",) * 3, vmem_limit_bytes=VMEM_LIMIT_BYTES),
        name="nsa_prompt",
    )(proj_p, proj_p, proj_p, proj_p, proj_p, proj_p, kvc, kvc, bt, bc, es)


SAMPLE_PG = 8
T_PAD = 8


def _compress_paged_kernel(pt_ref, *refs, nb):
    del pt_ref
    page_refs = refs[:SAMPLE_PG]
    w_ref, pe_ref, o_ref = refs[SAMPLE_PG:]
    jk, hd = 2 * NSA_KV_HEADS, NSA_HEAD_DIM
    rows = SAMPLE_PG * nb * jk
    acc = jnp.zeros((rows, 2 * hd), jnp.float32)
    for c in range(CMP_BLOCK):
        x = jnp.concatenate([pr[pl.ds(c, nb, stride=CMP_BLOCK), :, :] for pr in page_refs], axis=0)
        lhs = (x + pe_ref[c]).reshape(rows, hd).astype(jnp.bfloat16)
        acc = acc + jnp.dot(lhs, w_ref[c], preferred_element_type=jnp.float32)
    sub = lax.broadcasted_iota(jnp.int32, (rows, 1), 0) & (jk - 1)
    out = jnp.where(sub < NSA_KV_HEADS, acc[:, :hd], acc[:, hd:])
    o_ref[...] = out.reshape(SAMPLE_PG * nb, jk, hd)


def compress_paged(pool, page_table, cmp_pe, cmp_w):
    n_phys, page = pool.shape[:2]
    B, n_pages = page_table.shape
    jk, hd = 2 * NSA_KV_HEADS, NSA_HEAD_DIM
    nb = page // CMP_BLOCK
    assert n_pages % SAMPLE_PG == 0 and page % CMP_BLOCK == 0
    pool3 = pool.reshape(n_phys, page, jk, hd)
    w01 = jnp.concatenate([cmp_w[0], cmp_w[1]], axis=-1).astype(jnp.bfloat16)
    pe8 = jnp.repeat(jnp.swapaxes(cmp_pe, 0, 1), NSA_KV_HEADS, axis=1)
    page_spec = lambda i: pl.BlockSpec((None, page, jk, hd),
                                       lambda b, g, pt: (pt[b * n_pages + g * SAMPLE_PG + i], 0, 0, 0))
    return pl.pallas_call(
        functools.partial(_compress_paged_kernel, nb=nb),
        grid_spec=pltpu.PrefetchScalarGridSpec(
            num_scalar_prefetch=1, grid=(B, n_pages // SAMPLE_PG),
            in_specs=[page_spec(i) for i in range(SAMPLE_PG)]
            + [pl.BlockSpec((CMP_BLOCK, hd, 2 * hd), lambda b, g, pt: (0, 0, 0)),
               pl.BlockSpec((CMP_BLOCK, jk, hd), lambda b, g, pt: (0, 0, 0))],
            out_specs=pl.BlockSpec((None, SAMPLE_PG * nb, jk, hd), lambda b, g, pt: (b, g, 0, 0))),
        out_shape=jax.ShapeDtypeStruct((B, n_pages * nb, jk, hd), jnp.float32),
        compiler_params=pltpu.CompilerParams(dimension_semantics=("arbitrary",) * 2, vmem_limit_bytes=VMEM_LIMIT_BYTES),
        name="compress_paged",
    )(page_table.reshape(-1), *([pool3] * SAMPLE_PG), w01, pe8)


def _nsa_sample_kernel(pt_ref, *refs, past, n_steps, n_sb, half):
    del pt_ref
    f32, bf16 = jnp.float32, jnp.bfloat16
    KVH, G, hd = NSA_KV_HEADS, NSA_GROUP, NSA_HEAD_DIM
    q_refs = refs[0:KVH]
    gate_ref, ksn_ref, vsn_ref, kwn_ref, vwn_ref, kvc_ref, bc_ref = refs[KVH:KVH + 7]
    page_refs = refs[KVH + 7:KVH + 7 + SAMPLE_PG]
    es_ref, bs_ref, bn_ref, win_ref, bw_ref, o_ref, m_ref, l_ref, acc_ref, oc_ref, sel_ref = refs[KVH + 7 + SAMPLE_PG:]
    g_step = pl.program_id(1)
    R = G * T_PAD
    scale = NSA_HEAD_DIM ** -0.5
    nt = (((1,), (1,)), ((), ()))
    t_row = past + (lax.broadcasted_iota(jnp.int32, (R, 1), 0) & (T_PAD - 1))
    qs = [jnp.concatenate([q_refs[kh][:, g * hd:(g + 1) * hd] for g in range(G)], axis=0).astype(bf16) for kh in range(KVH)]

    @pl.when(g_step == 0)
    def _():
        m_ref[...] = jnp.full(m_ref.shape, NEG_BIG, f32)
        l_ref[...] = jnp.zeros(l_ref.shape, f32)
        acc_ref[...] = jnp.zeros(acc_ref.shape, f32)
        lane = lax.broadcasted_iota(jnp.int32, (1, 2 * half), 1)
        lo = lane < half
        blk_id = jnp.where(lo, 2 * lane, 2 * (lane - half) + 1)
        real = jnp.where(lo, lane, lane - half) < n_sb
        vis = real & (blk_id * CMP_BLOCK + (CMP_BLOCK - 1) <= t_row)
        tpos = past + lax.broadcasted_iota(jnp.int32, (T_PAD, 1), 0)
        cur = tpos // SEL_BLOCK
        bl = lax.broadcasted_iota(jnp.int32, (T_PAD, half), 1)
        force = (bl == 0) | (bl == cur) | (bl == cur - 1)
        for kh in range(KVH):
            lc = lax.dot_general(qs[kh], kvc_ref[kh].astype(bf16), nt, preferred_element_type=f32) * scale + bc_ref[kh]
            lc = jnp.where(vis, lc, -jnp.inf)
            mc = jnp.max(lc, axis=-1, keepdims=True)
            pc = jnp.exp(lc - jnp.where(mc > -jnp.inf, mc, 0.0))
            sc = jnp.sum(pc, axis=-1, keepdims=True)
            pc = pc / jnp.where(sc > 0, sc, 1.0)
            oc_ref[kh * R:(kh + 1) * R, :] = jnp.dot(pc.astype(bf16), kvc_ref[KVH + kh].astype(bf16),
                                                     preferred_element_type=f32)
            pg = pc[0:T_PAD]
            for g in range(1, G):
                pg = pg + pc[g * T_PAD:(g + 1) * T_PAD]
            imp = pg[:, :half] + pg[:, half:]
            score = jnp.where(force, FORCE_SCORE, imp)
            score = jnp.where((bl <= cur) & (bl < n_sb), score, -jnp.inf)
            rank = jnp.zeros((T_PAD, half), jnp.int32)
            for i in range(n_sb):
                si = score[:, i:i + 1]
                rank = rank + ((si > score) | ((si == score) & (i < bl))).astype(jnp.int32)
            selm = ((rank < N_SEL) & (bl < n_sb)).astype(bf16)
            sel_ref[kh * R:(kh + 1) * R, :] = jnp.concatenate([selm] * G, axis=0)

    last = g_step == n_steps - 1
    bsel = bs_ref[jnp.where(last, 1, 0)]
    okf = jnp.dot(sel_ref[...], es_ref[...], preferred_element_type=f32) > 0.5
    for kh in range(KVH):
        rs = slice(kh * R, (kh + 1) * R)
        k = jnp.concatenate([pr[:, kh * hd:(kh + 1) * hd] for pr in page_refs], axis=0).astype(bf16)
        v = jnp.concatenate([pr[:, (KVH + kh) * hd:(KVH + kh + 1) * hd] for pr in page_refs], axis=0).astype(bf16)
        s = lax.dot_general(qs[kh], k, nt, preferred_element_type=f32) * scale + bsel[rs]
        ok = okf[rs]
        s = jnp.where(ok, s, NEG_BIG)
        m_old = m_ref[rs]
        m_new = jnp.maximum(m_old, jnp.max(s, axis=-1, keepdims=True))
        p = jnp.where(ok, jnp.exp(s - m_new), 0.0)
        alpha = jnp.exp(m_old - m_new)
        l_ref[rs] = alpha * l_ref[rs] + jnp.sum(p, axis=-1, keepdims=True)
        acc_ref[rs] = alpha * acc_ref[rs] + jnp.dot(p.astype(bf16), v, preferred_element_type=f32)
        m_ref[rs] = m_new

    @pl.when(last)
    def _():
        gate = jax.nn.sigmoid(gate_ref[...])
        ncol = lax.broadcasted_iota(jnp.int32, (1, T_PAD), 1)
        wcol = lax.broadcasted_iota(jnp.int32, (1, win_ref.shape[0]), 1)
        wb = win_ref.shape[0]
        for kh in range(KVH):
            rs = slice(kh * R, (kh + 1) * R)
            cs = slice(kh * hd, (kh + 1) * hd)
            s = lax.dot_general(qs[kh], ksn_ref[:, cs].astype(bf16), nt, preferred_element_type=f32) * scale + bn_ref[kh]
            ok = (past + ncol) <= t_row
            s = jnp.where(ok, s, NEG_BIG)
            m_old = m_ref[rs]
            m_new = jnp.maximum(m_old, jnp.max(s, axis=-1, keepdims=True))
            p = jnp.where(ok, jnp.exp(s - m_new), 0.0)
            alpha = jnp.exp(m_old - m_new)
            l_s = alpha * l_ref[rs] + jnp.sum(p, axis=-1, keepdims=True)
            osel = (alpha * acc_ref[rs] + jnp.dot(p.astype(bf16), vsn_ref[:, cs].astype(bf16),
                                                  preferred_element_type=f32)) / l_s
            s1 = lax.dot_general(qs[kh], win_ref[:, cs].astype(bf16), nt, preferred_element_type=f32) * scale + bw_ref[kh]
            d1 = t_row - (past - wb + wcol)
            ok1 = (d1 >= 0) & (d1 <= WINDOW)
            s1 = jnp.where(ok1, s1, NEG_BIG)
            s2 = lax.dot_general(qs[kh], kwn_ref[:, cs].astype(bf16), nt, preferred_element_type=f32) * scale + bn_ref[kh]
            s2 = jnp.where(ok, s2, NEG_BIG)
            mw = jnp.maximum(jnp.max(s1, axis=-1, keepdims=True), jnp.max(s2, axis=-1, keepdims=True))
            p1 = jnp.where(ok1, jnp.exp(s1 - mw), 0.0)
            p2 = jnp.where(ok, jnp.exp(s2 - mw), 0.0)
            l_w = jnp.sum(p1, axis=-1, keepdims=True) + jnp.sum(p2, axis=-1, keepdims=True)
            vcs = slice((KVH + kh) * hd, (KVH + kh + 1) * hd)
            owin = (jnp.dot(p1.astype(bf16), win_ref[:, vcs].astype(bf16), preferred_element_type=f32)
                    + jnp.dot(p2.astype(bf16), vwn_ref[:, cs].astype(bf16), preferred_element_type=f32)) / l_w
            oc = oc_ref[rs]
            for g in range(G):
                r = slice(g * T_PAD, (g + 1) * T_PAD)
                c0 = kh * LANE + 3 * g
                o_ref[:, (kh * G + g) * hd:(kh * G + g + 1) * hd] = (
                    gate[:, c0:c0 + 1] * oc[r] + gate[:, c0 + 1:c0 + 2] * osel[r] + gate[:, c0 + 2:c0 + 3] * owin[r])


def nsa_sample(proj_pad, kvc_t, pool_sel, win_buf, page_table, rel_bias, B, T):
    import numpy as np
    KVH, G, hd = NSA_KV_HEADS, NSA_GROUP, NSA_HEAD_DIM
    n_phys, page = pool_sel.shape[:2]
    n_pages = page_table.shape[1]
    past = n_pages * page
    wb = win_buf.shape[1]
    half = kvc_t.shape[2] // 2
    n_sb = -(-(past + T) // SEL_BLOCK)
    n_steps = n_pages // SAMPLE_PG
    keys = SAMPLE_PG * page
    R = G * T_PAD
    assert n_pages % SAMPLE_PG == 0 and page % SEL_BLOCK == 0 and T <= T_PAD and n_sb <= half and past % SEL_BLOCK == 0
    bucket = bucket_table(past + T_PAD + 1)
    tab = rel_bias.astype(jnp.float32)
    qpos = past + np.arange(T_PAD)

    def bias_rows(dist):
        b = tab[bucket[np.maximum(dist, 0)]]
        return b.reshape(T_PAD, dist.shape[1], KVH, G).transpose(2, 3, 0, 1).reshape(KVH, R, dist.shape[1])

    lane = np.arange(2 * half)
    blk = np.where(lane < half, 2 * lane, 2 * (lane - half) + 1)
    bc = bias_rows(qpos[:, None] - (blk * CMP_BLOCK + CMP_BLOCK - 1)[None, :])
    far = bias_rows(np.full((T_PAD, keys), past, np.int64))
    assert bucket[page + 1] == NUM_BUCKETS - 1
    near = bias_rows(qpos[:, None] - ((n_steps - 1) * keys + np.arange(keys))[None, :])
    bs = jnp.stack([far, near], axis=0).reshape(2, KVH * R, keys)
    bn = bias_rows(qpos[:, None] - (past + np.arange(T_PAD))[None, :])
    bw = bias_rows(qpos[:, None] - (past - wb + np.arange(wb))[None, :])
    es = (np.arange(half)[None, :, None] == (np.arange(n_steps)[:, None, None] * (keys // SEL_BLOCK)
                                              + np.arange(keys)[None, None, :] // SEL_BLOCK))
    es = jnp.asarray(es, jnp.bfloat16)
    pool2 = pool_sel.reshape(n_phys, page, 2 * KVH * hd)
    win2 = win_buf.reshape(B, wb, 2 * KVH * hd)
    wq = G * hd
    c = lambda *idx: (lambda b, g, pt: idx)
    row_blk = lambda w, col: pl.BlockSpec((T_PAD, w), lambda b, g, pt: (b, col))
    in_specs = ([row_blk(wq, P_Q // wq + kh) for kh in range(KVH)]
                + [row_blk(KVH * LANE, P_GATES // (KVH * LANE)),
                   row_blk(KVH * hd, P_KVS // (KVH * hd)), row_blk(KVH * hd, P_KVS // (KVH * hd) + 1),
                   row_blk(KVH * hd, P_KVW // (KVH * hd)), row_blk(KVH * hd, P_KVW // (KVH * hd) + 1),
                   pl.BlockSpec((None, 2 * KVH, 2 * half, hd), lambda b, g, pt: (b, 0, 0, 0)),
                   pl.BlockSpec((KVH, R, 2 * half), c(0, 0, 0))]
                + [pl.BlockSpec((None, page, 2 * KVH * hd),
                                (lambda i: lambda b, g, pt: (pt[b * n_pages + g * SAMPLE_PG + i], 0, 0))(i))
                   for i in range(SAMPLE_PG)]
                + [pl.BlockSpec((None, half, keys), lambda b, g, pt: (g, 0, 0)),
                   pl.BlockSpec((2, KVH * R, keys), c(0, 0, 0)),
                   pl.BlockSpec((KVH, R, T_PAD), c(0, 0, 0)),
                   pl.BlockSpec((None, wb, 2 * KVH * hd), lambda b, g, pt: (b, 0, 0)),
                   pl.BlockSpec((KVH, R, wb), c(0, 0, 0))])
    f32 = jnp.float32
    return pl.pallas_call(
        functools.partial(_nsa_sample_kernel, past=past, n_steps=n_steps, n_sb=n_sb, half=half),
        grid_spec=pltpu.PrefetchScalarGridSpec(
            num_scalar_prefetch=1, grid=(B, n_steps), in_specs=in_specs,
            out_specs=pl.BlockSpec((T_PAD, NSA_WIDTH), lambda b, g, pt: (b, 0)),
            scratch_shapes=[pltpu.VMEM((KVH * R, 1), f32), pltpu.VMEM((KVH * R, 1), f32), pltpu.VMEM((KVH * R, hd), f32),
                            pltpu.VMEM((KVH * R, hd), f32), pltpu.VMEM((KVH * R, half), jnp.bfloat16)]),
        out_shape=jax.ShapeDtypeStruct((B * T_PAD, NSA_WIDTH), f32),
        compiler_params=pltpu.CompilerParams(dimension_semantics=("arbitrary",) * 2, vmem_limit_bytes=VMEM_LIMIT_BYTES),
        name="nsa_sample",
    )(page_table.reshape(-1), *([proj_pad] * (KVH + 5)), kvc_t, bc, *([pool2] * SAMPLE_PG), es, bs, bn, win2, bw)


RWKV_PAIR = 2 * RWKV_HEAD_DIM
HIGHEST = lax.Precision.HIGHEST


def _rwkv_kernel(r_ref, k_ref, v_ref, xw_ref, xa_ref, xg_ref,
                 sr_ref, sk_ref, sv_ref, sxw_ref, sxa_ref, sxg_ref,
                 mr_ref, mk_ref, mv_ref, mxw_ref, mxa_ref, mxg_ref,
                 s0_ref, w0_ref, a0_ref, kk_ref, ka_ref, rk_ref, lnw_ref, lnb_ref,
                 wup_ref, aup_ref, gup_ref,
                 o_ref, sout_ref,
                 S_ref, cr_ref, ck_ref, cv_ref, cxw_ref, cxa_ref, cxg_ref, *, C, t_valid, n_chunks):
    f32, bf16 = jnp.float32, jnp.bfloat16
    hd, W = RWKV_HEAD_DIM, RWKV_PAIR
    ci = pl.program_id(2)
    lane = lax.broadcasted_iota(jnp.int32, (1, W), 1)
    h0 = lane < hd
    row = lax.broadcasted_iota(jnp.int32, (C, 1), 0)
    valid = (ci * C + row) < t_valid
    carries = ((cr_ref, sr_ref), (ck_ref, sk_ref), (cv_ref, sv_ref), (cxw_ref, sxw_ref), (cxa_ref, sxa_ref),
               (cxg_ref, sxg_ref))

    @pl.when(ci == 0)
    def _():
        S_ref[...] = jnp.zeros((W, W), f32)
        S_ref[0:hd, 0:hd] = s0_ref[0]
        S_ref[hd:W, hd:W] = s0_ref[1]
        for c_ref, s_ref in carries:
            c_ref[0:1, :] = s_ref[...]

    def shifted(x_ref, c_ref, mu_ref):
        x = x_ref[...]
        prev = jnp.where(row == 0, c_ref[0:1, :], pltpu.roll(x, 1, axis=0))
        c_ref[0:1, :] = x[C - 1:C, :]
        return x + (prev - x) * mu_ref[...]

    def seg_sum(x):
        s_lo = jnp.sum(jnp.where(h0, x, 0.0), axis=-1, keepdims=True)
        s_hi = jnp.sum(jnp.where(h0, 0.0, x), axis=-1, keepdims=True)
        return jnp.where(h0, s_lo, s_hi)

    def stack(x):
        return jnp.concatenate([jnp.where(h0, x, 0.0), jnp.where(h0, 0.0, x)], axis=0)

    mr = shifted(r_ref, cr_ref, mr_ref)
    mk = shifted(k_ref, ck_ref, mk_ref)
    mv = shifted(v_ref, cv_ref, mv_ref)
    mxw = shifted(xw_ref, cxw_ref, mxw_ref)
    mxa = shifted(xa_ref, cxa_ref, mxa_ref)
    mxg = shifted(xg_ref, cxg_ref, mxg_ref)

    z = w0_ref[...] + jnp.dot(jnp.tanh(mxw).astype(bf16), wup_ref[...].astype(bf16), preferred_element_type=f32)
    w_log = -jax.nn.softplus(-z) - 0.5
    lw = -jnp.exp(w_log)
    a = jax.nn.sigmoid(a0_ref[...] + jnp.dot(mxa.astype(bf16), aup_ref[...].astype(bf16), preferred_element_type=f32))
    g = jnp.dot(jax.nn.sigmoid(mxg).astype(bf16), gup_ref[...].astype(bf16), preferred_element_type=f32)
    kk = mk * kk_ref[...]
    kk = kk / jnp.maximum(jnp.sqrt(seg_sum(kk * kk)), 1e-12)
    kp = mk * (1.0 + (a - 1.0) * ka_ref[...])
    lw = jnp.where(valid, lw, 0.0)
    a_t = jnp.where(valid, -kk, 0.0)
    b_t = jnp.where(valid, kk * a, 0.0)
    k_t = jnp.where(valid, kp, 0.0)
    v_t = jnp.where(valid, mv, 0.0)

    ti = lax.broadcasted_iota(jnp.int32, (C, C), 0)
    tj = lax.broadcasted_iota(jnp.int32, (C, C), 1)
    cl = jnp.dot((tj <= ti).astype(f32), lw, precision=HIGHEST, preferred_element_type=f32)
    e_neg = jnp.exp(-cl)
    At = a_t * jnp.exp(cl - lw)
    Bt = b_t * e_neg
    Kt = k_t * e_neg
    Rt = mr * jnp.exp(cl)
    G = jnp.concatenate([stack(At), stack(Rt)], axis=0).astype(bf16)
    Z = jnp.concatenate([Bt, Bt, Kt, Kt], axis=0).astype(bf16)
    nt = (((1,), (1,)), ((), ()))
    M = lax.dot_general(G, Z, nt, preferred_element_type=f32)
    ri = lax.broadcasted_iota(jnp.int32, (2 * C, 2 * C), 0)
    cj = lax.broadcasted_iota(jnp.int32, (2 * C, 2 * C), 1)
    same = (ri >= C) == (cj >= C)
    strict = same & ((cj & (C - 1)) < (ri & (C - 1)))
    incl = same & ((cj & (C - 1)) <= (ri & (C - 1)))
    L = jnp.where(strict, M[0:2 * C, 0:2 * C], 0.0)
    AK = jnp.where(strict, M[0:2 * C, 2 * C:4 * C], 0.0)
    RB = jnp.where(incl, M[2 * C:4 * C, 0:2 * C], 0.0)
    RK = jnp.where(incl, M[2 * C:4 * C, 2 * C:4 * C], 0.0)
    S = S_ref[...]
    GS = lax.dot_general(G, S.astype(bf16), nt, preferred_element_type=f32)
    Vs = stack(v_t)
    U = GS[0:2 * C] + jnp.dot(AK.astype(bf16), Vs.astype(bf16), preferred_element_type=f32)
    P = L
    levels = C.bit_length() - 1
    for lvl in range(levels):
        U = U + jnp.dot(P, U, precision=HIGHEST, preferred_element_type=f32)
        if lvl + 1 < levels:
            P = jnp.dot(P, P, precision=HIGHEST, preferred_element_type=f32)
    X = jnp.concatenate([U, Vs], axis=0).astype(bf16)
    Y = GS[2 * C:4 * C] + jnp.dot(jnp.concatenate([RB, RK], axis=1).astype(bf16), X, preferred_element_type=f32)
    y = Y[0:C] + Y[C:2 * C]
    dS = lax.dot_general(X, Z, (((0,), (0,)), ((), ())), preferred_element_type=f32)
    vi = lax.broadcasted_iota(jnp.int32, (W, W), 0)
    kj = lax.broadcasted_iota(jnp.int32, (W, W), 1)
    S_new = jnp.where((vi < hd) == (kj < hd), S + dS, 0.0) * jnp.exp(cl[C - 1:C, :])
    S_ref[...] = S_new

    mean = seg_sum(y) * (1.0 / hd)
    yc = y - mean
    var = seg_sum(yc * yc) * (1.0 / hd)
    yn = yc * lax.rsqrt(var + LNX_EPS) * lnw_ref[...] + lnb_ref[...]
    bonus = seg_sum(mr * kp * rk_ref[...]) * mv
    o_ref[...] = (yn + bonus) * g

    @pl.when(ci == n_chunks - 1)
    def _():
        sout_ref[0] = S_new[0:hd, 0:hd]
        sout_ref[1] = S_new[hd:W, hd:W]


def rwkv_mix(proj_p, shift_prev, wkv0, lp, B, T, t_valid, C):
    f32 = jnp.float32
    W, hd = RWKV_PAIR, RWKV_HEAD_DIM
    n_pairs = RWKV_WIDTH // W
    n_chunks = T // C
    assert T % C == 0 and C & (C - 1) == 0 and C >= 8

    def pad_vec(x):
        z = lambda n: jnp.zeros(x.shape[:-1] + (n,), x.dtype)
        return jnp.concatenate([x[..., :RWKV_SPLITS[2]], x[..., RWKV_SPLITS[2]:RWKV_SPLITS[3]], z(LANE - DECAY_LORA),
                                x[..., RWKV_SPLITS[3]:RWKV_SPLITS[4]], z(LANE - AAA_LORA), x[..., RWKV_SPLITS[4]:]], -1)

    shift_p = pad_vec(shift_prev.astype(f32))[:, None, :]
    mu_p = pad_vec(lp['rwkv_mu'].astype(f32))[None, :]
    pad_rows_to = lambda w: jnp.pad(w, ((0, LANE - w.shape[0]), (0, 0)))
    wup, aup = pad_rows_to(lp['rwkv_w_up']), pad_rows_to(lp['rwkv_a_up'])
    vec = lambda x: x.reshape(1, RWKV_WIDTH).astype(f32)
    nw = RWKV_WIDTH // W
    col_r, col_k, col_v = 0, nw, 2 * nw
    col_xw, col_xa, col_xg = P_XW // LANE, P_XA // LANE, P_XG // GATE_LORA
    row_spec = lambda w, cfn: pl.BlockSpec((C, w), lambda b, p, c: (b * n_chunks + c, cfn(p)))
    sh_spec = lambda w, cfn: pl.BlockSpec((None, 1, w), lambda b, p, c: (b, 0, cfn(p)))
    mu_spec = lambda w, cfn: pl.BlockSpec((1, w), lambda b, p, c: (0, cfn(p)))
    cols = [(W, lambda p: col_r + p), (W, lambda p: col_k + p), (W, lambda p: col_v + p),
            (LANE, lambda p: col_xw), (LANE, lambda p: col_xa), (GATE_LORA, lambda p: col_xg)]
    pvec = pl.BlockSpec((1, W), lambda b, p, c: (0, p))
    state_spec = pl.BlockSpec((None, 2, hd, hd), lambda b, p, c: (b, p, 0, 0))
    in_specs = ([row_spec(w, f) for w, f in cols] + [sh_spec(w, f) for w, f in cols] + [mu_spec(w, f) for w, f in cols]
                + [state_spec] + [pvec] * 7
                + [pl.BlockSpec((LANE, W), lambda b, p, c: (0, p)), pl.BlockSpec((LANE, W), lambda b, p, c: (0, p)),
                   pl.BlockSpec((GATE_LORA, W), lambda b, p, c: (0, p))])
    scratch = [pltpu.VMEM((W, W), f32)] + [pltpu.VMEM((8, w), f32) for w, _ in cols]
    o, s_fin = pl.pallas_call(
        functools.partial(_rwkv_kernel, C=C, t_valid=t_valid, n_chunks=n_chunks),
        grid=(B, n_pairs, n_chunks),
        in_specs=in_specs,
        out_specs=[pl.BlockSpec((C, W), lambda b, p, c: (b * n_chunks + c, p)), state_spec],
        out_shape=[jax.ShapeDtypeStruct((B * T, RWKV_WIDTH), f32),
                   jax.ShapeDtypeStruct((B, RWKV_HEADS, hd, hd), f32)],
        scratch_shapes=scratch,
        compiler_params=pltpu.CompilerParams(dimension_semantics=("arbitrary",) * 3, vmem_limit_bytes=VMEM_LIMIT_BYTES),
        name="rwkv_mix",
    )(*([proj_p] * 6), *([shift_p] * 6), *([mu_p] * 6), wkv0.astype(f32),
      vec(lp['rwkv_w0']), vec(lp['rwkv_a0']), vec(lp['rwkv_k_k']), vec(lp['rwkv_k_a']), vec(lp['rwkv_r_k']),
      vec(lp['rwkv_lnx_w']), vec(lp['rwkv_lnx_b']), wup, aup, lp['rwkv_g_up'])
    return o, s_fin


def _ffn_up_kernel(be_ref, nv_ref, x_ref, wg_ref, wu_ref, h_ref):
    del be_ref
    i = pl.program_id(0)

    @pl.when(i < nv_ref[0])
    def _():
        x = x_ref[...]
        hg = jnp.dot(x, wg_ref[...].astype(jnp.bfloat16), preferred_element_type=jnp.float32)
        hu = jnp.dot(x, wu_ref[...].astype(jnp.bfloat16), preferred_element_type=jnp.float32)
        h_ref[...] = (jax.nn.silu(hg) * hu).astype(h_ref.dtype)

    @pl.when(i >= nv_ref[0])
    def _():
        h_ref[...] = jnp.zeros(h_ref.shape, h_ref.dtype)


def _ffn_down_kernel(be_ref, nv_ref, h_ref, wd_ref, rw_ref, o_ref):
    del be_ref
    i = pl.program_id(0)

    @pl.when(i < nv_ref[0])
    def _():
        o_ref[...] = jnp.dot(h_ref[...].astype(jnp.bfloat16), wd_ref[...].astype(jnp.bfloat16),
                             preferred_element_type=jnp.float32) * rw_ref[...]

    @pl.when(i >= nv_ref[0])
    def _():
        o_ref[...] = jnp.zeros(o_ref.shape, o_ref.dtype)


def expert_ffn(xs, w_gate, w_up, w_down, row_w, blk_exp, n_valid, blk):
    rows, d = xs.shape
    ff = w_gate.shape[-1]
    n_blocks = rows // blk
    params = pltpu.CompilerParams(dimension_semantics=("arbitrary",), vmem_limit_bytes=VMEM_LIMIT_BYTES)
    h = pl.pallas_call(
        _ffn_up_kernel,
        grid_spec=pltpu.PrefetchScalarGridSpec(
            num_scalar_prefetch=2, grid=(n_blocks,),
            in_specs=[pl.BlockSpec((blk, d), lambda i, be, nv: (i, 0)),
                      pl.BlockSpec((None, d, ff), lambda i, be, nv: (be[i], 0, 0)),
                      pl.BlockSpec((None, d, ff), lambda i, be, nv: (be[i], 0, 0))],
            out_specs=pl.BlockSpec((blk, ff), lambda i, be, nv: (i, 0))),
        out_shape=jax.ShapeDtypeStruct((rows, ff), jnp.bfloat16),
        compiler_params=params, name="ffn_up",
    )(blk_exp, n_valid, xs, w_gate, w_up)
    return pl.pallas_call(
        _ffn_down_kernel,
        grid_spec=pltpu.PrefetchScalarGridSpec(
            num_scalar_prefetch=2, grid=(n_blocks,),
            in_specs=[pl.BlockSpec((blk, ff), lambda i, be, nv: (i, 0)),
                      pl.BlockSpec((None, ff, d), lambda i, be, nv: (be[i], 0, 0)),
                      pl.BlockSpec((blk, 1), lambda i, be, nv: (i, 0))],
            out_specs=pl.BlockSpec((blk, d), lambda i, be, nv: (i, 0))),
        out_shape=jax.ShapeDtypeStruct((rows, d), jnp.float32),
        compiler_params=params, name="ffn_down",
    )(blk_exp, n_valid, h, w_down, row_w)


def moe_ffn_pallas(h, lp, blk):
    B, T, D = h.shape
    x = h.reshape(B * T, D)
    n = x.shape[0]
    f32 = jnp.float32
    scores = jax.nn.sigmoid(pallas_matmul(x, lp['router_w']).astype(f32))
    biased = scores + lp['router_bias'].astype(f32)
    grp = biased.reshape(n, N_GROUPS, N_EXPERTS // N_GROUPS)
    grp_score = lax.top_k(grp, 2)[0].sum(-1)
    _, gidx = lax.top_k(grp_score, TOPK_GROUPS)
    gmask = jax.nn.one_hot(gidx, N_GROUPS, dtype=f32).sum(1)
    emask = jnp.repeat(gmask, N_EXPERTS // N_GROUPS, axis=1) > 0
    _, eidx = lax.top_k(jnp.where(emask, biased, -jnp.inf), TOP_K)
    w = jnp.take_along_axis(scores, eidx, axis=1)
    w = w / jnp.sum(w, axis=-1, keepdims=True) * ROUTED_SCALE

    nk = n * TOP_K
    flat_e = eidx.reshape(nk)
    order = jnp.argsort(flat_e)
    se = flat_e[order]
    counts = jnp.bincount(flat_e, length=N_EXPERTS)
    padded = (counts + blk - 1) // blk * blk
    pad_end = jnp.cumsum(padded)
    dest = (pad_end - padded)[se] + jnp.arange(nk) - (jnp.cumsum(counts) - counts)[se]
    n_blocks = -(-(nk + N_EXPERTS * (blk - 1)) // blk)
    rows = n_blocks * blk
    row_tok = jnp.zeros((rows,), jnp.int32).at[dest].set((order // TOP_K).astype(jnp.int32))
    row_w = jnp.zeros((rows,), f32).at[dest].set(w.reshape(nk)[order])
    blk_exp = jnp.minimum(jnp.searchsorted(pad_end, jnp.arange(n_blocks) * blk, side='right'),
                          N_EXPERTS - 1).astype(jnp.int32)
    n_valid = (pad_end[-1] // blk).astype(jnp.int32).reshape(1)
    pos = jnp.zeros((nk,), jnp.int32).at[order].set(dest.astype(jnp.int32)).reshape(n, TOP_K)

    xb = x.astype(jnp.bfloat16)
    ys = expert_ffn(xb[row_tok], lp['exp_w_gate'], lp['exp_w_up'], lp['exp_w_down'], row_w[:, None], blk_exp,
                    n_valid, blk)
    routed = ys[pos].sum(axis=1)
    sblk = min(256, n)
    one = jnp.ones((n, 1), f32)
    shared = expert_ffn(xb, lp['sh_w_gate'][None], lp['sh_w_up'][None], lp['sh_w_down'][None], one,
                        jnp.zeros((n // sblk,), jnp.int32), jnp.full((1,), n // sblk, jnp.int32), sblk)
    return (routed + shared).astype(h.dtype).reshape(B, T, D)


def pad_rows(a, length):
    return jnp.pad(a, [(0, 0), (0, length - a.shape[1])] + [(0, 0)] * (a.ndim - 2))


def rwkv7_mix(pr, shift_prev, wkv0, lp):
    B, T, _ = pr.shape
    f32 = jnp.float32
    prev = jnp.concatenate([shift_prev[:, None, :].astype(pr.dtype), pr[:, :-1]], axis=1)
    m = pr + (prev - pr) * lp['rwkv_mu']
    r, k, v, xw, xa, xg = jnp.split(m, RWKV_SPLITS, axis=-1)
    w_log = -jax.nn.softplus(-(lp['rwkv_w0'] + jnp.tanh(xw) @ lp['rwkv_w_up']).astype(f32)) - 0.5
    decay = jnp.exp(-jnp.exp(w_log))
    a = jax.nn.sigmoid((lp['rwkv_a0'] + xa @ lp['rwkv_a_up']).astype(f32))
    g = jax.nn.sigmoid(xg) @ lp['rwkv_g_up']

    def heads(t):
        return t.astype(f32).reshape(B, T, RWKV_HEADS, RWKV_HEAD_DIM)

    r, k, v, decay, a = heads(r), heads(k), heads(v), heads(decay), heads(a)
    kk = k * lp['rwkv_k_k'].astype(f32).reshape(RWKV_HEADS, RWKV_HEAD_DIM)
    kk = kk / jnp.maximum(jnp.linalg.norm(kk, axis=-1, keepdims=True), 1e-12)
    k = k * (1.0 + (a - 1.0) * lp['rwkv_k_a'].astype(f32).reshape(RWKV_HEADS, RWKV_HEAD_DIM))

    def step(S, inp):
        r_t, w_t, k_t, v_t, a_t, b_t = inp
        sa = jnp.einsum('bhvk,bhk->bhv', S, a_t)
        S = S * w_t[:, :, None, :] + sa[..., None] * b_t[:, :, None, :] + v_t[..., None] * k_t[:, :, None, :]
        return S, jnp.einsum('bhvk,bhk->bhv', S, r_t)

    xs = tuple(jnp.moveaxis(t, 1, 0) for t in (r, decay, k, v, -kk, kk * a))
    s_final, ys = lax.scan(step, wkv0.astype(f32), xs)
    y = jnp.moveaxis(ys, 0, 1)
    mu = jnp.mean(y, axis=-1, keepdims=True)
    var = jnp.mean(jnp.square(y - mu), axis=-1, keepdims=True)
    y = ((y - mu) * lax.rsqrt(var + LNX_EPS)).reshape(B, T, RWKV_WIDTH)
    y = y * lp['rwkv_lnx_w'].astype(f32) + lp['rwkv_lnx_b'].astype(f32)
    bonus = (jnp.sum(r * k * lp['rwkv_r_k'].astype(f32), axis=-1, keepdims=True) * v).reshape(B, T, RWKV_WIDTH)
    out = ((y + bonus) * g.astype(f32)).astype(pr.dtype)
    return out, pr[:, -1], s_final.astype(wkv0.dtype)


def compress_blocks(rows, pe, w):
    B, L = rows.shape[:2]
    blk = rows.reshape(B, L // CMP_BLOCK, CMP_BLOCK, 2, NSA_KV_HEADS, NSA_HEAD_DIM)
    blk = blk + jnp.swapaxes(pe, 0, 1)[None, None, :, :, None, :]
    return jnp.einsum('bncjkd,jcde->bnjke', blk, w)


def nsa_attend(q, q_pos, gates, kc, vc, c_end, n_sel_blocks, gather_sel, kw, vw, w_pos, rel_bias):
    B, Tq = q.shape[:2]
    f32 = jnp.float32
    scale = NSA_HEAD_DIM ** -0.5
    qg = q.reshape(B, Tq, NSA_KV_HEADS, NSA_GROUP, NSA_HEAD_DIM)
    bias_tab = rel_bias.astype(f32).reshape(NUM_BUCKETS, NSA_KV_HEADS, NSA_GROUP)

    dc = q_pos[:, None] - c_end[None, :]
    bias_c = bias_tab[rel_bucket(dc)].transpose(0, 2, 3, 1)
    lc = jnp.einsum('btkgd,bnkd->btkgn', qg, kc).astype(f32) * scale + bias_c
    pc = masked_softmax(lc, (dc >= 0)[:, None, None, :])
    oc = jnp.einsum('btkgn,bnkd->btkgd', pc.astype(vc.dtype), vc)

    imp = pc.sum(3).reshape(B, Tq, NSA_KV_HEADS, n_sel_blocks, SEL_BLOCK // CMP_BLOCK).sum(-1)
    blk = jnp.arange(n_sel_blocks)
    cur = (q_pos // SEL_BLOCK)[:, None]
    force = (blk == 0) | (blk == cur) | (blk == cur - 1)
    score = jnp.where(force[:, None, :], FORCE_SCORE, imp)
    score = jnp.where((blk <= cur)[:, None, :], score, -jnp.inf)
    _, idx = lax.top_k(jnp.moveaxis(score, 2, 1), min(N_SEL, n_sel_blocks))

    sel = gather_sel(idx)
    ks, vs = sel[..., 0, :], sel[..., 1, :]
    spos = idx[..., None] * SEL_BLOCK + jnp.arange(SEL_BLOCK)
    ds = q_pos[None, None, :, None, None] - spos
    hi = jnp.arange(NSA_KV_HEADS)[None, :, None, None, None]
    bias_s = jnp.moveaxis(bias_tab[rel_bucket(ds), hi], -1, 3)
    ls = jnp.einsum('btkgd,bktjsd->bktgjs', qg, ks).astype(f32) * scale + bias_s
    ms = (ds >= 0)[:, :, :, None]
    ps = masked_softmax(ls.reshape(ls.shape[:4] + (-1,)), ms.reshape(ms.shape[:4] + (-1,))).reshape(ls.shape)
    osel = jnp.einsum('bktgjs,bktjsd->btkgd', ps.astype(vs.dtype), vs)

    dw = q_pos[:, None] - w_pos[None, :]
    mw = (dw >= 0) & (dw <= WINDOW) & (w_pos >= 0)[None, :]
    bias_w = bias_tab[rel_bucket(dw)].transpose(0, 2, 3, 1)
    lw = jnp.einsum('btkgd,blkd->btkgl', qg, kw).astype(f32) * scale + bias_w
    pw = masked_softmax(lw, mw[:, None, None, :])
    ow = jnp.einsum('btkgl,blkd->btkgd', pw.astype(vw.dtype), vw)

    gt = gates.reshape(B, Tq, NSA_KV_HEADS, NSA_GROUP, 3)
    o = gt[..., 0:1] * oc + gt[..., 1:2] * osel + gt[..., 2:3] * ow
    return o.reshape(B, Tq, NSA_WIDTH)


def split_proj(h, lp):
    B, T, _ = h.shape
    proj = mm3(h, lp['w_in'])
    pr = proj[..., :RWKV_PROJ]
    pn = proj[..., RWKV_PROJ:]
    q = pn[..., :NSA_WIDTH].reshape(B, T, NSA_HEADS, NSA_HEAD_DIM)
    kv = pn[..., NSA_WIDTH:NSA_WIDTH + 3 * NSA_KV_WIDTH].reshape(B, T, 3, 2, NSA_KV_HEADS, NSA_HEAD_DIM)
    gates = jax.nn.sigmoid(pn[..., NSA_WIDTH + 3 * NSA_KV_WIDTH:].astype(jnp.float32))
    gates = gates.reshape(B, T, NSA_HEADS, 3).astype(h.dtype)
    return pr, q, kv[:, :, 0], kv[:, :, 1], kv[:, :, 2], gates


def merge_groups(o_rwkv, o_nsa, lp):
    o = jnp.concatenate([o_rwkv, rms_norm(o_nsa, lp['nsa_out_g'])], axis=-1)
    return mm3(o, lp['w_out'])


def mixer_prompt(h, lp, rel_bias):
    B, T, d = h.shape
    proj_p = pallas_matmul(h.reshape(B * T, d), lp['w_in_p'])
    o_r, wkv = rwkv_mix(proj_p, jnp.zeros((B, RWKV_PROJ), h.dtype),
                        jnp.zeros((B, RWKV_HEADS, RWKV_HEAD_DIM, RWKV_HEAD_DIM), jnp.float32), lp, B, T, T, 64)
    o_r = o_r.reshape(B, T, RWKV_WIDTH)
    shift_last = unpad_rwkv(proj_p.reshape(B, T, P_TOTAL)[:, -1])
    kvc = compress_prompt(proj_p, lp['cmp_pe'], lp['cmp_w'], B, T)
    o_n = nsa_prompt(proj_p, kvc, rel_bias, B, T).reshape(B, T, NSA_WIDTH)
    kv = proj_p[:, P_KVC:P_GATES].reshape(B, T, 3, 2, NSA_KV_HEADS, NSA_HEAD_DIM)
    kv_c, kv_s, kv_w = kv[:, :, 0], kv[:, :, 1], kv[:, :, 2]
    win_keep = min(WINDOW, T)
    return merge_groups(o_r, o_n, lp), (kv_c, kv_s, kv_w[:, T - win_keep:], wkv, shift_last)


def mixer_prompt_gather(h, lp, rel_bias):
    B, T, _ = h.shape
    pr, q, kv_c, kv_s, kv_w, gates = split_proj(h, lp)
    o_r, shift_last, wkv = rwkv7_mix(pr, jnp.zeros((B, RWKV_PROJ), h.dtype),
                                     jnp.zeros((B, RWKV_HEADS, RWKV_HEAD_DIM, RWKV_HEAD_DIM), jnp.float32), lp)
    lp_len = -(-T // SEL_BLOCK) * SEL_BLOCK
    comp = compress_blocks(pad_rows(kv_c, lp_len), lp['cmp_pe'], lp['cmp_w'])
    kc, vc = comp[:, :, 0], comp[:, :, 1]
    c_end = jnp.arange(lp_len // CMP_BLOCK) * CMP_BLOCK + (CMP_BLOCK - 1)
    n_sb = lp_len // SEL_BLOCK
    sel_blocks = pad_rows(kv_s, lp_len).reshape(B, n_sb, SEL_BLOCK, 2, NSA_KV_HEADS, NSA_HEAD_DIM)
    bi = jnp.arange(B)[:, None, None, None]
    hi = jnp.arange(NSA_KV_HEADS)[None, :, None, None]

    def gather_sel(idx):
        return sel_blocks[bi, idx, :, :, hi, :]

    kw_pad = jnp.pad(kv_w, ((0, 0), (WINDOW, 0), (0, 0), (0, 0), (0, 0)))

    def chunk(i):
        t0 = i * Q_CHUNK
        qc = lax.dynamic_slice_in_dim(q, t0, Q_CHUNK, axis=1)
        gc = lax.dynamic_slice_in_dim(gates, t0, Q_CHUNK, axis=1)
        wc = lax.dynamic_slice_in_dim(kw_pad, t0, WINDOW + Q_CHUNK, axis=1)
        q_pos = t0 + jnp.arange(Q_CHUNK)
        w_pos = t0 - WINDOW + jnp.arange(WINDOW + Q_CHUNK)
        return nsa_attend(qc, q_pos, gc, kc, vc, c_end, n_sb, gather_sel,
                          wc[:, :, 0], wc[:, :, 1], w_pos, rel_bias)

    o_n = lax.map(chunk, jnp.arange(T // Q_CHUNK))
    o_n = jnp.moveaxis(o_n, 0, 1).reshape(B, T, NSA_WIDTH)
    win_keep = min(WINDOW, T)
    return merge_groups(o_r, o_n, lp), (kv_c, kv_s, kv_w[:, T - win_keep:], wkv, shift_last)


def mixer_sample(h, lp, rel_bias, pool_cmp, pool_sel, win_buf, wkv0, shift0, page_table):
    B, T, _ = h.shape
    n_pages = page_table.shape[1]
    page = pool_cmp.shape[1]
    past = n_pages * page
    d = h.shape[-1]
    proj_p = pallas_matmul(h.reshape(B * T, d), lp['w_in_p']).reshape(B, T, P_TOTAL)
    import numpy as np
    kv = proj_p[..., P_KVC:P_GATES].reshape(B, T, 3, 2, NSA_KV_HEADS, NSA_HEAD_DIM)
    kv_c, kv_s, kv_w = kv[:, :, 0], kv[:, :, 1], kv[:, :, 2]
    proj_pad = jnp.pad(proj_p, ((0, 0), (0, T_PAD - T), (0, 0))).reshape(B * T_PAD, P_TOTAL)
    o_r, wkv = rwkv_mix(proj_pad, shift0, wkv0, lp, B, T_PAD, T, T_PAD)
    o_r = o_r.reshape(B, T_PAD, RWKV_WIDTH)[:, :T]
    shift_last = unpad_rwkv(proj_p[:, -1])
    lp_len = -(-(past + T) // SEL_BLOCK) * SEL_BLOCK

    jk, hd = 2 * NSA_KV_HEADS, NSA_HEAD_DIM
    kvc_past = compress_paged(pool_cmp, page_table, lp['cmp_pe'], lp['cmp_w'])
    new_len = lp_len - past
    new_rows = pad_rows(proj_p[..., P_KVC:P_KVS], new_len).reshape(B * new_len, NSA_KV_WIDTH)
    kvc_new = compress_prompt(new_rows, lp['cmp_pe'], lp['cmp_w'], B, new_len, col0=0)
    nb_new = new_len // CMP_BLOCK
    assert nb_new <= 2
    kvc_new = kvc_new.reshape(B, jk, nb_new, hd).transpose(0, 2, 1, 3)
    kvc_all = jnp.concatenate([kvc_past, kvc_new], axis=1)
    nc = lp_len // CMP_BLOCK
    n_half = nc // 2
    half = -(-n_half // LANE) * LANE
    idx = np.zeros((2 * half,), np.int32)
    idx[:n_half] = np.arange(0, nc, 2)
    idx[half:half + n_half] = np.arange(1, nc, 2)
    kvc_t = kvc_all[:, idx].transpose(0, 2, 1, 3)
    o_n = nsa_sample(proj_pad, kvc_t, pool_sel, win_buf, page_table, rel_bias, B, T)
    o_n = o_n.reshape(B, T_PAD, NSA_WIDTH)[:, :T]
    new_win = jnp.concatenate([win_buf, kv_w], axis=1)[:, T:]
    return merge_groups(o_r, o_n, lp), (kv_c, kv_s, new_win, wkv, shift_last)


def moe_ffn(h, lp):
    B, T, D = h.shape
    x = h.reshape(B * T, D)
    n = x.shape[0]
    f32 = jnp.float32
    scores = jax.nn.sigmoid((x @ lp['router_w']).astype(f32))
    biased = scores + lp['router_bias'].astype(f32)
    grp = biased.reshape(n, N_GROUPS, N_EXPERTS // N_GROUPS)
    grp_score = lax.top_k(grp, 2)[0].sum(-1)
    _, gidx = lax.top_k(grp_score, TOPK_GROUPS)
    gmask = jax.nn.one_hot(gidx, N_GROUPS, dtype=f32).sum(1)
    emask = jnp.repeat(gmask, N_EXPERTS // N_GROUPS, axis=1) > 0
    _, eidx = lax.top_k(jnp.where(emask, biased, -jnp.inf), TOP_K)
    w = jnp.take_along_axis(scores, eidx, axis=1)
    w = w / jnp.sum(w, axis=-1, keepdims=True) * ROUTED_SCALE

    nk = n * TOP_K
    flat_e = eidx.reshape(nk)
    order = jnp.argsort(flat_e)
    se = flat_e[order]
    counts = jnp.bincount(flat_e, length=N_EXPERTS)
    padded = (counts + MOE_BLOCK - 1) // MOE_BLOCK * MOE_BLOCK
    pad_end = jnp.cumsum(padded)
    dest = (pad_end - padded)[se] + jnp.arange(nk) - (jnp.cumsum(counts) - counts)[se]
    n_blocks = -(-(nk + N_EXPERTS * (MOE_BLOCK - 1)) // MOE_BLOCK)
    rows = n_blocks * MOE_BLOCK
    row_tok = jnp.zeros((rows,), jnp.int32).at[dest].set((order // TOP_K).astype(jnp.int32))
    row_w = jnp.zeros((rows,), f32).at[dest].set(w.reshape(nk)[order])
    blk_exp = jnp.minimum(jnp.searchsorted(pad_end, jnp.arange(n_blocks) * MOE_BLOCK, side='right'),
                          N_EXPERTS - 1)

    def expert_block(args):
        tok, wt, e = args
        xb = x[tok]
        hb = jax.nn.silu(xb @ lp['exp_w_gate'][e]) * (xb @ lp['exp_w_up'][e])
        return (hb @ lp['exp_w_down'][e]).astype(f32) * wt[:, None]

    out = lax.map(expert_block, (row_tok.reshape(n_blocks, MOE_BLOCK), row_w.reshape(n_blocks, MOE_BLOCK), blk_exp))
    routed = jnp.zeros((n, D), f32).at[row_tok].add(out.reshape(rows, D))
    shared = (jax.nn.silu(x @ lp['sh_w_gate']) * (x @ lp['sh_w_up'])) @ lp['sh_w_down']
    return (routed + shared.astype(f32)).astype(h.dtype).reshape(B, T, D)


def trunk_layer(x, c, lp, mixer):
    B, _, D = x.shape
    mod = (pallas_matmul(jax.nn.silu(c), lp['w_ada']) + lp['b_ada']).reshape(B, 6, 1, D)
    sh1, sc1, gt1, sh2, sc2, gt2 = (mod[:, i] for i in range(6))
    g = lp['norm_g']
    h = rms_norm(x, g[0]) * (1 + sc1) + sh1
    o, st = mixer(h)
    x = x + gt1 * rms_norm(o, g[1])
    h = rms_norm(x, g[2]) * (1 + sc2) + sh2
    x = x + gt2 * rms_norm(moe_ffn_pallas(h, lp, 256 if x.shape[0] * x.shape[1] >= 4096 else 32), g[3])
    return x, st


def kernel(x_prompt, x_sample, c_prompt, c_sample, cache_cmp, cache_sel, state_win, state_wkv,
           state_shift, page_table, rel_bias, w_ada, b_ada, norm_g, w_in, w_out, rwkv_mu, rwkv_w0,
           rwkv_w_up, rwkv_a0, rwkv_a_up, rwkv_g_up, rwkv_k_k, rwkv_k_a, rwkv_r_k, rwkv_lnx_w,
           rwkv_lnx_b, cmp_pe, cmp_w, nsa_out_g, router_w, router_bias, exp_w_gate, exp_w_up,
           exp_w_down, sh_w_gate, sh_w_up, sh_w_down):
    l = 0
    lp = dict(w_ada=w_ada[l], b_ada=b_ada[l], norm_g=norm_g[l], w_in=w_in[l], w_out=w_out[l],
              rwkv_mu=rwkv_mu[l], rwkv_w0=rwkv_w0[l], rwkv_w_up=rwkv_w_up[l], rwkv_a0=rwkv_a0[l],
              rwkv_a_up=rwkv_a_up[l], rwkv_g_up=rwkv_g_up[l], rwkv_k_k=rwkv_k_k[l],
              rwkv_k_a=rwkv_k_a[l], rwkv_r_k=rwkv_r_k[l], rwkv_lnx_w=rwkv_lnx_w[l],
              rwkv_lnx_b=rwkv_lnx_b[l], cmp_pe=cmp_pe[l], cmp_w=cmp_w[l], nsa_out_g=nsa_out_g[l],
              router_w=router_w[l], router_bias=router_bias[l], exp_w_gate=exp_w_gate[l],
              exp_w_up=exp_w_up[l], exp_w_down=exp_w_down[l], sh_w_gate=sh_w_gate[l],
              sh_w_up=sh_w_up[l], sh_w_down=sh_w_down[l])
    lp['w_in_p'] = pad_w_in(w_in[l])
    yp, st_p = trunk_layer(x_prompt, c_prompt, lp, lambda h: mixer_prompt(h, lp, rel_bias))
    ys, st_s = trunk_layer(x_sample, c_sample, lp, lambda h: mixer_sample(
        h, lp, rel_bias, cache_cmp[l], cache_sel[l], state_win[l], state_wkv[l], state_shift[l], page_table))
    p_cmp, p_sel, p_win, p_wkv, p_shift = [a[None] for a in st_p]
    s_cmp, s_sel, s_win, s_wkv, s_shift = [a[None] for a in st_s]
    return (yp, ys, p_cmp, p_sel, p_win, p_wkv, p_shift, s_cmp, s_sel, s_win, s_wkv, s_shift)
```

```python
import functools
import math

import jax
import jax.numpy as jnp
from jax import lax
from jax.experimental import pallas as pl
from jax.experimental.pallas import tpu as pltpu

D_MODEL = 4096
RWKV_WIDTH = D_MODEL // 2
NSA_WIDTH = D_MODEL - RWKV_WIDTH
RWKV_HEAD_DIM = 64
RWKV_HEADS = RWKV_WIDTH // RWKV_HEAD_DIM
DECAY_LORA = 96
AAA_LORA = 96
GATE_LORA = 256
RWKV_PROJ = 3 * RWKV_WIDTH + DECAY_LORA + AAA_LORA + GATE_LORA
RWKV_SPLITS = (RWKV_WIDTH, 2 * RWKV_WIDTH, 3 * RWKV_WIDTH, 3 * RWKV_WIDTH + DECAY_LORA,
               3 * RWKV_WIDTH + DECAY_LORA + AAA_LORA)
LNX_EPS = 64e-5
NSA_HEAD_DIM = 128
NSA_HEADS = NSA_WIDTH // NSA_HEAD_DIM
NSA_KV_HEADS = 4
NSA_GROUP = NSA_HEADS // NSA_KV_HEADS
CMP_BLOCK = 32
SEL_BLOCK = 64
N_SEL = 16
WINDOW = 512
Q_CHUNK = 32
FORCE_SCORE = 1e4
NSA_KV_WIDTH = 2 * NSA_KV_HEADS * NSA_HEAD_DIM
NUM_BUCKETS = 32
MAX_DISTANCE = 128
N_EXPERTS = 64
N_GROUPS = 8
TOPK_GROUPS = 4
TOP_K = 8
ROUTED_SCALE = 2.5
MOE_BLOCK = 128
RMS_EPS = 1e-6

VMEM_LIMIT_BYTES = 56 * 1024 * 1024


def _matmul_kernel(x_ref, w_ref, o_ref):
    o_ref[...] = jnp.dot(x_ref[...].astype(jnp.bfloat16), w_ref[...].astype(jnp.bfloat16),
                         preferred_element_type=jnp.float32)


def pallas_matmul(x, w, tn=512):
    m, k = x.shape
    n = w.shape[1]
    row_mult = 16 if x.dtype == jnp.bfloat16 else 8
    mp = -(-m // row_mult) * row_mult
    tm = min(1024 if x.dtype == jnp.bfloat16 else 512, mp)
    mp = -(-mp // tm) * tm
    np_ = -(-n // tn) * tn
    if mp != m:
        x = jnp.pad(x, ((0, mp - m), (0, 0)))
    if np_ != n:
        w = jnp.pad(w, ((0, 0), (0, np_ - n)))
    out = pl.pallas_call(
        _matmul_kernel,
        grid=(mp // tm, np_ // tn),
        in_specs=[pl.BlockSpec((tm, k), lambda i, j: (i, 0)),
                  pl.BlockSpec((k, tn), lambda i, j: (0, j))],
        out_specs=pl.BlockSpec((tm, tn), lambda i, j: (i, j)),
        out_shape=jax.ShapeDtypeStruct((mp, np_), jnp.float32),
        compiler_params=pltpu.CompilerParams(
            dimension_semantics=("arbitrary", "arbitrary"), vmem_limit_bytes=VMEM_LIMIT_BYTES),
        name="matmul",
    )(x, w)
    return out[:m, :n]


def mm3(h, w):
    b, t, d = h.shape
    return pallas_matmul(h.reshape(b * t, d), w).reshape(b, t, w.shape[1])


def rms_norm(x, g):
    xf = x.astype(jnp.float32)
    y = xf * lax.rsqrt(jnp.mean(xf * xf, axis=-1, keepdims=True) + RMS_EPS)
    return (y * g.astype(jnp.float32)).astype(x.dtype)


def rel_bucket(dist):
    max_exact = NUM_BUCKETS // 2
    n = jnp.maximum(dist, 0)
    nf = jnp.maximum(n, max_exact).astype(jnp.float32)
    large = max_exact + (jnp.log(nf / max_exact) / math.log(MAX_DISTANCE / max_exact)
                         * (NUM_BUCKETS - max_exact)).astype(jnp.int32)
    return jnp.where(n < max_exact, n, jnp.minimum(large, NUM_BUCKETS - 1))


def masked_softmax(logits, mask):
    logits = jnp.where(mask, logits.astype(jnp.float32), -jnp.inf)
    m = jnp.max(logits, axis=-1, keepdims=True)
    p = jnp.exp(logits - jnp.where(jnp.isfinite(m), m, 0.0))
    s = jnp.sum(p, axis=-1, keepdims=True)
    return p / jnp.where(s > 0, s, 1.0)


LANE = 128
P_XW = 3 * RWKV_WIDTH
P_XA = P_XW + LANE
P_XG = P_XA + LANE
P_Q = P_XG + GATE_LORA
P_KVC = P_Q + NSA_WIDTH
P_KVS = P_KVC + NSA_KV_WIDTH
P_KVW = P_KVS + NSA_KV_WIDTH
P_GATES = P_KVW + NSA_KV_WIDTH
P_TOTAL = P_GATES + NSA_KV_HEADS * LANE
N_GATE_COLS = 3 * NSA_GROUP


def pad_w_in(w_in):
    d = w_in.shape[0]
    z = lambda n: jnp.zeros((d, n), w_in.dtype)
    o_xw, o_xa, o_xg = RWKV_SPLITS[2], RWKV_SPLITS[3], RWKV_SPLITS[4]
    o_g = RWKV_PROJ + NSA_WIDTH + 3 * NSA_KV_WIDTH
    parts = [w_in[:, :o_xw], w_in[:, o_xw:o_xa], z(LANE - DECAY_LORA), w_in[:, o_xa:o_xg], z(LANE - AAA_LORA),
             w_in[:, o_xg:o_g]]
    for kh in range(NSA_KV_HEADS):
        parts += [w_in[:, o_g + kh * N_GATE_COLS:o_g + (kh + 1) * N_GATE_COLS], z(LANE - N_GATE_COLS)]
    return jnp.concatenate(parts, axis=1).astype(jnp.bfloat16)


def unpad_rwkv(proj_p):
    return jnp.concatenate([proj_p[..., :P_XW], proj_p[..., P_XW:P_XW + DECAY_LORA],
                            proj_p[..., P_XA:P_XA + AAA_LORA], proj_p[..., P_XG:P_XG + GATE_LORA]], axis=-1)


def bucket_table(n):
    import numpy as np
    d = np.arange(n)
    max_exact = NUM_BUCKETS // 2
    nf = np.maximum(d, max_exact).astype(np.float64)
    large = max_exact + (np.log(nf / max_exact) / math.log(MAX_DISTANCE / max_exact)
                         * (NUM_BUCKETS - max_exact)).astype(np.int64)
    return np.where(d < max_exact, d, np.minimum(large, NUM_BUCKETS - 1)).astype(np.int32)


def _compress_kernel(x_ref, w_ref, pe_ref, o_ref, *, nblk):
    half = nblk // 2
    acc = jnp.zeros((nblk, NSA_HEAD_DIM), jnp.float32)
    for c in range(CMP_BLOCK):
        xe = x_ref[pl.ds(c, half, stride=2 * CMP_BLOCK), :]
        xo = x_ref[pl.ds(CMP_BLOCK + c, half, stride=2 * CMP_BLOCK), :]
        lhs = jnp.concatenate([xe, xo], axis=0) + pe_ref[c:c + 1, :]
        acc = acc + jnp.dot(lhs.astype(jnp.bfloat16), w_ref[c].astype(jnp.bfloat16),
                            preferred_element_type=jnp.float32)
    o_ref[...] = acc


def compress_prompt(proj_p, cmp_pe, cmp_w, B, T, col0=P_KVC):
    nblk = T // CMP_BLOCK
    hd, KVH = NSA_HEAD_DIM, NSA_KV_HEADS
    return pl.pallas_call(
        functools.partial(_compress_kernel, nblk=nblk),
        grid=(B, 2, KVH),
        in_specs=[pl.BlockSpec((T, hd), lambda b, j, kh: (b, col0 // hd + j * KVH + kh)),
                  pl.BlockSpec((None, CMP_BLOCK, hd, hd), lambda b, j, kh: (j, 0, 0, 0)),
                  pl.BlockSpec((None, CMP_BLOCK, hd), lambda b, j, kh: (j, 0, 0))],
        out_specs=pl.BlockSpec((None, None, None, nblk, hd), lambda b, j, kh: (b, j, kh, 0, 0)),
        out_shape=jax.ShapeDtypeStruct((B, 2, KVH, nblk, hd), jnp.float32),
        compiler_params=pltpu.CompilerParams(dimension_semantics=("arbitrary",) * 3, vmem_limit_bytes=VMEM_LIMIT_BYTES),
        name="compress_prompt",
    )(proj_p, cmp_w, cmp_pe)


NSA_TQ = 128
NEG_BIG = -1e30


def _nsa_prompt_kernel(q_ref, gate_ref, ks_ref, vs_ref, kw_ref, vw_ref, kc_ref, vc_ref, bt_ref, bc_ref, es_ref,
                       o_ref, *, n_sb):
    f32, bf16 = jnp.float32, jnp.bfloat16
    tq, G, hd = NSA_TQ, NSA_GROUP, NSA_HEAD_DIM
    qi = pl.program_id(2)
    t0 = qi * tq
    scale = NSA_HEAD_DIM ** -0.5
    q = jnp.concatenate([q_ref[:, g * hd:(g + 1) * hd] for g in range(G)], axis=0).astype(bf16)
    rows = G * tq
    t_row = t0 + (lax.broadcasted_iota(jnp.int32, (rows, 1), 0) & (tq - 1))

    def qk(k):
        return lax.dot_general(q, k.astype(bf16), (((1,), (1,)), ((), ())), preferred_element_type=f32)

    ncmp = kc_ref.shape[0]
    lane = lax.broadcasted_iota(jnp.int32, (1, ncmp), 1)
    blk_id = jnp.where(lane < ncmp // 2, 2 * lane, 2 * (lane - ncmp // 2) + 1)
    c_end = blk_id * CMP_BLOCK + (CMP_BLOCK - 1)
    vis = c_end <= t_row
    lc = jnp.where(vis, qk(kc_ref[...]) * scale + bc_ref[...], -jnp.inf)
    mc = jnp.max(lc, axis=-1, keepdims=True)
    pc = jnp.exp(lc - jnp.where(mc > -jnp.inf, mc, 0.0))
    sc = jnp.sum(pc, axis=-1, keepdims=True)
    pc = pc / jnp.where(sc > 0, sc, 1.0)
    oc = jnp.dot(pc.astype(bf16), vc_ref[...].astype(bf16), preferred_element_type=f32)

    pg = pc[0:tq]
    for g in range(1, G):
        pg = pg + pc[g * tq:(g + 1) * tq]
    imp = pg[:, :n_sb] + pg[:, n_sb:]
    tpos = t0 + lax.broadcasted_iota(jnp.int32, (tq, 1), 0)
    cur = tpos // SEL_BLOCK
    bl = lax.broadcasted_iota(jnp.int32, (tq, n_sb), 1)
    force = (bl == 0) | (bl == cur) | (bl == cur - 1)
    score = jnp.where(force, FORCE_SCORE, imp)
    score = jnp.where(bl <= cur, score, -jnp.inf)
    rank = jnp.zeros((tq, n_sb), jnp.int32)
    for i in range(n_sb):
        si = score[:, i:i + 1]
        ahead = (si > score) | ((si == score) & (i < bl))
        rank = rank + ahead.astype(jnp.int32)
    selmask = (rank < N_SEL).astype(bf16)
    selmask = jnp.concatenate([selmask] * G, axis=0)

    col = lax.broadcasted_iota(jnp.int32, (1, tq), 1)

    def attend(c, carry, k_ref, v_ref, selected):
        m, l, acc = carry
        s0 = pl.multiple_of(c * tq, tq)
        s = qk(k_ref[pl.ds(s0, tq), :]) * scale + bt_ref[jnp.minimum(qi - c, 2)]
        dist = t_row - (s0 + col)
        if selected:
            ok = (jnp.dot(selmask, es_ref[c], preferred_element_type=f32) > 0.5) & (dist >= 0)
        else:
            ok = (dist >= 0) & (dist <= WINDOW)
        s = jnp.where(ok, s, NEG_BIG)
        m_new = jnp.maximum(m, jnp.max(s, axis=-1, keepdims=True))
        p = jnp.where(ok, jnp.exp(s - m_new), 0.0)
        alpha = jnp.exp(m - m_new)
        l = alpha * l + jnp.sum(p, axis=-1, keepdims=True)
        acc = alpha * acc + jnp.dot(p.astype(bf16), v_ref[pl.ds(s0, tq), :].astype(bf16), preferred_element_type=f32)
        return m_new, l, acc

    init = (jnp.full((rows, 1), NEG_BIG, f32), jnp.zeros((rows, 1), f32), jnp.zeros((rows, hd), f32))
    _, l_s, acc_s = lax.fori_loop(0, qi + 1, functools.partial(attend, k_ref=ks_ref, v_ref=vs_ref, selected=True), init)
    c_lo = jnp.maximum(qi - WINDOW // tq, 0)
    _, l_w, acc_w = lax.fori_loop(c_lo, qi + 1, functools.partial(attend, k_ref=kw_ref, v_ref=vw_ref, selected=False), init)
    osel = acc_s / l_s
    owin = acc_w / l_w

    gate = jax.nn.sigmoid(gate_ref[...])
    for g in range(G):
        r = slice(g * tq, (g + 1) * tq)
        o_ref[:, g * hd:(g + 1) * hd] = (gate[:, 3 * g:3 * g + 1] * oc[r] + gate[:, 3 * g + 1:3 * g + 2] * osel[r]
                                         + gate[:, 3 * g + 2:3 * g + 3] * owin[r])


def nsa_prompt(proj_p, kvc, rel_bias, B, T):
    import numpy as np
    tq, G, hd, KVH = NSA_TQ, NSA_GROUP, NSA_HEAD_DIM, NSA_KV_HEADS
    nq = T // tq
    n_sb = T // SEL_BLOCK
    ncmp = T // CMP_BLOCK
    assert T % tq == 0 and T % SEL_BLOCK == 0 and ncmp == 2 * n_sb
    bucket = bucket_table(max(T, 3 * tq))
    tab = rel_bias.astype(jnp.float32)
    ii, jj = np.meshgrid(np.arange(tq), np.arange(tq), indexing="ij")
    didx = np.stack([bucket[np.maximum(ii - jj, 0)], bucket[tq + ii - jj], np.full((tq, tq), NUM_BUCKETS - 1)])
    assert bucket[tq + 1] == NUM_BUCKETS - 1
    bt = tab[didx].reshape(3, tq, tq, KVH, G).transpose(3, 0, 4, 1, 2).reshape(KVH, 3, G * tq, tq)
    blk = np.concatenate([np.arange(0, ncmp, 2), np.arange(1, ncmp, 2)])
    dc = np.arange(T)[:, None] - (blk * CMP_BLOCK + CMP_BLOCK - 1)[None, :]
    bc = tab[bucket[np.maximum(dc, 0)]].reshape(nq, tq, ncmp, KVH, G).transpose(3, 0, 4, 1, 2).reshape(KVH, nq, G * tq, ncmp)
    es = (np.arange(n_sb)[None, :, None] == (np.arange(nq)[:, None, None] * (tq // SEL_BLOCK)
                                              + np.arange(tq)[None, None, :] // SEL_BLOCK))
    es = jnp.asarray(es, jnp.bfloat16)
    col = lambda off: off // hd
    kv_spec = lambda off: pl.BlockSpec((T, hd), lambda b, kh, i: (b, col(off) + kh))
    return pl.pallas_call(
        functools.partial(_nsa_prompt_kernel, n_sb=n_sb),
        grid=(B, KVH, nq),
        in_specs=[pl.BlockSpec((tq, G * hd), lambda b, kh, i: (b * nq + i, P_Q // (G * hd) + kh)),
                  pl.BlockSpec((tq, LANE), lambda b, kh, i: (b * nq + i, P_GATES // LANE + kh)),
                  kv_spec(P_KVS), kv_spec(P_KVS + KVH * hd), kv_spec(P_KVW), kv_spec(P_KVW + KVH * hd),
                  pl.BlockSpec((None, None, None, ncmp, hd), lambda b, kh, i: (b, 0, kh, 0, 0)),
                  pl.BlockSpec((None, None, None, ncmp, hd), lambda b, kh, i: (b, 1, kh, 0, 0)),
                  pl.BlockSpec((None, 3, G * tq, tq), lambda b, kh, i: (kh, 0, 0, 0)),
                  pl.BlockSpec((None, None, G * tq, ncmp), lambda b, kh, i: (kh, i, 0, 0)),
                  pl.BlockSpec((nq, n_sb, tq), lambda b, kh, i: (0, 0, 0))],
        out_specs=pl.BlockSpec((tq, G * hd), lambda b, kh, i: (b * nq + i, kh)),
        out_shape=jax.ShapeDtypeStruct((B * T, NSA_WIDTH), jnp.float32),
        compiler_params=pltpu.CompilerParams(dimension_semantics=("arbitrary",) * 3, vmem_limit_bytes=VMEM_LIMIT_BYTES),
        name="nsa_prompt",
    )(proj_p, proj_p, proj_p, proj_p, proj_p, proj_p, kvc, kvc, bt, bc, es)


SAMPLE_PG = 8
T_PAD = 8


def _compress_paged_kernel(pt_ref, *refs, nb):
    del pt_ref
    page_refs = refs[:SAMPLE_PG]
    w_ref, pe_ref, o_ref = refs[SAMPLE_PG:]
    jk, hd = 2 * NSA_KV_HEADS, NSA_HEAD_DIM
    rows = SAMPLE_PG * nb * jk
    acc = jnp.zeros((rows, 2 * hd), jnp.float32)
    for c in range(CMP_BLOCK):
        x = jnp.concatenate([pr[pl.ds(c, nb, stride=CMP_BLOCK), :, :] for pr in page_refs], axis=0)
        lhs = (x + pe_ref[c]).reshape(rows, hd).astype(jnp.bfloat16)
        acc = acc + jnp.dot(lhs, w_ref[c], preferred_element_type=jnp.float32)
    sub = lax.broadcasted_iota(jnp.int32, (rows, 1), 0) & (jk - 1)
    out = jnp.where(sub < NSA_KV_HEADS, acc[:, :hd], acc[:, hd:])
    o_ref[...] = out.reshape(SAMPLE_PG * nb, jk, hd)


def compress_paged(pool, page_table, cmp_pe, cmp_w):
    n_phys, page = pool.shape[:2]
    B, n_pages = page_table.shape
    jk, hd = 2 * NSA_KV_HEADS, NSA_HEAD_DIM
    nb = page // CMP_BLOCK
    assert n_pages % SAMPLE_PG == 0 and page % CMP_BLOCK == 0
    pool3 = pool.reshape(n_phys, page, jk, hd)
    w01 = jnp.concatenate([cmp_w[0], cmp_w[1]], axis=-1).astype(jnp.bfloat16)
    pe8 = jnp.repeat(jnp.swapaxes(cmp_pe, 0, 1), NSA_KV_HEADS, axis=1)
    page_spec = lambda i: pl.BlockSpec((None, page, jk, hd),
                                       lambda b, g, pt: (pt[b * n_pages + g * SAMPLE_PG + i], 0, 0, 0))
    return pl.pallas_call(
        functools.partial(_compress_paged_kernel, nb=nb),
        grid_spec=pltpu.PrefetchScalarGridSpec(
            num_scalar_prefetch=1, grid=(B, n_pages // SAMPLE_PG),
            in_specs=[page_spec(i) for i in range(SAMPLE_PG)]
            + [pl.BlockSpec((CMP_BLOCK, hd, 2 * hd), lambda b, g, pt: (0, 0, 0)),
               pl.BlockSpec((CMP_BLOCK, jk, hd), lambda b, g, pt: (0, 0, 0))],
            out_specs=pl.BlockSpec((None, SAMPLE_PG * nb, jk, hd), lambda b, g, pt: (b, g, 0, 0))),
        out_shape=jax.ShapeDtypeStruct((B, n_pages * nb, jk, hd), jnp.float32),
        compiler_params=pltpu.CompilerParams(dimension_semantics=("arbitrary",) * 2, vmem_limit_bytes=VMEM_LIMIT_BYTES),
        name="compress_paged",
    )(page_table.reshape(-1), *([pool3] * SAMPLE_PG), w01, pe8)


def _nsa_sample_kernel(pt_ref, *refs, past, page, n_steps, n_sb, half):
    del pt_ref
    f32, bf16 = jnp.float32, jnp.bfloat16
    KVH, G, hd = NSA_KV_HEADS, NSA_GROUP, NSA_HEAD_DIM
    q_refs = refs[0:KVH]
    gate_ref, ksn_ref, vsn_ref, kwn_ref, vwn_ref, kvc_ref, bc_ref = refs[KVH:KVH + 7]
    page_refs = refs[KVH + 7:KVH + 7 + SAMPLE_PG]
    es_ref, bs_ref, bn_ref, win_ref, bw_ref, o_ref, m_ref, l_ref, acc_ref, oc_ref, sel_ref = refs[KVH + 7 + SAMPLE_PG:]
    g_step = pl.program_id(1)
    R = G * T_PAD
    scale = NSA_HEAD_DIM ** -0.5
    nt = (((1,), (1,)), ((), ()))
    t_row = past + (lax.broadcasted_iota(jnp.int32, (R, 1), 0) & (T_PAD - 1))
    qs = [jnp.concatenate([q_refs[kh][:, g * hd:(g + 1) * hd] for g in range(G)], axis=0).astype(bf16) for kh in range(KVH)]

    @pl.when(g_step == 0)
    def _():
        m_ref[...] = jnp.full(m_ref.shape, NEG_BIG, f32)
        l_ref[...] = jnp.zeros(l_ref.shape, f32)
        acc_ref[...] = jnp.zeros(acc_ref.shape, f32)
        lane = lax.broadcasted_iota(jnp.int32, (1, 2 * half), 1)
        lo = lane < half
        blk_id = jnp.where(lo, 2 * lane, 2 * (lane - half) + 1)
        real = jnp.where(lo, lane, lane - half) < n_sb
        vis = real & (blk_id * CMP_BLOCK + (CMP_BLOCK - 1) <= t_row)
        tpos = past + lax.broadcasted_iota(jnp.int32, (T_PAD, 1), 0)
        cur = tpos // SEL_BLOCK
        bl = lax.broadcasted_iota(jnp.int32, (T_PAD, half), 1)
        force = (bl == 0) | (bl == cur) | (bl == cur - 1)
        for kh in range(KVH):
            lc = lax.dot_general(qs[kh], kvc_ref[kh].astype(bf16), nt, preferred_element_type=f32) * scale + bc_ref[kh]
            lc = jnp.where(vis, lc, -jnp.inf)
            mc = jnp.max(lc, axis=-1, keepdims=True)
            pc = jnp.exp(lc - jnp.where(mc > -jnp.inf, mc, 0.0))
            sc = jnp.sum(pc, axis=-1, keepdims=True)
            pc = pc / jnp.where(sc > 0, sc, 1.0)
            oc_ref[kh * R:(kh + 1) * R, :] = jnp.dot(pc.astype(bf16), kvc_ref[KVH + kh].astype(bf16),
                                                     preferred_element_type=f32)
            pg = pc[0:T_PAD]
            for g in range(1, G):
                pg = pg + pc[g * T_PAD:(g + 1) * T_PAD]
            imp = pg[:, :half] + pg[:, half:]
            score = jnp.where(force, FORCE_SCORE, imp)
            score = jnp.where((bl <= cur) & (bl < n_sb), score, -jnp.inf)
            rank = jnp.zeros((T_PAD, half), jnp.int32)
            for i in range(n_sb):
                si = score[:, i:i + 1]
                rank = rank + ((si > score) | ((si == score) & (i < bl))).astype(jnp.int32)
            selm = ((rank < N_SEL) & (bl < n_sb)).astype(bf16)
            sel_ref[kh * R:(kh + 1) * R, :] = jnp.concatenate([selm] * G, axis=0)

    last = g_step == n_steps - 1
    bsel = bs_ref[jnp.where(last, 1, 0)]
    okf = jnp.dot(sel_ref[...], es_ref[...], preferred_element_type=f32) > 0.5
    for kh in range(KVH):
        rs = slice(kh * R, (kh + 1) * R)
        k = jnp.concatenate([pr[pl.ds(kh, page, stride=2 * KVH), :] for pr in page_refs], axis=0).astype(bf16)
        v = jnp.concatenate([pr[pl.ds(KVH + kh, page, stride=2 * KVH), :] for pr in page_refs], axis=0).astype(bf16)
        s = lax.dot_general(qs[kh], k, nt, preferred_element_type=f32) * scale + bsel[rs]
        ok = okf[rs]
        s = jnp.where(ok, s, NEG_BIG)
        m_old = m_ref[rs]
        m_new = jnp.maximum(m_old, jnp.max(s, axis=-1, keepdims=True))
        p = jnp.where(ok, jnp.exp(s - m_new), 0.0)
        alpha = jnp.exp(m_old - m_new)
        l_ref[rs] = alpha * l_ref[rs] + jnp.sum(p, axis=-1, keepdims=True)
        acc_ref[rs] = alpha * acc_ref[rs] + jnp.dot(p.astype(bf16), v, preferred_element_type=f32)
        m_ref[rs] = m_new

    @pl.when(last)
    def _():
        gate = jax.nn.sigmoid(gate_ref[...])
        ncol = lax.broadcasted_iota(jnp.int32, (1, T_PAD), 1)
        wb = win_ref.shape[0] // (2 * KVH)
        wcol = lax.broadcasted_iota(jnp.int32, (1, wb), 1)
        for kh in range(KVH):
            rs = slice(kh * R, (kh + 1) * R)
            cs = slice(kh * hd, (kh + 1) * hd)
            s = lax.dot_general(qs[kh], ksn_ref[:, cs].astype(bf16), nt, preferred_element_type=f32) * scale + bn_ref[kh]
            ok = (past + ncol) <= t_row
            s = jnp.where(ok, s, NEG_BIG)
            m_old = m_ref[rs]
            m_new = jnp.maximum(m_old, jnp.max(s, axis=-1, keepdims=True))
            p = jnp.where(ok, jnp.exp(s - m_new), 0.0)
            alpha = jnp.exp(m_old - m_new)
            l_s = alpha * l_ref[rs] + jnp.sum(p, axis=-1, keepdims=True)
            osel = (alpha * acc_ref[rs] + jnp.dot(p.astype(bf16), vsn_ref[:, cs].astype(bf16),
                                                  preferred_element_type=f32)) / l_s
            s1 = lax.dot_general(qs[kh], win_ref[pl.ds(kh, wb, stride=2 * KVH), :].astype(bf16), nt,
                                 preferred_element_type=f32) * scale + bw_ref[kh]
            d1 = t_row - (past - wb + wcol)
            ok1 = (d1 >= 0) & (d1 <= WINDOW)
            s1 = jnp.where(ok1, s1, NEG_BIG)
            s2 = lax.dot_general(qs[kh], kwn_ref[:, cs].astype(bf16), nt, preferred_element_type=f32) * scale + bn_ref[kh]
            s2 = jnp.where(ok, s2, NEG_BIG)
            mw = jnp.maximum(jnp.max(s1, axis=-1, keepdims=True), jnp.max(s2, axis=-1, keepdims=True))
            p1 = jnp.where(ok1, jnp.exp(s1 - mw), 0.0)
            p2 = jnp.where(ok, jnp.exp(s2 - mw), 0.0)
            l_w = jnp.sum(p1, axis=-1, keepdims=True) + jnp.sum(p2, axis=-1, keepdims=True)
            owin = (jnp.dot(p1.astype(bf16), win_ref[pl.ds(KVH + kh, wb, stride=2 * KVH), :].astype(bf16),
                            preferred_element_type=f32)
                    + jnp.dot(p2.astype(bf16), vwn_ref[:, cs].astype(bf16), preferred_element_type=f32)) / l_w
            oc = oc_ref[rs]
            for g in range(G):
                r = slice(g * T_PAD, (g + 1) * T_PAD)
                c0 = kh * LANE + 3 * g
                o_ref[:, (kh * G + g) * hd:(kh * G + g + 1) * hd] = (
                    gate[:, c0:c0 + 1] * oc[r] + gate[:, c0 + 1:c0 + 2] * osel[r] + gate[:, c0 + 2:c0 + 3] * owin[r])


def nsa_sample(proj_pad, kvc_t, pool_sel, win_buf, page_table, rel_bias, B, T):
    import numpy as np
    KVH, G, hd = NSA_KV_HEADS, NSA_GROUP, NSA_HEAD_DIM
    n_phys, page = pool_sel.shape[:2]
    n_pages = page_table.shape[1]
    past = n_pages * page
    wb = win_buf.shape[1]
    half = kvc_t.shape[2] // 2
    n_sb = -(-(past + T) // SEL_BLOCK)
    n_steps = n_pages // SAMPLE_PG
    keys = SAMPLE_PG * page
    R = G * T_PAD
    assert n_pages % SAMPLE_PG == 0 and page % SEL_BLOCK == 0 and T <= T_PAD and n_sb <= half and past % SEL_BLOCK == 0
    bucket = bucket_table(past + T_PAD + 1)
    tab = rel_bias.astype(jnp.float32)
    qpos = past + np.arange(T_PAD)

    def bias_rows(dist):
        b = tab[bucket[np.maximum(dist, 0)]]
        return b.reshape(T_PAD, dist.shape[1], KVH, G).transpose(2, 3, 0, 1).reshape(KVH, R, dist.shape[1])

    lane = np.arange(2 * half)
    blk = np.where(lane < half, 2 * lane, 2 * (lane - half) + 1)
    bc = bias_rows(qpos[:, None] - (blk * CMP_BLOCK + CMP_BLOCK - 1)[None, :])
    far = bias_rows(np.full((T_PAD, keys), past, np.int64))
    assert bucket[page + 1] == NUM_BUCKETS - 1
    near = bias_rows(qpos[:, None] - ((n_steps - 1) * keys + np.arange(keys))[None, :])
    bs = jnp.stack([far, near], axis=0).reshape(2, KVH * R, keys)
    bn = bias_rows(qpos[:, None] - (past + np.arange(T_PAD))[None, :])
    bw = bias_rows(qpos[:, None] - (past - wb + np.arange(wb))[None, :])
    es = (np.arange(half)[None, :, None] == (np.arange(n_steps)[:, None, None] * (keys // SEL_BLOCK)
                                              + np.arange(keys)[None, None, :] // SEL_BLOCK))
    es = jnp.asarray(es, jnp.bfloat16)
    pool2 = pool_sel.reshape(n_phys, page * 2 * KVH, hd)
    win2 = win_buf.reshape(B, wb * 2 * KVH, hd)
    wq = G * hd
    c = lambda *idx: (lambda b, g, pt: idx)
    row_blk = lambda w, col: pl.BlockSpec((T_PAD, w), lambda b, g, pt: (b, col))
    in_specs = ([row_blk(wq, P_Q // wq + kh) for kh in range(KVH)]
                + [row_blk(KVH * LANE, P_GATES // (KVH * LANE)),
                   row_blk(KVH * hd, P_KVS // (KVH * hd)), row_blk(KVH * hd, P_KVS // (KVH * hd) + 1),
                   row_blk(KVH * hd, P_KVW // (KVH * hd)), row_blk(KVH * hd, P_KVW // (KVH * hd) + 1),
                   pl.BlockSpec((None, 2 * KVH, 2 * half, hd), lambda b, g, pt: (b, 0, 0, 0)),
                   pl.BlockSpec((KVH, R, 2 * half), c(0, 0, 0))]
                + [pl.BlockSpec((None, page * 2 * KVH, hd),
                                (lambda i: lambda b, g, pt: (pt[b * n_pages + g * SAMPLE_PG + i], 0, 0))(i))
                   for i in range(SAMPLE_PG)]
                + [pl.BlockSpec((None, half, keys), lambda b, g, pt: (g, 0, 0)),
                   pl.BlockSpec((2, KVH * R, keys), c(0, 0, 0)),
                   pl.BlockSpec((KVH, R, T_PAD), c(0, 0, 0)),
                   pl.BlockSpec((None, wb * 2 * KVH, hd), lambda b, g, pt: (b, 0, 0)),
                   pl.BlockSpec((KVH, R, wb), c(0, 0, 0))])
    f32 = jnp.float32
    return pl.pallas_call(
        functools.partial(_nsa_sample_kernel, past=past, page=page, n_steps=n_steps, n_sb=n_sb, half=half),
        grid_spec=pltpu.PrefetchScalarGridSpec(
            num_scalar_prefetch=1, grid=(B, n_steps), in_specs=in_specs,
            out_specs=pl.BlockSpec((T_PAD, NSA_WIDTH), lambda b, g, pt: (b, 0)),
            scratch_shapes=[pltpu.VMEM((KVH * R, 1), f32), pltpu.VMEM((KVH * R, 1), f32), pltpu.VMEM((KVH * R, hd), f32),
                            pltpu.VMEM((KVH * R, hd), f32), pltpu.VMEM((KVH * R, half), jnp.bfloat16)]),
        out_shape=jax.ShapeDtypeStruct((B * T_PAD, NSA_WIDTH), f32),
        compiler_params=pltpu.CompilerParams(dimension_semantics=("arbitrary",) * 2, vmem_limit_bytes=VMEM_LIMIT_BYTES),
        name="nsa_sample",
    )(page_table.reshape(-1), *([proj_pad] * (KVH + 5)), kvc_t, bc, *([pool2] * SAMPLE_PG), es, bs, bn, win2, bw)


RWKV_PAIR = 2 * RWKV_HEAD_DIM
HIGHEST = lax.Precision.HIGHEST


def _rwkv_kernel_one_pair(r_ref, k_ref, v_ref, xw_ref, xa_ref, xg_ref,
                 sr_ref, sk_ref, sv_ref, sxw_ref, sxa_ref, sxg_ref,
                 mr_ref, mk_ref, mv_ref, mxw_ref, mxa_ref, mxg_ref,
                 s0_ref, w0_ref, a0_ref, kk_ref, ka_ref, rk_ref, lnw_ref, lnb_ref,
                 wup_ref, aup_ref, gup_ref,
                 o_ref, sout_ref,
                 S_ref, cr_ref, ck_ref, cv_ref, cxw_ref, cxa_ref, cxg_ref, *, C, t_valid, n_chunks):
    f32, bf16 = jnp.float32, jnp.bfloat16
    hd, W = RWKV_HEAD_DIM, RWKV_PAIR
    ci = pl.program_id(2)
    lane = lax.broadcasted_iota(jnp.int32, (1, W), 1)
    h0 = lane < hd
    row = lax.broadcasted_iota(jnp.int32, (C, 1), 0)
    valid = (ci * C + row) < t_valid
    carries = ((cr_ref, sr_ref), (ck_ref, sk_ref), (cv_ref, sv_ref), (cxw_ref, sxw_ref), (cxa_ref, sxa_ref),
               (cxg_ref, sxg_ref))

    @pl.when(ci == 0)
    def _():
        S_ref[...] = jnp.zeros((W, W), f32)
        S_ref[0:hd, 0:hd] = s0_ref[0]
        S_ref[hd:W, hd:W] = s0_ref[1]
        for c_ref, s_ref in carries:
            c_ref[0:1, :] = s_ref[...]

    def shifted(x_ref, c_ref, mu_ref):
        x = x_ref[...]
        prev = jnp.where(row == 0, c_ref[0:1, :], pltpu.roll(x, 1, axis=0))
        c_ref[0:1, :] = x[C - 1:C, :]
        return x + (prev - x) * mu_ref[...]

    def seg_sum(x):
        s_lo = jnp.sum(jnp.where(h0, x, 0.0), axis=-1, keepdims=True)
        s_hi = jnp.sum(jnp.where(h0, 0.0, x), axis=-1, keepdims=True)
        return jnp.where(h0, s_lo, s_hi)

    def stack(x):
        return jnp.concatenate([jnp.where(h0, x, 0.0), jnp.where(h0, 0.0, x)], axis=0)

    mr = shifted(r_ref, cr_ref, mr_ref)
    mk = shifted(k_ref, ck_ref, mk_ref)
    mv = shifted(v_ref, cv_ref, mv_ref)
    mxw = shifted(xw_ref, cxw_ref, mxw_ref)
    mxa = shifted(xa_ref, cxa_ref, mxa_ref)
    mxg = shifted(xg_ref, cxg_ref, mxg_ref)

    z = w0_ref[...] + jnp.dot(jnp.tanh(mxw).astype(bf16), wup_ref[...].astype(bf16), preferred_element_type=f32)
    w_log = -jax.nn.softplus(-z) - 0.5
    lw = -jnp.exp(w_log)
    a = jax.nn.sigmoid(a0_ref[...] + jnp.dot(mxa.astype(bf16), aup_ref[...].astype(bf16), preferred_element_type=f32))
    g = jnp.dot(jax.nn.sigmoid(mxg).astype(bf16), gup_ref[...].astype(bf16), preferred_element_type=f32)
    kk = mk * kk_ref[...]
    kk = kk / jnp.maximum(jnp.sqrt(seg_sum(kk * kk)), 1e-12)
    kp = mk * (1.0 + (a - 1.0) * ka_ref[...])
    lw = jnp.where(valid, lw, 0.0)
    a_t = jnp.where(valid, -kk, 0.0)
    b_t = jnp.where(valid, kk * a, 0.0)
    k_t = jnp.where(valid, kp, 0.0)
    v_t = jnp.where(valid, mv, 0.0)

    ti = lax.broadcasted_iota(jnp.int32, (C, C), 0)
    tj = lax.broadcasted_iota(jnp.int32, (C, C), 1)
    cl = jnp.dot((tj <= ti).astype(f32), lw, precision=HIGHEST, preferred_element_type=f32)
    e_neg = jnp.exp(-cl)
    At = a_t * jnp.exp(cl - lw)
    Bt = b_t * e_neg
    Kt = k_t * e_neg
    Rt = mr * jnp.exp(cl)
    G = jnp.concatenate([stack(At), stack(Rt)], axis=0).astype(bf16)
    Z = jnp.concatenate([Bt, Bt, Kt, Kt], axis=0).astype(bf16)
    nt = (((1,), (1,)), ((), ()))
    M = lax.dot_general(G, Z, nt, preferred_element_type=f32)
    ri = lax.broadcasted_iota(jnp.int32, (2 * C, 2 * C), 0)
    cj = lax.broadcasted_iota(jnp.int32, (2 * C, 2 * C), 1)
    same = (ri >= C) == (cj >= C)
    strict = same & ((cj & (C - 1)) < (ri & (C - 1)))
    incl = same & ((cj & (C - 1)) <= (ri & (C - 1)))
    L = jnp.where(strict, M[0:2 * C, 0:2 * C], 0.0)
    AK = jnp.where(strict, M[0:2 * C, 2 * C:4 * C], 0.0)
    RB = jnp.where(incl, M[2 * C:4 * C, 0:2 * C], 0.0)
    RK = jnp.where(incl, M[2 * C:4 * C, 2 * C:4 * C], 0.0)
    S = S_ref[...]
    GS = lax.dot_general(G, S.astype(bf16), nt, preferred_element_type=f32)
    Vs = stack(v_t)
    U = GS[0:2 * C] + jnp.dot(AK.astype(bf16), Vs.astype(bf16), preferred_element_type=f32)
    P = L
    levels = C.bit_length() - 1
    for lvl in range(levels):
        U = U + jnp.dot(P, U, precision=HIGHEST, preferred_element_type=f32)
        if lvl + 1 < levels:
            P = jnp.dot(P, P, precision=HIGHEST, preferred_element_type=f32)
    X = jnp.concatenate([U, Vs], axis=0).astype(bf16)
    Y = GS[2 * C:4 * C] + jnp.dot(jnp.concatenate([RB, RK], axis=1).astype(bf16), X, preferred_element_type=f32)
    y = Y[0:C] + Y[C:2 * C]
    dS = lax.dot_general(X, Z, (((0,), (0,)), ((), ())), preferred_element_type=f32)
    vi = lax.broadcasted_iota(jnp.int32, (W, W), 0)
    kj = lax.broadcasted_iota(jnp.int32, (W, W), 1)
    S_new = jnp.where((vi < hd) == (kj < hd), S + dS, 0.0) * jnp.exp(cl[C - 1:C, :])
    S_ref[...] = S_new

    mean = seg_sum(y) * (1.0 / hd)
    yc = y - mean
    var = seg_sum(yc * yc) * (1.0 / hd)
    yn = yc * lax.rsqrt(var + LNX_EPS) * lnw_ref[...] + lnb_ref[...]
    bonus = seg_sum(mr * kp * rk_ref[...]) * mv
    o_ref[...] = (yn + bonus) * g

    @pl.when(ci == n_chunks - 1)
    def _():
        sout_ref[0] = S_new[0:hd, 0:hd]
        sout_ref[1] = S_new[hd:W, hd:W]


def rwkv_mix_one_pair(proj_p, shift_prev, wkv0, lp, B, T, t_valid, C):
    f32 = jnp.float32
    W, hd = RWKV_PAIR, RWKV_HEAD_DIM
    n_pairs = RWKV_WIDTH // W
    n_chunks = T // C
    assert T % C == 0 and C & (C - 1) == 0 and C >= 8

    def pad_vec(x):
        z = lambda n: jnp.zeros(x.shape[:-1] + (n,), x.dtype)
        return jnp.concatenate([x[..., :RWKV_SPLITS[2]], x[..., RWKV_SPLITS[2]:RWKV_SPLITS[3]], z(LANE - DECAY_LORA),
                                x[..., RWKV_SPLITS[3]:RWKV_SPLITS[4]], z(LANE - AAA_LORA), x[..., RWKV_SPLITS[4]:]], -1)

    shift_p = pad_vec(shift_prev.astype(f32))[:, None, :]
    mu_p = pad_vec(lp['rwkv_mu'].astype(f32))[None, :]
    pad_rows_to = lambda w: jnp.pad(w, ((0, LANE - w.shape[0]), (0, 0)))
    wup, aup = pad_rows_to(lp['rwkv_w_up']), pad_rows_to(lp['rwkv_a_up'])
    vec = lambda x: x.reshape(1, RWKV_WIDTH).astype(f32)
    nw = RWKV_WIDTH // W
    col_r, col_k, col_v = 0, nw, 2 * nw
    col_xw, col_xa, col_xg = P_XW // LANE, P_XA // LANE, P_XG // GATE_LORA
    row_spec = lambda w, cfn: pl.BlockSpec((C, w), lambda b, p, c: (b * n_chunks + c, cfn(p)))
    sh_spec = lambda w, cfn: pl.BlockSpec((None, 1, w), lambda b, p, c: (b, 0, cfn(p)))
    mu_spec = lambda w, cfn: pl.BlockSpec((1, w), lambda b, p, c: (0, cfn(p)))
    cols = [(W, lambda p: col_r + p), (W, lambda p: col_k + p), (W, lambda p: col_v + p),
            (LANE, lambda p: col_xw), (LANE, lambda p: col_xa), (GATE_LORA, lambda p: col_xg)]
    pvec = pl.BlockSpec((1, W), lambda b, p, c: (0, p))
    state_spec = pl.BlockSpec((None, 2, hd, hd), lambda b, p, c: (b, p, 0, 0))
    in_specs = ([row_spec(w, f) for w, f in cols] + [sh_spec(w, f) for w, f in cols] + [mu_spec(w, f) for w, f in cols]
                + [state_spec] + [pvec] * 7
                + [pl.BlockSpec((LANE, W), lambda b, p, c: (0, p)), pl.BlockSpec((LANE, W), lambda b, p, c: (0, p)),
                   pl.BlockSpec((GATE_LORA, W), lambda b, p, c: (0, p))])
    scratch = [pltpu.VMEM((W, W), f32)] + [pltpu.VMEM((8, w), f32) for w, _ in cols]
    o, s_fin = pl.pallas_call(
        functools.partial(_rwkv_kernel, C=C, t_valid=t_valid, n_chunks=n_chunks),
        grid=(B, n_pairs, n_chunks),
        in_specs=in_specs,
        out_specs=[pl.BlockSpec((C, W), lambda b, p, c: (b * n_chunks + c, p)), state_spec],
        out_shape=[jax.ShapeDtypeStruct((B * T, RWKV_WIDTH), f32),
                   jax.ShapeDtypeStruct((B, RWKV_HEADS, hd, hd), f32)],
        scratch_shapes=scratch,
        compiler_params=pltpu.CompilerParams(dimension_semantics=("arbitrary",) * 3, vmem_limit_bytes=VMEM_LIMIT_BYTES),
        name="rwkv_mix",
    )(*([proj_p] * 6), *([shift_p] * 6), *([mu_p] * 6), wkv0.astype(f32),
      vec(lp['rwkv_w0']), vec(lp['rwkv_a0']), vec(lp['rwkv_k_k']), vec(lp['rwkv_k_a']), vec(lp['rwkv_r_k']),
      vec(lp['rwkv_lnx_w']), vec(lp['rwkv_lnx_b']), wup, aup, lp['rwkv_g_up'])
    return o, s_fin


def _dot3(a, b):
    f32, bf16 = jnp.float32, jnp.bfloat16
    ah = a.astype(bf16)
    al = (a - ah.astype(f32)).astype(bf16)
    bh = b.astype(bf16)
    bl = (b - bh.astype(f32)).astype(bf16)
    d = lambda x, y: jnp.dot(x, y, preferred_element_type=f32)
    return d(ah, bh) + (d(ah, bl) + d(al, bh))


def _rwkv_kernel(r_ref, k_ref, v_ref, xw_ref, xa_ref, xg_ref,
                 sr_ref, sk_ref, sv_ref, sxw_ref, sxa_ref, sxg_ref,
                 mr_ref, mk_ref, mv_ref, mxw_ref, mxa_ref, mxg_ref,
                 s0_ref, w0_ref, a0_ref, kk_ref, ka_ref, rk_ref, lnw_ref, lnb_ref,
                 wup_ref, aup_ref, gup_ref,
                 o_ref, sout_ref,
                 S_ref, cr_ref, ck_ref, cv_ref, cxw_ref, cxa_ref, cxg_ref, *, C, NP, t_valid, n_chunks):
    f32, bf16 = jnp.float32, jnp.bfloat16
    hd, W = RWKV_HEAD_DIM, RWKV_PAIR
    ci = pl.program_id(2)
    lane = lax.broadcasted_iota(jnp.int32, (1, W), 1)
    h0 = lane < hd
    row = lax.broadcasted_iota(jnp.int32, (C, 1), 0)
    valid = (ci * C + row) < t_valid
    carries = ((cr_ref, sr_ref), (ck_ref, sk_ref), (cv_ref, sv_ref), (cxw_ref, sxw_ref), (cxa_ref, sxa_ref),
               (cxg_ref, sxg_ref))

    @pl.when(ci == 0)
    def _():
        S_ref[...] = jnp.zeros(S_ref.shape, f32)
        for p in range(NP):
            S_ref[p, 0:hd, 0:hd] = s0_ref[2 * p]
            S_ref[p, hd:W, hd:W] = s0_ref[2 * p + 1]
        for c_ref, s_ref in carries:
            c_ref[0:1, :] = s_ref[...]

    def shifted(x_ref, c_ref, mu_ref):
        x = x_ref[...]
        prev = jnp.where(row == 0, c_ref[0:1, :], pltpu.roll(x, 1, axis=0))
        c_ref[0:1, :] = x[C - 1:C, :]
        return x + (prev - x) * mu_ref[...]

    def seg_sum(x):
        s_lo = jnp.sum(jnp.where(h0, x, 0.0), axis=-1, keepdims=True)
        s_hi = jnp.sum(jnp.where(h0, 0.0, x), axis=-1, keepdims=True)
        return jnp.where(h0, s_lo, s_hi)

    def stack(x):
        return jnp.concatenate([jnp.where(h0, x, 0.0), jnp.where(h0, 0.0, x)], axis=0)

    mr_all = shifted(r_ref, cr_ref, mr_ref)
    mk_all = shifted(k_ref, ck_ref, mk_ref)
    mv_all = shifted(v_ref, cv_ref, mv_ref)
    mxw = shifted(xw_ref, cxw_ref, mxw_ref)
    mxa = shifted(xa_ref, cxa_ref, mxa_ref)
    mxg = shifted(xg_ref, cxg_ref, mxg_ref)

    z_all = w0_ref[...] + jnp.dot(jnp.tanh(mxw).astype(bf16), wup_ref[...].astype(bf16), preferred_element_type=f32)
    a_all = jax.nn.sigmoid(a0_ref[...] + jnp.dot(mxa.astype(bf16), aup_ref[...].astype(bf16), preferred_element_type=f32))
    g_all = jnp.dot(jax.nn.sigmoid(mxg).astype(bf16), gup_ref[...].astype(bf16), preferred_element_type=f32)

    ti = lax.broadcasted_iota(jnp.int32, (C, C), 0)
    tj = lax.broadcasted_iota(jnp.int32, (C, C), 1)
    tri = (tj <= ti).astype(f32)
    ri = lax.broadcasted_iota(jnp.int32, (2 * C, 2 * C), 0)
    cj = lax.broadcasted_iota(jnp.int32, (2 * C, 2 * C), 1)
    same = (ri >= C) == (cj >= C)
    strict = same & ((cj & (C - 1)) < (ri & (C - 1)))
    incl = same & ((cj & (C - 1)) <= (ri & (C - 1)))
    vi = lax.broadcasted_iota(jnp.int32, (W, W), 0)
    kj = lax.broadcasted_iota(jnp.int32, (W, W), 1)
    blockdiag = (vi < hd) == (kj < hd)
    nt = (((1,), (1,)), ((), ()))
    levels = C.bit_length() - 1

    stage = []
    for p in range(NP):
        ls = slice(p * W, (p + 1) * W)
        mr, mk, mv, a, g = mr_all[:, ls], mk_all[:, ls], mv_all[:, ls], a_all[:, ls], g_all[:, ls]
        w_log = -jax.nn.softplus(-z_all[:, ls]) - 0.5
        lw = -jnp.exp(w_log)
        kk = mk * kk_ref[:, ls]
        kk = kk / jnp.maximum(jnp.sqrt(seg_sum(kk * kk)), 1e-12)
        kp = mk * (1.0 + (a - 1.0) * ka_ref[:, ls])
        lw = jnp.where(valid, lw, 0.0)
        a_t = jnp.where(valid, -kk, 0.0)
        b_t = jnp.where(valid, kk * a, 0.0)
        k_t = jnp.where(valid, kp, 0.0)
        v_t = jnp.where(valid, mv, 0.0)

        cl = jnp.dot(tri, lw, precision=HIGHEST, preferred_element_type=f32)
        e_neg = jnp.exp(-cl)
        At = a_t * jnp.exp(cl - lw)
        Bt = b_t * e_neg
        Kt = k_t * e_neg
        Rt = mr * jnp.exp(cl)
        G = jnp.concatenate([stack(At), stack(Rt)], axis=0).astype(bf16)
        Z = jnp.concatenate([Bt, Bt, Kt, Kt], axis=0).astype(bf16)
        M = lax.dot_general(G, Z, nt, preferred_element_type=f32)
        L = jnp.where(strict, M[0:2 * C, 0:2 * C], 0.0)
        AK = jnp.where(strict, M[0:2 * C, 2 * C:4 * C], 0.0)
        RB = jnp.where(incl, M[2 * C:4 * C, 0:2 * C], 0.0)
        RK = jnp.where(incl, M[2 * C:4 * C, 2 * C:4 * C], 0.0)
        S = S_ref[p]
        GS = lax.dot_general(G, S.astype(bf16), nt, preferred_element_type=f32)
        Vs = stack(v_t)
        U = GS[0:2 * C] + jnp.dot(AK.astype(bf16), Vs.astype(bf16), preferred_element_type=f32)
        stage.append(dict(ls=ls, mr=mr, mv=mv, kp=kp, g=g, cl=cl, Z=Z, GS=GS, Vs=Vs, RB=RB, RK=RK, S=S, U=U, P=L))

    for lvl in range(levels):
        for st in stage:
            st['U'] = st['U'] + _dot3(st['P'], st['U'])
            if lvl + 1 < levels:
                st['P'] = _dot3(st['P'], st['P'])

    for p, st in enumerate(stage):
        ls, mr, mv, kp, g, cl, Z, GS, Vs, RB, RK, S, U = (st[n] for n in ('ls', 'mr', 'mv', 'kp', 'g', 'cl', 'Z', 'GS', 'Vs',
                                                                       'RB', 'RK', 'S', 'U'))
        X = jnp.concatenate([U, Vs], axis=0).astype(bf16)
        Y = GS[2 * C:4 * C] + jnp.dot(jnp.concatenate([RB, RK], axis=1).astype(bf16), X, preferred_element_type=f32)
        y = Y[0:C] + Y[C:2 * C]
        dS = lax.dot_general(X, Z, (((0,), (0,)), ((), ())), preferred_element_type=f32)
        S_new = jnp.where(blockdiag, S + dS, 0.0) * jnp.exp(cl[C - 1:C, :])
        S_ref[p] = S_new

        mean = seg_sum(y) * (1.0 / hd)
        yc = y - mean
        var = seg_sum(yc * yc) * (1.0 / hd)
        yn = yc * lax.rsqrt(var + LNX_EPS) * lnw_ref[:, ls] + lnb_ref[:, ls]
        bonus = seg_sum(mr * kp * rk_ref[:, ls]) * mv
        o_ref[:, ls] = (yn + bonus) * g

    @pl.when(ci == n_chunks - 1)
    def _():
        for p in range(NP):
            s_fin = S_ref[p]
            sout_ref[2 * p] = s_fin[0:hd, 0:hd]
            sout_ref[2 * p + 1] = s_fin[hd:W, hd:W]


def rwkv_mix(proj_p, shift_prev, wkv0, lp, B, T, t_valid, C, NP):
    f32 = jnp.float32
    hd = RWKV_HEAD_DIM
    W = NP * RWKV_PAIR
    n_steps = RWKV_WIDTH // W
    n_chunks = T // C
    assert T % C == 0 and C & (C - 1) == 0 and C >= 8 and RWKV_WIDTH % W == 0

    def pad_vec(x):
        z = lambda n: jnp.zeros(x.shape[:-1] + (n,), x.dtype)
        return jnp.concatenate([x[..., :RWKV_SPLITS[2]], x[..., RWKV_SPLITS[2]:RWKV_SPLITS[3]], z(LANE - DECAY_LORA),
                                x[..., RWKV_SPLITS[3]:RWKV_SPLITS[4]], z(LANE - AAA_LORA), x[..., RWKV_SPLITS[4]:]], -1)

    shift_p = pad_vec(shift_prev.astype(f32))[:, None, :]
    mu_p = pad_vec(lp['rwkv_mu'].astype(f32))[None, :]
    pad_rows_to = lambda w: jnp.pad(w, ((0, LANE - w.shape[0]), (0, 0)))
    wup, aup = pad_rows_to(lp['rwkv_w_up']), pad_rows_to(lp['rwkv_a_up'])
    vec = lambda x: x.reshape(1, RWKV_WIDTH).astype(f32)
    nw = RWKV_WIDTH // W
    col_r, col_k, col_v = 0, nw, 2 * nw
    col_xw, col_xa, col_xg = P_XW // LANE, P_XA // LANE, P_XG // GATE_LORA
    row_spec = lambda w, cfn: pl.BlockSpec((C, w), lambda b, p, c: (b * n_chunks + c, cfn(p)))
    sh_spec = lambda w, cfn: pl.BlockSpec((None, 1, w), lambda b, p, c: (b, 0, cfn(p)))
    mu_spec = lambda w, cfn: pl.BlockSpec((1, w), lambda b, p, c: (0, cfn(p)))
    cols = [(W, lambda p: col_r + p), (W, lambda p: col_k + p), (W, lambda p: col_v + p),
            (LANE, lambda p: col_xw), (LANE, lambda p: col_xa), (GATE_LORA, lambda p: col_xg)]
    pvec = pl.BlockSpec((1, W), lambda b, p, c: (0, p))
    state_spec = pl.BlockSpec((None, 2 * NP, hd, hd), lambda b, p, c: (b, p, 0, 0))
    in_specs = ([row_spec(w, f) for w, f in cols] + [sh_spec(w, f) for w, f in cols] + [mu_spec(w, f) for w, f in cols]
                + [state_spec] + [pvec] * 7
                + [pl.BlockSpec((LANE, W), lambda b, p, c: (0, p)), pl.BlockSpec((LANE, W), lambda b, p, c: (0, p)),
                   pl.BlockSpec((GATE_LORA, W), lambda b, p, c: (0, p))])
    scratch = [pltpu.VMEM((NP, RWKV_PAIR, RWKV_PAIR), f32)] + [pltpu.VMEM((8, w), f32) for w, _ in cols]
    o, s_fin = pl.pallas_call(
        functools.partial(_rwkv_kernel, C=C, NP=NP, t_valid=t_valid, n_chunks=n_chunks),
        grid=(B, n_steps, n_chunks),
        in_specs=in_specs,
        out_specs=[pl.BlockSpec((C, W), lambda b, p, c: (b * n_chunks + c, p)), state_spec],
        out_shape=[jax.ShapeDtypeStruct((B * T, RWKV_WIDTH), f32),
                   jax.ShapeDtypeStruct((B, RWKV_HEADS, hd, hd), f32)],
        scratch_shapes=scratch,
        compiler_params=pltpu.CompilerParams(dimension_semantics=("arbitrary",) * 3, vmem_limit_bytes=VMEM_LIMIT_BYTES),
        name="rwkv_mix",
    )(*([proj_p] * 6), *([shift_p] * 6), *([mu_p] * 6), wkv0.astype(f32),
      vec(lp['rwkv_w0']), vec(lp['rwkv_a0']), vec(lp['rwkv_k_k']), vec(lp['rwkv_k_a']), vec(lp['rwkv_r_k']),
      vec(lp['rwkv_lnx_w']), vec(lp['rwkv_lnx_b']), wup, aup, lp['rwkv_g_up'])
    return o, s_fin


def _ffn_up_kernel(be_ref, nv_ref, x_ref, wg_ref, wu_ref, h_ref):
    del be_ref
    i = pl.program_id(0)

    @pl.when(i < nv_ref[0])
    def _():
        x = x_ref[...]
        hg = jnp.dot(x, wg_ref[...].astype(jnp.bfloat16), preferred_element_type=jnp.float32)
        hu = jnp.dot(x, wu_ref[...].astype(jnp.bfloat16), preferred_element_type=jnp.float32)
        h_ref[...] = (jax.nn.silu(hg) * hu).astype(h_ref.dtype)

    @pl.when(i >= nv_ref[0])
    def _():
        h_ref[...] = jnp.zeros(h_ref.shape, h_ref.dtype)


def _ffn_down_kernel(be_ref, nv_ref, h_ref, wd_ref, rw_ref, o_ref):
    del be_ref
    i = pl.program_id(0)

    @pl.when(i < nv_ref[0])
    def _():
        o_ref[...] = jnp.dot(h_ref[...].astype(jnp.bfloat16), wd_ref[...].astype(jnp.bfloat16),
                             preferred_element_type=jnp.float32) * rw_ref[...]

    @pl.when(i >= nv_ref[0])
    def _():
        o_ref[...] = jnp.zeros(o_ref.shape, o_ref.dtype)


def expert_ffn(xs, w_gate, w_up, w_down, row_w, blk_exp, n_valid, blk):
    rows, d = xs.shape
    ff = w_gate.shape[-1]
    n_blocks = rows // blk
    params = pltpu.CompilerParams(dimension_semantics=("arbitrary",), vmem_limit_bytes=VMEM_LIMIT_BYTES)
    h = pl.pallas_call(
        _ffn_up_kernel,
        grid_spec=pltpu.PrefetchScalarGridSpec(
            num_scalar_prefetch=2, grid=(n_blocks,),
            in_specs=[pl.BlockSpec((blk, d), lambda i, be, nv: (i, 0)),
                      pl.BlockSpec((None, d, ff), lambda i, be, nv: (be[i], 0, 0)),
                      pl.BlockSpec((None, d, ff), lambda i, be, nv: (be[i], 0, 0))],
            out_specs=pl.BlockSpec((blk, ff), lambda i, be, nv: (i, 0))),
        out_shape=jax.ShapeDtypeStruct((rows, ff), jnp.bfloat16),
        compiler_params=params, name="ffn_up",
    )(blk_exp, n_valid, xs, w_gate, w_up)
    return pl.pallas_call(
        _ffn_down_kernel,
        grid_spec=pltpu.PrefetchScalarGridSpec(
            num_scalar_prefetch=2, grid=(n_blocks,),
            in_specs=[pl.BlockSpec((blk, ff), lambda i, be, nv: (i, 0)),
                      pl.BlockSpec((None, ff, d), lambda i, be, nv: (be[i], 0, 0)),
                      pl.BlockSpec((blk, 1), lambda i, be, nv: (i, 0))],
            out_specs=pl.BlockSpec((blk, d), lambda i, be, nv: (i, 0))),
        out_shape=jax.ShapeDtypeStruct((rows, d), jnp.float32),
        compiler_params=params, name="ffn_down",
    )(blk_exp, n_valid, h, w_down, row_w)


def _router_kernel(x_ref, wt_ref, b_ref, e_ref, w_ref):
    f32 = jnp.float32
    tm = x_ref.shape[0]
    gs = N_EXPERTS // N_GROUPS
    logits = lax.dot_general(wt_ref[...].astype(jnp.bfloat16), x_ref[...].astype(jnp.bfloat16),
                             (((1,), (1,)), ((), ())), preferred_element_type=f32)
    scores = jax.nn.sigmoid(logits)
    biased = scores + b_ref[...]
    g3 = biased.reshape(N_GROUPS, gs, tm)
    it = lax.broadcasted_iota(jnp.int32, (N_GROUPS, gs, tm), 1)
    m1 = jnp.max(g3, axis=1, keepdims=True)
    i1 = jnp.min(jnp.where(g3 == m1, it, gs), axis=1, keepdims=True)
    m2 = jnp.max(jnp.where(it == i1, -jnp.inf, g3), axis=1, keepdims=True)
    gscore = (m1 + m2).reshape(N_GROUPS, tm)
    gi = lax.broadcasted_iota(jnp.int32, (N_GROUPS, tm), 0)
    grank = jnp.zeros((N_GROUPS, tm), jnp.int32)
    for j in range(N_GROUPS):
        sj = gscore[j:j + 1, :]
        grank = grank + ((sj > gscore) | ((sj == gscore) & (j < gi))).astype(jnp.int32)
    gmask = (grank < TOPK_GROUPS).reshape(N_GROUPS, 1, tm)
    masked = jnp.where(gmask, g3, -jnp.inf).reshape(N_EXPERTS, tm)
    ei = lax.broadcasted_iota(jnp.int32, (N_EXPERTS, tm), 0)
    idxs, wts = [], []
    for _ in range(TOP_K):
        m = jnp.max(masked, axis=0, keepdims=True)
        idx = jnp.min(jnp.where(masked == m, ei, N_EXPERTS), axis=0, keepdims=True)
        hit = ei == idx
        idxs.append(idx)
        wts.append(jnp.sum(jnp.where(hit, scores, 0.0), axis=0, keepdims=True))
        masked = jnp.where(hit, -jnp.inf, masked)
    w = jnp.concatenate(wts, axis=0)
    e_ref[...] = jnp.concatenate(idxs, axis=0)
    w_ref[...] = w / jnp.sum(w, axis=0, keepdims=True) * ROUTED_SCALE


def moe_route(x, router_w, router_bias):
    n, d = x.shape
    tm = min(512, n)
    assert n % tm == 0
    e, w = pl.pallas_call(
        _router_kernel,
        grid=(n // tm,),
        in_specs=[pl.BlockSpec((tm, d), lambda i: (i, 0)),
                  pl.BlockSpec((N_EXPERTS, d), lambda i: (0, 0)),
                  pl.BlockSpec((N_EXPERTS, 1), lambda i: (0, 0))],
        out_specs=[pl.BlockSpec((TOP_K, tm), lambda i: (0, i)), pl.BlockSpec((TOP_K, tm), lambda i: (0, i))],
        out_shape=[jax.ShapeDtypeStruct((TOP_K, n), jnp.int32), jax.ShapeDtypeStruct((TOP_K, n), jnp.float32)],
        compiler_params=pltpu.CompilerParams(dimension_semantics=("arbitrary",), vmem_limit_bytes=VMEM_LIMIT_BYTES),
        name="moe_route",
    )(x, router_w.T, router_bias.astype(jnp.float32).reshape(N_EXPERTS, 1))
    return e.T, w.T


def moe_ffn_pallas(h, lp, blk):
    B, T, D = h.shape
    x = h.reshape(B * T, D)
    n = x.shape[0]
    f32 = jnp.float32
    eidx, w = moe_route(x, lp['router_w'], lp['router_bias'])

    nk = n * TOP_K
    flat_e = eidx.reshape(nk)
    order = jnp.argsort(flat_e)
    se = flat_e[order]
    counts = jnp.bincount(flat_e, length=N_EXPERTS)
    padded = (counts + blk - 1) // blk * blk
    pad_end = jnp.cumsum(padded)
    dest = (pad_end - padded)[se] + jnp.arange(nk) - (jnp.cumsum(counts) - counts)[se]
    n_blocks = -(-(nk + N_EXPERTS * (blk - 1)) // blk)
    rows = n_blocks * blk
    row_tok = jnp.zeros((rows,), jnp.int32).at[dest].set((order // TOP_K).astype(jnp.int32))
    row_w = jnp.zeros((rows,), f32).at[dest].set(w.reshape(nk)[order])
    blk_exp = jnp.minimum(jnp.searchsorted(pad_end, jnp.arange(n_blocks) * blk, side='right'),
                          N_EXPERTS - 1).astype(jnp.int32)
    n_valid = (pad_end[-1] // blk).astype(jnp.int32).reshape(1)
    pos = jnp.zeros((nk,), jnp.int32).at[order].set(dest.astype(jnp.int32)).reshape(n, TOP_K)

    xb = x.astype(jnp.bfloat16)
    ys = expert_ffn(xb[row_tok], lp['exp_w_gate'], lp['exp_w_up'], lp['exp_w_down'], row_w[:, None], blk_exp,
                    n_valid, blk)
    routed = ys[pos].sum(axis=1)
    sblk = min(256, n)
    one = jnp.ones((n, 1), f32)
    shared = expert_ffn(xb, lp['sh_w_gate'][None], lp['sh_w_up'][None], lp['sh_w_down'][None], one,
                        jnp.zeros((n // sblk,), jnp.int32), jnp.full((1,), n // sblk, jnp.int32), sblk)
    return (routed + shared).reshape(B, T, D)


def pad_rows(a, length):
    return jnp.pad(a, [(0, 0), (0, length - a.shape[1])] + [(0, 0)] * (a.ndim - 2))


def rwkv7_mix(pr, shift_prev, wkv0, lp):
    B, T, _ = pr.shape
    f32 = jnp.float32
    prev = jnp.concatenate([shift_prev[:, None, :].astype(pr.dtype), pr[:, :-1]], axis=1)
    m = pr + (prev - pr) * lp['rwkv_mu']
    r, k, v, xw, xa, xg = jnp.split(m, RWKV_SPLITS, axis=-1)
    w_log = -jax.nn.softplus(-(lp['rwkv_w0'] + jnp.tanh(xw) @ lp['rwkv_w_up']).astype(f32)) - 0.5
    decay = jnp.exp(-jnp.exp(w_log))
    a = jax.nn.sigmoid((lp['rwkv_a0'] + xa @ lp['rwkv_a_up']).astype(f32))
    g = jax.nn.sigmoid(xg) @ lp['rwkv_g_up']

    def heads(t):
        return t.astype(f32).reshape(B, T, RWKV_HEADS, RWKV_HEAD_DIM)

    r, k, v, decay, a = heads(r), heads(k), heads(v), heads(decay), heads(a)
    kk = k * lp['rwkv_k_k'].astype(f32).reshape(RWKV_HEADS, RWKV_HEAD_DIM)
    kk = kk / jnp.maximum(jnp.linalg.norm(kk, axis=-1, keepdims=True), 1e-12)
    k = k * (1.0 + (a - 1.0) * lp['rwkv_k_a'].astype(f32).reshape(RWKV_HEADS, RWKV_HEAD_DIM))

    def step(S, inp):
        r_t, w_t, k_t, v_t, a_t, b_t = inp
        sa = jnp.einsum('bhvk,bhk->bhv', S, a_t)
        S = S * w_t[:, :, None, :] + sa[..., None] * b_t[:, :, None, :] + v_t[..., None] * k_t[:, :, None, :]
        return S, jnp.einsum('bhvk,bhk->bhv', S, r_t)

    xs = tuple(jnp.moveaxis(t, 1, 0) for t in (r, decay, k, v, -kk, kk * a))
    s_final, ys = lax.scan(step, wkv0.astype(f32), xs)
    y = jnp.moveaxis(ys, 0, 1)
    mu = jnp.mean(y, axis=-1, keepdims=True)
    var = jnp.mean(jnp.square(y - mu), axis=-1, keepdims=True)
    y = ((y - mu) * lax.rsqrt(var + LNX_EPS)).reshape(B, T, RWKV_WIDTH)
    y = y * lp['rwkv_lnx_w'].astype(f32) + lp['rwkv_lnx_b'].astype(f32)
    bonus = (jnp.sum(r * k * lp['rwkv_r_k'].astype(f32), axis=-1, keepdims=True) * v).reshape(B, T, RWKV_WIDTH)
    out = ((y + bonus) * g.astype(f32)).astype(pr.dtype)
    return out, pr[:, -1], s_final.astype(wkv0.dtype)


def compress_blocks(rows, pe, w):
    B, L = rows.shape[:2]
    blk = rows.reshape(B, L // CMP_BLOCK, CMP_BLOCK, 2, NSA_KV_HEADS, NSA_HEAD_DIM)
    blk = blk + jnp.swapaxes(pe, 0, 1)[None, None, :, :, None, :]
    return jnp.einsum('bncjkd,jcde->bnjke', blk, w)


def nsa_attend(q, q_pos, gates, kc, vc, c_end, n_sel_blocks, gather_sel, kw, vw, w_pos, rel_bias):
    B, Tq = q.shape[:2]
    f32 = jnp.float32
    scale = NSA_HEAD_DIM ** -0.5
    qg = q.reshape(B, Tq, NSA_KV_HEADS, NSA_GROUP, NSA_HEAD_DIM)
    bias_tab = rel_bias.astype(f32).reshape(NUM_BUCKETS, NSA_KV_HEADS, NSA_GROUP)

    dc = q_pos[:, None] - c_end[None, :]
    bias_c = bias_tab[rel_bucket(dc)].transpose(0, 2, 3, 1)
    lc = jnp.einsum('btkgd,bnkd->btkgn', qg, kc).astype(f32) * scale + bias_c
    pc = masked_softmax(lc, (dc >= 0)[:, None, None, :])
    oc = jnp.einsum('btkgn,bnkd->btkgd', pc.astype(vc.dtype), vc)

    imp = pc.sum(3).reshape(B, Tq, NSA_KV_HEADS, n_sel_blocks, SEL_BLOCK // CMP_BLOCK).sum(-1)
    blk = jnp.arange(n_sel_blocks)
    cur = (q_pos // SEL_BLOCK)[:, None]
    force = (blk == 0) | (blk == cur) | (blk == cur - 1)
    score = jnp.where(force[:, None, :], FORCE_SCORE, imp)
    score = jnp.where((blk <= cur)[:, None, :], score, -jnp.inf)
    _, idx = lax.top_k(jnp.moveaxis(score, 2, 1), min(N_SEL, n_sel_blocks))

    sel = gather_sel(idx)
    ks, vs = sel[..., 0, :], sel[..., 1, :]
    spos = idx[..., None] * SEL_BLOCK + jnp.arange(SEL_BLOCK)
    ds = q_pos[None, None, :, None, None] - spos
    hi = jnp.arange(NSA_KV_HEADS)[None, :, None, None, None]
    bias_s = jnp.moveaxis(bias_tab[rel_bucket(ds), hi], -1, 3)
    ls = jnp.einsum('btkgd,bktjsd->bktgjs', qg, ks).astype(f32) * scale + bias_s
    ms = (ds >= 0)[:, :, :, None]
    ps = masked_softmax(ls.reshape(ls.shape[:4] + (-1,)), ms.reshape(ms.shape[:4] + (-1,))).reshape(ls.shape)
    osel = jnp.einsum('bktgjs,bktjsd->btkgd', ps.astype(vs.dtype), vs)

    dw = q_pos[:, None] - w_pos[None, :]
    mw = (dw >= 0) & (dw <= WINDOW) & (w_pos >= 0)[None, :]
    bias_w = bias_tab[rel_bucket(dw)].transpose(0, 2, 3, 1)
    lw = jnp.einsum('btkgd,blkd->btkgl', qg, kw).astype(f32) * scale + bias_w
    pw = masked_softmax(lw, mw[:, None, None, :])
    ow = jnp.einsum('btkgl,blkd->btkgd', pw.astype(vw.dtype), vw)

    gt = gates.reshape(B, Tq, NSA_KV_HEADS, NSA_GROUP, 3)
    o = gt[..., 0:1] * oc + gt[..., 1:2] * osel + gt[..., 2:3] * ow
    return o.reshape(B, Tq, NSA_WIDTH)


def split_proj(h, lp):
    B, T, _ = h.shape
    proj = mm3(h, lp['w_in'])
    pr = proj[..., :RWKV_PROJ]
    pn = proj[..., RWKV_PROJ:]
    q = pn[..., :NSA_WIDTH].reshape(B, T, NSA_HEADS, NSA_HEAD_DIM)
    kv = pn[..., NSA_WIDTH:NSA_WIDTH + 3 * NSA_KV_WIDTH].reshape(B, T, 3, 2, NSA_KV_HEADS, NSA_HEAD_DIM)
    gates = jax.nn.sigmoid(pn[..., NSA_WIDTH + 3 * NSA_KV_WIDTH:].astype(jnp.float32))
    gates = gates.reshape(B, T, NSA_HEADS, 3).astype(h.dtype)
    return pr, q, kv[:, :, 0], kv[:, :, 1], kv[:, :, 2], gates


def merge_groups(o_rwkv, o_nsa, lp):
    o = jnp.concatenate([o_rwkv, rms_norm(o_nsa, lp['nsa_out_g'])], axis=-1).astype(jnp.bfloat16)
    return mm3(o, lp['w_out'])


def mixer_prompt(h, lp, rel_bias):
    B, T, d = h.shape
    proj_p = pallas_matmul(h.reshape(B * T, d), lp['w_in_p'])
    o_r, wkv = rwkv_mix(proj_p, jnp.zeros((B, RWKV_PROJ), h.dtype),
                        jnp.zeros((B, RWKV_HEADS, RWKV_HEAD_DIM, RWKV_HEAD_DIM), jnp.float32), lp, B, T, T, 64, 8)
    o_r = o_r.reshape(B, T, RWKV_WIDTH)
    shift_last = unpad_rwkv(proj_p.reshape(B, T, P_TOTAL)[:, -1])
    kvc = compress_prompt(proj_p, lp['cmp_pe'], lp['cmp_w'], B, T)
    o_n = nsa_prompt(proj_p, kvc, rel_bias, B, T).reshape(B, T, NSA_WIDTH)
    kv = proj_p[:, P_KVC:P_GATES].reshape(B, T, 3, 2, NSA_KV_HEADS, NSA_HEAD_DIM)
    kv_c, kv_s, kv_w = kv[:, :, 0], kv[:, :, 1], kv[:, :, 2]
    win_keep = min(WINDOW, T)
    return merge_groups(o_r, o_n, lp), (kv_c, kv_s, kv_w[:, T - win_keep:], wkv, shift_last)


def mixer_prompt_gather(h, lp, rel_bias):
    B, T, _ = h.shape
    pr, q, kv_c, kv_s, kv_w, gates = split_proj(h, lp)
    o_r, shift_last, wkv = rwkv7_mix(pr, jnp.zeros((B, RWKV_PROJ), h.dtype),
                                     jnp.zeros((B, RWKV_HEADS, RWKV_HEAD_DIM, RWKV_HEAD_DIM), jnp.float32), lp)
    lp_len = -(-T // SEL_BLOCK) * SEL_BLOCK
    comp = compress_blocks(pad_rows(kv_c, lp_len), lp['cmp_pe'], lp['cmp_w'])
    kc, vc = comp[:, :, 0], comp[:, :, 1]
    c_end = jnp.arange(lp_len // CMP_BLOCK) * CMP_BLOCK + (CMP_BLOCK - 1)
    n_sb = lp_len // SEL_BLOCK
    sel_blocks = pad_rows(kv_s, lp_len).reshape(B, n_sb, SEL_BLOCK, 2, NSA_KV_HEADS, NSA_HEAD_DIM)
    bi = jnp.arange(B)[:, None, None, None]
    hi = jnp.arange(NSA_KV_HEADS)[None, :, None, None]

    def gather_sel(idx):
        return sel_blocks[bi, idx, :, :, hi, :]

    kw_pad = jnp.pad(kv_w, ((0, 0), (WINDOW, 0), (0, 0), (0, 0), (0, 0)))

    def chunk(i):
        t0 = i * Q_CHUNK
        qc = lax.dynamic_slice_in_dim(q, t0, Q_CHUNK, axis=1)
        gc = lax.dynamic_slice_in_dim(gates, t0, Q_CHUNK, axis=1)
        wc = lax.dynamic_slice_in_dim(kw_pad, t0, WINDOW + Q_CHUNK, axis=1)
        q_pos = t0 + jnp.arange(Q_CHUNK)
        w_pos = t0 - WINDOW + jnp.arange(WINDOW + Q_CHUNK)
        return nsa_attend(qc, q_pos, gc, kc, vc, c_end, n_sb, gather_sel,
                          wc[:, :, 0], wc[:, :, 1], w_pos, rel_bias)

    o_n = lax.map(chunk, jnp.arange(T // Q_CHUNK))
    o_n = jnp.moveaxis(o_n, 0, 1).reshape(B, T, NSA_WIDTH)
    win_keep = min(WINDOW, T)
    return merge_groups(o_r, o_n, lp), (kv_c, kv_s, kv_w[:, T - win_keep:], wkv, shift_last)


def mixer_sample(h, lp, rel_bias, pool_cmp, pool_sel, win_buf, wkv0, shift0, page_table):
    B, T, _ = h.shape
    n_pages = page_table.shape[1]
    page = pool_cmp.shape[1]
    past = n_pages * page
    d = h.shape[-1]
    proj_p = pallas_matmul(h.reshape(B * T, d), lp['w_in_p']).reshape(B, T, P_TOTAL)
    import numpy as np
    kv = proj_p[..., P_KVC:P_GATES].reshape(B, T, 3, 2, NSA_KV_HEADS, NSA_HEAD_DIM)
    kv_c, kv_s, kv_w = kv[:, :, 0], kv[:, :, 1], kv[:, :, 2]
    proj_pad = jnp.pad(proj_p, ((0, 0), (0, T_PAD - T), (0, 0))).reshape(B * T_PAD, P_TOTAL)
    o_r, wkv = rwkv_mix(proj_pad, shift0, wkv0, lp, B, T_PAD, T, T_PAD, 4)
    o_r = o_r.reshape(B, T_PAD, RWKV_WIDTH)[:, :T]
    shift_last = unpad_rwkv(proj_p[:, -1])
    lp_len = -(-(past + T) // SEL_BLOCK) * SEL_BLOCK

    jk, hd = 2 * NSA_KV_HEADS, NSA_HEAD_DIM
    kvc_past = compress_paged(pool_cmp, page_table, lp['cmp_pe'], lp['cmp_w'])
    new_len = lp_len - past
    new_rows = pad_rows(proj_p[..., P_KVC:P_KVS], new_len).reshape(B * new_len, NSA_KV_WIDTH)
    kvc_new = compress_prompt(new_rows, lp['cmp_pe'], lp['cmp_w'], B, new_len, col0=0)
    nb_new = new_len // CMP_BLOCK
    assert nb_new <= 2
    kvc_new = kvc_new.reshape(B, jk, nb_new, hd).transpose(0, 2, 1, 3)
    kvc_all = jnp.concatenate([kvc_past, kvc_new], axis=1)
    nc = lp_len // CMP_BLOCK
    n_half = nc // 2
    half = -(-n_half // LANE) * LANE
    idx = np.zeros((2 * half,), np.int32)
    idx[:n_half] = np.arange(0, nc, 2)
    idx[half:half + n_half] = np.arange(1, nc, 2)
    kvc_t = kvc_all[:, idx].transpose(0, 2, 1, 3)
    o_n = nsa_sample(proj_pad, kvc_t, pool_sel, win_buf, page_table, rel_bias, B, T)
    o_n = o_n.reshape(B, T_PAD, NSA_WIDTH)[:, :T]
    new_win = jnp.concatenate([win_buf, kv_w], axis=1)[:, T:]
    return merge_groups(o_r, o_n, lp), (kv_c, kv_s, new_win, wkv, shift_last)


def moe_ffn(h, lp):
    B, T, D = h.shape
    x = h.reshape(B * T, D)
    n = x.shape[0]
    f32 = jnp.float32
    scores = jax.nn.sigmoid((x @ lp['router_w']).astype(f32))
    biased = scores + lp['router_bias'].astype(f32)
    grp = biased.reshape(n, N_GROUPS, N_EXPERTS // N_GROUPS)
    grp_score = lax.top_k(grp, 2)[0].sum(-1)
    _, gidx = lax.top_k(grp_score, TOPK_GROUPS)
    gmask = jax.nn.one_hot(gidx, N_GROUPS, dtype=f32).sum(1)
    emask = jnp.repeat(gmask, N_EXPERTS // N_GROUPS, axis=1) > 0
    _, eidx = lax.top_k(jnp.where(emask, biased, -jnp.inf), TOP_K)
    w = jnp.take_along_axis(scores, eidx, axis=1)
    w = w / jnp.sum(w, axis=-1, keepdims=True) * ROUTED_SCALE

    nk = n * TOP_K
    flat_e = eidx.reshape(nk)
    order = jnp.argsort(flat_e)
    se = flat_e[order]
    counts = jnp.bincount(flat_e, length=N_EXPERTS)
    padded = (counts + MOE_BLOCK - 1) // MOE_BLOCK * MOE_BLOCK
    pad_end = jnp.cumsum(padded)
    dest = (pad_end - padded)[se] + jnp.arange(nk) - (jnp.cumsum(counts) - counts)[se]
    n_blocks = -(-(nk + N_EXPERTS * (MOE_BLOCK - 1)) // MOE_BLOCK)
    rows = n_blocks * MOE_BLOCK
    row_tok = jnp.zeros((rows,), jnp.int32).at[dest].set((order // TOP_K).astype(jnp.int32))
    row_w = jnp.zeros((rows,), f32).at[dest].set(w.reshape(nk)[order])
    blk_exp = jnp.minimum(jnp.searchsorted(pad_end, jnp.arange(n_blocks) * MOE_BLOCK, side='right'),
                          N_EXPERTS - 1)

    def expert_block(args):
        tok, wt, e = args
        xb = x[tok]
        hb = jax.nn.silu(xb @ lp['exp_w_gate'][e]) * (xb @ lp['exp_w_up'][e])
        return (hb @ lp['exp_w_down'][e]).astype(f32) * wt[:, None]

    out = lax.map(expert_block, (row_tok.reshape(n_blocks, MOE_BLOCK), row_w.reshape(n_blocks, MOE_BLOCK), blk_exp))
    routed = jnp.zeros((n, D), f32).at[row_tok].add(out.reshape(rows, D))
    shared = (jax.nn.silu(x @ lp['sh_w_gate']) * (x @ lp['sh_w_up'])) @ lp['sh_w_down']
    return (routed + shared.astype(f32)).astype(h.dtype).reshape(B, T, D)


def trunk_layer(x, mod, lp, mixer):
    B, _, D = x.shape
    mod = mod.reshape(B, 6, 1, D)
    sh1, sc1, gt1, sh2, sc2, gt2 = (mod[:, i] for i in range(6))
    g = lp['norm_g']
    h = (rms_norm(x, g[0]) * (1 + sc1) + sh1).astype(jnp.bfloat16)
    o, st = mixer(h)
    x = x + gt1 * rms_norm(o, g[1])
    h = (rms_norm(x, g[2]) * (1 + sc2) + sh2).astype(jnp.bfloat16)
    x = x + gt2 * rms_norm(moe_ffn_pallas(h, lp, 256 if x.shape[0] * x.shape[1] >= 4096 else 32), g[3])
    return x, st


def kernel(x_prompt, x_sample, c_prompt, c_sample, cache_cmp, cache_sel, state_win, state_wkv,
           state_shift, page_table, rel_bias, w_ada, b_ada, norm_g, w_in, w_out, rwkv_mu, rwkv_w0,
           rwkv_w_up, rwkv_a0, rwkv_a_up, rwkv_g_up, rwkv_k_k, rwkv_k_a, rwkv_r_k, rwkv_lnx_w,
           rwkv_lnx_b, cmp_pe, cmp_w, nsa_out_g, router_w, router_bias, exp_w_gate, exp_w_up,
           exp_w_down, sh_w_gate, sh_w_up, sh_w_down):
    l = 0
    lp = dict(w_ada=w_ada[l], b_ada=b_ada[l], norm_g=norm_g[l], w_in=w_in[l], w_out=w_out[l],
              rwkv_mu=rwkv_mu[l], rwkv_w0=rwkv_w0[l], rwkv_w_up=rwkv_w_up[l], rwkv_a0=rwkv_a0[l],
              rwkv_a_up=rwkv_a_up[l], rwkv_g_up=rwkv_g_up[l], rwkv_k_k=rwkv_k_k[l],
              rwkv_k_a=rwkv_k_a[l], rwkv_r_k=rwkv_r_k[l], rwkv_lnx_w=rwkv_lnx_w[l],
              rwkv_lnx_b=rwkv_lnx_b[l], cmp_pe=cmp_pe[l], cmp_w=cmp_w[l], nsa_out_g=nsa_out_g[l],
              router_w=router_w[l], router_bias=router_bias[l], exp_w_gate=exp_w_gate[l],
              exp_w_up=exp_w_up[l], exp_w_down=exp_w_down[l], sh_w_gate=sh_w_gate[l],
              sh_w_up=sh_w_up[l], sh_w_down=sh_w_down[l])
    lp['w_in_p'] = pad_w_in(w_in[l])
    nb_p = x_prompt.shape[0]
    mod = pallas_matmul(jax.nn.silu(jnp.concatenate([c_prompt, c_sample], axis=0)), lp['w_ada']) + lp['b_ada']
    yp, st_p = trunk_layer(x_prompt, mod[:nb_p], lp, lambda h: mixer_prompt(h, lp, rel_bias))
    ys, st_s = trunk_layer(x_sample, mod[nb_p:], lp, lambda h: mixer_sample(
        h, lp, rel_bias, cache_cmp[l], cache_sel[l], state_win[l], state_wkv[l], state_shift[l], page_table))
    p_cmp, p_sel, p_win, p_wkv, p_shift = [a[None] for a in st_p]
    s_cmp, s_sel, s_win, s_wkv, s_shift = [a[None] for a in st_s]
    return (yp, ys, p_cmp, p_sel, p_win, p_wkv, p_shift, s_cmp, s_sel, s_win, s_wkv, s_shift)
```

```python
import functools
import math

import jax
import jax.numpy as jnp
from jax import lax
from jax.experimental import pallas as pl
from jax.experimental.pallas import tpu as pltpu

D_MODEL = 4096
RWKV_WIDTH = D_MODEL // 2
NSA_WIDTH = D_MODEL - RWKV_WIDTH
RWKV_HEAD_DIM = 64
RWKV_HEADS = RWKV_WIDTH // RWKV_HEAD_DIM
DECAY_LORA = 96
AAA_LORA = 96
GATE_LORA = 256
RWKV_PROJ = 3 * RWKV_WIDTH + DECAY_LORA + AAA_LORA + GATE_LORA
RWKV_SPLITS = (RWKV_WIDTH, 2 * RWKV_WIDTH, 3 * RWKV_WIDTH, 3 * RWKV_WIDTH + DECAY_LORA,
               3 * RWKV_WIDTH + DECAY_LORA + AAA_LORA)
LNX_EPS = 64e-5
NSA_HEAD_DIM = 128
NSA_HEADS = NSA_WIDTH // NSA_HEAD_DIM
NSA_KV_HEADS = 4
NSA_GROUP = NSA_HEADS // NSA_KV_HEADS
CMP_BLOCK = 32
SEL_BLOCK = 64
N_SEL = 16
WINDOW = 512
Q_CHUNK = 32
FORCE_SCORE = 1e4
NSA_KV_WIDTH = 2 * NSA_KV_HEADS * NSA_HEAD_DIM
NUM_BUCKETS = 32
MAX_DISTANCE = 128
N_EXPERTS = 64
N_GROUPS = 8
TOPK_GROUPS = 4
TOP_K = 8
ROUTED_SCALE = 2.5
MOE_BLOCK = 128
RMS_EPS = 1e-6

VMEM_LIMIT_BYTES = 56 * 1024 * 1024


def _matmul_kernel(x_ref, w_ref, o_ref):
    o_ref[...] = jnp.dot(x_ref[...].astype(jnp.bfloat16), w_ref[...].astype(jnp.bfloat16),
                         preferred_element_type=jnp.float32)


def pallas_matmul(x, w, tn=512):
    m, k = x.shape
    n = w.shape[1]
    row_mult = 16 if x.dtype == jnp.bfloat16 else 8
    mp = -(-m // row_mult) * row_mult
    tm = min(1024 if x.dtype == jnp.bfloat16 else 512, mp)
    mp = -(-mp // tm) * tm
    np_ = -(-n // tn) * tn
    if mp != m:
        x = jnp.pad(x, ((0, mp - m), (0, 0)))
    if np_ != n:
        w = jnp.pad(w, ((0, 0), (0, np_ - n)))
    out = pl.pallas_call(
        _matmul_kernel,
        grid=(mp // tm, np_ // tn),
        in_specs=[pl.BlockSpec((tm, k), lambda i, j: (i, 0)),
                  pl.BlockSpec((k, tn), lambda i, j: (0, j))],
        out_specs=pl.BlockSpec((tm, tn), lambda i, j: (i, j)),
        out_shape=jax.ShapeDtypeStruct((mp, np_), jnp.float32),
        compiler_params=pltpu.CompilerParams(
            dimension_semantics=("arbitrary", "arbitrary"), vmem_limit_bytes=VMEM_LIMIT_BYTES),
        name="matmul",
    )(x, w)
    return out[:m, :n]


def mm3(h, w):
    b, t, d = h.shape
    return pallas_matmul(h.reshape(b * t, d), w).reshape(b, t, w.shape[1])


def rms_norm(x, g):
    xf = x.astype(jnp.float32)
    y = xf * lax.rsqrt(jnp.mean(xf * xf, axis=-1, keepdims=True) + RMS_EPS)
    return (y * g.astype(jnp.float32)).astype(x.dtype)


def rel_bucket(dist):
    max_exact = NUM_BUCKETS // 2
    n = jnp.maximum(dist, 0)
    nf = jnp.maximum(n, max_exact).astype(jnp.float32)
    large = max_exact + (jnp.log(nf / max_exact) / math.log(MAX_DISTANCE / max_exact)
                         * (NUM_BUCKETS - max_exact)).astype(jnp.int32)
    return jnp.where(n < max_exact, n, jnp.minimum(large, NUM_BUCKETS - 1))


def masked_softmax(logits, mask):
    logits = jnp.where(mask, logits.astype(jnp.float32), -jnp.inf)
    m = jnp.max(logits, axis=-1, keepdims=True)
    p = jnp.exp(logits - jnp.where(jnp.isfinite(m), m, 0.0))
    s = jnp.sum(p, axis=-1, keepdims=True)
    return p / jnp.where(s > 0, s, 1.0)


LANE = 128
P_XW = 3 * RWKV_WIDTH
P_XA = P_XW + LANE
P_XG = P_XA + LANE
P_Q = P_XG + GATE_LORA
P_KVC = P_Q + NSA_WIDTH
P_KVS = P_KVC + NSA_KV_WIDTH
P_KVW = P_KVS + NSA_KV_WIDTH
P_GATES = P_KVW + NSA_KV_WIDTH
P_TOTAL = P_GATES + NSA_KV_HEADS * LANE
N_GATE_COLS = 3 * NSA_GROUP


def pad_w_in(w_in):
    d = w_in.shape[0]
    z = lambda n: jnp.zeros((d, n), w_in.dtype)
    o_xw, o_xa, o_xg = RWKV_SPLITS[2], RWKV_SPLITS[3], RWKV_SPLITS[4]
    o_g = RWKV_PROJ + NSA_WIDTH + 3 * NSA_KV_WIDTH
    parts = [w_in[:, :o_xw], w_in[:, o_xw:o_xa], z(LANE - DECAY_LORA), w_in[:, o_xa:o_xg], z(LANE - AAA_LORA),
             w_in[:, o_xg:o_g]]
    for kh in range(NSA_KV_HEADS):
        parts += [w_in[:, o_g + kh * N_GATE_COLS:o_g + (kh + 1) * N_GATE_COLS], z(LANE - N_GATE_COLS)]
    return jnp.concatenate(parts, axis=1).astype(jnp.bfloat16)


def unpad_rwkv(proj_p):
    return jnp.concatenate([proj_p[..., :P_XW], proj_p[..., P_XW:P_XW + DECAY_LORA],
                            proj_p[..., P_XA:P_XA + AAA_LORA], proj_p[..., P_XG:P_XG + GATE_LORA]], axis=-1)


def bias_lookup(tab, idx):
    sel = jnp.asarray(idx.reshape(-1, 1), jnp.int32) == jnp.arange(tab.shape[0], dtype=jnp.int32)[None, :]
    out = jnp.dot(sel.astype(jnp.float32), tab, precision=lax.Precision.HIGHEST)
    return out.reshape(idx.shape + (tab.shape[1],))


def bucket_table(n):
    import numpy as np
    d = np.arange(n)
    max_exact = NUM_BUCKETS // 2
    nf = np.maximum(d, max_exact).astype(np.float64)
    large = max_exact + (np.log(nf / max_exact) / math.log(MAX_DISTANCE / max_exact)
                         * (NUM_BUCKETS - max_exact)).astype(np.int64)
    return np.where(d < max_exact, d, np.minimum(large, NUM_BUCKETS - 1)).astype(np.int32)


def _compress_kernel(x_ref, w_ref, pe_ref, o_ref, *, nblk):
    half = nblk // 2
    acc = jnp.zeros((nblk, NSA_HEAD_DIM), jnp.float32)
    for c in range(CMP_BLOCK):
        xe = x_ref[pl.ds(c, half, stride=2 * CMP_BLOCK), :]
        xo = x_ref[pl.ds(CMP_BLOCK + c, half, stride=2 * CMP_BLOCK), :]
        lhs = jnp.concatenate([xe, xo], axis=0) + pe_ref[c:c + 1, :]
        acc = acc + jnp.dot(lhs.astype(jnp.bfloat16), w_ref[c].astype(jnp.bfloat16),
                            preferred_element_type=jnp.float32)
    o_ref[...] = acc


def compress_prompt(proj_p, cmp_pe, cmp_w, B, T, col0=P_KVC):
    nblk = T // CMP_BLOCK
    hd, KVH = NSA_HEAD_DIM, NSA_KV_HEADS
    return pl.pallas_call(
        functools.partial(_compress_kernel, nblk=nblk),
        grid=(B, 2, KVH),
        in_specs=[pl.BlockSpec((T, hd), lambda b, j, kh: (b, col0 // hd + j * KVH + kh)),
                  pl.BlockSpec((None, CMP_BLOCK, hd, hd), lambda b, j, kh: (j, 0, 0, 0)),
                  pl.BlockSpec((None, CMP_BLOCK, hd), lambda b, j, kh: (j, 0, 0))],
        out_specs=pl.BlockSpec((None, None, None, nblk, hd), lambda b, j, kh: (b, j, kh, 0, 0)),
        out_shape=jax.ShapeDtypeStruct((B, 2, KVH, nblk, hd), jnp.float32),
        compiler_params=pltpu.CompilerParams(dimension_semantics=("arbitrary",) * 3, vmem_limit_bytes=VMEM_LIMIT_BYTES),
        name="compress_prompt",
    )(proj_p, cmp_w, cmp_pe)


NSA_TQ = 128
NEG_BIG = -1e30


def _nsa_prompt_kernel(q_ref, gate_ref, ks_ref, vs_ref, kw_ref, vw_ref, kc_ref, vc_ref, bt_ref, bc_ref, es_ref,
                       o_ref, sel_ref, qs_ref, m_ref, l_ref, acc_ref, *, n_sb):
    f32, bf16 = jnp.float32, jnp.bfloat16
    tq, G, hd = NSA_TQ, NSA_GROUP, NSA_HEAD_DIM
    qi = pl.program_id(2)
    t0 = qi * tq
    scale = NSA_HEAD_DIM ** -0.5
    qs_ref[...] = jnp.transpose(jnp.concatenate([q_ref[:, g * hd:(g + 1) * hd] for g in range(G)], axis=0)).astype(bf16)
    tlane = t0 + lax.broadcasted_iota(jnp.int32, (1, tq), 1)

    ncmp = kc_ref.shape[0]
    brow = lax.broadcasted_iota(jnp.int32, (ncmp, 1), 0)
    blk_id = jnp.where(brow < ncmp // 2, 2 * brow, 2 * (brow - ncmp // 2) + 1)
    vis = (blk_id * CMP_BLOCK + (CMP_BLOCK - 1)) <= tlane
    vis = jnp.concatenate([vis] * G, axis=1)
    lc = jnp.dot(kc_ref[...].astype(bf16), qs_ref[...], preferred_element_type=f32) * scale + bc_ref[...]
    lc = jnp.where(vis, lc, -jnp.inf)
    mc = jnp.max(lc, axis=0, keepdims=True)
    pc = jnp.exp(lc - jnp.where(mc > -jnp.inf, mc, 0.0))
    sc = jnp.sum(pc, axis=0, keepdims=True)
    pc = pc / jnp.where(sc > 0, sc, 1.0)
    oct = lax.dot_general(vc_ref[...].astype(bf16), pc.astype(bf16), (((0,), (0,)), ((), ())),
                          preferred_element_type=f32)

    pg = pc[:, 0:tq]
    for g in range(1, G):
        pg = pg + pc[:, g * tq:(g + 1) * tq]
    imp = pg[:n_sb] + pg[n_sb:]
    cur = tlane // SEL_BLOCK
    bl = lax.broadcasted_iota(jnp.int32, (n_sb, tq), 0)
    force = (bl == 0) | (bl == cur) | (bl == cur - 1)
    score = jnp.where(force, FORCE_SCORE, imp)
    score = jnp.where(bl <= cur, score, -jnp.inf)
    rank = jnp.zeros((n_sb, tq), jnp.int32)
    for i in range(n_sb):
        si = score[i:i + 1, :]
        ahead = (si > score) | ((si == score) & (i < bl))
        rank = rank + ahead.astype(jnp.int32)
    sel_ref[...] = (rank < N_SEL).astype(bf16)
    gate = jax.nn.sigmoid(gate_ref[...])
    for g in range(G):
        o_ref[:, g * hd:(g + 1) * hd] = gate[:, 3 * g:3 * g + 1] * jnp.transpose(oct[:, g * tq:(g + 1) * tq])

    kcol = lax.broadcasted_iota(jnp.int32, (tq, 1), 0)
    tlane = t0 + lax.broadcasted_iota(jnp.int32, (1, tq), 1)

    def attend(c, carry, k_ref, v_ref, selected):
        s0 = pl.multiple_of(c * tq, tq)
        dist = tlane - (s0 + kcol)
        if selected:
            ok = (jnp.dot(es_ref[c], sel_ref[...], preferred_element_type=f32) > 0.5) & (dist >= 0)
        else:
            ok = (dist >= 0) & (dist <= WINDOW)
        ok = jnp.concatenate([ok] * G, axis=1)
        k = k_ref[pl.ds(s0, tq), :].astype(bf16)
        vt = jnp.transpose(v_ref[pl.ds(s0, tq), :]).astype(bf16)
        s = jnp.dot(k, qs_ref[...], preferred_element_type=f32) * scale + bt_ref[jnp.minimum(qi - c, 2)]
        s = jnp.where(ok, s, NEG_BIG)
        m_old = m_ref[...]
        m_new = jnp.maximum(m_old, jnp.max(s, axis=0, keepdims=True))
        p = jnp.exp(s - m_new)
        alpha = jnp.exp(m_old - m_new)
        l_ref[...] = alpha * l_ref[...] + jnp.sum(p, axis=0, keepdims=True)
        acc_ref[...] = alpha * acc_ref[...] + jnp.dot(vt, p.astype(bf16), preferred_element_type=f32)
        m_ref[...] = m_new
        return carry

    def run_branch(lo, k_ref, v_ref, selected, gate_col):
        m_ref[...] = jnp.full(m_ref.shape, NEG_BIG, f32)
        l_ref[...] = jnp.zeros(l_ref.shape, f32)
        acc_ref[...] = jnp.zeros(acc_ref.shape, f32)
        lax.fori_loop(lo, qi + 1, functools.partial(attend, k_ref=k_ref, v_ref=v_ref, selected=selected), 0)
        ot = acc_ref[...] / l_ref[...]
        for g in range(G):
            o_ref[:, g * hd:(g + 1) * hd] += (gate[:, 3 * g + gate_col:3 * g + gate_col + 1]
                                              * jnp.transpose(ot[:, g * tq:(g + 1) * tq]))

    run_branch(0, ks_ref, vs_ref, True, 1)
    run_branch(jnp.maximum(qi - WINDOW // tq, 0), kw_ref, vw_ref, False, 2)


def nsa_prompt(proj_p, kvc, rel_bias, B, T):
    import numpy as np
    tq, G, hd, KVH = NSA_TQ, NSA_GROUP, NSA_HEAD_DIM, NSA_KV_HEADS
    nq = T // tq
    n_sb = T // SEL_BLOCK
    ncmp = T // CMP_BLOCK
    assert T % tq == 0 and T % SEL_BLOCK == 0 and ncmp == 2 * n_sb
    bucket = bucket_table(max(T, 3 * tq))
    tab = rel_bias.astype(jnp.float32)
    ii, jj = np.meshgrid(np.arange(tq), np.arange(tq), indexing="ij")
    didx = np.stack([bucket[np.maximum(ii - jj, 0)], bucket[tq + ii - jj], np.full((tq, tq), NUM_BUCKETS - 1)])
    assert bucket[tq + 1] == NUM_BUCKETS - 1
    bt = bias_lookup(tab, didx).reshape(3, tq, tq, KVH, G).transpose(3, 0, 2, 4, 1).reshape(KVH, 3, tq, G * tq)
    blk = np.concatenate([np.arange(0, ncmp, 2), np.arange(1, ncmp, 2)])
    dc = np.arange(T)[:, None] - (blk * CMP_BLOCK + CMP_BLOCK - 1)[None, :]
    bc = bias_lookup(tab, bucket[np.maximum(dc, 0)]).reshape(nq, tq, ncmp, KVH, G).transpose(3, 0, 2, 4, 1).reshape(KVH, nq, ncmp, G * tq)
    es = (np.arange(n_sb)[None, :, None] == (np.arange(nq)[:, None, None] * (tq // SEL_BLOCK)
                                              + np.arange(tq)[None, None, :] // SEL_BLOCK))
    es = jnp.asarray(np.swapaxes(es, 1, 2), jnp.bfloat16)
    col = lambda off: off // hd
    kv_spec = lambda off: pl.BlockSpec((T, hd), lambda b, kh, i: (b, col(off) + kh))
    return pl.pallas_call(
        functools.partial(_nsa_prompt_kernel, n_sb=n_sb),
        grid=(B, KVH, nq),
        in_specs=[pl.BlockSpec((tq, G * hd), lambda b, kh, i: (b * nq + i, P_Q // (G * hd) + kh)),
                  pl.BlockSpec((tq, LANE), lambda b, kh, i: (b * nq + i, P_GATES // LANE + kh)),
                  kv_spec(P_KVS), kv_spec(P_KVS + KVH * hd), kv_spec(P_KVW), kv_spec(P_KVW + KVH * hd),
                  pl.BlockSpec((None, None, None, ncmp, hd), lambda b, kh, i: (b, 0, kh, 0, 0)),
                  pl.BlockSpec((None, None, None, ncmp, hd), lambda b, kh, i: (b, 1, kh, 0, 0)),
                  pl.BlockSpec((None, 3, tq, G * tq), lambda b, kh, i: (kh, 0, 0, 0)),
                  pl.BlockSpec((None, None, ncmp, G * tq), lambda b, kh, i: (kh, i, 0, 0)),
                  pl.BlockSpec((nq, tq, n_sb), lambda b, kh, i: (0, 0, 0))],
        out_specs=pl.BlockSpec((tq, G * hd), lambda b, kh, i: (b * nq + i, kh)),
        out_shape=jax.ShapeDtypeStruct((B * T, NSA_WIDTH), jnp.float32),
        scratch_shapes=[pltpu.VMEM((n_sb, tq), jnp.bfloat16), pltpu.VMEM((hd, G * tq), jnp.bfloat16),
                        pltpu.VMEM((1, G * tq), jnp.float32), pltpu.VMEM((1, G * tq), jnp.float32),
                        pltpu.VMEM((hd, G * tq), jnp.float32)],
        compiler_params=pltpu.CompilerParams(dimension_semantics=("arbitrary",) * 3, vmem_limit_bytes=VMEM_LIMIT_BYTES),
        name="nsa_prompt",
    )(proj_p, proj_p, proj_p, proj_p, proj_p, proj_p, kvc, kvc, bt, bc, es)


SAMPLE_PG = 8
T_PAD = 8


def _compress_paged_kernel(pt_ref, *refs, nb):
    del pt_ref
    page_refs = refs[:SAMPLE_PG]
    w_ref, pe_ref, o_ref = refs[SAMPLE_PG:]
    jk, hd = 2 * NSA_KV_HEADS, NSA_HEAD_DIM
    rows = SAMPLE_PG * nb * jk
    acc = jnp.zeros((rows, 2 * hd), jnp.float32)
    for c in range(CMP_BLOCK):
        x = jnp.concatenate([pr[pl.ds(c, nb, stride=CMP_BLOCK), :, :] for pr in page_refs], axis=0)
        lhs = (x + pe_ref[c]).reshape(rows, hd).astype(jnp.bfloat16)
        acc = acc + jnp.dot(lhs, w_ref[c], preferred_element_type=jnp.float32)
    sub = lax.broadcasted_iota(jnp.int32, (rows, 1), 0) & (jk - 1)
    out = jnp.where(sub < NSA_KV_HEADS, acc[:, :hd], acc[:, hd:])
    o_ref[...] = out.reshape(SAMPLE_PG * nb, jk, hd)


def compress_paged(pool, page_table, cmp_pe, cmp_w):
    n_phys, page = pool.shape[:2]
    B, n_pages = page_table.shape
    jk, hd = 2 * NSA_KV_HEADS, NSA_HEAD_DIM
    nb = page // CMP_BLOCK
    assert n_pages % SAMPLE_PG == 0 and page % CMP_BLOCK == 0
    pool3 = pool.reshape(n_phys, page, jk, hd)
    w01 = jnp.concatenate([cmp_w[0], cmp_w[1]], axis=-1).astype(jnp.bfloat16)
    pe8 = jnp.repeat(jnp.swapaxes(cmp_pe, 0, 1), NSA_KV_HEADS, axis=1)
    page_spec = lambda i: pl.BlockSpec((None, page, jk, hd),
                                       lambda b, g, pt: (pt[b * n_pages + g * SAMPLE_PG + i], 0, 0, 0))
    return pl.pallas_call(
        functools.partial(_compress_paged_kernel, nb=nb),
        grid_spec=pltpu.PrefetchScalarGridSpec(
            num_scalar_prefetch=1, grid=(B, n_pages // SAMPLE_PG),
            in_specs=[page_spec(i) for i in range(SAMPLE_PG)]
            + [pl.BlockSpec((CMP_BLOCK, hd, 2 * hd), lambda b, g, pt: (0, 0, 0)),
               pl.BlockSpec((CMP_BLOCK, jk, hd), lambda b, g, pt: (0, 0, 0))],
            out_specs=pl.BlockSpec((None, SAMPLE_PG * nb, jk, hd), lambda b, g, pt: (b, g, 0, 0))),
        out_shape=jax.ShapeDtypeStruct((B, n_pages * nb, jk, hd), jnp.float32),
        compiler_params=pltpu.CompilerParams(dimension_semantics=("arbitrary",) * 2, vmem_limit_bytes=VMEM_LIMIT_BYTES),
        name="compress_paged",
    )(page_table.reshape(-1), *([pool3] * SAMPLE_PG), w01, pe8)


def _nsa_sample_kernel(pt_ref, *refs, past, page, n_steps, n_sb, half):
    del pt_ref
    f32, bf16 = jnp.float32, jnp.bfloat16
    KVH, G, hd = NSA_KV_HEADS, NSA_GROUP, NSA_HEAD_DIM
    q_refs = refs[0:KVH]
    gate_ref, ksn_ref, vsn_ref, kwn_ref, vwn_ref, kvc_ref, bc_ref = refs[KVH:KVH + 7]
    page_refs = refs[KVH + 7:KVH + 7 + SAMPLE_PG]
    es_ref, bs_ref, bn_ref, win_ref, bw_ref, o_ref, m_ref, l_ref, acc_ref, oc_ref, sel_ref = refs[KVH + 7 + SAMPLE_PG:]
    g_step = pl.program_id(1)
    R = G * T_PAD
    scale = NSA_HEAD_DIM ** -0.5
    nt = (((1,), (1,)), ((), ()))
    t_row = past + (lax.broadcasted_iota(jnp.int32, (R, 1), 0) & (T_PAD - 1))
    qs = [jnp.concatenate([q_refs[kh][:, g * hd:(g + 1) * hd] for g in range(G)], axis=0).astype(bf16) for kh in range(KVH)]

    @pl.when(g_step == 0)
    def _():
        m_ref[...] = jnp.full(m_ref.shape, NEG_BIG, f32)
        l_ref[...] = jnp.zeros(l_ref.shape, f32)
        acc_ref[...] = jnp.zeros(acc_ref.shape, f32)
        lane = lax.broadcasted_iota(jnp.int32, (1, 2 * half), 1)
        lo = lane < half
        blk_id = jnp.where(lo, 2 * lane, 2 * (lane - half) + 1)
        real = jnp.where(lo, lane, lane - half) < n_sb
        vis = real & (blk_id * CMP_BLOCK + (CMP_BLOCK - 1) <= t_row)
        tpos = past + lax.broadcasted_iota(jnp.int32, (T_PAD, 1), 0)
        cur = tpos // SEL_BLOCK
        bl = lax.broadcasted_iota(jnp.int32, (T_PAD, half), 1)
        force = (bl == 0) | (bl == cur) | (bl == cur - 1)
        for kh in range(KVH):
            lc = lax.dot_general(qs[kh], kvc_ref[kh].astype(bf16), nt, preferred_element_type=f32) * scale + bc_ref[kh]
            lc = jnp.where(vis, lc, -jnp.inf)
            mc = jnp.max(lc, axis=-1, keepdims=True)
            pc = jnp.exp(lc - jnp.where(mc > -jnp.inf, mc, 0.0))
            sc = jnp.sum(pc, axis=-1, keepdims=True)
            pc = pc / jnp.where(sc > 0, sc, 1.0)
            oc_ref[kh * R:(kh + 1) * R, :] = jnp.dot(pc.astype(bf16), kvc_ref[KVH + kh].astype(bf16),
                                                     preferred_element_type=f32)
            pg = pc[0:T_PAD]
            for g in range(1, G):
                pg = pg + pc[g * T_PAD:(g + 1) * T_PAD]
            imp = pg[:, :half] + pg[:, half:]
            score = jnp.where(force, FORCE_SCORE, imp)
            score = jnp.where((bl <= cur) & (bl < n_sb), score, -jnp.inf)
            rank = jnp.zeros((T_PAD, half), jnp.int32)
            for i in range(n_sb):
                si = score[:, i:i + 1]
                rank = rank + ((si > score) | ((si == score) & (i < bl))).astype(jnp.int32)
            selm = ((rank < N_SEL) & (bl < n_sb)).astype(bf16)
            sel_ref[kh * R:(kh + 1) * R, :] = jnp.concatenate([selm] * G, axis=0)

    last = g_step == n_steps - 1
    bsel = bs_ref[jnp.where(last, 1, 0)]
    okf = jnp.dot(sel_ref[...], es_ref[...], preferred_element_type=f32) > 0.5
    for kh in range(KVH):
        rs = slice(kh * R, (kh + 1) * R)
        k = jnp.concatenate([pr[pl.ds(kh, page, stride=2 * KVH), :] for pr in page_refs], axis=0).astype(bf16)
        v = jnp.concatenate([pr[pl.ds(KVH + kh, page, stride=2 * KVH), :] for pr in page_refs], axis=0).astype(bf16)
        s = lax.dot_general(qs[kh], k, nt, preferred_element_type=f32) * scale + bsel[rs]
        ok = okf[rs]
        s = jnp.where(ok, s, NEG_BIG)
        m_old = m_ref[rs]
        m_new = jnp.maximum(m_old, jnp.max(s, axis=-1, keepdims=True))
        p = jnp.where(ok, jnp.exp(s - m_new), 0.0)
        alpha = jnp.exp(m_old - m_new)
        l_ref[rs] = alpha * l_ref[rs] + jnp.sum(p, axis=-1, keepdims=True)
        acc_ref[rs] = alpha * acc_ref[rs] + jnp.dot(p.astype(bf16), v, preferred_element_type=f32)
        m_ref[rs] = m_new

    @pl.when(last)
    def _():
        gate = jax.nn.sigmoid(gate_ref[...])
        ncol = lax.broadcasted_iota(jnp.int32, (1, T_PAD), 1)
        wb = win_ref.shape[0] // (2 * KVH)
        wcol = lax.broadcasted_iota(jnp.int32, (1, wb), 1)
        for kh in range(KVH):
            rs = slice(kh * R, (kh + 1) * R)
            cs = slice(kh * hd, (kh + 1) * hd)
            s = lax.dot_general(qs[kh], ksn_ref[:, cs].astype(bf16), nt, preferred_element_type=f32) * scale + bn_ref[kh]
            ok = (past + ncol) <= t_row
            s = jnp.where(ok, s, NEG_BIG)
            m_old = m_ref[rs]
            m_new = jnp.maximum(m_old, jnp.max(s, axis=-1, keepdims=True))
            p = jnp.where(ok, jnp.exp(s - m_new), 0.0)
            alpha = jnp.exp(m_old - m_new)
            l_s = alpha * l_ref[rs] + jnp.sum(p, axis=-1, keepdims=True)
            osel = (alpha * acc_ref[rs] + jnp.dot(p.astype(bf16), vsn_ref[:, cs].astype(bf16),
                                                  preferred_element_type=f32)) / l_s
            s1 = lax.dot_general(qs[kh], win_ref[pl.ds(kh, wb, stride=2 * KVH), :].astype(bf16), nt,
                                 preferred_element_type=f32) * scale + bw_ref[kh]
            d1 = t_row - (past - wb + wcol)
            ok1 = (d1 >= 0) & (d1 <= WINDOW)
            s1 = jnp.where(ok1, s1, NEG_BIG)
            s2 = lax.dot_general(qs[kh], kwn_ref[:, cs].astype(bf16), nt, preferred_element_type=f32) * scale + bn_ref[kh]
            s2 = jnp.where(ok, s2, NEG_BIG)
            mw = jnp.maximum(jnp.max(s1, axis=-1, keepdims=True), jnp.max(s2, axis=-1, keepdims=True))
            p1 = jnp.where(ok1, jnp.exp(s1 - mw), 0.0)
            p2 = jnp.where(ok, jnp.exp(s2 - mw), 0.0)
            l_w = jnp.sum(p1, axis=-1, keepdims=True) + jnp.sum(p2, axis=-1, keepdims=True)
            owin = (jnp.dot(p1.astype(bf16), win_ref[pl.ds(KVH + kh, wb, stride=2 * KVH), :].astype(bf16),
                            preferred_element_type=f32)
                    + jnp.dot(p2.astype(bf16), vwn_ref[:, cs].astype(bf16), preferred_element_type=f32)) / l_w
            oc = oc_ref[rs]
            for g in range(G):
                r = slice(g * T_PAD, (g + 1) * T_PAD)
                c0 = kh * LANE + 3 * g
                o_ref[:, (kh * G + g) * hd:(kh * G + g + 1) * hd] = (
                    gate[:, c0:c0 + 1] * oc[r] + gate[:, c0 + 1:c0 + 2] * osel[r] + gate[:, c0 + 2:c0 + 3] * owin[r])


def nsa_sample(proj_pad, kvc_t, pool_sel, win_buf, page_table, rel_bias, B, T):
    import numpy as np
    KVH, G, hd = NSA_KV_HEADS, NSA_GROUP, NSA_HEAD_DIM
    n_phys, page = pool_sel.shape[:2]
    n_pages = page_table.shape[1]
    past = n_pages * page
    wb = win_buf.shape[1]
    half = kvc_t.shape[2] // 2
    n_sb = -(-(past + T) // SEL_BLOCK)
    n_steps = n_pages // SAMPLE_PG
    keys = SAMPLE_PG * page
    R = G * T_PAD
    assert n_pages % SAMPLE_PG == 0 and page % SEL_BLOCK == 0 and T <= T_PAD and n_sb <= half and past % SEL_BLOCK == 0
    bucket = bucket_table(past + T_PAD + 1)
    tab = rel_bias.astype(jnp.float32)
    qpos = past + np.arange(T_PAD)

    def bias_rows(dist):
        b = bias_lookup(tab, bucket[np.maximum(dist, 0)])
        return b.reshape(T_PAD, dist.shape[1], KVH, G).transpose(2, 3, 0, 1).reshape(KVH, R, dist.shape[1])

    lane = np.arange(2 * half)
    blk = np.where(lane < half, 2 * lane, 2 * (lane - half) + 1)
    bc = bias_rows(qpos[:, None] - (blk * CMP_BLOCK + CMP_BLOCK - 1)[None, :])
    far = bias_rows(np.full((T_PAD, keys), past, np.int64))
    assert bucket[page + 1] == NUM_BUCKETS - 1
    near = bias_rows(qpos[:, None] - ((n_steps - 1) * keys + np.arange(keys))[None, :])
    bs = jnp.stack([far, near], axis=0).reshape(2, KVH * R, keys)
    bn = bias_rows(qpos[:, None] - (past + np.arange(T_PAD))[None, :])
    bw = bias_rows(qpos[:, None] - (past - wb + np.arange(wb))[None, :])
    es = (np.arange(half)[None, :, None] == (np.arange(n_steps)[:, None, None] * (keys // SEL_BLOCK)
                                              + np.arange(keys)[None, None, :] // SEL_BLOCK))
    es = jnp.asarray(es, jnp.bfloat16)
    pool2 = pool_sel.reshape(n_phys, page * 2 * KVH, hd)
    win2 = win_buf.reshape(B, wb * 2 * KVH, hd)
    wq = G * hd
    c = lambda *idx: (lambda b, g, pt: idx)
    row_blk = lambda w, col: pl.BlockSpec((T_PAD, w), lambda b, g, pt: (b, col))
    in_specs = ([row_blk(wq, P_Q // wq + kh) for kh in range(KVH)]
                + [row_blk(KVH * LANE, P_GATES // (KVH * LANE)),
                   row_blk(KVH * hd, P_KVS // (KVH * hd)), row_blk(KVH * hd, P_KVS // (KVH * hd) + 1),
                   row_blk(KVH * hd, P_KVW // (KVH * hd)), row_blk(KVH * hd, P_KVW // (KVH * hd) + 1),
                   pl.BlockSpec((None, 2 * KVH, 2 * half, hd), lambda b, g, pt: (b, 0, 0, 0)),
                   pl.BlockSpec((KVH, R, 2 * half), c(0, 0, 0))]
                + [pl.BlockSpec((None, page * 2 * KVH, hd),
                                (lambda i: lambda b, g, pt: (pt[b * n_pages + g * SAMPLE_PG + i], 0, 0))(i))
                   for i in range(SAMPLE_PG)]
                + [pl.BlockSpec((None, half, keys), lambda b, g, pt: (g, 0, 0)),
                   pl.BlockSpec((2, KVH * R, keys), c(0, 0, 0)),
                   pl.BlockSpec((KVH, R, T_PAD), c(0, 0, 0)),
                   pl.BlockSpec((None, wb * 2 * KVH, hd), lambda b, g, pt: (b, 0, 0)),
                   pl.BlockSpec((KVH, R, wb), c(0, 0, 0))])
    f32 = jnp.float32
    return pl.pallas_call(
        functools.partial(_nsa_sample_kernel, past=past, page=page, n_steps=n_steps, n_sb=n_sb, half=half),
        grid_spec=pltpu.PrefetchScalarGridSpec(
            num_scalar_prefetch=1, grid=(B, n_steps), in_specs=in_specs,
            out_specs=pl.BlockSpec((T_PAD, NSA_WIDTH), lambda b, g, pt: (b, 0)),
            scratch_shapes=[pltpu.VMEM((KVH * R, 1), f32), pltpu.VMEM((KVH * R, 1), f32), pltpu.VMEM((KVH * R, hd), f32),
                            pltpu.VMEM((KVH * R, hd), f32), pltpu.VMEM((KVH * R, half), jnp.bfloat16)]),
        out_shape=jax.ShapeDtypeStruct((B * T_PAD, NSA_WIDTH), f32),
        compiler_params=pltpu.CompilerParams(dimension_semantics=("arbitrary",) * 2, vmem_limit_bytes=VMEM_LIMIT_BYTES),
        name="nsa_sample",
    )(page_table.reshape(-1), *([proj_pad] * (KVH + 5)), kvc_t, bc, *([pool2] * SAMPLE_PG), es, bs, bn, win2, bw)


RWKV_PAIR = 2 * RWKV_HEAD_DIM
HIGHEST = lax.Precision.HIGHEST


def _rwkv_kernel_one_pair(r_ref, k_ref, v_ref, xw_ref, xa_ref, xg_ref,
                 sr_ref, sk_ref, sv_ref, sxw_ref, sxa_ref, sxg_ref,
                 mr_ref, mk_ref, mv_ref, mxw_ref, mxa_ref, mxg_ref,
                 s0_ref, w0_ref, a0_ref, kk_ref, ka_ref, rk_ref, lnw_ref, lnb_ref,
                 wup_ref, aup_ref, gup_ref,
                 o_ref, sout_ref,
                 S_ref, cr_ref, ck_ref, cv_ref, cxw_ref, cxa_ref, cxg_ref, *, C, t_valid, n_chunks):
    f32, bf16 = jnp.float32, jnp.bfloat16
    hd, W = RWKV_HEAD_DIM, RWKV_PAIR
    ci = pl.program_id(2)
    lane = lax.broadcasted_iota(jnp.int32, (1, W), 1)
    h0 = lane < hd
    row = lax.broadcasted_iota(jnp.int32, (C, 1), 0)
    valid = (ci * C + row) < t_valid
    carries = ((cr_ref, sr_ref), (ck_ref, sk_ref), (cv_ref, sv_ref), (cxw_ref, sxw_ref), (cxa_ref, sxa_ref),
               (cxg_ref, sxg_ref))

    @pl.when(ci == 0)
    def _():
        S_ref[...] = jnp.zeros((W, W), f32)
        S_ref[0:hd, 0:hd] = s0_ref[0]
        S_ref[hd:W, hd:W] = s0_ref[1]
        for c_ref, s_ref in carries:
            c_ref[0:1, :] = s_ref[...]

    def shifted(x_ref, c_ref, mu_ref):
        x = x_ref[...]
        prev = jnp.where(row == 0, c_ref[0:1, :], pltpu.roll(x, 1, axis=0))
        c_ref[0:1, :] = x[C - 1:C, :]
        return x + (prev - x) * mu_ref[...]

    def seg_sum(x):
        s_lo = jnp.sum(jnp.where(h0, x, 0.0), axis=-1, keepdims=True)
        s_hi = jnp.sum(jnp.where(h0, 0.0, x), axis=-1, keepdims=True)
        return jnp.where(h0, s_lo, s_hi)

    def stack(x):
        return jnp.concatenate([jnp.where(h0, x, 0.0), jnp.where(h0, 0.0, x)], axis=0)

    mr = shifted(r_ref, cr_ref, mr_ref)
    mk = shifted(k_ref, ck_ref, mk_ref)
    mv = shifted(v_ref, cv_ref, mv_ref)
    mxw = shifted(xw_ref, cxw_ref, mxw_ref)
    mxa = shifted(xa_ref, cxa_ref, mxa_ref)
    mxg = shifted(xg_ref, cxg_ref, mxg_ref)

    z = w0_ref[...] + jnp.dot(jnp.tanh(mxw).astype(bf16), wup_ref[...].astype(bf16), preferred_element_type=f32)
    w_log = -jax.nn.softplus(-z) - 0.5
    lw = -jnp.exp(w_log)
    a = jax.nn.sigmoid(a0_ref[...] + jnp.dot(mxa.astype(bf16), aup_ref[...].astype(bf16), preferred_element_type=f32))
    g = jnp.dot(jax.nn.sigmoid(mxg).astype(bf16), gup_ref[...].astype(bf16), preferred_element_type=f32)
    kk = mk * kk_ref[...]
    kk = kk / jnp.maximum(jnp.sqrt(seg_sum(kk * kk)), 1e-12)
    kp = mk * (1.0 + (a - 1.0) * ka_ref[...])
    lw = jnp.where(valid, lw, 0.0)
    a_t = jnp.where(valid, -kk, 0.0)
    b_t = jnp.where(valid, kk * a, 0.0)
    k_t = jnp.where(valid, kp, 0.0)
    v_t = jnp.where(valid, mv, 0.0)

    ti = lax.broadcasted_iota(jnp.int32, (C, C), 0)
    tj = lax.broadcasted_iota(jnp.int32, (C, C), 1)
    cl = jnp.dot((tj <= ti).astype(f32), lw, precision=HIGHEST, preferred_element_type=f32)
    e_neg = jnp.exp(-cl)
    At = a_t * jnp.exp(cl - lw)
    Bt = b_t * e_neg
    Kt = k_t * e_neg
    Rt = mr * jnp.exp(cl)
    G = jnp.concatenate([stack(At), stack(Rt)], axis=0).astype(bf16)
    Z = jnp.concatenate([Bt, Bt, Kt, Kt], axis=0).astype(bf16)
    nt = (((1,), (1,)), ((), ()))
    M = lax.dot_general(G, Z, nt, preferred_element_type=f32)
    ri = lax.broadcasted_iota(jnp.int32, (2 * C, 2 * C), 0)
    cj = lax.broadcasted_iota(jnp.int32, (2 * C, 2 * C), 1)
    same = (ri >= C) == (cj >= C)
    strict = same & ((cj & (C - 1)) < (ri & (C - 1)))
    incl = same & ((cj & (C - 1)) <= (ri & (C - 1)))
    L = jnp.where(strict, M[0:2 * C, 0:2 * C], 0.0)
    AK = jnp.where(strict, M[0:2 * C, 2 * C:4 * C], 0.0)
    RB = jnp.where(incl, M[2 * C:4 * C, 0:2 * C], 0.0)
    RK = jnp.where(incl, M[2 * C:4 * C, 2 * C:4 * C], 0.0)
    S = S_ref[...]
    GS = lax.dot_general(G, S.astype(bf16), nt, preferred_element_type=f32)
    Vs = stack(v_t)
    U = GS[0:2 * C] + jnp.dot(AK.astype(bf16), Vs.astype(bf16), preferred_element_type=f32)
    P = L
    levels = C.bit_length() - 1
    for lvl in range(levels):
        U = U + jnp.dot(P, U, precision=HIGHEST, preferred_element_type=f32)
        if lvl + 1 < levels:
            P = jnp.dot(P, P, precision=HIGHEST, preferred_element_type=f32)
    X = jnp.concatenate([U, Vs], axis=0).astype(bf16)
    Y = GS[2 * C:4 * C] + jnp.dot(jnp.concatenate([RB, RK], axis=1).astype(bf16), X, preferred_element_type=f32)
    y = Y[0:C] + Y[C:2 * C]
    dS = lax.dot_general(X, Z, (((0,), (0,)), ((), ())), preferred_element_type=f32)
    vi = lax.broadcasted_iota(jnp.int32, (W, W), 0)
    kj = lax.broadcasted_iota(jnp.int32, (W, W), 1)
    S_new = jnp.where((vi < hd) == (kj < hd), S + dS, 0.0) * jnp.exp(cl[C - 1:C, :])
    S_ref[...] = S_new

    mean = seg_sum(y) * (1.0 / hd)
    yc = y - mean
    var = seg_sum(yc * yc) * (1.0 / hd)
    yn = yc * lax.rsqrt(var + LNX_EPS) * lnw_ref[...] + lnb_ref[...]
    bonus = seg_sum(mr * kp * rk_ref[...]) * mv
    o_ref[...] = (yn + bonus) * g

    @pl.when(ci == n_chunks - 1)
    def _():
        sout_ref[0] = S_new[0:hd, 0:hd]
        sout_ref[1] = S_new[hd:W, hd:W]


def rwkv_mix_one_pair(proj_p, shift_prev, wkv0, lp, B, T, t_valid, C):
    f32 = jnp.float32
    W, hd = RWKV_PAIR, RWKV_HEAD_DIM
    n_pairs = RWKV_WIDTH // W
    n_chunks = T // C
    assert T % C == 0 and C & (C - 1) == 0 and C >= 8

    def pad_vec(x):
        z = lambda n: jnp.zeros(x.shape[:-1] + (n,), x.dtype)
        return jnp.concatenate([x[..., :RWKV_SPLITS[2]], x[..., RWKV_SPLITS[2]:RWKV_SPLITS[3]], z(LANE - DECAY_LORA),
                                x[..., RWKV_SPLITS[3]:RWKV_SPLITS[4]], z(LANE - AAA_LORA), x[..., RWKV_SPLITS[4]:]], -1)

    shift_p = pad_vec(shift_prev.astype(f32))[:, None, :]
    mu_p = pad_vec(lp['rwkv_mu'].astype(f32))[None, :]
    pad_rows_to = lambda w: jnp.pad(w, ((0, LANE - w.shape[0]), (0, 0)))
    wup, aup = pad_rows_to(lp['rwkv_w_up']), pad_rows_to(lp['rwkv_a_up'])
    vec = lambda x: x.reshape(1, RWKV_WIDTH).astype(f32)
    nw = RWKV_WIDTH // W
    col_r, col_k, col_v = 0, nw, 2 * nw
    col_xw, col_xa, col_xg = P_XW // LANE, P_XA // LANE, P_XG // GATE_LORA
    row_spec = lambda w, cfn: pl.BlockSpec((C, w), lambda b, p, c: (b * n_chunks + c, cfn(p)))
    sh_spec = lambda w, cfn: pl.BlockSpec((None, 1, w), lambda b, p, c: (b, 0, cfn(p)))
    mu_spec = lambda w, cfn: pl.BlockSpec((1, w), lambda b, p, c: (0, cfn(p)))
    cols = [(W, lambda p: col_r + p), (W, lambda p: col_k + p), (W, lambda p: col_v + p),
            (LANE, lambda p: col_xw), (LANE, lambda p: col_xa), (GATE_LORA, lambda p: col_xg)]
    pvec = pl.BlockSpec((1, W), lambda b, p, c: (0, p))
    state_spec = pl.BlockSpec((None, 2, hd, hd), lambda b, p, c: (b, p, 0, 0))
    in_specs = ([row_spec(w, f) for w, f in cols] + [sh_spec(w, f) for w, f in cols] + [mu_spec(w, f) for w, f in cols]
                + [state_spec] + [pvec] * 7
                + [pl.BlockSpec((LANE, W), lambda b, p, c: (0, p)), pl.BlockSpec((LANE, W), lambda b, p, c: (0, p)),
                   pl.BlockSpec((GATE_LORA, W), lambda b, p, c: (0, p))])
    scratch = [pltpu.VMEM((W, W), f32)] + [pltpu.VMEM((8, w), f32) for w, _ in cols]
    o, s_fin = pl.pallas_call(
        functools.partial(_rwkv_kernel, C=C, t_valid=t_valid, n_chunks=n_chunks),
        grid=(B, n_pairs, n_chunks),
        in_specs=in_specs,
        out_specs=[pl.BlockSpec((C, W), lambda b, p, c: (b * n_chunks + c, p)), state_spec],
        out_shape=[jax.ShapeDtypeStruct((B * T, RWKV_WIDTH), f32),
                   jax.ShapeDtypeStruct((B, RWKV_HEADS, hd, hd), f32)],
        scratch_shapes=scratch,
        compiler_params=pltpu.CompilerParams(dimension_semantics=("arbitrary",) * 3, vmem_limit_bytes=VMEM_LIMIT_BYTES),
        name="rwkv_mix",
    )(*([proj_p] * 6), *([shift_p] * 6), *([mu_p] * 6), wkv0.astype(f32),
      vec(lp['rwkv_w0']), vec(lp['rwkv_a0']), vec(lp['rwkv_k_k']), vec(lp['rwkv_k_a']), vec(lp['rwkv_r_k']),
      vec(lp['rwkv_lnx_w']), vec(lp['rwkv_lnx_b']), wup, aup, lp['rwkv_g_up'])
    return o, s_fin


def _dot3(a, b):
    f32, bf16 = jnp.float32, jnp.bfloat16
    ah = a.astype(bf16)
    al = (a - ah.astype(f32)).astype(bf16)
    bh = b.astype(bf16)
    bl = (b - bh.astype(f32)).astype(bf16)
    d = lambda x, y: jnp.dot(x, y, preferred_element_type=f32)
    return d(ah, bh) + (d(ah, bl) + d(al, bh))


def _rwkv_kernel(r_ref, k_ref, v_ref, xw_ref, xa_ref, xg_ref,
                 sr_ref, sk_ref, sv_ref, sxw_ref, sxa_ref, sxg_ref,
                 mr_ref, mk_ref, mv_ref, mxw_ref, mxa_ref, mxg_ref,
                 s0_ref, w0_ref, a0_ref, kk_ref, ka_ref, rk_ref, lnw_ref, lnb_ref,
                 wup_ref, aup_ref, gup_ref,
                 o_ref, sout_ref,
                 S_ref, cr_ref, ck_ref, cv_ref, cxw_ref, cxa_ref, cxg_ref, *, C, NP, t_valid, n_chunks):
    f32, bf16 = jnp.float32, jnp.bfloat16
    hd, W = RWKV_HEAD_DIM, RWKV_PAIR
    ci = pl.program_id(2)
    lane = lax.broadcasted_iota(jnp.int32, (1, W), 1)
    h0 = lane < hd
    row = lax.broadcasted_iota(jnp.int32, (C, 1), 0)
    valid = (ci * C + row) < t_valid
    carries = ((cr_ref, sr_ref), (ck_ref, sk_ref), (cv_ref, sv_ref), (cxw_ref, sxw_ref), (cxa_ref, sxa_ref),
               (cxg_ref, sxg_ref))

    @pl.when(ci == 0)
    def _():
        S_ref[...] = jnp.zeros(S_ref.shape, f32)
        for p in range(NP):
            S_ref[p, 0:hd, 0:hd] = s0_ref[2 * p]
            S_ref[p, hd:W, hd:W] = s0_ref[2 * p + 1]
        for c_ref, s_ref in carries:
            c_ref[0:1, :] = s_ref[...]

    def shifted(x_ref, c_ref, mu_ref):
        x = x_ref[...]
        prev = jnp.where(row == 0, c_ref[0:1, :], pltpu.roll(x, 1, axis=0))
        c_ref[0:1, :] = x[C - 1:C, :]
        return x + (prev - x) * mu_ref[...]

    def seg_sum(x):
        s_lo = jnp.sum(jnp.where(h0, x, 0.0), axis=-1, keepdims=True)
        s_hi = jnp.sum(jnp.where(h0, 0.0, x), axis=-1, keepdims=True)
        return jnp.where(h0, s_lo, s_hi)

    def stack(x):
        return jnp.concatenate([jnp.where(h0, x, 0.0), jnp.where(h0, 0.0, x)], axis=0)

    mr_all = shifted(r_ref, cr_ref, mr_ref)
    mk_all = shifted(k_ref, ck_ref, mk_ref)
    mv_all = shifted(v_ref, cv_ref, mv_ref)
    mxw = shifted(xw_ref, cxw_ref, mxw_ref)
    mxa = shifted(xa_ref, cxa_ref, mxa_ref)
    mxg = shifted(xg_ref, cxg_ref, mxg_ref)

    z_all = w0_ref[...] + jnp.dot(jnp.tanh(mxw).astype(bf16), wup_ref[...].astype(bf16), preferred_element_type=f32)
    a_all = jax.nn.sigmoid(a0_ref[...] + jnp.dot(mxa.astype(bf16), aup_ref[...].astype(bf16), preferred_element_type=f32))
    g_all = jnp.dot(jax.nn.sigmoid(mxg).astype(bf16), gup_ref[...].astype(bf16), preferred_element_type=f32)

    ti = lax.broadcasted_iota(jnp.int32, (C, C), 0)
    tj = lax.broadcasted_iota(jnp.int32, (C, C), 1)
    tri = (tj <= ti).astype(f32)
    ri = lax.broadcasted_iota(jnp.int32, (2 * C, 2 * C), 0)
    cj = lax.broadcasted_iota(jnp.int32, (2 * C, 2 * C), 1)
    same = (ri >= C) == (cj >= C)
    strict = same & ((cj & (C - 1)) < (ri & (C - 1)))
    incl = same & ((cj & (C - 1)) <= (ri & (C - 1)))
    vi = lax.broadcasted_iota(jnp.int32, (W, W), 0)
    kj = lax.broadcasted_iota(jnp.int32, (W, W), 1)
    blockdiag = (vi < hd) == (kj < hd)
    nt = (((1,), (1,)), ((), ()))
    levels = C.bit_length() - 1

    stage = []
    for p in range(NP):
        ls = slice(p * W, (p + 1) * W)
        mr, mk, mv, a, g = mr_all[:, ls], mk_all[:, ls], mv_all[:, ls], a_all[:, ls], g_all[:, ls]
        w_log = -jax.nn.softplus(-z_all[:, ls]) - 0.5
        lw = -jnp.exp(w_log)
        kk = mk * kk_ref[:, ls]
        kk = kk / jnp.maximum(jnp.sqrt(seg_sum(kk * kk)), 1e-12)
        kp = mk * (1.0 + (a - 1.0) * ka_ref[:, ls])
        lw = jnp.where(valid, lw, 0.0)
        a_t = jnp.where(valid, -kk, 0.0)
        b_t = jnp.where(valid, kk * a, 0.0)
        k_t = jnp.where(valid, kp, 0.0)
        v_t = jnp.where(valid, mv, 0.0)

        cl = jnp.dot(tri, lw, precision=HIGHEST, preferred_element_type=f32)
        e_neg = jnp.exp(-cl)
        At = a_t * jnp.exp(cl - lw)
        Bt = b_t * e_neg
        Kt = k_t * e_neg
        Rt = mr * jnp.exp(cl)
        G = jnp.concatenate([stack(At), stack(Rt)], axis=0).astype(bf16)
        Z = jnp.concatenate([Bt, Bt, Kt, Kt], axis=0).astype(bf16)
        M = lax.dot_general(G, Z, nt, preferred_element_type=f32)
        L = jnp.where(strict, M[0:2 * C, 0:2 * C], 0.0)
        AK = jnp.where(strict, M[0:2 * C, 2 * C:4 * C], 0.0)
        RB = jnp.where(incl, M[2 * C:4 * C, 0:2 * C], 0.0)
        RK = jnp.where(incl, M[2 * C:4 * C, 2 * C:4 * C], 0.0)
        S = S_ref[p]
        GS = lax.dot_general(G, S.astype(bf16), nt, preferred_element_type=f32)
        Vs = stack(v_t)
        U = GS[0:2 * C] + jnp.dot(AK.astype(bf16), Vs.astype(bf16), preferred_element_type=f32)
        stage.append(dict(ls=ls, mr=mr, mv=mv, kp=kp, g=g, cl=cl, Z=Z, GS=GS, Vs=Vs, RB=RB, RK=RK, S=S, U=U, P=L))

    for lvl in range(levels):
        for st in stage:
            st['U'] = st['U'] + _dot3(st['P'], st['U'])
            if lvl + 1 < levels:
                st['P'] = _dot3(st['P'], st['P'])

    for p, st in enumerate(stage):
        ls, mr, mv, kp, g, cl, Z, GS, Vs, RB, RK, S, U = (st[n] for n in ('ls', 'mr', 'mv', 'kp', 'g', 'cl', 'Z', 'GS', 'Vs',
                                                                       'RB', 'RK', 'S', 'U'))
        X = jnp.concatenate([U, Vs], axis=0).astype(bf16)
        Y = GS[2 * C:4 * C] + jnp.dot(jnp.concatenate([RB, RK], axis=1).astype(bf16), X, preferred_element_type=f32)
        y = Y[0:C] + Y[C:2 * C]
        dS = lax.dot_general(X, Z, (((0,), (0,)), ((), ())), preferred_element_type=f32)
        S_new = jnp.where(blockdiag, S + dS, 0.0) * jnp.exp(cl[C - 1:C, :])
        S_ref[p] = S_new

        mean = seg_sum(y) * (1.0 / hd)
        yc = y - mean
        var = seg_sum(yc * yc) * (1.0 / hd)
        yn = yc * lax.rsqrt(var + LNX_EPS) * lnw_ref[:, ls] + lnb_ref[:, ls]
        bonus = seg_sum(mr * kp * rk_ref[:, ls]) * mv
        o_ref[:, ls] = (yn + bonus) * g

    @pl.when(ci == n_chunks - 1)
    def _():
        for p in range(NP):
            s_fin = S_ref[p]
            sout_ref[2 * p] = s_fin[0:hd, 0:hd]
            sout_ref[2 * p + 1] = s_fin[hd:W, hd:W]


def rwkv_mix(proj_p, shift_prev, wkv0, lp, B, T, t_valid, C, NP):
    f32 = jnp.float32
    hd = RWKV_HEAD_DIM
    W = NP * RWKV_PAIR
    n_steps = RWKV_WIDTH // W
    n_chunks = T // C
    assert T % C == 0 and C & (C - 1) == 0 and C >= 8 and RWKV_WIDTH % W == 0

    def pad_vec(x):
        z = lambda n: jnp.zeros(x.shape[:-1] + (n,), x.dtype)
        return jnp.concatenate([x[..., :RWKV_SPLITS[2]], x[..., RWKV_SPLITS[2]:RWKV_SPLITS[3]], z(LANE - DECAY_LORA),
                                x[..., RWKV_SPLITS[3]:RWKV_SPLITS[4]], z(LANE - AAA_LORA), x[..., RWKV_SPLITS[4]:]], -1)

    shift_p = pad_vec(shift_prev.astype(f32))[:, None, :]
    mu_p = pad_vec(lp['rwkv_mu'].astype(f32))[None, :]
    pad_rows_to = lambda w: jnp.pad(w, ((0, LANE - w.shape[0]), (0, 0)))
    wup, aup = pad_rows_to(lp['rwkv_w_up']), pad_rows_to(lp['rwkv_a_up'])
    vec = lambda x: x.reshape(1, RWKV_WIDTH).astype(f32)
    nw = RWKV_WIDTH // W
    col_r, col_k, col_v = 0, nw, 2 * nw
    col_xw, col_xa, col_xg = P_XW // LANE, P_XA // LANE, P_XG // GATE_LORA
    row_spec = lambda w, cfn: pl.BlockSpec((C, w), lambda b, p, c: (b * n_chunks + c, cfn(p)))
    sh_spec = lambda w, cfn: pl.BlockSpec((None, 1, w), lambda b, p, c: (b, 0, cfn(p)))
    mu_spec = lambda w, cfn: pl.BlockSpec((1, w), lambda b, p, c: (0, cfn(p)))
    cols = [(W, lambda p: col_r + p), (W, lambda p: col_k + p), (W, lambda p: col_v + p),
            (LANE, lambda p: col_xw), (LANE, lambda p: col_xa), (GATE_LORA, lambda p: col_xg)]
    pvec = pl.BlockSpec((1, W), lambda b, p, c: (0, p))
    state_spec = pl.BlockSpec((None, 2 * NP, hd, hd), lambda b, p, c: (b, p, 0, 0))
    in_specs = ([row_spec(w, f) for w, f in cols] + [sh_spec(w, f) for w, f in cols] + [mu_spec(w, f) for w, f in cols]
                + [state_spec] + [pvec] * 7
                + [pl.BlockSpec((LANE, W), lambda b, p, c: (0, p)), pl.BlockSpec((LANE, W), lambda b, p, c: (0, p)),
                   pl.BlockSpec((GATE_LORA, W), lambda b, p, c: (0, p))])
    scratch = [pltpu.VMEM((NP, RWKV_PAIR, RWKV_PAIR), f32)] + [pltpu.VMEM((8, w), f32) for w, _ in cols]
    o, s_fin = pl.pallas_call(
        functools.partial(_rwkv_kernel, C=C, NP=NP, t_valid=t_valid, n_chunks=n_chunks),
        grid=(B, n_steps, n_chunks),
        in_specs=in_specs,
        out_specs=[pl.BlockSpec((C, W), lambda b, p, c: (b * n_chunks + c, p)), state_spec],
        out_shape=[jax.ShapeDtypeStruct((B * T, RWKV_WIDTH), f32),
                   jax.ShapeDtypeStruct((B, RWKV_HEADS, hd, hd), f32)],
        scratch_shapes=scratch,
        compiler_params=pltpu.CompilerParams(dimension_semantics=("arbitrary",) * 3, vmem_limit_bytes=VMEM_LIMIT_BYTES),
        name="rwkv_mix",
    )(*([proj_p] * 6), *([shift_p] * 6), *([mu_p] * 6), wkv0.astype(f32),
      vec(lp['rwkv_w0']), vec(lp['rwkv_a0']), vec(lp['rwkv_k_k']), vec(lp['rwkv_k_a']), vec(lp['rwkv_r_k']),
      vec(lp['rwkv_lnx_w']), vec(lp['rwkv_lnx_b']), wup, aup, lp['rwkv_g_up'])
    return o, s_fin


def _expert_changed(be_ref, i):
    return jnp.logical_or(i == 0, be_ref[i] != be_ref[jnp.maximum(i - 1, 0)])


def _ffn_up_kernel(be_ref, nv_ref, x_ref, wg_ref, wu_ref, h_ref, wgb_ref, wub_ref):
    i = pl.program_id(0)
    live = i < nv_ref[0]

    @pl.when(live & _expert_changed(be_ref, i))
    def _():
        wgb_ref[...] = wg_ref[...].astype(jnp.bfloat16)
        wub_ref[...] = wu_ref[...].astype(jnp.bfloat16)

    @pl.when(live)
    def _():
        x = x_ref[...]
        hg = jnp.dot(x, wgb_ref[...], preferred_element_type=jnp.float32)
        hu = jnp.dot(x, wub_ref[...], preferred_element_type=jnp.float32)
        h_ref[...] = (jax.nn.silu(hg) * hu).astype(h_ref.dtype)

    @pl.when(i >= nv_ref[0])
    def _():
        h_ref[...] = jnp.zeros(h_ref.shape, h_ref.dtype)


def _ffn_down_kernel(be_ref, nv_ref, h_ref, wd_ref, rw_ref, o_ref, wdb_ref):
    i = pl.program_id(0)
    live = i < nv_ref[0]

    @pl.when(live & _expert_changed(be_ref, i))
    def _():
        wdb_ref[...] = wd_ref[...].astype(jnp.bfloat16)

    @pl.when(live)
    def _():
        o_ref[...] = jnp.dot(h_ref[...].astype(jnp.bfloat16), wdb_ref[...],
                             preferred_element_type=jnp.float32) * rw_ref[...]

    @pl.when(i >= nv_ref[0])
    def _():
        o_ref[...] = jnp.zeros(o_ref.shape, o_ref.dtype)


def expert_ffn(xs, w_gate, w_up, w_down, row_w, blk_exp, n_valid, blk):
    rows, d = xs.shape
    ff = w_gate.shape[-1]
    n_blocks = rows // blk
    params = pltpu.CompilerParams(dimension_semantics=("arbitrary",), vmem_limit_bytes=VMEM_LIMIT_BYTES)
    h = pl.pallas_call(
        _ffn_up_kernel,
        grid_spec=pltpu.PrefetchScalarGridSpec(
            num_scalar_prefetch=2, grid=(n_blocks,),
            in_specs=[pl.BlockSpec((blk, d), lambda i, be, nv: (i, 0)),
                      pl.BlockSpec((None, d, ff), lambda i, be, nv: (be[i], 0, 0)),
                      pl.BlockSpec((None, d, ff), lambda i, be, nv: (be[i], 0, 0))],
            out_specs=pl.BlockSpec((blk, ff), lambda i, be, nv: (i, 0)),
            scratch_shapes=[pltpu.VMEM((d, ff), jnp.bfloat16), pltpu.VMEM((d, ff), jnp.bfloat16)]),
        out_shape=jax.ShapeDtypeStruct((rows, ff), jnp.bfloat16),
        compiler_params=params, name="ffn_up",
    )(blk_exp, n_valid, xs, w_gate, w_up)
    return pl.pallas_call(
        _ffn_down_kernel,
        grid_spec=pltpu.PrefetchScalarGridSpec(
            num_scalar_prefetch=2, grid=(n_blocks,),
            in_specs=[pl.BlockSpec((blk, ff), lambda i, be, nv: (i, 0)),
                      pl.BlockSpec((None, ff, d), lambda i, be, nv: (be[i], 0, 0)),
                      pl.BlockSpec((blk, 1), lambda i, be, nv: (i, 0))],
            out_specs=pl.BlockSpec((blk, d), lambda i, be, nv: (i, 0)),
            scratch_shapes=[pltpu.VMEM((ff, d), jnp.bfloat16)]),
        out_shape=jax.ShapeDtypeStruct((rows, d), jnp.float32),
        compiler_params=params, name="ffn_down",
    )(blk_exp, n_valid, h, w_down, row_w)


def _router_kernel(x_ref, wt_ref, b_ref, e_ref, w_ref):
    f32 = jnp.float32
    tm = x_ref.shape[0]
    gs = N_EXPERTS // N_GROUPS
    logits = lax.dot_general(wt_ref[...].astype(jnp.bfloat16), x_ref[...].astype(jnp.bfloat16),
                             (((1,), (1,)), ((), ())), preferred_element_type=f32)
    scores = jax.nn.sigmoid(logits)
    biased = scores + b_ref[...]
    g3 = biased.reshape(N_GROUPS, gs, tm)
    it = lax.broadcasted_iota(jnp.int32, (N_GROUPS, gs, tm), 1)
    m1 = jnp.max(g3, axis=1, keepdims=True)
    i1 = jnp.min(jnp.where(g3 == m1, it, gs), axis=1, keepdims=True)
    m2 = jnp.max(jnp.where(it == i1, -jnp.inf, g3), axis=1, keepdims=True)
    gscore = (m1 + m2).reshape(N_GROUPS, tm)
    gi = lax.broadcasted_iota(jnp.int32, (N_GROUPS, tm), 0)
    grank = jnp.zeros((N_GROUPS, tm), jnp.int32)
    for j in range(N_GROUPS):
        sj = gscore[j:j + 1, :]
        grank = grank + ((sj > gscore) | ((sj == gscore) & (j < gi))).astype(jnp.int32)
    gmask = (grank < TOPK_GROUPS).reshape(N_GROUPS, 1, tm)
    masked = jnp.where(gmask, g3, -jnp.inf).reshape(N_EXPERTS, tm)
    ei = lax.broadcasted_iota(jnp.int32, (N_EXPERTS, tm), 0)
    idxs, wts = [], []
    for _ in range(TOP_K):
        m = jnp.max(masked, axis=0, keepdims=True)
        idx = jnp.min(jnp.where(masked == m, ei, N_EXPERTS), axis=0, keepdims=True)
        hit = ei == idx
        idxs.append(idx)
        wts.append(jnp.sum(jnp.where(hit, scores, 0.0), axis=0, keepdims=True))
        masked = jnp.where(hit, -jnp.inf, masked)
    w = jnp.concatenate(wts, axis=0)
    e_ref[...] = jnp.concatenate(idxs, axis=0)
    w_ref[...] = w / jnp.sum(w, axis=0, keepdims=True) * ROUTED_SCALE


def moe_route(x, router_w, router_bias):
    n, d = x.shape
    tm = min(512, n)
    assert n % tm == 0
    e, w = pl.pallas_call(
        _router_kernel,
        grid=(n // tm,),
        in_specs=[pl.BlockSpec((tm, d), lambda i: (i, 0)),
                  pl.BlockSpec((N_EXPERTS, d), lambda i: (0, 0)),
                  pl.BlockSpec((N_EXPERTS, 1), lambda i: (0, 0))],
        out_specs=[pl.BlockSpec((TOP_K, tm), lambda i: (0, i)), pl.BlockSpec((TOP_K, tm), lambda i: (0, i))],
        out_shape=[jax.ShapeDtypeStruct((TOP_K, n), jnp.int32), jax.ShapeDtypeStruct((TOP_K, n), jnp.float32)],
        compiler_params=pltpu.CompilerParams(dimension_semantics=("arbitrary",), vmem_limit_bytes=VMEM_LIMIT_BYTES),
        name="moe_route",
    )(x, router_w.T, router_bias.astype(jnp.float32).reshape(N_EXPERTS, 1))
    return e.T, w.T


def moe_ffn_pallas(h, lp, blk):
    B, T, D = h.shape
    x = h.reshape(B * T, D)
    n = x.shape[0]
    f32 = jnp.float32
    eidx, w = moe_route(x, lp['router_w'], lp['router_bias'])

    nk = n * TOP_K
    flat_e = eidx.reshape(nk)
    order = jnp.argsort(flat_e)
    se = flat_e[order]
    counts = jnp.bincount(flat_e, length=N_EXPERTS)
    padded = (counts + blk - 1) // blk * blk
    pad_end = jnp.cumsum(padded)
    dest = (pad_end - padded)[se] + jnp.arange(nk) - (jnp.cumsum(counts) - counts)[se]
    n_blocks = -(-(nk + N_EXPERTS * (blk - 1)) // blk)
    rows = n_blocks * blk
    row_tok = jnp.zeros((rows,), jnp.int32).at[dest].set((order // TOP_K).astype(jnp.int32))
    row_w = jnp.zeros((rows,), f32).at[dest].set(w.reshape(nk)[order])
    blk_exp = jnp.minimum(jnp.searchsorted(pad_end, jnp.arange(n_blocks) * blk, side='right'),
                          N_EXPERTS - 1).astype(jnp.int32)
    n_valid = (pad_end[-1] // blk).astype(jnp.int32).reshape(1)
    pos = jnp.zeros((nk,), jnp.int32).at[order].set(dest.astype(jnp.int32)).reshape(n, TOP_K)

    xb = x.astype(jnp.bfloat16)
    ys = expert_ffn(xb[row_tok], lp['exp_w_gate'], lp['exp_w_up'], lp['exp_w_down'], row_w[:, None], blk_exp,
                    n_valid, blk)
    routed = ys[pos].sum(axis=1)
    sblk = min(256, n)
    one = jnp.ones((n, 1), f32)
    shared = expert_ffn(xb, lp['sh_w_gate'][None], lp['sh_w_up'][None], lp['sh_w_down'][None], one,
                        jnp.zeros((n // sblk,), jnp.int32), jnp.full((1,), n // sblk, jnp.int32), sblk)
    return (routed + shared).reshape(B, T, D)


def pad_rows(a, length):
    return jnp.pad(a, [(0, 0), (0, length - a.shape[1])] + [(0, 0)] * (a.ndim - 2))


def rwkv7_mix(pr, shift_prev, wkv0, lp):
    B, T, _ = pr.shape
    f32 = jnp.float32
    prev = jnp.concatenate([shift_prev[:, None, :].astype(pr.dtype), pr[:, :-1]], axis=1)
    m = pr + (prev - pr) * lp['rwkv_mu']
    r, k, v, xw, xa, xg = jnp.split(m, RWKV_SPLITS, axis=-1)
    w_log = -jax.nn.softplus(-(lp['rwkv_w0'] + jnp.tanh(xw) @ lp['rwkv_w_up']).astype(f32)) - 0.5
    decay = jnp.exp(-jnp.exp(w_log))
    a = jax.nn.sigmoid((lp['rwkv_a0'] + xa @ lp['rwkv_a_up']).astype(f32))
    g = jax.nn.sigmoid(xg) @ lp['rwkv_g_up']

    def heads(t):
        return t.astype(f32).reshape(B, T, RWKV_HEADS, RWKV_HEAD_DIM)

    r, k, v, decay, a = heads(r), heads(k), heads(v), heads(decay), heads(a)
    kk = k * lp['rwkv_k_k'].astype(f32).reshape(RWKV_HEADS, RWKV_HEAD_DIM)
    kk = kk / jnp.maximum(jnp.linalg.norm(kk, axis=-1, keepdims=True), 1e-12)
    k = k * (1.0 + (a - 1.0) * lp['rwkv_k_a'].astype(f32).reshape(RWKV_HEADS, RWKV_HEAD_DIM))

    def step(S, inp):
        r_t, w_t, k_t, v_t, a_t, b_t = inp
        sa = jnp.einsum('bhvk,bhk->bhv', S, a_t)
        S = S * w_t[:, :, None, :] + sa[..., None] * b_t[:, :, None, :] + v_t[..., None] * k_t[:, :, None, :]
        return S, jnp.einsum('bhvk,bhk->bhv', S, r_t)

    xs = tuple(jnp.moveaxis(t, 1, 0) for t in (r, decay, k, v, -kk, kk * a))
    s_final, ys = lax.scan(step, wkv0.astype(f32), xs)
    y = jnp.moveaxis(ys, 0, 1)
    mu = jnp.mean(y, axis=-1, keepdims=True)
    var = jnp.mean(jnp.square(y - mu), axis=-1, keepdims=True)
    y = ((y - mu) * lax.rsqrt(var + LNX_EPS)).reshape(B, T, RWKV_WIDTH)
    y = y * lp['rwkv_lnx_w'].astype(f32) + lp['rwkv_lnx_b'].astype(f32)
    bonus = (jnp.sum(r * k * lp['rwkv_r_k'].astype(f32), axis=-1, keepdims=True) * v).reshape(B, T, RWKV_WIDTH)
    out = ((y + bonus) * g.astype(f32)).astype(pr.dtype)
    return out, pr[:, -1], s_final.astype(wkv0.dtype)


def compress_blocks(rows, pe, w):
    B, L = rows.shape[:2]
    blk = rows.reshape(B, L // CMP_BLOCK, CMP_BLOCK, 2, NSA_KV_HEADS, NSA_HEAD_DIM)
    blk = blk + jnp.swapaxes(pe, 0, 1)[None, None, :, :, None, :]
    return jnp.einsum('bncjkd,jcde->bnjke', blk, w)


def nsa_attend(q, q_pos, gates, kc, vc, c_end, n_sel_blocks, gather_sel, kw, vw, w_pos, rel_bias):
    B, Tq = q.shape[:2]
    f32 = jnp.float32
    scale = NSA_HEAD_DIM ** -0.5
    qg = q.reshape(B, Tq, NSA_KV_HEADS, NSA_GROUP, NSA_HEAD_DIM)
    bias_tab = rel_bias.astype(f32).reshape(NUM_BUCKETS, NSA_KV_HEADS, NSA_GROUP)

    dc = q_pos[:, None] - c_end[None, :]
    bias_c = bias_tab[rel_bucket(dc)].transpose(0, 2, 3, 1)
    lc = jnp.einsum('btkgd,bnkd->btkgn', qg, kc).astype(f32) * scale + bias_c
    pc = masked_softmax(lc, (dc >= 0)[:, None, None, :])
    oc = jnp.einsum('btkgn,bnkd->btkgd', pc.astype(vc.dtype), vc)

    imp = pc.sum(3).reshape(B, Tq, NSA_KV_HEADS, n_sel_blocks, SEL_BLOCK // CMP_BLOCK).sum(-1)
    blk = jnp.arange(n_sel_blocks)
    cur = (q_pos // SEL_BLOCK)[:, None]
    force = (blk == 0) | (blk == cur) | (blk == cur - 1)
    score = jnp.where(force[:, None, :], FORCE_SCORE, imp)
    score = jnp.where((blk <= cur)[:, None, :], score, -jnp.inf)
    _, idx = lax.top_k(jnp.moveaxis(score, 2, 1), min(N_SEL, n_sel_blocks))

    sel = gather_sel(idx)
    ks, vs = sel[..., 0, :], sel[..., 1, :]
    spos = idx[..., None] * SEL_BLOCK + jnp.arange(SEL_BLOCK)
    ds = q_pos[None, None, :, None, None] - spos
    hi = jnp.arange(NSA_KV_HEADS)[None, :, None, None, None]
    bias_s = jnp.moveaxis(bias_tab[rel_bucket(ds), hi], -1, 3)
    ls = jnp.einsum('btkgd,bktjsd->bktgjs', qg, ks).astype(f32) * scale + bias_s
    ms = (ds >= 0)[:, :, :, None]
    ps = masked_softmax(ls.reshape(ls.shape[:4] + (-1,)), ms.reshape(ms.shape[:4] + (-1,))).reshape(ls.shape)
    osel = jnp.einsum('bktgjs,bktjsd->btkgd', ps.astype(vs.dtype), vs)

    dw = q_pos[:, None] - w_pos[None, :]
    mw = (dw >= 0) & (dw <= WINDOW) & (w_pos >= 0)[None, :]
    bias_w = bias_tab[rel_bucket(dw)].transpose(0, 2, 3, 1)
    lw = jnp.einsum('btkgd,blkd->btkgl', qg, kw).astype(f32) * scale + bias_w
    pw = masked_softmax(lw, mw[:, None, None, :])
    ow = jnp.einsum('btkgl,blkd->btkgd', pw.astype(vw.dtype), vw)

    gt = gates.reshape(B, Tq, NSA_KV_HEADS, NSA_GROUP, 3)
    o = gt[..., 0:1] * oc + gt[..., 1:2] * osel + gt[..., 2:3] * ow
    return o.reshape(B, Tq, NSA_WIDTH)


def split_proj(h, lp):
    B, T, _ = h.shape
    proj = mm3(h, lp['w_in'])
    pr = proj[..., :RWKV_PROJ]
    pn = proj[..., RWKV_PROJ:]
    q = pn[..., :NSA_WIDTH].reshape(B, T, NSA_HEADS, NSA_HEAD_DIM)
    kv = pn[..., NSA_WIDTH:NSA_WIDTH + 3 * NSA_KV_WIDTH].reshape(B, T, 3, 2, NSA_KV_HEADS, NSA_HEAD_DIM)
    gates = jax.nn.sigmoid(pn[..., NSA_WIDTH + 3 * NSA_KV_WIDTH:].astype(jnp.float32))
    gates = gates.reshape(B, T, NSA_HEADS, 3).astype(h.dtype)
    return pr, q, kv[:, :, 0], kv[:, :, 1], kv[:, :, 2], gates


def merge_groups(o_rwkv, o_nsa, lp):
    o = jnp.concatenate([o_rwkv, rms_norm(o_nsa, lp['nsa_out_g'])], axis=-1).astype(jnp.bfloat16)
    return mm3(o, lp['w_out'])


def mixer_prompt(h, lp, rel_bias):
    B, T, d = h.shape
    proj_p = pallas_matmul(h.reshape(B * T, d), lp['w_in_p'])
    o_r, wkv = rwkv_mix(proj_p, jnp.zeros((B, RWKV_PROJ), h.dtype),
                        jnp.zeros((B, RWKV_HEADS, RWKV_HEAD_DIM, RWKV_HEAD_DIM), jnp.float32), lp, B, T, T, 64, 8)
    o_r = o_r.reshape(B, T, RWKV_WIDTH)
    shift_last = unpad_rwkv(proj_p.reshape(B, T, P_TOTAL)[:, -1])
    kvc = compress_prompt(proj_p, lp['cmp_pe'], lp['cmp_w'], B, T)
    o_n = nsa_prompt(proj_p, kvc, rel_bias, B, T).reshape(B, T, NSA_WIDTH)
    kv = proj_p[:, P_KVC:P_GATES].reshape(B, T, 3, 2, NSA_KV_HEADS, NSA_HEAD_DIM)
    kv_c, kv_s, kv_w = kv[:, :, 0], kv[:, :, 1], kv[:, :, 2]
    win_keep = min(WINDOW, T)
    return merge_groups(o_r, o_n, lp), (kv_c, kv_s, kv_w[:, T - win_keep:], wkv, shift_last)


def mixer_prompt_gather(h, lp, rel_bias):
    B, T, _ = h.shape
    pr, q, kv_c, kv_s, kv_w, gates = split_proj(h, lp)
    o_r, shift_last, wkv = rwkv7_mix(pr, jnp.zeros((B, RWKV_PROJ), h.dtype),
                                     jnp.zeros((B, RWKV_HEADS, RWKV_HEAD_DIM, RWKV_HEAD_DIM), jnp.float32), lp)
    lp_len = -(-T // SEL_BLOCK) * SEL_BLOCK
    comp = compress_blocks(pad_rows(kv_c, lp_len), lp['cmp_pe'], lp['cmp_w'])
    kc, vc = comp[:, :, 0], comp[:, :, 1]
    c_end = jnp.arange(lp_len // CMP_BLOCK) * CMP_BLOCK + (CMP_BLOCK - 1)
    n_sb = lp_len // SEL_BLOCK
    sel_blocks = pad_rows(kv_s, lp_len).reshape(B, n_sb, SEL_BLOCK, 2, NSA_KV_HEADS, NSA_HEAD_DIM)
    bi = jnp.arange(B)[:, None, None, None]
    hi = jnp.arange(NSA_KV_HEADS)[None, :, None, None]

    def gather_sel(idx):
        return sel_blocks[bi, idx, :, :, hi, :]

    kw_pad = jnp.pad(kv_w, ((0, 0), (WINDOW, 0), (0, 0), (0, 0), (0, 0)))

    def chunk(i):
        t0 = i * Q_CHUNK
        qc = lax.dynamic_slice_in_dim(q, t0, Q_CHUNK, axis=1)
        gc = lax.dynamic_slice_in_dim(gates, t0, Q_CHUNK, axis=1)
        wc = lax.dynamic_slice_in_dim(kw_pad, t0, WINDOW + Q_CHUNK, axis=1)
        q_pos = t0 + jnp.arange(Q_CHUNK)
        w_pos = t0 - WINDOW + jnp.arange(WINDOW + Q_CHUNK)
        return nsa_attend(qc, q_pos, gc, kc, vc, c_end, n_sb, gather_sel,
                          wc[:, :, 0], wc[:, :, 1], w_pos, rel_bias)

    o_n = lax.map(chunk, jnp.arange(T // Q_CHUNK))
    o_n = jnp.moveaxis(o_n, 0, 1).reshape(B, T, NSA_WIDTH)
    win_keep = min(WINDOW, T)
    return merge_groups(o_r, o_n, lp), (kv_c, kv_s, kv_w[:, T - win_keep:], wkv, shift_last)


def mixer_sample(h, lp, rel_bias, pool_cmp, pool_sel, win_buf, wkv0, shift0, page_table):
    B, T, _ = h.shape
    n_pages = page_table.shape[1]
    page = pool_cmp.shape[1]
    past = n_pages * page
    d = h.shape[-1]
    proj_p = pallas_matmul(h.reshape(B * T, d), lp['w_in_p']).reshape(B, T, P_TOTAL)
    import numpy as np
    kv = proj_p[..., P_KVC:P_GATES].reshape(B, T, 3, 2, NSA_KV_HEADS, NSA_HEAD_DIM)
    kv_c, kv_s, kv_w = kv[:, :, 0], kv[:, :, 1], kv[:, :, 2]
    proj_pad = jnp.pad(proj_p, ((0, 0), (0, T_PAD - T), (0, 0))).reshape(B * T_PAD, P_TOTAL)
    o_r, wkv = rwkv_mix(proj_pad, shift0, wkv0, lp, B, T_PAD, T, T_PAD, 4)
    o_r = o_r.reshape(B, T_PAD, RWKV_WIDTH)[:, :T]
    shift_last = unpad_rwkv(proj_p[:, -1])
    lp_len = -(-(past + T) // SEL_BLOCK) * SEL_BLOCK

    jk, hd = 2 * NSA_KV_HEADS, NSA_HEAD_DIM
    kvc_past = compress_paged(pool_cmp, page_table, lp['cmp_pe'], lp['cmp_w'])
    new_len = lp_len - past
    new_rows = pad_rows(proj_p[..., P_KVC:P_KVS], new_len).reshape(B * new_len, NSA_KV_WIDTH)
    kvc_new = compress_prompt(new_rows, lp['cmp_pe'], lp['cmp_w'], B, new_len, col0=0)
    nb_new = new_len // CMP_BLOCK
    assert nb_new <= 2
    kvc_new = kvc_new.reshape(B, jk, nb_new, hd).transpose(0, 2, 1, 3)
    kvc_all = jnp.concatenate([kvc_past, kvc_new], axis=1)
    nc = lp_len // CMP_BLOCK
    n_half = nc // 2
    half = -(-n_half // LANE) * LANE
    idx = np.zeros((2 * half,), np.int32)
    idx[:n_half] = np.arange(0, nc, 2)
    idx[half:half + n_half] = np.arange(1, nc, 2)
    kvc_t = kvc_all[:, idx].transpose(0, 2, 1, 3)
    o_n = nsa_sample(proj_pad, kvc_t, pool_sel, win_buf, page_table, rel_bias, B, T)
    o_n = o_n.reshape(B, T_PAD, NSA_WIDTH)[:, :T]
    new_win = jnp.concatenate([win_buf, kv_w], axis=1)[:, T:]
    return merge_groups(o_r, o_n, lp), (kv_c, kv_s, new_win, wkv, shift_last)


def moe_ffn(h, lp):
    B, T, D = h.shape
    x = h.reshape(B * T, D)
    n = x.shape[0]
    f32 = jnp.float32
    scores = jax.nn.sigmoid((x @ lp['router_w']).astype(f32))
    biased = scores + lp['router_bias'].astype(f32)
    grp = biased.reshape(n, N_GROUPS, N_EXPERTS // N_GROUPS)
    grp_score = lax.top_k(grp, 2)[0].sum(-1)
    _, gidx = lax.top_k(grp_score, TOPK_GROUPS)
    gmask = jax.nn.one_hot(gidx, N_GROUPS, dtype=f32).sum(1)
    emask = jnp.repeat(gmask, N_EXPERTS // N_GROUPS, axis=1) > 0
    _, eidx = lax.top_k(jnp.where(emask, biased, -jnp.inf), TOP_K)
    w = jnp.take_along_axis(scores, eidx, axis=1)
    w = w / jnp.sum(w, axis=-1, keepdims=True) * ROUTED_SCALE

    nk = n * TOP_K
    flat_e = eidx.reshape(nk)
    order = jnp.argsort(flat_e)
    se = flat_e[order]
    counts = jnp.bincount(flat_e, length=N_EXPERTS)
    padded = (counts + MOE_BLOCK - 1) // MOE_BLOCK * MOE_BLOCK
    pad_end = jnp.cumsum(padded)
    dest = (pad_end - padded)[se] + jnp.arange(nk) - (jnp.cumsum(counts) - counts)[se]
    n_blocks = -(-(nk + N_EXPERTS * (MOE_BLOCK - 1)) // MOE_BLOCK)
    rows = n_blocks * MOE_BLOCK
    row_tok = jnp.zeros((rows,), jnp.int32).at[dest].set((order // TOP_K).astype(jnp.int32))
    row_w = jnp.zeros((rows,), f32).at[dest].set(w.reshape(nk)[order])
    blk_exp = jnp.minimum(jnp.searchsorted(pad_end, jnp.arange(n_blocks) * MOE_BLOCK, side='right'),
                          N_EXPERTS - 1)

    def expert_block(args):
        tok, wt, e = args
        xb = x[tok]
        hb = jax.nn.silu(xb @ lp['exp_w_gate'][e]) * (xb @ lp['exp_w_up'][e])
        return (hb @ lp['exp_w_down'][e]).astype(f32) * wt[:, None]

    out = lax.map(expert_block, (row_tok.reshape(n_blocks, MOE_BLOCK), row_w.reshape(n_blocks, MOE_BLOCK), blk_exp))
    routed = jnp.zeros((n, D), f32).at[row_tok].add(out.reshape(rows, D))
    shared = (jax.nn.silu(x @ lp['sh_w_gate']) * (x @ lp['sh_w_up'])) @ lp['sh_w_down']
    return (routed + shared.astype(f32)).astype(h.dtype).reshape(B, T, D)


def trunk_layer(x, mod, lp, mixer):
    B, _, D = x.shape
    mod = mod.reshape(B, 6, 1, D)
    sh1, sc1, gt1, sh2, sc2, gt2 = (mod[:, i] for i in range(6))
    g = lp['norm_g']
    h = (rms_norm(x, g[0]) * (1 + sc1) + sh1).astype(jnp.bfloat16)
    o, st = mixer(h)
    x = x + gt1 * rms_norm(o, g[1])
    h = (rms_norm(x, g[2]) * (1 + sc2) + sh2).astype(jnp.bfloat16)
    x = x + gt2 * rms_norm(moe_ffn_pallas(h, lp, 256 if x.shape[0] * x.shape[1] >= 4096 else 32), g[3])
    return x, st


def kernel(x_prompt, x_sample, c_prompt, c_sample, cache_cmp, cache_sel, state_win, state_wkv,
           state_shift, page_table, rel_bias, w_ada, b_ada, norm_g, w_in, w_out, rwkv_mu, rwkv_w0,
           rwkv_w_up, rwkv_a0, rwkv_a_up, rwkv_g_up, rwkv_k_k, rwkv_k_a, rwkv_r_k, rwkv_lnx_w,
           rwkv_lnx_b, cmp_pe, cmp_w, nsa_out_g, router_w, router_bias, exp_w_gate, exp_w_up,
           exp_w_down, sh_w_gate, sh_w_up, sh_w_down):
    l = 0
    lp = dict(w_ada=w_ada[l], b_ada=b_ada[l], norm_g=norm_g[l], w_in=w_in[l], w_out=w_out[l],
              rwkv_mu=rwkv_mu[l], rwkv_w0=rwkv_w0[l], rwkv_w_up=rwkv_w_up[l], rwkv_a0=rwkv_a0[l],
              rwkv_a_up=rwkv_a_up[l], rwkv_g_up=rwkv_g_up[l], rwkv_k_k=rwkv_k_k[l],
              rwkv_k_a=rwkv_k_a[l], rwkv_r_k=rwkv_r_k[l], rwkv_lnx_w=rwkv_lnx_w[l],
              rwkv_lnx_b=rwkv_lnx_b[l], cmp_pe=cmp_pe[l], cmp_w=cmp_w[l], nsa_out_g=nsa_out_g[l],
              router_w=router_w[l], router_bias=router_bias[l], exp_w_gate=exp_w_gate[l],
              exp_w_up=exp_w_up[l], exp_w_down=exp_w_down[l], sh_w_gate=sh_w_gate[l],
              sh_w_up=sh_w_up[l], sh_w_down=sh_w_down[l])
    lp['w_in_p'] = pad_w_in(w_in[l])
    nb_p = x_prompt.shape[0]
    mod = pallas_matmul(jax.nn.silu(jnp.concatenate([c_prompt, c_sample], axis=0)), lp['w_ada']) + lp['b_ada']
    yp, st_p = trunk_layer(x_prompt, mod[:nb_p], lp, lambda h: mixer_prompt(h, lp, rel_bias))
    ys, st_s = trunk_layer(x_sample, mod[nb_p:], lp, lambda h: mixer_sample(
        h, lp, rel_bias, cache_cmp[l], cache_sel[l], state_win[l], state_wkv[l], state_shift[l], page_table))
    p_cmp, p_sel, p_win, p_wkv, p_shift = [a[None] for a in st_p]
    s_cmp, s_sel, s_win, s_wkv, s_shift = [a[None] for a in st_s]
    return (yp, ys, p_cmp, p_sel, p_win, p_wkv, p_shift, s_cmp, s_sel, s_win, s_wkv, s_shift)
```

```python
import functools
import math

import jax
import jax.numpy as jnp
from jax import lax
from jax.experimental import pallas as pl
from jax.experimental.pallas import tpu as pltpu

D_MODEL = 4096
RWKV_WIDTH = D_MODEL // 2
NSA_WIDTH = D_MODEL - RWKV_WIDTH
RWKV_HEAD_DIM = 64
RWKV_HEADS = RWKV_WIDTH // RWKV_HEAD_DIM
DECAY_LORA = 96
AAA_LORA = 96
GATE_LORA = 256
RWKV_PROJ = 3 * RWKV_WIDTH + DECAY_LORA + AAA_LORA + GATE_LORA
RWKV_SPLITS = (RWKV_WIDTH, 2 * RWKV_WIDTH, 3 * RWKV_WIDTH, 3 * RWKV_WIDTH + DECAY_LORA,
               3 * RWKV_WIDTH + DECAY_LORA + AAA_LORA)
LNX_EPS = 64e-5
NSA_HEAD_DIM = 128
NSA_HEADS = NSA_WIDTH // NSA_HEAD_DIM
NSA_KV_HEADS = 4
NSA_GROUP = NSA_HEADS // NSA_KV_HEADS
CMP_BLOCK = 32
SEL_BLOCK = 64
N_SEL = 16
WINDOW = 512
Q_CHUNK = 32
FORCE_SCORE = 1e4
NSA_KV_WIDTH = 2 * NSA_KV_HEADS * NSA_HEAD_DIM
NUM_BUCKETS = 32
MAX_DISTANCE = 128
N_EXPERTS = 64
N_GROUPS = 8
TOPK_GROUPS = 4
TOP_K = 8
ROUTED_SCALE = 2.5
MOE_BLOCK = 128
RMS_EPS = 1e-6

VMEM_LIMIT_BYTES = 56 * 1024 * 1024


def _matmul_kernel(x_ref, w_ref, o_ref):
    o_ref[...] = jnp.dot(x_ref[...].astype(jnp.bfloat16), w_ref[...].astype(jnp.bfloat16),
                         preferred_element_type=jnp.float32)


def pallas_matmul(x, w, tn=512):
    m, k = x.shape
    n = w.shape[1]
    row_mult = 16 if x.dtype == jnp.bfloat16 else 8
    mp = -(-m // row_mult) * row_mult
    tm = min(1024 if x.dtype == jnp.bfloat16 else 512, mp)
    mp = -(-mp // tm) * tm
    np_ = -(-n // tn) * tn
    if mp != m:
        x = jnp.pad(x, ((0, mp - m), (0, 0)))
    if np_ != n:
        w = jnp.pad(w, ((0, 0), (0, np_ - n)))
    out = pl.pallas_call(
        _matmul_kernel,
        grid=(mp // tm, np_ // tn),
        in_specs=[pl.BlockSpec((tm, k), lambda i, j: (i, 0)),
                  pl.BlockSpec((k, tn), lambda i, j: (0, j))],
        out_specs=pl.BlockSpec((tm, tn), lambda i, j: (i, j)),
        out_shape=jax.ShapeDtypeStruct((mp, np_), jnp.float32),
        compiler_params=pltpu.CompilerParams(
            dimension_semantics=("arbitrary", "arbitrary"), vmem_limit_bytes=VMEM_LIMIT_BYTES),
        name="matmul",
    )(x, w)
    return out[:m, :n]


def mm3(h, w):
    b, t, d = h.shape
    return pallas_matmul(h.reshape(b * t, d), w).reshape(b, t, w.shape[1])


def rms_norm(x, g):
    xf = x.astype(jnp.float32)
    y = xf * lax.rsqrt(jnp.mean(xf * xf, axis=-1, keepdims=True) + RMS_EPS)
    return (y * g.astype(jnp.float32)).astype(x.dtype)


def rel_bucket(dist):
    max_exact = NUM_BUCKETS // 2
    n = jnp.maximum(dist, 0)
    nf = jnp.maximum(n, max_exact).astype(jnp.float32)
    large = max_exact + (jnp.log(nf / max_exact) / math.log(MAX_DISTANCE / max_exact)
                         * (NUM_BUCKETS - max_exact)).astype(jnp.int32)
    return jnp.where(n < max_exact, n, jnp.minimum(large, NUM_BUCKETS - 1))


def masked_softmax(logits, mask):
    logits = jnp.where(mask, logits.astype(jnp.float32), -jnp.inf)
    m = jnp.max(logits, axis=-1, keepdims=True)
    p = jnp.exp(logits - jnp.where(jnp.isfinite(m), m, 0.0))
    s = jnp.sum(p, axis=-1, keepdims=True)
    return p / jnp.where(s > 0, s, 1.0)


LANE = 128
P_XW = 3 * RWKV_WIDTH
P_XA = P_XW + LANE
P_XG = P_XA + LANE
P_Q = P_XG + GATE_LORA
P_KVC = P_Q + NSA_WIDTH
P_KVS = P_KVC + NSA_KV_WIDTH
P_KVW = P_KVS + NSA_KV_WIDTH
P_GATES = P_KVW + NSA_KV_WIDTH
P_TOTAL = P_GATES + NSA_KV_HEADS * LANE
N_GATE_COLS = 3 * NSA_GROUP


def pad_w_in(w_in):
    d = w_in.shape[0]
    z = lambda n: jnp.zeros((d, n), w_in.dtype)
    o_xw, o_xa, o_xg = RWKV_SPLITS[2], RWKV_SPLITS[3], RWKV_SPLITS[4]
    o_g = RWKV_PROJ + NSA_WIDTH + 3 * NSA_KV_WIDTH
    parts = [w_in[:, :o_xw], w_in[:, o_xw:o_xa], z(LANE - DECAY_LORA), w_in[:, o_xa:o_xg], z(LANE - AAA_LORA),
             w_in[:, o_xg:o_g]]
    for kh in range(NSA_KV_HEADS):
        parts += [w_in[:, o_g + kh * N_GATE_COLS:o_g + (kh + 1) * N_GATE_COLS], z(LANE - N_GATE_COLS)]
    return jnp.concatenate(parts, axis=1).astype(jnp.bfloat16)


def unpad_rwkv(proj_p):
    return jnp.concatenate([proj_p[..., :P_XW], proj_p[..., P_XW:P_XW + DECAY_LORA],
                            proj_p[..., P_XA:P_XA + AAA_LORA], proj_p[..., P_XG:P_XG + GATE_LORA]], axis=-1)


def bias_lookup(tab, idx):
    sel = jnp.asarray(idx.reshape(-1, 1), jnp.int32) == jnp.arange(tab.shape[0], dtype=jnp.int32)[None, :]
    out = jnp.dot(sel.astype(jnp.float32), tab, precision=lax.Precision.HIGHEST)
    return out.reshape(idx.shape + (tab.shape[1],))


def bucket_table(n):
    import numpy as np
    d = np.arange(n)
    max_exact = NUM_BUCKETS // 2
    nf = np.maximum(d, max_exact).astype(np.float64)
    large = max_exact + (np.log(nf / max_exact) / math.log(MAX_DISTANCE / max_exact)
                         * (NUM_BUCKETS - max_exact)).astype(np.int64)
    return np.where(d < max_exact, d, np.minimum(large, NUM_BUCKETS - 1)).astype(np.int32)


def _compress_kernel(x_ref, w_ref, pe_ref, o_ref, *, nblk):
    half = nblk // 2
    acc = jnp.zeros((nblk, NSA_HEAD_DIM), jnp.float32)
    for c in range(CMP_BLOCK):
        xe = x_ref[pl.ds(c, half, stride=2 * CMP_BLOCK), :]
        xo = x_ref[pl.ds(CMP_BLOCK + c, half, stride=2 * CMP_BLOCK), :]
        lhs = jnp.concatenate([xe, xo], axis=0) + pe_ref[c:c + 1, :]
        acc = acc + jnp.dot(lhs.astype(jnp.bfloat16), w_ref[c].astype(jnp.bfloat16),
                            preferred_element_type=jnp.float32)
    o_ref[...] = acc


def _compress_dense_kernel(x_ref, w_ref, pe_ref, o_ref, *, nblk):
    f32, bf16 = jnp.float32, jnp.bfloat16
    L = nblk * CMP_BLOCK
    half = nblk // 2
    rowc = lax.broadcasted_iota(jnp.int32, (L, 1), 0) & (CMP_BLOCK - 1)
    xb = (x_ref[...] + jnp.concatenate([pe_ref[...]] * nblk, axis=0)).astype(bf16)
    y = jnp.zeros((L, NSA_HEAD_DIM), f32)
    for c in range(CMP_BLOCK):
        y = y + jnp.dot(jnp.where(rowc == c, xb, jnp.zeros_like(xb)), w_ref[c].astype(bf16), preferred_element_type=f32)
    n_i = lax.broadcasted_iota(jnp.int32, (nblk, L), 0)
    r_i = lax.broadcasted_iota(jnp.int32, (nblk, L), 1)
    want = jnp.where(n_i < half, 2 * n_i, 2 * (n_i - half) + 1)
    bsum = ((r_i // CMP_BLOCK) == want).astype(bf16)
    yh = y.astype(bf16)
    ym = (y - yh.astype(f32)).astype(bf16)
    yl = (y - yh.astype(f32) - ym.astype(f32)).astype(bf16)
    d = lambda a: jnp.dot(bsum, a, preferred_element_type=f32)
    o_ref[...] = d(yh) + (d(ym) + d(yl))


def compress_prompt(proj_p, cmp_pe, cmp_w, B, T, col0=P_KVC):
    nblk = T // CMP_BLOCK
    hd, KVH = NSA_HEAD_DIM, NSA_KV_HEADS
    return pl.pallas_call(
        functools.partial(_compress_dense_kernel if nblk >= 16 else _compress_kernel, nblk=nblk),
        grid=(B, 2, KVH),
        in_specs=[pl.BlockSpec((T, hd), lambda b, j, kh: (b, col0 // hd + j * KVH + kh)),
                  pl.BlockSpec((None, CMP_BLOCK, hd, hd), lambda b, j, kh: (j, 0, 0, 0)),
                  pl.BlockSpec((None, CMP_BLOCK, hd), lambda b, j, kh: (j, 0, 0))],
        out_specs=pl.BlockSpec((None, None, None, nblk, hd), lambda b, j, kh: (b, j, kh, 0, 0)),
        out_shape=jax.ShapeDtypeStruct((B, 2, KVH, nblk, hd), jnp.float32),
        compiler_params=pltpu.CompilerParams(dimension_semantics=("arbitrary",) * 3, vmem_limit_bytes=VMEM_LIMIT_BYTES),
        name="compress_prompt",
    )(proj_p, cmp_w, cmp_pe)


NSA_TQ = 128
NEG_BIG = -1e30


def _nsa_prompt_kernel(q_ref, gate_ref, ks_ref, vs_ref, kw_ref, vw_ref, kc_ref, vc_ref, bt_ref, bc_ref, es_ref,
                       o_ref, sel_ref, qs_ref, m_ref, l_ref, acc_ref, *, n_sb):
    f32, bf16 = jnp.float32, jnp.bfloat16
    tq, G, hd = NSA_TQ, NSA_GROUP, NSA_HEAD_DIM
    qi = pl.program_id(2)
    t0 = qi * tq
    scale = NSA_HEAD_DIM ** -0.5
    qs_ref[...] = jnp.transpose(jnp.concatenate([q_ref[:, g * hd:(g + 1) * hd] for g in range(G)], axis=0)).astype(bf16)
    tlane = t0 + lax.broadcasted_iota(jnp.int32, (1, tq), 1)

    ncmp = kc_ref.shape[0]
    brow = lax.broadcasted_iota(jnp.int32, (ncmp, 1), 0)
    blk_id = jnp.where(brow < ncmp // 2, 2 * brow, 2 * (brow - ncmp // 2) + 1)
    vis = (blk_id * CMP_BLOCK + (CMP_BLOCK - 1)) <= tlane
    vis = jnp.concatenate([vis] * G, axis=1)
    lc = jnp.dot(kc_ref[...].astype(bf16), qs_ref[...], preferred_element_type=f32) * scale + bc_ref[...]
    lc = jnp.where(vis, lc, -jnp.inf)
    mc = jnp.max(lc, axis=0, keepdims=True)
    pc = jnp.exp(lc - jnp.where(mc > -jnp.inf, mc, 0.0))
    sc = jnp.sum(pc, axis=0, keepdims=True)
    pc = pc / jnp.where(sc > 0, sc, 1.0)
    oct = lax.dot_general(vc_ref[...].astype(bf16), pc.astype(bf16), (((0,), (0,)), ((), ())),
                          preferred_element_type=f32)

    pg = pc[:, 0:tq]
    for g in range(1, G):
        pg = pg + pc[:, g * tq:(g + 1) * tq]
    imp = pg[:n_sb] + pg[n_sb:]
    cur = tlane // SEL_BLOCK
    bl = lax.broadcasted_iota(jnp.int32, (n_sb, tq), 0)
    force = (bl == 0) | (bl == cur) | (bl == cur - 1)
    score = jnp.where(force, FORCE_SCORE, imp)
    score = jnp.where(bl <= cur, score, -jnp.inf)
    rank = jnp.zeros((n_sb, tq), jnp.int32)
    for i in range(n_sb):
        si = score[i:i + 1, :]
        ahead = (si > score) | ((si == score) & (i < bl))
        rank = rank + ahead.astype(jnp.int32)
    sel_ref[...] = (rank < N_SEL).astype(bf16)
    gate = jax.nn.sigmoid(gate_ref[...])
    for g in range(G):
        o_ref[:, g * hd:(g + 1) * hd] = gate[:, 3 * g:3 * g + 1] * jnp.transpose(oct[:, g * tq:(g + 1) * tq])

    kcol = lax.broadcasted_iota(jnp.int32, (tq, 1), 0)
    tlane = t0 + lax.broadcasted_iota(jnp.int32, (1, tq), 1)

    def attend(c, carry, k_ref, v_ref, selected):
        s0 = pl.multiple_of(c * tq, tq)
        dist = tlane - (s0 + kcol)
        if selected:
            ok = (jnp.dot(es_ref[c], sel_ref[...], preferred_element_type=f32) > 0.5) & (dist >= 0)
        else:
            ok = (dist >= 0) & (dist <= WINDOW)
        ok = jnp.concatenate([ok] * G, axis=1)
        k = k_ref[pl.ds(s0, tq), :].astype(bf16)
        vt = jnp.transpose(v_ref[pl.ds(s0, tq), :]).astype(bf16)
        s = jnp.dot(k, qs_ref[...], preferred_element_type=f32) * scale + bt_ref[jnp.minimum(qi - c, 2)]
        s = jnp.where(ok, s, NEG_BIG)
        m_old = m_ref[...]
        m_new = jnp.maximum(m_old, jnp.max(s, axis=0, keepdims=True))
        p = jnp.exp(s - m_new)
        alpha = jnp.exp(m_old - m_new)
        l_ref[...] = alpha * l_ref[...] + jnp.sum(p, axis=0, keepdims=True)
        acc_ref[...] = alpha * acc_ref[...] + jnp.dot(vt, p.astype(bf16), preferred_element_type=f32)
        m_ref[...] = m_new
        return carry

    def run_branch(lo, k_ref, v_ref, selected, gate_col):
        m_ref[...] = jnp.full(m_ref.shape, NEG_BIG, f32)
        l_ref[...] = jnp.zeros(l_ref.shape, f32)
        acc_ref[...] = jnp.zeros(acc_ref.shape, f32)
        lax.fori_loop(lo, qi + 1, functools.partial(attend, k_ref=k_ref, v_ref=v_ref, selected=selected), 0)
        ot = acc_ref[...] / l_ref[...]
        for g in range(G):
            o_ref[:, g * hd:(g + 1) * hd] += (gate[:, 3 * g + gate_col:3 * g + gate_col + 1]
                                              * jnp.transpose(ot[:, g * tq:(g + 1) * tq]))

    run_branch(0, ks_ref, vs_ref, True, 1)
    run_branch(jnp.maximum(qi - WINDOW // tq, 0), kw_ref, vw_ref, False, 2)


def nsa_prompt(proj_p, kvc, rel_bias, B, T):
    import numpy as np
    tq, G, hd, KVH = NSA_TQ, NSA_GROUP, NSA_HEAD_DIM, NSA_KV_HEADS
    nq = T // tq
    n_sb = T // SEL_BLOCK
    ncmp = T // CMP_BLOCK
    assert T % tq == 0 and T % SEL_BLOCK == 0 and ncmp == 2 * n_sb
    bucket = bucket_table(max(T, 3 * tq))
    tab = rel_bias.astype(jnp.float32)
    ii, jj = np.meshgrid(np.arange(tq), np.arange(tq), indexing="ij")
    didx = np.stack([bucket[np.maximum(ii - jj, 0)], bucket[tq + ii - jj], np.full((tq, tq), NUM_BUCKETS - 1)])
    assert bucket[tq + 1] == NUM_BUCKETS - 1
    bt = bias_lookup(tab, didx).reshape(3, tq, tq, KVH, G).transpose(3, 0, 2, 4, 1).reshape(KVH, 3, tq, G * tq)
    blk = np.concatenate([np.arange(0, ncmp, 2), np.arange(1, ncmp, 2)])
    dc = np.arange(T)[:, None] - (blk * CMP_BLOCK + CMP_BLOCK - 1)[None, :]
    bc = bias_lookup(tab, bucket[np.maximum(dc, 0)]).reshape(nq, tq, ncmp, KVH, G).transpose(3, 0, 2, 4, 1).reshape(KVH, nq, ncmp, G * tq)
    es = (np.arange(n_sb)[None, :, None] == (np.arange(nq)[:, None, None] * (tq // SEL_BLOCK)
                                              + np.arange(tq)[None, None, :] // SEL_BLOCK))
    es = jnp.asarray(np.swapaxes(es, 1, 2), jnp.bfloat16)
    col = lambda off: off // hd
    kv_spec = lambda off: pl.BlockSpec((T, hd), lambda b, kh, i: (b, col(off) + kh))
    return pl.pallas_call(
        functools.partial(_nsa_prompt_kernel, n_sb=n_sb),
        grid=(B, KVH, nq),
        in_specs=[pl.BlockSpec((tq, G * hd), lambda b, kh, i: (b * nq + i, P_Q // (G * hd) + kh)),
                  pl.BlockSpec((tq, LANE), lambda b, kh, i: (b * nq + i, P_GATES // LANE + kh)),
                  kv_spec(P_KVS), kv_spec(P_KVS + KVH * hd), kv_spec(P_KVW), kv_spec(P_KVW + KVH * hd),
                  pl.BlockSpec((None, None, None, ncmp, hd), lambda b, kh, i: (b, 0, kh, 0, 0)),
                  pl.BlockSpec((None, None, None, ncmp, hd), lambda b, kh, i: (b, 1, kh, 0, 0)),
                  pl.BlockSpec((None, 3, tq, G * tq), lambda b, kh, i: (kh, 0, 0, 0)),
                  pl.BlockSpec((None, None, ncmp, G * tq), lambda b, kh, i: (kh, i, 0, 0)),
                  pl.BlockSpec((nq, tq, n_sb), lambda b, kh, i: (0, 0, 0))],
        out_specs=pl.BlockSpec((tq, G * hd), lambda b, kh, i: (b * nq + i, kh)),
        out_shape=jax.ShapeDtypeStruct((B * T, NSA_WIDTH), jnp.float32),
        scratch_shapes=[pltpu.VMEM((n_sb, tq), jnp.bfloat16), pltpu.VMEM((hd, G * tq), jnp.bfloat16),
                        pltpu.VMEM((1, G * tq), jnp.float32), pltpu.VMEM((1, G * tq), jnp.float32),
                        pltpu.VMEM((hd, G * tq), jnp.float32)],
        compiler_params=pltpu.CompilerParams(dimension_semantics=("arbitrary",) * 3, vmem_limit_bytes=VMEM_LIMIT_BYTES),
        name="nsa_prompt",
    )(proj_p, proj_p, proj_p, proj_p, proj_p, proj_p, kvc, kvc, bt, bc, es)


SAMPLE_PG = 8
T_PAD = 8


def _compress_paged_kernel(pt_ref, *refs, nb):
    del pt_ref
    page_refs = refs[:SAMPLE_PG]
    w_ref, pe_ref, o_ref = refs[SAMPLE_PG:]
    jk, hd = 2 * NSA_KV_HEADS, NSA_HEAD_DIM
    rows = SAMPLE_PG * nb * jk
    acc = jnp.zeros((rows, 2 * hd), jnp.float32)
    for c in range(CMP_BLOCK):
        x = jnp.concatenate([pr[pl.ds(c, nb, stride=CMP_BLOCK), :, :] for pr in page_refs], axis=0)
        lhs = (x + pe_ref[c]).reshape(rows, hd).astype(jnp.bfloat16)
        acc = acc + jnp.dot(lhs, w_ref[c], preferred_element_type=jnp.float32)
    sub = lax.broadcasted_iota(jnp.int32, (rows, 1), 0) & (jk - 1)
    out = jnp.where(sub < NSA_KV_HEADS, acc[:, :hd], acc[:, hd:])
    o_ref[...] = out.reshape(SAMPLE_PG * nb, jk, hd)


def compress_paged(pool, page_table, cmp_pe, cmp_w):
    n_phys, page = pool.shape[:2]
    B, n_pages = page_table.shape
    jk, hd = 2 * NSA_KV_HEADS, NSA_HEAD_DIM
    nb = page // CMP_BLOCK
    assert n_pages % SAMPLE_PG == 0 and page % CMP_BLOCK == 0
    pool3 = pool.reshape(n_phys, page, jk, hd)
    w01 = jnp.concatenate([cmp_w[0], cmp_w[1]], axis=-1).astype(jnp.bfloat16)
    pe8 = jnp.repeat(jnp.swapaxes(cmp_pe, 0, 1), NSA_KV_HEADS, axis=1)
    page_spec = lambda i: pl.BlockSpec((None, page, jk, hd),
                                       lambda b, g, pt: (pt[b * n_pages + g * SAMPLE_PG + i], 0, 0, 0))
    return pl.pallas_call(
        functools.partial(_compress_paged_kernel, nb=nb),
        grid_spec=pltpu.PrefetchScalarGridSpec(
            num_scalar_prefetch=1, grid=(B, n_pages // SAMPLE_PG),
            in_specs=[page_spec(i) for i in range(SAMPLE_PG)]
            + [pl.BlockSpec((CMP_BLOCK, hd, 2 * hd), lambda b, g, pt: (0, 0, 0)),
               pl.BlockSpec((CMP_BLOCK, jk, hd), lambda b, g, pt: (0, 0, 0))],
            out_specs=pl.BlockSpec((None, SAMPLE_PG * nb, jk, hd), lambda b, g, pt: (b, g, 0, 0))),
        out_shape=jax.ShapeDtypeStruct((B, n_pages * nb, jk, hd), jnp.float32),
        compiler_params=pltpu.CompilerParams(dimension_semantics=("arbitrary",) * 2, vmem_limit_bytes=VMEM_LIMIT_BYTES),
        name="compress_paged",
    )(page_table.reshape(-1), *([pool3] * SAMPLE_PG), w01, pe8)


def _nsa_sample_kernel(pt_ref, *refs, past, page, n_steps, n_sb, half):
    del pt_ref
    f32, bf16 = jnp.float32, jnp.bfloat16
    KVH, G, hd = NSA_KV_HEADS, NSA_GROUP, NSA_HEAD_DIM
    q_refs = refs[0:KVH]
    gate_ref, ksn_ref, vsn_ref, kwn_ref, vwn_ref, kvc_ref, bc_ref = refs[KVH:KVH + 7]
    page_refs = refs[KVH + 7:KVH + 7 + SAMPLE_PG]
    es_ref, bs_ref, bn_ref, win_ref, bw_ref, o_ref, m_ref, l_ref, acc_ref, oc_ref, sel_ref = refs[KVH + 7 + SAMPLE_PG:]
    g_step = pl.program_id(1)
    R = G * T_PAD
    scale = NSA_HEAD_DIM ** -0.5
    nt = (((1,), (1,)), ((), ()))
    t_row = past + (lax.broadcasted_iota(jnp.int32, (R, 1), 0) & (T_PAD - 1))
    qs = [jnp.concatenate([q_refs[kh][:, g * hd:(g + 1) * hd] for g in range(G)], axis=0).astype(bf16) for kh in range(KVH)]

    @pl.when(g_step == 0)
    def _():
        m_ref[...] = jnp.full(m_ref.shape, NEG_BIG, f32)
        l_ref[...] = jnp.zeros(l_ref.shape, f32)
        acc_ref[...] = jnp.zeros(acc_ref.shape, f32)
        lane = lax.broadcasted_iota(jnp.int32, (1, 2 * half), 1)
        lo = lane < half
        blk_id = jnp.where(lo, 2 * lane, 2 * (lane - half) + 1)
        real = jnp.where(lo, lane, lane - half) < n_sb
        vis = real & (blk_id * CMP_BLOCK + (CMP_BLOCK - 1) <= t_row)
        tpos = past + lax.broadcasted_iota(jnp.int32, (T_PAD, 1), 0)
        cur = tpos // SEL_BLOCK
        bl = lax.broadcasted_iota(jnp.int32, (T_PAD, half), 1)
        force = (bl == 0) | (bl == cur) | (bl == cur - 1)
        for kh in range(KVH):
            lc = lax.dot_general(qs[kh], kvc_ref[kh].astype(bf16), nt, preferred_element_type=f32) * scale + bc_ref[kh]
            lc = jnp.where(vis, lc, -jnp.inf)
            mc = jnp.max(lc, axis=-1, keepdims=True)
            pc = jnp.exp(lc - jnp.where(mc > -jnp.inf, mc, 0.0))
            sc = jnp.sum(pc, axis=-1, keepdims=True)
            pc = pc / jnp.where(sc > 0, sc, 1.0)
            oc_ref[kh * R:(kh + 1) * R, :] = jnp.dot(pc.astype(bf16), kvc_ref[KVH + kh].astype(bf16),
                                                     preferred_element_type=f32)
            pg = pc[0:T_PAD]
            for g in range(1, G):
                pg = pg + pc[g * T_PAD:(g + 1) * T_PAD]
            imp = pg[:, :half] + pg[:, half:]
            score = jnp.where(force, FORCE_SCORE, imp)
            score = jnp.where((bl <= cur) & (bl < n_sb), score, -jnp.inf)
            rank = jnp.zeros((T_PAD, half), jnp.int32)
            for i in range(n_sb):
                si = score[:, i:i + 1]
                rank = rank + ((si > score) | ((si == score) & (i < bl))).astype(jnp.int32)
            selm = ((rank < N_SEL) & (bl < n_sb)).astype(bf16)
            sel_ref[kh * R:(kh + 1) * R, :] = jnp.concatenate([selm] * G, axis=0)

    last = g_step == n_steps - 1
    bsel = bs_ref[jnp.where(last, 1, 0)]
    okf = jnp.dot(sel_ref[...], es_ref[...], preferred_element_type=f32) > 0.5
    for kh in range(KVH):
        rs = slice(kh * R, (kh + 1) * R)
        k = jnp.concatenate([pr[pl.ds(kh, page, stride=2 * KVH), :] for pr in page_refs], axis=0).astype(bf16)
        v = jnp.concatenate([pr[pl.ds(KVH + kh, page, stride=2 * KVH), :] for pr in page_refs], axis=0).astype(bf16)
        s = lax.dot_general(qs[kh], k, nt, preferred_element_type=f32) * scale + bsel[rs]
        ok = okf[rs]
        s = jnp.where(ok, s, NEG_BIG)
        m_old = m_ref[rs]
        m_new = jnp.maximum(m_old, jnp.max(s, axis=-1, keepdims=True))
        p = jnp.where(ok, jnp.exp(s - m_new), 0.0)
        alpha = jnp.exp(m_old - m_new)
        l_ref[rs] = alpha * l_ref[rs] + jnp.sum(p, axis=-1, keepdims=True)
        acc_ref[rs] = alpha * acc_ref[rs] + jnp.dot(p.astype(bf16), v, preferred_element_type=f32)
        m_ref[rs] = m_new

    @pl.when(last)
    def _():
        gate = jax.nn.sigmoid(gate_ref[...])
        ncol = lax.broadcasted_iota(jnp.int32, (1, T_PAD), 1)
        wb = win_ref.shape[0] // (2 * KVH)
        wcol = lax.broadcasted_iota(jnp.int32, (1, wb), 1)
        for kh in range(KVH):
            rs = slice(kh * R, (kh + 1) * R)
            cs = slice(kh * hd, (kh + 1) * hd)
            s = lax.dot_general(qs[kh], ksn_ref[:, cs].astype(bf16), nt, preferred_element_type=f32) * scale + bn_ref[kh]
            ok = (past + ncol) <= t_row
            s = jnp.where(ok, s, NEG_BIG)
            m_old = m_ref[rs]
            m_new = jnp.maximum(m_old, jnp.max(s, axis=-1, keepdims=True))
            p = jnp.where(ok, jnp.exp(s - m_new), 0.0)
            alpha = jnp.exp(m_old - m_new)
            l_s = alpha * l_ref[rs] + jnp.sum(p, axis=-1, keepdims=True)
            osel = (alpha * acc_ref[rs] + jnp.dot(p.astype(bf16), vsn_ref[:, cs].astype(bf16),
                                                  preferred_element_type=f32)) / l_s
            s1 = lax.dot_general(qs[kh], win_ref[pl.ds(kh, wb, stride=2 * KVH), :].astype(bf16), nt,
                                 preferred_element_type=f32) * scale + bw_ref[kh]
            d1 = t_row - (past - wb + wcol)
            ok1 = (d1 >= 0) & (d1 <= WINDOW)
            s1 = jnp.where(ok1, s1, NEG_BIG)
            s2 = lax.dot_general(qs[kh], kwn_ref[:, cs].astype(bf16), nt, preferred_element_type=f32) * scale + bn_ref[kh]
            s2 = jnp.where(ok, s2, NEG_BIG)
            mw = jnp.maximum(jnp.max(s1, axis=-1, keepdims=True), jnp.max(s2, axis=-1, keepdims=True))
            p1 = jnp.where(ok1, jnp.exp(s1 - mw), 0.0)
            p2 = jnp.where(ok, jnp.exp(s2 - mw), 0.0)
            l_w = jnp.sum(p1, axis=-1, keepdims=True) + jnp.sum(p2, axis=-1, keepdims=True)
            owin = (jnp.dot(p1.astype(bf16), win_ref[pl.ds(KVH + kh, wb, stride=2 * KVH), :].astype(bf16),
                            preferred_element_type=f32)
                    + jnp.dot(p2.astype(bf16), vwn_ref[:, cs].astype(bf16), preferred_element_type=f32)) / l_w
            oc = oc_ref[rs]
            for g in range(G):
                r = slice(g * T_PAD, (g + 1) * T_PAD)
                c0 = kh * LANE + 3 * g
                o_ref[:, (kh * G + g) * hd:(kh * G + g + 1) * hd] = (
                    gate[:, c0:c0 + 1] * oc[r] + gate[:, c0 + 1:c0 + 2] * osel[r] + gate[:, c0 + 2:c0 + 3] * owin[r])


def nsa_sample(proj_pad, kvc_t, pool_sel, win_buf, page_table, rel_bias, B, T):
    import numpy as np
    KVH, G, hd = NSA_KV_HEADS, NSA_GROUP, NSA_HEAD_DIM
    n_phys, page = pool_sel.shape[:2]
    n_pages = page_table.shape[1]
    past = n_pages * page
    wb = win_buf.shape[1]
    half = kvc_t.shape[2] // 2
    n_sb = -(-(past + T) // SEL_BLOCK)
    n_steps = n_pages // SAMPLE_PG
    keys = SAMPLE_PG * page
    R = G * T_PAD
    assert n_pages % SAMPLE_PG == 0 and page % SEL_BLOCK == 0 and T <= T_PAD and n_sb <= half and past % SEL_BLOCK == 0
    bucket = bucket_table(past + T_PAD + 1)
    tab = rel_bias.astype(jnp.float32)
    qpos = past + np.arange(T_PAD)

    def bias_rows(dist):
        b = bias_lookup(tab, bucket[np.maximum(dist, 0)])
        return b.reshape(T_PAD, dist.shape[1], KVH, G).transpose(2, 3, 0, 1).reshape(KVH, R, dist.shape[1])

    lane = np.arange(2 * half)
    blk = np.where(lane < half, 2 * lane, 2 * (lane - half) + 1)
    bc = bias_rows(qpos[:, None] - (blk * CMP_BLOCK + CMP_BLOCK - 1)[None, :])
    far = bias_rows(np.full((T_PAD, keys), past, np.int64))
    assert bucket[page + 1] == NUM_BUCKETS - 1
    near = bias_rows(qpos[:, None] - ((n_steps - 1) * keys + np.arange(keys))[None, :])
    bs = jnp.stack([far, near], axis=0).reshape(2, KVH * R, keys)
    bn = bias_rows(qpos[:, None] - (past + np.arange(T_PAD))[None, :])
    bw = bias_rows(qpos[:, None] - (past - wb + np.arange(wb))[None, :])
    es = (np.arange(half)[None, :, None] == (np.arange(n_steps)[:, None, None] * (keys // SEL_BLOCK)
                                              + np.arange(keys)[None, None, :] // SEL_BLOCK))
    es = jnp.asarray(es, jnp.bfloat16)
    pool2 = pool_sel.reshape(n_phys, page * 2 * KVH, hd)
    win2 = win_buf.reshape(B, wb * 2 * KVH, hd)
    wq = G * hd
    c = lambda *idx: (lambda b, g, pt: idx)
    row_blk = lambda w, col: pl.BlockSpec((T_PAD, w), lambda b, g, pt: (b, col))
    in_specs = ([row_blk(wq, P_Q // wq + kh) for kh in range(KVH)]
                + [row_blk(KVH * LANE, P_GATES // (KVH * LANE)),
                   row_blk(KVH * hd, P_KVS // (KVH * hd)), row_blk(KVH * hd, P_KVS // (KVH * hd) + 1),
                   row_blk(KVH * hd, P_KVW // (KVH * hd)), row_blk(KVH * hd, P_KVW // (KVH * hd) + 1),
                   pl.BlockSpec((None, 2 * KVH, 2 * half, hd), lambda b, g, pt: (b, 0, 0, 0)),
                   pl.BlockSpec((KVH, R, 2 * half), c(0, 0, 0))]
                + [pl.BlockSpec((None, page * 2 * KVH, hd),
                                (lambda i: lambda b, g, pt: (pt[b * n_pages + g * SAMPLE_PG + i], 0, 0))(i))
                   for i in range(SAMPLE_PG)]
                + [pl.BlockSpec((None, half, keys), lambda b, g, pt: (g, 0, 0)),
                   pl.BlockSpec((2, KVH * R, keys), c(0, 0, 0)),
                   pl.BlockSpec((KVH, R, T_PAD), c(0, 0, 0)),
                   pl.BlockSpec((None, wb * 2 * KVH, hd), lambda b, g, pt: (b, 0, 0)),
                   pl.BlockSpec((KVH, R, wb), c(0, 0, 0))])
    f32 = jnp.float32
    return pl.pallas_call(
        functools.partial(_nsa_sample_kernel, past=past, page=page, n_steps=n_steps, n_sb=n_sb, half=half),
        grid_spec=pltpu.PrefetchScalarGridSpec(
            num_scalar_prefetch=1, grid=(B, n_steps), in_specs=in_specs,
            out_specs=pl.BlockSpec((T_PAD, NSA_WIDTH), lambda b, g, pt: (b, 0)),
            scratch_shapes=[pltpu.VMEM((KVH * R, 1), f32), pltpu.VMEM((KVH * R, 1), f32), pltpu.VMEM((KVH * R, hd), f32),
                            pltpu.VMEM((KVH * R, hd), f32), pltpu.VMEM((KVH * R, half), jnp.bfloat16)]),
        out_shape=jax.ShapeDtypeStruct((B * T_PAD, NSA_WIDTH), f32),
        compiler_params=pltpu.CompilerParams(dimension_semantics=("arbitrary",) * 2, vmem_limit_bytes=VMEM_LIMIT_BYTES),
        name="nsa_sample",
    )(page_table.reshape(-1), *([proj_pad] * (KVH + 5)), kvc_t, bc, *([pool2] * SAMPLE_PG), es, bs, bn, win2, bw)


RWKV_PAIR = 2 * RWKV_HEAD_DIM
HIGHEST = lax.Precision.HIGHEST


def _rwkv_kernel_one_pair(r_ref, k_ref, v_ref, xw_ref, xa_ref, xg_ref,
                 sr_ref, sk_ref, sv_ref, sxw_ref, sxa_ref, sxg_ref,
                 mr_ref, mk_ref, mv_ref, mxw_ref, mxa_ref, mxg_ref,
                 s0_ref, w0_ref, a0_ref, kk_ref, ka_ref, rk_ref, lnw_ref, lnb_ref,
                 wup_ref, aup_ref, gup_ref,
                 o_ref, sout_ref,
                 S_ref, cr_ref, ck_ref, cv_ref, cxw_ref, cxa_ref, cxg_ref, *, C, t_valid, n_chunks):
    f32, bf16 = jnp.float32, jnp.bfloat16
    hd, W = RWKV_HEAD_DIM, RWKV_PAIR
    ci = pl.program_id(2)
    lane = lax.broadcasted_iota(jnp.int32, (1, W), 1)
    h0 = lane < hd
    row = lax.broadcasted_iota(jnp.int32, (C, 1), 0)
    valid = (ci * C + row) < t_valid
    carries = ((cr_ref, sr_ref), (ck_ref, sk_ref), (cv_ref, sv_ref), (cxw_ref, sxw_ref), (cxa_ref, sxa_ref),
               (cxg_ref, sxg_ref))

    @pl.when(ci == 0)
    def _():
        S_ref[...] = jnp.zeros((W, W), f32)
        S_ref[0:hd, 0:hd] = s0_ref[0]
        S_ref[hd:W, hd:W] = s0_ref[1]
        for c_ref, s_ref in carries:
            c_ref[0:1, :] = s_ref[...]

    def shifted(x_ref, c_ref, mu_ref):
        x = x_ref[...]
        prev = jnp.where(row == 0, c_ref[0:1, :], pltpu.roll(x, 1, axis=0))
        c_ref[0:1, :] = x[C - 1:C, :]
        return x + (prev - x) * mu_ref[...]

    def seg_sum(x):
        s_lo = jnp.sum(jnp.where(h0, x, 0.0), axis=-1, keepdims=True)
        s_hi = jnp.sum(jnp.where(h0, 0.0, x), axis=-1, keepdims=True)
        return jnp.where(h0, s_lo, s_hi)

    def stack(x):
        return jnp.concatenate([jnp.where(h0, x, 0.0), jnp.where(h0, 0.0, x)], axis=0)

    mr = shifted(r_ref, cr_ref, mr_ref)
    mk = shifted(k_ref, ck_ref, mk_ref)
    mv = shifted(v_ref, cv_ref, mv_ref)
    mxw = shifted(xw_ref, cxw_ref, mxw_ref)
    mxa = shifted(xa_ref, cxa_ref, mxa_ref)
    mxg = shifted(xg_ref, cxg_ref, mxg_ref)

    z = w0_ref[...] + jnp.dot(jnp.tanh(mxw).astype(bf16), wup_ref[...].astype(bf16), preferred_element_type=f32)
    w_log = -jax.nn.softplus(-z) - 0.5
    lw = -jnp.exp(w_log)
    a = jax.nn.sigmoid(a0_ref[...] + jnp.dot(mxa.astype(bf16), aup_ref[...].astype(bf16), preferred_element_type=f32))
    g = jnp.dot(jax.nn.sigmoid(mxg).astype(bf16), gup_ref[...].astype(bf16), preferred_element_type=f32)
    kk = mk * kk_ref[...]
    kk = kk / jnp.maximum(jnp.sqrt(seg_sum(kk * kk)), 1e-12)
    kp = mk * (1.0 + (a - 1.0) * ka_ref[...])
    lw = jnp.where(valid, lw, 0.0)
    a_t = jnp.where(valid, -kk, 0.0)
    b_t = jnp.where(valid, kk * a, 0.0)
    k_t = jnp.where(valid, kp, 0.0)
    v_t = jnp.where(valid, mv, 0.0)

    ti = lax.broadcasted_iota(jnp.int32, (C, C), 0)
    tj = lax.broadcasted_iota(jnp.int32, (C, C), 1)
    cl = jnp.dot((tj <= ti).astype(f32), lw, precision=HIGHEST, preferred_element_type=f32)
    e_neg = jnp.exp(-cl)
    At = a_t * jnp.exp(cl - lw)
    Bt = b_t * e_neg
    Kt = k_t * e_neg
    Rt = mr * jnp.exp(cl)
    G = jnp.concatenate([stack(At), stack(Rt)], axis=0).astype(bf16)
    Z = jnp.concatenate([Bt, Bt, Kt, Kt], axis=0).astype(bf16)
    nt = (((1,), (1,)), ((), ()))
    M = lax.dot_general(G, Z, nt, preferred_element_type=f32)
    ri = lax.broadcasted_iota(jnp.int32, (2 * C, 2 * C), 0)
    cj = lax.broadcasted_iota(jnp.int32, (2 * C, 2 * C), 1)
    same = (ri >= C) == (cj >= C)
    strict = same & ((cj & (C - 1)) < (ri & (C - 1)))
    incl = same & ((cj & (C - 1)) <= (ri & (C - 1)))
    L = jnp.where(strict, M[0:2 * C, 0:2 * C], 0.0)
    AK = jnp.where(strict, M[0:2 * C, 2 * C:4 * C], 0.0)
    RB = jnp.where(incl, M[2 * C:4 * C, 0:2 * C], 0.0)
    RK = jnp.where(incl, M[2 * C:4 * C, 2 * C:4 * C], 0.0)
    S = S_ref[...]
    GS = lax.dot_general(G, S.astype(bf16), nt, preferred_element_type=f32)
    Vs = stack(v_t)
    U = GS[0:2 * C] + jnp.dot(AK.astype(bf16), Vs.astype(bf16), preferred_element_type=f32)
    P = L
    levels = C.bit_length() - 1
    for lvl in range(levels):
        U = U + jnp.dot(P, U, precision=HIGHEST, preferred_element_type=f32)
        if lvl + 1 < levels:
            P = jnp.dot(P, P, precision=HIGHEST, preferred_element_type=f32)
    X = jnp.concatenate([U, Vs], axis=0).astype(bf16)
    Y = GS[2 * C:4 * C] + jnp.dot(jnp.concatenate([RB, RK], axis=1).astype(bf16), X, preferred_element_type=f32)
    y = Y[0:C] + Y[C:2 * C]
    dS = lax.dot_general(X, Z, (((0,), (0,)), ((), ())), preferred_element_type=f32)
    vi = lax.broadcasted_iota(jnp.int32, (W, W), 0)
    kj = lax.broadcasted_iota(jnp.int32, (W, W), 1)
    S_new = jnp.where((vi < hd) == (kj < hd), S + dS, 0.0) * jnp.exp(cl[C - 1:C, :])
    S_ref[...] = S_new

    mean = seg_sum(y) * (1.0 / hd)
    yc = y - mean
    var = seg_sum(yc * yc) * (1.0 / hd)
    yn = yc * lax.rsqrt(var + LNX_EPS) * lnw_ref[...] + lnb_ref[...]
    bonus = seg_sum(mr * kp * rk_ref[...]) * mv
    o_ref[...] = (yn + bonus) * g

    @pl.when(ci == n_chunks - 1)
    def _():
        sout_ref[0] = S_new[0:hd, 0:hd]
        sout_ref[1] = S_new[hd:W, hd:W]


def rwkv_mix_one_pair(proj_p, shift_prev, wkv0, lp, B, T, t_valid, C):
    f32 = jnp.float32
    W, hd = RWKV_PAIR, RWKV_HEAD_DIM
    n_pairs = RWKV_WIDTH // W
    n_chunks = T // C
    assert T % C == 0 and C & (C - 1) == 0 and C >= 8

    def pad_vec(x):
        z = lambda n: jnp.zeros(x.shape[:-1] + (n,), x.dtype)
        return jnp.concatenate([x[..., :RWKV_SPLITS[2]], x[..., RWKV_SPLITS[2]:RWKV_SPLITS[3]], z(LANE - DECAY_LORA),
                                x[..., RWKV_SPLITS[3]:RWKV_SPLITS[4]], z(LANE - AAA_LORA), x[..., RWKV_SPLITS[4]:]], -1)

    shift_p = pad_vec(shift_prev.astype(f32))[:, None, :]
    mu_p = pad_vec(lp['rwkv_mu'].astype(f32))[None, :]
    pad_rows_to = lambda w: jnp.pad(w, ((0, LANE - w.shape[0]), (0, 0)))
    wup, aup = pad_rows_to(lp['rwkv_w_up']), pad_rows_to(lp['rwkv_a_up'])
    vec = lambda x: x.reshape(1, RWKV_WIDTH).astype(f32)
    nw = RWKV_WIDTH // W
    col_r, col_k, col_v = 0, nw, 2 * nw
    col_xw, col_xa, col_xg = P_XW // LANE, P_XA // LANE, P_XG // GATE_LORA
    row_spec = lambda w, cfn: pl.BlockSpec((C, w), lambda b, p, c: (b * n_chunks + c, cfn(p)))
    sh_spec = lambda w, cfn: pl.BlockSpec((None, 1, w), lambda b, p, c: (b, 0, cfn(p)))
    mu_spec = lambda w, cfn: pl.BlockSpec((1, w), lambda b, p, c: (0, cfn(p)))
    cols = [(W, lambda p: col_r + p), (W, lambda p: col_k + p), (W, lambda p: col_v + p),
            (LANE, lambda p: col_xw), (LANE, lambda p: col_xa), (GATE_LORA, lambda p: col_xg)]
    pvec = pl.BlockSpec((1, W), lambda b, p, c: (0, p))
    state_spec = pl.BlockSpec((None, 2, hd, hd), lambda b, p, c: (b, p, 0, 0))
    in_specs = ([row_spec(w, f) for w, f in cols] + [sh_spec(w, f) for w, f in cols] + [mu_spec(w, f) for w, f in cols]
                + [state_spec] + [pvec] * 7
                + [pl.BlockSpec((LANE, W), lambda b, p, c: (0, p)), pl.BlockSpec((LANE, W), lambda b, p, c: (0, p)),
                   pl.BlockSpec((GATE_LORA, W), lambda b, p, c: (0, p))])
    scratch = [pltpu.VMEM((W, W), f32)] + [pltpu.VMEM((8, w), f32) for w, _ in cols]
    o, s_fin = pl.pallas_call(
        functools.partial(_rwkv_kernel, C=C, t_valid=t_valid, n_chunks=n_chunks),
        grid=(B, n_pairs, n_chunks),
        in_specs=in_specs,
        out_specs=[pl.BlockSpec((C, W), lambda b, p, c: (b * n_chunks + c, p)), state_spec],
        out_shape=[jax.ShapeDtypeStruct((B * T, RWKV_WIDTH), f32),
                   jax.ShapeDtypeStruct((B, RWKV_HEADS, hd, hd), f32)],
        scratch_shapes=scratch,
        compiler_params=pltpu.CompilerParams(dimension_semantics=("arbitrary",) * 3, vmem_limit_bytes=VMEM_LIMIT_BYTES),
        name="rwkv_mix",
    )(*([proj_p] * 6), *([shift_p] * 6), *([mu_p] * 6), wkv0.astype(f32),
      vec(lp['rwkv_w0']), vec(lp['rwkv_a0']), vec(lp['rwkv_k_k']), vec(lp['rwkv_k_a']), vec(lp['rwkv_r_k']),
      vec(lp['rwkv_lnx_w']), vec(lp['rwkv_lnx_b']), wup, aup, lp['rwkv_g_up'])
    return o, s_fin


def _dot3(a, b):
    f32, bf16 = jnp.float32, jnp.bfloat16
    ah = a.astype(bf16)
    al = (a - ah.astype(f32)).astype(bf16)
    bh = b.astype(bf16)
    bl = (b - bh.astype(f32)).astype(bf16)
    d = lambda x, y: jnp.dot(x, y, preferred_element_type=f32)
    return d(ah, bh) + (d(ah, bl) + d(al, bh))


def _rwkv_kernel(r_ref, k_ref, v_ref, xw_ref, xa_ref, xg_ref,
                 sr_ref, sk_ref, sv_ref, sxw_ref, sxa_ref, sxg_ref,
                 mr_ref, mk_ref, mv_ref, mxw_ref, mxa_ref, mxg_ref,
                 s0_ref, w0_ref, a0_ref, kk_ref, ka_ref, rk_ref, lnw_ref, lnb_ref,
                 wup_ref, aup_ref, gup_ref,
                 o_ref, sout_ref,
                 S_ref, cr_ref, ck_ref, cv_ref, cxw_ref, cxa_ref, cxg_ref, *, C, NP, t_valid, n_chunks):
    f32, bf16 = jnp.float32, jnp.bfloat16
    hd, W = RWKV_HEAD_DIM, RWKV_PAIR
    ci = pl.program_id(2)
    lane = lax.broadcasted_iota(jnp.int32, (1, W), 1)
    h0 = lane < hd
    row = lax.broadcasted_iota(jnp.int32, (C, 1), 0)
    valid = (ci * C + row) < t_valid
    carries = ((cr_ref, sr_ref), (ck_ref, sk_ref), (cv_ref, sv_ref), (cxw_ref, sxw_ref), (cxa_ref, sxa_ref),
               (cxg_ref, sxg_ref))

    @pl.when(ci == 0)
    def _():
        S_ref[...] = jnp.zeros(S_ref.shape, f32)
        for p in range(NP):
            S_ref[p, 0:hd, 0:hd] = s0_ref[2 * p]
            S_ref[p, hd:W, hd:W] = s0_ref[2 * p + 1]
        for c_ref, s_ref in carries:
            c_ref[0:1, :] = s_ref[...]

    def shifted(x_ref, c_ref, mu_ref):
        x = x_ref[...]
        prev = jnp.where(row == 0, c_ref[0:1, :], pltpu.roll(x, 1, axis=0))
        c_ref[0:1, :] = x[C - 1:C, :]
        return x + (prev - x) * mu_ref[...]

    def seg_sum(x):
        s_lo = jnp.sum(jnp.where(h0, x, 0.0), axis=-1, keepdims=True)
        s_hi = jnp.sum(jnp.where(h0, 0.0, x), axis=-1, keepdims=True)
        return jnp.where(h0, s_lo, s_hi)

    def stack(x):
        return jnp.concatenate([jnp.where(h0, x, 0.0), jnp.where(h0, 0.0, x)], axis=0)

    mr_all = shifted(r_ref, cr_ref, mr_ref)
    mk_all = shifted(k_ref, ck_ref, mk_ref)
    mv_all = shifted(v_ref, cv_ref, mv_ref)
    mxw = shifted(xw_ref, cxw_ref, mxw_ref)
    mxa = shifted(xa_ref, cxa_ref, mxa_ref)
    mxg = shifted(xg_ref, cxg_ref, mxg_ref)

    z_all = w0_ref[...] + jnp.dot(jnp.tanh(mxw).astype(bf16), wup_ref[...].astype(bf16), preferred_element_type=f32)
    a_all = jax.nn.sigmoid(a0_ref[...] + jnp.dot(mxa.astype(bf16), aup_ref[...].astype(bf16), preferred_element_type=f32))
    g_all = jnp.dot(jax.nn.sigmoid(mxg).astype(bf16), gup_ref[...].astype(bf16), preferred_element_type=f32)

    ti = lax.broadcasted_iota(jnp.int32, (C, C), 0)
    tj = lax.broadcasted_iota(jnp.int32, (C, C), 1)
    tri = (tj <= ti).astype(f32)
    ri = lax.broadcasted_iota(jnp.int32, (2 * C, 2 * C), 0)
    cj = lax.broadcasted_iota(jnp.int32, (2 * C, 2 * C), 1)
    same = (ri >= C) == (cj >= C)
    strict = same & ((cj & (C - 1)) < (ri & (C - 1)))
    incl = same & ((cj & (C - 1)) <= (ri & (C - 1)))
    vi = lax.broadcasted_iota(jnp.int32, (W, W), 0)
    kj = lax.broadcasted_iota(jnp.int32, (W, W), 1)
    blockdiag = (vi < hd) == (kj < hd)
    nt = (((1,), (1,)), ((), ()))
    levels = C.bit_length() - 1

    stage = []
    for p in range(NP):
        ls = slice(p * W, (p + 1) * W)
        mr, mk, mv, a, g = mr_all[:, ls], mk_all[:, ls], mv_all[:, ls], a_all[:, ls], g_all[:, ls]
        w_log = -jax.nn.softplus(-z_all[:, ls]) - 0.5
        lw = -jnp.exp(w_log)
        kk = mk * kk_ref[:, ls]
        kk = kk / jnp.maximum(jnp.sqrt(seg_sum(kk * kk)), 1e-12)
        kp = mk * (1.0 + (a - 1.0) * ka_ref[:, ls])
        lw = jnp.where(valid, lw, 0.0)
        a_t = jnp.where(valid, -kk, 0.0)
        b_t = jnp.where(valid, kk * a, 0.0)
        k_t = jnp.where(valid, kp, 0.0)
        v_t = jnp.where(valid, mv, 0.0)

        cl = jnp.dot(tri, lw, precision=HIGHEST, preferred_element_type=f32)
        e_neg = jnp.exp(-cl)
        At = a_t * jnp.exp(cl - lw)
        Bt = b_t * e_neg
        Kt = k_t * e_neg
        Rt = mr * jnp.exp(cl)
        G = jnp.concatenate([stack(At), stack(Rt)], axis=0).astype(bf16)
        Z = jnp.concatenate([Bt, Bt, Kt, Kt], axis=0).astype(bf16)
        M = lax.dot_general(G, Z, nt, preferred_element_type=f32)
        L = jnp.where(strict, M[0:2 * C, 0:2 * C], 0.0)
        AK = jnp.where(strict, M[0:2 * C, 2 * C:4 * C], 0.0)
        RB = jnp.where(incl, M[2 * C:4 * C, 0:2 * C], 0.0)
        RK = jnp.where(incl, M[2 * C:4 * C, 2 * C:4 * C], 0.0)
        S = S_ref[p]
        GS = lax.dot_general(G, S.astype(bf16), nt, preferred_element_type=f32)
        Vs = stack(v_t)
        U = GS[0:2 * C] + jnp.dot(AK.astype(bf16), Vs.astype(bf16), preferred_element_type=f32)
        stage.append(dict(ls=ls, mr=mr, mv=mv, kp=kp, g=g, cl=cl, Z=Z, GS=GS, Vs=Vs, RB=RB, RK=RK, S=S, U=U, P=L))

    for lvl in range(levels):
        for st in stage:
            st['U'] = st['U'] + _dot3(st['P'], st['U'])
            if lvl + 1 < levels:
                st['P'] = _dot3(st['P'], st['P'])

    for p, st in enumerate(stage):
        ls, mr, mv, kp, g, cl, Z, GS, Vs, RB, RK, S, U = (st[n] for n in ('ls', 'mr', 'mv', 'kp', 'g', 'cl', 'Z', 'GS', 'Vs',
                                                                       'RB', 'RK', 'S', 'U'))
        X = jnp.concatenate([U, Vs], axis=0).astype(bf16)
        Y = GS[2 * C:4 * C] + jnp.dot(jnp.concatenate([RB, RK], axis=1).astype(bf16), X, preferred_element_type=f32)
        y = Y[0:C] + Y[C:2 * C]
        dS = lax.dot_general(X, Z, (((0,), (0,)), ((), ())), preferred_element_type=f32)
        S_new = jnp.where(blockdiag, S + dS, 0.0) * jnp.exp(cl[C - 1:C, :])
        S_ref[p] = S_new

        mean = seg_sum(y) * (1.0 / hd)
        yc = y - mean
        var = seg_sum(yc * yc) * (1.0 / hd)
        yn = yc * lax.rsqrt(var + LNX_EPS) * lnw_ref[:, ls] + lnb_ref[:, ls]
        bonus = seg_sum(mr * kp * rk_ref[:, ls]) * mv
        o_ref[:, ls] = (yn + bonus) * g

    @pl.when(ci == n_chunks - 1)
    def _():
        for p in range(NP):
            s_fin = S_ref[p]
            sout_ref[2 * p] = s_fin[0:hd, 0:hd]
            sout_ref[2 * p + 1] = s_fin[hd:W, hd:W]


def rwkv_mix(proj_p, shift_prev, wkv0, lp, B, T, t_valid, C, NP):
    f32 = jnp.float32
    hd = RWKV_HEAD_DIM
    W = NP * RWKV_PAIR
    n_steps = RWKV_WIDTH // W
    n_chunks = T // C
    assert T % C == 0 and C & (C - 1) == 0 and C >= 8 and RWKV_WIDTH % W == 0

    def pad_vec(x):
        z = lambda n: jnp.zeros(x.shape[:-1] + (n,), x.dtype)
        return jnp.concatenate([x[..., :RWKV_SPLITS[2]], x[..., RWKV_SPLITS[2]:RWKV_SPLITS[3]], z(LANE - DECAY_LORA),
                                x[..., RWKV_SPLITS[3]:RWKV_SPLITS[4]], z(LANE - AAA_LORA), x[..., RWKV_SPLITS[4]:]], -1)

    shift_p = pad_vec(shift_prev.astype(f32))[:, None, :]
    mu_p = pad_vec(lp['rwkv_mu'].astype(f32))[None, :]
    pad_rows_to = lambda w: jnp.pad(w, ((0, LANE - w.shape[0]), (0, 0)))
    wup, aup = pad_rows_to(lp['rwkv_w_up']), pad_rows_to(lp['rwkv_a_up'])
    vec = lambda x: x.reshape(1, RWKV_WIDTH).astype(f32)
    nw = RWKV_WIDTH // W
    col_r, col_k, col_v = 0, nw, 2 * nw
    col_xw, col_xa, col_xg = P_XW // LANE, P_XA // LANE, P_XG // GATE_LORA
    row_spec = lambda w, cfn: pl.BlockSpec((C, w), lambda b, p, c: (b * n_chunks + c, cfn(p)))
    sh_spec = lambda w, cfn: pl.BlockSpec((None, 1, w), lambda b, p, c: (b, 0, cfn(p)))
    mu_spec = lambda w, cfn: pl.BlockSpec((1, w), lambda b, p, c: (0, cfn(p)))
    cols = [(W, lambda p: col_r + p), (W, lambda p: col_k + p), (W, lambda p: col_v + p),
            (LANE, lambda p: col_xw), (LANE, lambda p: col_xa), (GATE_LORA, lambda p: col_xg)]
    pvec = pl.BlockSpec((1, W), lambda b, p, c: (0, p))
    state_spec = pl.BlockSpec((None, 2 * NP, hd, hd), lambda b, p, c: (b, p, 0, 0))
    in_specs = ([row_spec(w, f) for w, f in cols] + [sh_spec(w, f) for w, f in cols] + [mu_spec(w, f) for w, f in cols]
                + [state_spec] + [pvec] * 7
                + [pl.BlockSpec((LANE, W), lambda b, p, c: (0, p)), pl.BlockSpec((LANE, W), lambda b, p, c: (0, p)),
                   pl.BlockSpec((GATE_LORA, W), lambda b, p, c: (0, p))])
    scratch = [pltpu.VMEM((NP, RWKV_PAIR, RWKV_PAIR), f32)] + [pltpu.VMEM((8, w), f32) for w, _ in cols]
    o, s_fin = pl.pallas_call(
        functools.partial(_rwkv_kernel, C=C, NP=NP, t_valid=t_valid, n_chunks=n_chunks),
        grid=(B, n_steps, n_chunks),
        in_specs=in_specs,
        out_specs=[pl.BlockSpec((C, W), lambda b, p, c: (b * n_chunks + c, p)), state_spec],
        out_shape=[jax.ShapeDtypeStruct((B * T, RWKV_WIDTH), f32),
                   jax.ShapeDtypeStruct((B, RWKV_HEADS, hd, hd), f32)],
        scratch_shapes=scratch,
        compiler_params=pltpu.CompilerParams(dimension_semantics=("arbitrary",) * 3, vmem_limit_bytes=VMEM_LIMIT_BYTES),
        name="rwkv_mix",
    )(*([proj_p] * 6), *([shift_p] * 6), *([mu_p] * 6), wkv0.astype(f32),
      vec(lp['rwkv_w0']), vec(lp['rwkv_a0']), vec(lp['rwkv_k_k']), vec(lp['rwkv_k_a']), vec(lp['rwkv_r_k']),
      vec(lp['rwkv_lnx_w']), vec(lp['rwkv_lnx_b']), wup, aup, lp['rwkv_g_up'])
    return o, s_fin


def _expert_changed(be_ref, i):
    return jnp.logical_or(i == 0, be_ref[i] != be_ref[jnp.maximum(i - 1, 0)])


def _ffn_up_kernel(be_ref, nv_ref, x_ref, wg_ref, wu_ref, h_ref, wgb_ref, wub_ref):
    i = pl.program_id(0)
    live = i < nv_ref[0]

    @pl.when(live & _expert_changed(be_ref, i))
    def _():
        wgb_ref[...] = wg_ref[...].astype(jnp.bfloat16)
        wub_ref[...] = wu_ref[...].astype(jnp.bfloat16)

    @pl.when(live)
    def _():
        x = x_ref[...]
        hg = jnp.dot(x, wgb_ref[...], preferred_element_type=jnp.float32)
        hu = jnp.dot(x, wub_ref[...], preferred_element_type=jnp.float32)
        h_ref[...] = (jax.nn.silu(hg) * hu).astype(h_ref.dtype)

    @pl.when(i >= nv_ref[0])
    def _():
        h_ref[...] = jnp.zeros(h_ref.shape, h_ref.dtype)


def _ffn_down_kernel(be_ref, nv_ref, h_ref, wd_ref, rw_ref, o_ref, wdb_ref):
    i = pl.program_id(0)
    live = i < nv_ref[0]

    @pl.when(live & _expert_changed(be_ref, i))
    def _():
        wdb_ref[...] = wd_ref[...].astype(jnp.bfloat16)

    @pl.when(live)
    def _():
        o_ref[...] = jnp.dot(h_ref[...].astype(jnp.bfloat16), wdb_ref[...],
                             preferred_element_type=jnp.float32) * rw_ref[...]

    @pl.when(i >= nv_ref[0])
    def _():
        o_ref[...] = jnp.zeros(o_ref.shape, o_ref.dtype)


def expert_ffn(xs, w_gate, w_up, w_down, row_w, blk_exp, n_valid, blk):
    rows, d = xs.shape
    ff = w_gate.shape[-1]
    n_blocks = rows // blk
    params = pltpu.CompilerParams(dimension_semantics=("arbitrary",), vmem_limit_bytes=VMEM_LIMIT_BYTES)
    h = pl.pallas_call(
        _ffn_up_kernel,
        grid_spec=pltpu.PrefetchScalarGridSpec(
            num_scalar_prefetch=2, grid=(n_blocks,),
            in_specs=[pl.BlockSpec((blk, d), lambda i, be, nv: (i, 0)),
                      pl.BlockSpec((None, d, ff), lambda i, be, nv: (be[i], 0, 0)),
                      pl.BlockSpec((None, d, ff), lambda i, be, nv: (be[i], 0, 0))],
            out_specs=pl.BlockSpec((blk, ff), lambda i, be, nv: (i, 0)),
            scratch_shapes=[pltpu.VMEM((d, ff), jnp.bfloat16), pltpu.VMEM((d, ff), jnp.bfloat16)]),
        out_shape=jax.ShapeDtypeStruct((rows, ff), jnp.bfloat16),
        compiler_params=params, name="ffn_up",
    )(blk_exp, n_valid, xs, w_gate, w_up)
    return pl.pallas_call(
        _ffn_down_kernel,
        grid_spec=pltpu.PrefetchScalarGridSpec(
            num_scalar_prefetch=2, grid=(n_blocks,),
            in_specs=[pl.BlockSpec((blk, ff), lambda i, be, nv: (i, 0)),
                      pl.BlockSpec((None, ff, d), lambda i, be, nv: (be[i], 0, 0)),
                      pl.BlockSpec((blk, 1), lambda i, be, nv: (i, 0))],
            out_specs=pl.BlockSpec((blk, d), lambda i, be, nv: (i, 0)),
            scratch_shapes=[pltpu.VMEM((ff, d), jnp.bfloat16)]),
        out_shape=jax.ShapeDtypeStruct((rows, d), jnp.float32),
        compiler_params=params, name="ffn_down",
    )(blk_exp, n_valid, h, w_down, row_w)


def _router_kernel(x_ref, wt_ref, b_ref, e_ref, w_ref):
    f32 = jnp.float32
    tm = x_ref.shape[0]
    gs = N_EXPERTS // N_GROUPS
    logits = lax.dot_general(wt_ref[...].astype(jnp.bfloat16), x_ref[...].astype(jnp.bfloat16),
                             (((1,), (1,)), ((), ())), preferred_element_type=f32)
    scores = jax.nn.sigmoid(logits)
    biased = scores + b_ref[...]
    g3 = biased.reshape(N_GROUPS, gs, tm)
    it = lax.broadcasted_iota(jnp.int32, (N_GROUPS, gs, tm), 1)
    m1 = jnp.max(g3, axis=1, keepdims=True)
    i1 = jnp.min(jnp.where(g3 == m1, it, gs), axis=1, keepdims=True)
    m2 = jnp.max(jnp.where(it == i1, -jnp.inf, g3), axis=1, keepdims=True)
    gscore = (m1 + m2).reshape(N_GROUPS, tm)
    gi = lax.broadcasted_iota(jnp.int32, (N_GROUPS, tm), 0)
    grank = jnp.zeros((N_GROUPS, tm), jnp.int32)
    for j in range(N_GROUPS):
        sj = gscore[j:j + 1, :]
        grank = grank + ((sj > gscore) | ((sj == gscore) & (j < gi))).astype(jnp.int32)
    gmask = (grank < TOPK_GROUPS).reshape(N_GROUPS, 1, tm)
    masked = jnp.where(gmask, g3, -jnp.inf).reshape(N_EXPERTS, tm)
    ei = lax.broadcasted_iota(jnp.int32, (N_EXPERTS, tm), 0)
    idxs, wts = [], []
    for _ in range(TOP_K):
        m = jnp.max(masked, axis=0, keepdims=True)
        idx = jnp.min(jnp.where(masked == m, ei, N_EXPERTS), axis=0, keepdims=True)
        hit = ei == idx
        idxs.append(idx)
        wts.append(jnp.sum(jnp.where(hit, scores, 0.0), axis=0, keepdims=True))
        masked = jnp.where(hit, -jnp.inf, masked)
    w = jnp.concatenate(wts, axis=0)
    e_ref[...] = jnp.concatenate(idxs, axis=0)
    w_ref[...] = w / jnp.sum(w, axis=0, keepdims=True) * ROUTED_SCALE


def moe_route(x, router_w, router_bias):
    n, d = x.shape
    tm = min(512, n)
    assert n % tm == 0
    e, w = pl.pallas_call(
        _router_kernel,
        grid=(n // tm,),
        in_specs=[pl.BlockSpec((tm, d), lambda i: (i, 0)),
                  pl.BlockSpec((N_EXPERTS, d), lambda i: (0, 0)),
                  pl.BlockSpec((N_EXPERTS, 1), lambda i: (0, 0))],
        out_specs=[pl.BlockSpec((TOP_K, tm), lambda i: (0, i)), pl.BlockSpec((TOP_K, tm), lambda i: (0, i))],
        out_shape=[jax.ShapeDtypeStruct((TOP_K, n), jnp.int32), jax.ShapeDtypeStruct((TOP_K, n), jnp.float32)],
        compiler_params=pltpu.CompilerParams(dimension_semantics=("arbitrary",), vmem_limit_bytes=VMEM_LIMIT_BYTES),
        name="moe_route",
    )(x, router_w.T, router_bias.astype(jnp.float32).reshape(N_EXPERTS, 1))
    return e.T, w.T


def _moe_rank_kernel(e_ref, rank_ref, cnt_ref, carry_ref):
    f32, bf16 = jnp.float32, jnp.bfloat16
    ta = e_ref.shape[1]
    i = pl.program_id(0)

    @pl.when(i == 0)
    def _():
        carry_ref[...] = jnp.zeros(carry_ref.shape, f32)

    ei = lax.broadcasted_iota(jnp.int32, (N_EXPERTS, ta), 0)
    hot = ei == e_ref[...]
    r = lax.broadcasted_iota(jnp.int32, (ta, ta), 0)
    c = lax.broadcasted_iota(jnp.int32, (ta, ta), 1)
    before = jnp.dot(hot.astype(bf16), (r < c).astype(bf16), preferred_element_type=f32)
    carry = carry_ref[...]
    rank_ref[...] = jnp.sum(jnp.where(hot, before + carry, 0.0), axis=0, keepdims=True).astype(jnp.int32)
    carry = carry + jnp.sum(hot.astype(f32), axis=1, keepdims=True)
    carry_ref[...] = carry
    cnt_ref[...] = carry.astype(jnp.int32)


def moe_rank(flat_e):
    nk = flat_e.shape[0]
    ta = min(512, nk)
    assert nk % ta == 0 and nk < 2 ** 24
    rank, cnt = pl.pallas_call(
        _moe_rank_kernel,
        grid=(nk // ta,),
        in_specs=[pl.BlockSpec((1, ta), lambda i: (0, i))],
        out_specs=[pl.BlockSpec((1, ta), lambda i: (0, i)), pl.BlockSpec((N_EXPERTS, 1), lambda i: (0, 0))],
        out_shape=[jax.ShapeDtypeStruct((1, nk), jnp.int32), jax.ShapeDtypeStruct((N_EXPERTS, 1), jnp.int32)],
        scratch_shapes=[pltpu.VMEM((N_EXPERTS, 1), jnp.float32)],
        compiler_params=pltpu.CompilerParams(dimension_semantics=("arbitrary",), vmem_limit_bytes=VMEM_LIMIT_BYTES),
        name="moe_rank",
    )(flat_e.reshape(1, nk))
    return rank.reshape(nk), cnt.reshape(N_EXPERTS)


def moe_ffn_pallas(h, lp, blk):
    B, T, D = h.shape
    x = h.reshape(B * T, D)
    n = x.shape[0]
    f32 = jnp.float32
    eidx, w = moe_route(x, lp['router_w'], lp['router_bias'])

    nk = n * TOP_K
    flat_e = eidx.reshape(nk)
    rank, counts = moe_rank(flat_e)
    padded = (counts + blk - 1) // blk * blk
    pad_end = jnp.cumsum(padded)
    dest = jnp.take(pad_end - padded, flat_e) + rank
    n_blocks = -(-(nk + N_EXPERTS * (blk - 1)) // blk)
    rows = n_blocks * blk
    row_aid = jnp.full((rows,), -1, jnp.int32).at[dest].set(jnp.arange(nk, dtype=jnp.int32))
    live = row_aid >= 0
    row_aid = jnp.maximum(row_aid, 0)
    row_tok = row_aid // TOP_K
    row_w = jnp.where(live, jnp.take(w.reshape(nk), row_aid), 0.0)
    blk_exp = jnp.minimum(jnp.searchsorted(pad_end, jnp.arange(n_blocks) * blk, side='right'),
                          N_EXPERTS - 1).astype(jnp.int32)
    n_valid = (pad_end[-1] // blk).astype(jnp.int32).reshape(1)
    pos = dest.reshape(n, TOP_K)

    xb = x.astype(jnp.bfloat16)
    ys = expert_ffn(xb[row_tok], lp['exp_w_gate'], lp['exp_w_up'], lp['exp_w_down'], row_w[:, None], blk_exp,
                    n_valid, blk)
    routed = ys[pos].sum(axis=1)
    sblk = min(256, n)
    one = jnp.ones((n, 1), f32)
    shared = expert_ffn(xb, lp['sh_w_gate'][None], lp['sh_w_up'][None], lp['sh_w_down'][None], one,
                        jnp.zeros((n // sblk,), jnp.int32), jnp.full((1,), n // sblk, jnp.int32), sblk)
    return (routed + shared).reshape(B, T, D)


def pad_rows(a, length):
    return jnp.pad(a, [(0, 0), (0, length - a.shape[1])] + [(0, 0)] * (a.ndim - 2))


def rwkv7_mix(pr, shift_prev, wkv0, lp):
    B, T, _ = pr.shape
    f32 = jnp.float32
    prev = jnp.concatenate([shift_prev[:, None, :].astype(pr.dtype), pr[:, :-1]], axis=1)
    m = pr + (prev - pr) * lp['rwkv_mu']
    r, k, v, xw, xa, xg = jnp.split(m, RWKV_SPLITS, axis=-1)
    w_log = -jax.nn.softplus(-(lp['rwkv_w0'] + jnp.tanh(xw) @ lp['rwkv_w_up']).astype(f32)) - 0.5
    decay = jnp.exp(-jnp.exp(w_log))
    a = jax.nn.sigmoid((lp['rwkv_a0'] + xa @ lp['rwkv_a_up']).astype(f32))
    g = jax.nn.sigmoid(xg) @ lp['rwkv_g_up']

    def heads(t):
        return t.astype(f32).reshape(B, T, RWKV_HEADS, RWKV_HEAD_DIM)

    r, k, v, decay, a = heads(r), heads(k), heads(v), heads(decay), heads(a)
    kk = k * lp['rwkv_k_k'].astype(f32).reshape(RWKV_HEADS, RWKV_HEAD_DIM)
    kk = kk / jnp.maximum(jnp.linalg.norm(kk, axis=-1, keepdims=True), 1e-12)
    k = k * (1.0 + (a - 1.0) * lp['rwkv_k_a'].astype(f32).reshape(RWKV_HEADS, RWKV_HEAD_DIM))

    def step(S, inp):
        r_t, w_t, k_t, v_t, a_t, b_t = inp
        sa = jnp.einsum('bhvk,bhk->bhv', S, a_t)
        S = S * w_t[:, :, None, :] + sa[..., None] * b_t[:, :, None, :] + v_t[..., None] * k_t[:, :, None, :]
        return S, jnp.einsum('bhvk,bhk->bhv', S, r_t)

    xs = tuple(jnp.moveaxis(t, 1, 0) for t in (r, decay, k, v, -kk, kk * a))
    s_final, ys = lax.scan(step, wkv0.astype(f32), xs)
    y = jnp.moveaxis(ys, 0, 1)
    mu = jnp.mean(y, axis=-1, keepdims=True)
    var = jnp.mean(jnp.square(y - mu), axis=-1, keepdims=True)
    y = ((y - mu) * lax.rsqrt(var + LNX_EPS)).reshape(B, T, RWKV_WIDTH)
    y = y * lp['rwkv_lnx_w'].astype(f32) + lp['rwkv_lnx_b'].astype(f32)
    bonus = (jnp.sum(r * k * lp['rwkv_r_k'].astype(f32), axis=-1, keepdims=True) * v).reshape(B, T, RWKV_WIDTH)
    out = ((y + bonus) * g.astype(f32)).astype(pr.dtype)
    return out, pr[:, -1], s_final.astype(wkv0.dtype)


def compress_blocks(rows, pe, w):
    B, L = rows.shape[:2]
    blk = rows.reshape(B, L // CMP_BLOCK, CMP_BLOCK, 2, NSA_KV_HEADS, NSA_HEAD_DIM)
    blk = blk + jnp.swapaxes(pe, 0, 1)[None, None, :, :, None, :]
    return jnp.einsum('bncjkd,jcde->bnjke', blk, w)


def nsa_attend(q, q_pos, gates, kc, vc, c_end, n_sel_blocks, gather_sel, kw, vw, w_pos, rel_bias):
    B, Tq = q.shape[:2]
    f32 = jnp.float32
    scale = NSA_HEAD_DIM ** -0.5
    qg = q.reshape(B, Tq, NSA_KV_HEADS, NSA_GROUP, NSA_HEAD_DIM)
    bias_tab = rel_bias.astype(f32).reshape(NUM_BUCKETS, NSA_KV_HEADS, NSA_GROUP)

    dc = q_pos[:, None] - c_end[None, :]
    bias_c = bias_tab[rel_bucket(dc)].transpose(0, 2, 3, 1)
    lc = jnp.einsum('btkgd,bnkd->btkgn', qg, kc).astype(f32) * scale + bias_c
    pc = masked_softmax(lc, (dc >= 0)[:, None, None, :])
    oc = jnp.einsum('btkgn,bnkd->btkgd', pc.astype(vc.dtype), vc)

    imp = pc.sum(3).reshape(B, Tq, NSA_KV_HEADS, n_sel_blocks, SEL_BLOCK // CMP_BLOCK).sum(-1)
    blk = jnp.arange(n_sel_blocks)
    cur = (q_pos // SEL_BLOCK)[:, None]
    force = (blk == 0) | (blk == cur) | (blk == cur - 1)
    score = jnp.where(force[:, None, :], FORCE_SCORE, imp)
    score = jnp.where((blk <= cur)[:, None, :], score, -jnp.inf)
    _, idx = lax.top_k(jnp.moveaxis(score, 2, 1), min(N_SEL, n_sel_blocks))

    sel = gather_sel(idx)
    ks, vs = sel[..., 0, :], sel[..., 1, :]
    spos = idx[..., None] * SEL_BLOCK + jnp.arange(SEL_BLOCK)
    ds = q_pos[None, None, :, None, None] - spos
    hi = jnp.arange(NSA_KV_HEADS)[None, :, None, None, None]
    bias_s = jnp.moveaxis(bias_tab[rel_bucket(ds), hi], -1, 3)
    ls = jnp.einsum('btkgd,bktjsd->bktgjs', qg, ks).astype(f32) * scale + bias_s
    ms = (ds >= 0)[:, :, :, None]
    ps = masked_softmax(ls.reshape(ls.shape[:4] + (-1,)), ms.reshape(ms.shape[:4] + (-1,))).reshape(ls.shape)
    osel = jnp.einsum('bktgjs,bktjsd->btkgd', ps.astype(vs.dtype), vs)

    dw = q_pos[:, None] - w_pos[None, :]
    mw = (dw >= 0) & (dw <= WINDOW) & (w_pos >= 0)[None, :]
    bias_w = bias_tab[rel_bucket(dw)].transpose(0, 2, 3, 1)
    lw = jnp.einsum('btkgd,blkd->btkgl', qg, kw).astype(f32) * scale + bias_w
    pw = masked_softmax(lw, mw[:, None, None, :])
    ow = jnp.einsum('btkgl,blkd->btkgd', pw.astype(vw.dtype), vw)

    gt = gates.reshape(B, Tq, NSA_KV_HEADS, NSA_GROUP, 3)
    o = gt[..., 0:1] * oc + gt[..., 1:2] * osel + gt[..., 2:3] * ow
    return o.reshape(B, Tq, NSA_WIDTH)


def split_proj(h, lp):
    B, T, _ = h.shape
    proj = mm3(h, lp['w_in'])
    pr = proj[..., :RWKV_PROJ]
    pn = proj[..., RWKV_PROJ:]
    q = pn[..., :NSA_WIDTH].reshape(B, T, NSA_HEADS, NSA_HEAD_DIM)
    kv = pn[..., NSA_WIDTH:NSA_WIDTH + 3 * NSA_KV_WIDTH].reshape(B, T, 3, 2, NSA_KV_HEADS, NSA_HEAD_DIM)
    gates = jax.nn.sigmoid(pn[..., NSA_WIDTH + 3 * NSA_KV_WIDTH:].astype(jnp.float32))
    gates = gates.reshape(B, T, NSA_HEADS, 3).astype(h.dtype)
    return pr, q, kv[:, :, 0], kv[:, :, 1], kv[:, :, 2], gates


def merge_groups(o_rwkv, o_nsa, lp):
    o = jnp.concatenate([o_rwkv, rms_norm(o_nsa, lp['nsa_out_g'])], axis=-1).astype(jnp.bfloat16)
    return mm3(o, lp['w_out'])


def mixer_prompt(h, lp, rel_bias):
    B, T, d = h.shape
    proj_p = pallas_matmul(h.reshape(B * T, d), lp['w_in_p'])
    o_r, wkv = rwkv_mix(proj_p, jnp.zeros((B, RWKV_PROJ), h.dtype),
                        jnp.zeros((B, RWKV_HEADS, RWKV_HEAD_DIM, RWKV_HEAD_DIM), jnp.float32), lp, B, T, T, 64, 8)
    o_r = o_r.reshape(B, T, RWKV_WIDTH)
    shift_last = unpad_rwkv(proj_p.reshape(B, T, P_TOTAL)[:, -1])
    kvc = compress_prompt(proj_p, lp['cmp_pe'], lp['cmp_w'], B, T)
    o_n = nsa_prompt(proj_p, kvc, rel_bias, B, T).reshape(B, T, NSA_WIDTH)
    kv = proj_p[:, P_KVC:P_GATES].reshape(B, T, 3, 2, NSA_KV_HEADS, NSA_HEAD_DIM)
    kv_c, kv_s, kv_w = kv[:, :, 0], kv[:, :, 1], kv[:, :, 2]
    win_keep = min(WINDOW, T)
    return merge_groups(o_r, o_n, lp), (kv_c, kv_s, kv_w[:, T - win_keep:], wkv, shift_last)


def mixer_prompt_gather(h, lp, rel_bias):
    B, T, _ = h.shape
    pr, q, kv_c, kv_s, kv_w, gates = split_proj(h, lp)
    o_r, shift_last, wkv = rwkv7_mix(pr, jnp.zeros((B, RWKV_PROJ), h.dtype),
                                     jnp.zeros((B, RWKV_HEADS, RWKV_HEAD_DIM, RWKV_HEAD_DIM), jnp.float32), lp)
    lp_len = -(-T // SEL_BLOCK) * SEL_BLOCK
    comp = compress_blocks(pad_rows(kv_c, lp_len), lp['cmp_pe'], lp['cmp_w'])
    kc, vc = comp[:, :, 0], comp[:, :, 1]
    c_end = jnp.arange(lp_len // CMP_BLOCK) * CMP_BLOCK + (CMP_BLOCK - 1)
    n_sb = lp_len // SEL_BLOCK
    sel_blocks = pad_rows(kv_s, lp_len).reshape(B, n_sb, SEL_BLOCK, 2, NSA_KV_HEADS, NSA_HEAD_DIM)
    bi = jnp.arange(B)[:, None, None, None]
    hi = jnp.arange(NSA_KV_HEADS)[None, :, None, None]

    def gather_sel(idx):
        return sel_blocks[bi, idx, :, :, hi, :]

    kw_pad = jnp.pad(kv_w, ((0, 0), (WINDOW, 0), (0, 0), (0, 0), (0, 0)))

    def chunk(i):
        t0 = i * Q_CHUNK
        qc = lax.dynamic_slice_in_dim(q, t0, Q_CHUNK, axis=1)
        gc = lax.dynamic_slice_in_dim(gates, t0, Q_CHUNK, axis=1)
        wc = lax.dynamic_slice_in_dim(kw_pad, t0, WINDOW + Q_CHUNK, axis=1)
        q_pos = t0 + jnp.arange(Q_CHUNK)
        w_pos = t0 - WINDOW + jnp.arange(WINDOW + Q_CHUNK)
        return nsa_attend(qc, q_pos, gc, kc, vc, c_end, n_sb, gather_sel,
                          wc[:, :, 0], wc[:, :, 1], w_pos, rel_bias)

    o_n = lax.map(chunk, jnp.arange(T // Q_CHUNK))
    o_n = jnp.moveaxis(o_n, 0, 1).reshape(B, T, NSA_WIDTH)
    win_keep = min(WINDOW, T)
    return merge_groups(o_r, o_n, lp), (kv_c, kv_s, kv_w[:, T - win_keep:], wkv, shift_last)


def mixer_sample(h, lp, rel_bias, pool_cmp, pool_sel, win_buf, wkv0, shift0, page_table):
    B, T, _ = h.shape
    n_pages = page_table.shape[1]
    page = pool_cmp.shape[1]
    past = n_pages * page
    d = h.shape[-1]
    proj_p = pallas_matmul(h.reshape(B * T, d), lp['w_in_p']).reshape(B, T, P_TOTAL)
    import numpy as np
    kv = proj_p[..., P_KVC:P_GATES].reshape(B, T, 3, 2, NSA_KV_HEADS, NSA_HEAD_DIM)
    kv_c, kv_s, kv_w = kv[:, :, 0], kv[:, :, 1], kv[:, :, 2]
    proj_pad = jnp.pad(proj_p, ((0, 0), (0, T_PAD - T), (0, 0))).reshape(B * T_PAD, P_TOTAL)
    o_r, wkv = rwkv_mix(proj_pad, shift0, wkv0, lp, B, T_PAD, T, T_PAD, 4)
    o_r = o_r.reshape(B, T_PAD, RWKV_WIDTH)[:, :T]
    shift_last = unpad_rwkv(proj_p[:, -1])
    lp_len = -(-(past + T) // SEL_BLOCK) * SEL_BLOCK

    jk, hd = 2 * NSA_KV_HEADS, NSA_HEAD_DIM
    kvc_past = compress_paged(pool_cmp, page_table, lp['cmp_pe'], lp['cmp_w'])
    new_len = lp_len - past
    new_rows = pad_rows(proj_p[..., P_KVC:P_KVS], new_len).reshape(B * new_len, NSA_KV_WIDTH)
    kvc_new = compress_prompt(new_rows, lp['cmp_pe'], lp['cmp_w'], B, new_len, col0=0)
    nb_new = new_len // CMP_BLOCK
    assert nb_new <= 2
    kvc_new = kvc_new.reshape(B, jk, nb_new, hd).transpose(0, 2, 1, 3)
    kvc_all = jnp.concatenate([kvc_past, kvc_new], axis=1)
    nc = lp_len // CMP_BLOCK
    n_half = nc // 2
    half = -(-n_half // LANE) * LANE
    idx = np.zeros((2 * half,), np.int32)
    idx[:n_half] = np.arange(0, nc, 2)
    idx[half:half + n_half] = np.arange(1, nc, 2)
    kvc_t = kvc_all[:, idx].transpose(0, 2, 1, 3)
    o_n = nsa_sample(proj_pad, kvc_t, pool_sel, win_buf, page_table, rel_bias, B, T)
    o_n = o_n.reshape(B, T_PAD, NSA_WIDTH)[:, :T]
    new_win = jnp.concatenate([win_buf, kv_w], axis=1)[:, T:]
    return merge_groups(o_r, o_n, lp), (kv_c, kv_s, new_win, wkv, shift_last)


def moe_ffn(h, lp):
    B, T, D = h.shape
    x = h.reshape(B * T, D)
    n = x.shape[0]
    f32 = jnp.float32
    scores = jax.nn.sigmoid((x @ lp['router_w']).astype(f32))
    biased = scores + lp['router_bias'].astype(f32)
    grp = biased.reshape(n, N_GROUPS, N_EXPERTS // N_GROUPS)
    grp_score = lax.top_k(grp, 2)[0].sum(-1)
    _, gidx = lax.top_k(grp_score, TOPK_GROUPS)
    gmask = jax.nn.one_hot(gidx, N_GROUPS, dtype=f32).sum(1)
    emask = jnp.repeat(gmask, N_EXPERTS // N_GROUPS, axis=1) > 0
    _, eidx = lax.top_k(jnp.where(emask, biased, -jnp.inf), TOP_K)
    w = jnp.take_along_axis(scores, eidx, axis=1)
    w = w / jnp.sum(w, axis=-1, keepdims=True) * ROUTED_SCALE

    nk = n * TOP_K
    flat_e = eidx.reshape(nk)
    order = jnp.argsort(flat_e)
    se = flat_e[order]
    counts = jnp.bincount(flat_e, length=N_EXPERTS)
    padded = (counts + MOE_BLOCK - 1) // MOE_BLOCK * MOE_BLOCK
    pad_end = jnp.cumsum(padded)
    dest = (pad_end - padded)[se] + jnp.arange(nk) - (jnp.cumsum(counts) - counts)[se]
    n_blocks = -(-(nk + N_EXPERTS * (MOE_BLOCK - 1)) // MOE_BLOCK)
    rows = n_blocks * MOE_BLOCK
    row_tok = jnp.zeros((rows,), jnp.int32).at[dest].set((order // TOP_K).astype(jnp.int32))
    row_w = jnp.zeros((rows,), f32).at[dest].set(w.reshape(nk)[order])
    blk_exp = jnp.minimum(jnp.searchsorted(pad_end, jnp.arange(n_blocks) * MOE_BLOCK, side='right'),
                          N_EXPERTS - 1)

    def expert_block(args):
        tok, wt, e = args
        xb = x[tok]
        hb = jax.nn.silu(xb @ lp['exp_w_gate'][e]) * (xb @ lp['exp_w_up'][e])
        return (hb @ lp['exp_w_down'][e]).astype(f32) * wt[:, None]

    out = lax.map(expert_block, (row_tok.reshape(n_blocks, MOE_BLOCK), row_w.reshape(n_blocks, MOE_BLOCK), blk_exp))
    routed = jnp.zeros((n, D), f32).at[row_tok].add(out.reshape(rows, D))
    shared = (jax.nn.silu(x @ lp['sh_w_gate']) * (x @ lp['sh_w_up'])) @ lp['sh_w_down']
    return (routed + shared.astype(f32)).astype(h.dtype).reshape(B, T, D)


def trunk_layer(x, mod, lp, mixer):
    B, _, D = x.shape
    mod = mod.reshape(B, 6, 1, D)
    sh1, sc1, gt1, sh2, sc2, gt2 = (mod[:, i] for i in range(6))
    g = lp['norm_g']
    h = (rms_norm(x, g[0]) * (1 + sc1) + sh1).astype(jnp.bfloat16)
    o, st = mixer(h)
    x = x + gt1 * rms_norm(o, g[1])
    h = (rms_norm(x, g[2]) * (1 + sc2) + sh2).astype(jnp.bfloat16)
    x = x + gt2 * rms_norm(moe_ffn_pallas(h, lp, 512 if x.shape[0] * x.shape[1] >= 4096 else 32), g[3])
    return x, st


def kernel(x_prompt, x_sample, c_prompt, c_sample, cache_cmp, cache_sel, state_win, state_wkv,
           state_shift, page_table, rel_bias, w_ada, b_ada, norm_g, w_in, w_out, rwkv_mu, rwkv_w0,
           rwkv_w_up, rwkv_a0, rwkv_a_up, rwkv_g_up, rwkv_k_k, rwkv_k_a, rwkv_r_k, rwkv_lnx_w,
           rwkv_lnx_b, cmp_pe, cmp_w, nsa_out_g, router_w, router_bias, exp_w_gate, exp_w_up,
           exp_w_down, sh_w_gate, sh_w_up, sh_w_down):
    l = 0
    lp = dict(w_ada=w_ada[l], b_ada=b_ada[l], norm_g=norm_g[l], w_in=w_in[l], w_out=w_out[l],
              rwkv_mu=rwkv_mu[l], rwkv_w0=rwkv_w0[l], rwkv_w_up=rwkv_w_up[l], rwkv_a0=rwkv_a0[l],
              rwkv_a_up=rwkv_a_up[l], rwkv_g_up=rwkv_g_up[l], rwkv_k_k=rwkv_k_k[l],
              rwkv_k_a=rwkv_k_a[l], rwkv_r_k=rwkv_r_k[l], rwkv_lnx_w=rwkv_lnx_w[l],
              rwkv_lnx_b=rwkv_lnx_b[l], cmp_pe=cmp_pe[l], cmp_w=cmp_w[l], nsa_out_g=nsa_out_g[l],
              router_w=router_w[l], router_bias=router_bias[l], exp_w_gate=exp_w_gate[l],
              exp_w_up=exp_w_up[l], exp_w_down=exp_w_down[l], sh_w_gate=sh_w_gate[l],
              sh_w_up=sh_w_up[l], sh_w_down=sh_w_down[l])
    lp['w_in_p'] = pad_w_in(w_in[l])
    nb_p = x_prompt.shape[0]
    mod = pallas_matmul(jax.nn.silu(jnp.concatenate([c_prompt, c_sample], axis=0)), lp['w_ada']) + lp['b_ada']
    yp, st_p = trunk_layer(x_prompt, mod[:nb_p], lp, lambda h: mixer_prompt(h, lp, rel_bias))
    ys, st_s = trunk_layer(x_sample, mod[nb_p:], lp, lambda h: mixer_sample(
        h, lp, rel_bias, cache_cmp[l], cache_sel[l], state_win[l], state_wkv[l], state_shift[l], page_table))
    p_cmp, p_sel, p_win, p_wkv, p_shift = [a[None] for a in st_p]
    s_cmp, s_sel, s_win, s_wkv, s_shift = [a[None] for a in st_s]
    return (yp, ys, p_cmp, p_sel, p_win, p_wkv, p_shift, s_cmp, s_sel, s_win, s_wkv, s_shift)
```

```python
import functools
import math

import jax
import jax.numpy as jnp
from jax import lax
from jax.experimental import pallas as pl
from jax.experimental.pallas import tpu as pltpu

D_MODEL = 4096
RWKV_WIDTH = D_MODEL // 2
NSA_WIDTH = D_MODEL - RWKV_WIDTH
RWKV_HEAD_DIM = 64
RWKV_HEADS = RWKV_WIDTH // RWKV_HEAD_DIM
DECAY_LORA = 96
AAA_LORA = 96
GATE_LORA = 256
RWKV_PROJ = 3 * RWKV_WIDTH + DECAY_LORA + AAA_LORA + GATE_LORA
RWKV_SPLITS = (RWKV_WIDTH, 2 * RWKV_WIDTH, 3 * RWKV_WIDTH, 3 * RWKV_WIDTH + DECAY_LORA,
               3 * RWKV_WIDTH + DECAY_LORA + AAA_LORA)
LNX_EPS = 64e-5
NSA_HEAD_DIM = 128
NSA_HEADS = NSA_WIDTH // NSA_HEAD_DIM
NSA_KV_HEADS = 4
NSA_GROUP = NSA_HEADS // NSA_KV_HEADS
CMP_BLOCK = 32
SEL_BLOCK = 64
N_SEL = 16
WINDOW = 512
Q_CHUNK = 32
FORCE_SCORE = 1e4
NSA_KV_WIDTH = 2 * NSA_KV_HEADS * NSA_HEAD_DIM
NUM_BUCKETS = 32
MAX_DISTANCE = 128
N_EXPERTS = 64
N_GROUPS = 8
TOPK_GROUPS = 4
TOP_K = 8
ROUTED_SCALE = 2.5
MOE_BLOCK = 128
RMS_EPS = 1e-6

VMEM_LIMIT_BYTES = 56 * 1024 * 1024


def _matmul_kernel(x_ref, w_ref, o_ref):
    o_ref[...] = jnp.dot(x_ref[...].astype(jnp.bfloat16), w_ref[...].astype(jnp.bfloat16),
                         preferred_element_type=jnp.float32)


def pallas_matmul(x, w, tn=512):
    m, k = x.shape
    n = w.shape[1]
    row_mult = 16 if x.dtype == jnp.bfloat16 else 8
    mp = -(-m // row_mult) * row_mult
    tm = min(1024 if x.dtype == jnp.bfloat16 else 512, mp)
    mp = -(-mp // tm) * tm
    np_ = -(-n // tn) * tn
    if mp != m:
        x = jnp.pad(x, ((0, mp - m), (0, 0)))
    if np_ != n:
        w = jnp.pad(w, ((0, 0), (0, np_ - n)))
    out = pl.pallas_call(
        _matmul_kernel,
        grid=(mp // tm, np_ // tn),
        in_specs=[pl.BlockSpec((tm, k), lambda i, j: (i, 0)),
                  pl.BlockSpec((k, tn), lambda i, j: (0, j))],
        out_specs=pl.BlockSpec((tm, tn), lambda i, j: (i, j)),
        out_shape=jax.ShapeDtypeStruct((mp, np_), jnp.float32),
        compiler_params=pltpu.CompilerParams(
            dimension_semantics=("arbitrary", "arbitrary"), vmem_limit_bytes=VMEM_LIMIT_BYTES),
        name="matmul",
    )(x, w)
    return out[:m, :n]


def mm3(h, w):
    b, t, d = h.shape
    return pallas_matmul(h.reshape(b * t, d), w).reshape(b, t, w.shape[1])


def rms_norm(x, g):
    xf = x.astype(jnp.float32)
    y = xf * lax.rsqrt(jnp.mean(xf * xf, axis=-1, keepdims=True) + RMS_EPS)
    return (y * g.astype(jnp.float32)).astype(x.dtype)


def rel_bucket(dist):
    max_exact = NUM_BUCKETS // 2
    n = jnp.maximum(dist, 0)
    nf = jnp.maximum(n, max_exact).astype(jnp.float32)
    large = max_exact + (jnp.log(nf / max_exact) / math.log(MAX_DISTANCE / max_exact)
                         * (NUM_BUCKETS - max_exact)).astype(jnp.int32)
    return jnp.where(n < max_exact, n, jnp.minimum(large, NUM_BUCKETS - 1))


def masked_softmax(logits, mask):
    logits = jnp.where(mask, logits.astype(jnp.float32), -jnp.inf)
    m = jnp.max(logits, axis=-1, keepdims=True)
    p = jnp.exp(logits - jnp.where(jnp.isfinite(m), m, 0.0))
    s = jnp.sum(p, axis=-1, keepdims=True)
    return p / jnp.where(s > 0, s, 1.0)


LANE = 128
P_XW = 3 * RWKV_WIDTH
P_XA = P_XW + LANE
P_XG = P_XA + LANE
P_Q = P_XG + GATE_LORA
P_KVC = P_Q + NSA_WIDTH
P_KVS = P_KVC + NSA_KV_WIDTH
P_KVW = P_KVS + NSA_KV_WIDTH
P_GATES = P_KVW + NSA_KV_WIDTH
P_TOTAL = P_GATES + NSA_KV_HEADS * LANE
N_GATE_COLS = 3 * NSA_GROUP


def pad_w_in(w_in):
    d = w_in.shape[0]
    z = lambda n: jnp.zeros((d, n), w_in.dtype)
    o_xw, o_xa, o_xg = RWKV_SPLITS[2], RWKV_SPLITS[3], RWKV_SPLITS[4]
    o_g = RWKV_PROJ + NSA_WIDTH + 3 * NSA_KV_WIDTH
    parts = [w_in[:, :o_xw], w_in[:, o_xw:o_xa], z(LANE - DECAY_LORA), w_in[:, o_xa:o_xg], z(LANE - AAA_LORA),
             w_in[:, o_xg:o_g]]
    for kh in range(NSA_KV_HEADS):
        parts += [w_in[:, o_g + kh * N_GATE_COLS:o_g + (kh + 1) * N_GATE_COLS], z(LANE - N_GATE_COLS)]
    return jnp.concatenate(parts, axis=1).astype(jnp.bfloat16)


def unpad_rwkv(proj_p):
    return jnp.concatenate([proj_p[..., :P_XW], proj_p[..., P_XW:P_XW + DECAY_LORA],
                            proj_p[..., P_XA:P_XA + AAA_LORA], proj_p[..., P_XG:P_XG + GATE_LORA]], axis=-1)


def bias_lookup(tab, idx):
    sel = jnp.asarray(idx.reshape(-1, 1), jnp.int32) == jnp.arange(tab.shape[0], dtype=jnp.int32)[None, :]
    out = jnp.dot(sel.astype(jnp.float32), tab, precision=lax.Precision.HIGHEST)
    return out.reshape(idx.shape + (tab.shape[1],))


def bucket_table(n):
    import numpy as np
    d = np.arange(n)
    max_exact = NUM_BUCKETS // 2
    nf = np.maximum(d, max_exact).astype(np.float64)
    large = max_exact + (np.log(nf / max_exact) / math.log(MAX_DISTANCE / max_exact)
                         * (NUM_BUCKETS - max_exact)).astype(np.int64)
    return np.where(d < max_exact, d, np.minimum(large, NUM_BUCKETS - 1)).astype(np.int32)


def _compress_kernel(x_ref, w_ref, pe_ref, o_ref, *, nblk):
    half = nblk // 2
    acc = jnp.zeros((nblk, NSA_HEAD_DIM), jnp.float32)
    for c in range(CMP_BLOCK):
        xe = x_ref[pl.ds(c, half, stride=2 * CMP_BLOCK), :]
        xo = x_ref[pl.ds(CMP_BLOCK + c, half, stride=2 * CMP_BLOCK), :]
        lhs = jnp.concatenate([xe, xo], axis=0) + pe_ref[c:c + 1, :]
        acc = acc + jnp.dot(lhs.astype(jnp.bfloat16), w_ref[c].astype(jnp.bfloat16),
                            preferred_element_type=jnp.float32)
    o_ref[...] = acc


def _compress_dense_kernel(x_ref, w_ref, pe_ref, o_ref, *, nblk):
    f32, bf16 = jnp.float32, jnp.bfloat16
    L = nblk * CMP_BLOCK
    half = nblk // 2
    rowc = lax.broadcasted_iota(jnp.int32, (L, 1), 0) & (CMP_BLOCK - 1)
    xb = (x_ref[...] + jnp.concatenate([pe_ref[...]] * nblk, axis=0)).astype(bf16)
    y = jnp.zeros((L, NSA_HEAD_DIM), f32)
    for c in range(CMP_BLOCK):
        y = y + jnp.dot(jnp.where(rowc == c, xb, jnp.zeros_like(xb)), w_ref[c].astype(bf16), preferred_element_type=f32)
    n_i = lax.broadcasted_iota(jnp.int32, (nblk, L), 0)
    r_i = lax.broadcasted_iota(jnp.int32, (nblk, L), 1)
    want = jnp.where(n_i < half, 2 * n_i, 2 * (n_i - half) + 1)
    bsum = ((r_i // CMP_BLOCK) == want).astype(bf16)
    yh = y.astype(bf16)
    ym = (y - yh.astype(f32)).astype(bf16)
    yl = (y - yh.astype(f32) - ym.astype(f32)).astype(bf16)
    d = lambda a: jnp.dot(bsum, a, preferred_element_type=f32)
    o_ref[...] = d(yh) + (d(ym) + d(yl))


def compress_prompt(proj_p, cmp_pe, cmp_w, B, T, col0=P_KVC):
    nblk = T // CMP_BLOCK
    hd, KVH = NSA_HEAD_DIM, NSA_KV_HEADS
    return pl.pallas_call(
        functools.partial(_compress_dense_kernel if nblk >= 16 else _compress_kernel, nblk=nblk),
        grid=(B, 2, KVH),
        in_specs=[pl.BlockSpec((T, hd), lambda b, j, kh: (b, col0 // hd + j * KVH + kh)),
                  pl.BlockSpec((None, CMP_BLOCK, hd, hd), lambda b, j, kh: (j, 0, 0, 0)),
                  pl.BlockSpec((None, CMP_BLOCK, hd), lambda b, j, kh: (j, 0, 0))],
        out_specs=pl.BlockSpec((None, None, None, nblk, hd), lambda b, j, kh: (b, j, kh, 0, 0)),
        out_shape=jax.ShapeDtypeStruct((B, 2, KVH, nblk, hd), jnp.float32),
        compiler_params=pltpu.CompilerParams(dimension_semantics=("arbitrary",) * 3, vmem_limit_bytes=VMEM_LIMIT_BYTES),
        name="compress_prompt",
    )(proj_p, cmp_w, cmp_pe)


NSA_TQ = 128
NEG_BIG = -1e30


def _nsa_prompt_kernel(q_ref, gate_ref, ks_ref, vs_ref, kw_ref, vw_ref, kc_ref, vc_ref, bt_ref, bc_ref, es_ref,
                       o_ref, sel_ref, qs_ref, m_ref, l_ref, acc_ref, *, n_sb):
    f32, bf16 = jnp.float32, jnp.bfloat16
    tq, G, hd = NSA_TQ, NSA_GROUP, NSA_HEAD_DIM
    qi = pl.program_id(2)
    t0 = qi * tq
    scale = NSA_HEAD_DIM ** -0.5
    qs_ref[...] = jnp.transpose(jnp.concatenate([q_ref[:, g * hd:(g + 1) * hd] for g in range(G)], axis=0)).astype(bf16)
    tlane = t0 + lax.broadcasted_iota(jnp.int32, (1, tq), 1)

    ncmp = kc_ref.shape[0]
    brow = lax.broadcasted_iota(jnp.int32, (ncmp, 1), 0)
    blk_id = jnp.where(brow < ncmp // 2, 2 * brow, 2 * (brow - ncmp // 2) + 1)
    vis = (blk_id * CMP_BLOCK + (CMP_BLOCK - 1)) <= tlane
    vis = jnp.concatenate([vis] * G, axis=1)
    lc = jnp.dot(kc_ref[...].astype(bf16), qs_ref[...], preferred_element_type=f32) * scale + bc_ref[...]
    lc = jnp.where(vis, lc, -jnp.inf)
    mc = jnp.max(lc, axis=0, keepdims=True)
    pc = jnp.exp(lc - jnp.where(mc > -jnp.inf, mc, 0.0))
    sc = jnp.sum(pc, axis=0, keepdims=True)
    pc = pc / jnp.where(sc > 0, sc, 1.0)
    oct = lax.dot_general(vc_ref[...].astype(bf16), pc.astype(bf16), (((0,), (0,)), ((), ())),
                          preferred_element_type=f32)

    pg = pc[:, 0:tq]
    for g in range(1, G):
        pg = pg + pc[:, g * tq:(g + 1) * tq]
    imp = pg[:n_sb] + pg[n_sb:]
    cur = tlane // SEL_BLOCK
    bl = lax.broadcasted_iota(jnp.int32, (n_sb, tq), 0)
    force = (bl == 0) | (bl == cur) | (bl == cur - 1)
    score = jnp.where(force, FORCE_SCORE, imp)
    score = jnp.where(bl <= cur, score, -jnp.inf)
    rank = jnp.zeros((n_sb, tq), jnp.int32)
    for i in range(n_sb):
        si = score[i:i + 1, :]
        ahead = (si > score) | ((si == score) & (i < bl))
        rank = rank + ahead.astype(jnp.int32)
    sel_ref[...] = (rank < N_SEL).astype(bf16)
    gate = jax.nn.sigmoid(gate_ref[...])
    for g in range(G):
        o_ref[:, g * hd:(g + 1) * hd] = gate[:, 3 * g:3 * g + 1] * jnp.transpose(oct[:, g * tq:(g + 1) * tq])

    kcol = lax.broadcasted_iota(jnp.int32, (tq, 1), 0)
    tlane = t0 + lax.broadcasted_iota(jnp.int32, (1, tq), 1)

    def attend(c, carry, k_ref, v_ref, selected):
        s0 = pl.multiple_of(c * tq, tq)
        dist = tlane - (s0 + kcol)
        if selected:
            ok = (jnp.dot(es_ref[c], sel_ref[...], preferred_element_type=f32) > 0.5) & (dist >= 0)
        else:
            ok = (dist >= 0) & (dist <= WINDOW)
        ok = jnp.concatenate([ok] * G, axis=1)
        k = k_ref[pl.ds(s0, tq), :].astype(bf16)
        vt = jnp.transpose(v_ref[pl.ds(s0, tq), :]).astype(bf16)
        s = jnp.dot(k, qs_ref[...], preferred_element_type=f32) * scale + bt_ref[jnp.minimum(qi - c, 2)]
        s = jnp.where(ok, s, NEG_BIG)
        m_old = m_ref[...]
        m_new = jnp.maximum(m_old, jnp.max(s, axis=0, keepdims=True))
        p = jnp.exp(s - m_new)
        alpha = jnp.exp(m_old - m_new)
        l_ref[...] = alpha * l_ref[...] + jnp.sum(p, axis=0, keepdims=True)
        acc_ref[...] = alpha * acc_ref[...] + jnp.dot(vt, p.astype(bf16), preferred_element_type=f32)
        m_ref[...] = m_new
        return carry

    def run_branch(lo, k_ref, v_ref, selected, gate_col):
        m_ref[...] = jnp.full(m_ref.shape, NEG_BIG, f32)
        l_ref[...] = jnp.zeros(l_ref.shape, f32)
        acc_ref[...] = jnp.zeros(acc_ref.shape, f32)
        lax.fori_loop(lo, qi + 1, functools.partial(attend, k_ref=k_ref, v_ref=v_ref, selected=selected), 0)
        ot = acc_ref[...] / l_ref[...]
        for g in range(G):
            o_ref[:, g * hd:(g + 1) * hd] += (gate[:, 3 * g + gate_col:3 * g + gate_col + 1]
                                              * jnp.transpose(ot[:, g * tq:(g + 1) * tq]))

    run_branch(0, ks_ref, vs_ref, True, 1)
    run_branch(jnp.maximum(qi - WINDOW // tq, 0), kw_ref, vw_ref, False, 2)


def nsa_prompt(proj_p, kvc, rel_bias, B, T):
    import numpy as np
    tq, G, hd, KVH = NSA_TQ, NSA_GROUP, NSA_HEAD_DIM, NSA_KV_HEADS
    nq = T // tq
    n_sb = T // SEL_BLOCK
    ncmp = T // CMP_BLOCK
    assert T % tq == 0 and T % SEL_BLOCK == 0 and ncmp == 2 * n_sb
    bucket = bucket_table(max(T, 3 * tq))
    tab = rel_bias.astype(jnp.float32)
    ii, jj = np.meshgrid(np.arange(tq), np.arange(tq), indexing="ij")
    didx = np.stack([bucket[np.maximum(ii - jj, 0)], bucket[tq + ii - jj], np.full((tq, tq), NUM_BUCKETS - 1)])
    assert bucket[tq + 1] == NUM_BUCKETS - 1
    bt = bias_lookup(tab, didx).reshape(3, tq, tq, KVH, G).transpose(3, 0, 2, 4, 1).reshape(KVH, 3, tq, G * tq)
    blk = np.concatenate([np.arange(0, ncmp, 2), np.arange(1, ncmp, 2)])
    dc = np.arange(T)[:, None] - (blk * CMP_BLOCK + CMP_BLOCK - 1)[None, :]
    bc = bias_lookup(tab, bucket[np.maximum(dc, 0)]).reshape(nq, tq, ncmp, KVH, G).transpose(3, 0, 2, 4, 1).reshape(KVH, nq, ncmp, G * tq)
    es = (np.arange(n_sb)[None, :, None] == (np.arange(nq)[:, None, None] * (tq // SEL_BLOCK)
                                              + np.arange(tq)[None, None, :] // SEL_BLOCK))
    es = jnp.asarray(np.swapaxes(es, 1, 2), jnp.bfloat16)
    col = lambda off: off // hd
    kv_spec = lambda off: pl.BlockSpec((T, hd), lambda b, kh, i: (b, col(off) + kh))
    return pl.pallas_call(
        functools.partial(_nsa_prompt_kernel, n_sb=n_sb),
        grid=(B, KVH, nq),
        in_specs=[pl.BlockSpec((tq, G * hd), lambda b, kh, i: (b * nq + i, P_Q // (G * hd) + kh)),
                  pl.BlockSpec((tq, LANE), lambda b, kh, i: (b * nq + i, P_GATES // LANE + kh)),
                  kv_spec(P_KVS), kv_spec(P_KVS + KVH * hd), kv_spec(P_KVW), kv_spec(P_KVW + KVH * hd),
                  pl.BlockSpec((None, None, None, ncmp, hd), lambda b, kh, i: (b, 0, kh, 0, 0)),
                  pl.BlockSpec((None, None, None, ncmp, hd), lambda b, kh, i: (b, 1, kh, 0, 0)),
                  pl.BlockSpec((None, 3, tq, G * tq), lambda b, kh, i: (kh, 0, 0, 0)),
                  pl.BlockSpec((None, None, ncmp, G * tq), lambda b, kh, i: (kh, i, 0, 0)),
                  pl.BlockSpec((nq, tq, n_sb), lambda b, kh, i: (0, 0, 0))],
        out_specs=pl.BlockSpec((tq, G * hd), lambda b, kh, i: (b * nq + i, kh)),
        out_shape=jax.ShapeDtypeStruct((B * T, NSA_WIDTH), jnp.float32),
        scratch_shapes=[pltpu.VMEM((n_sb, tq), jnp.bfloat16), pltpu.VMEM((hd, G * tq), jnp.bfloat16),
                        pltpu.VMEM((1, G * tq), jnp.float32), pltpu.VMEM((1, G * tq), jnp.float32),
                        pltpu.VMEM((hd, G * tq), jnp.float32)],
        compiler_params=pltpu.CompilerParams(dimension_semantics=("arbitrary",) * 3, vmem_limit_bytes=VMEM_LIMIT_BYTES),
        name="nsa_prompt",
    )(proj_p, proj_p, proj_p, proj_p, proj_p, proj_p, kvc, kvc, bt, bc, es)


SAMPLE_PG = 8
T_PAD = 8


def _compress_paged_kernel(pt_ref, *refs, nb):
    del pt_ref
    page_refs = refs[:SAMPLE_PG]
    w_ref, pe_ref, o_ref = refs[SAMPLE_PG:]
    jk, hd = 2 * NSA_KV_HEADS, NSA_HEAD_DIM
    rows = SAMPLE_PG * nb * jk
    acc = jnp.zeros((rows, 2 * hd), jnp.float32)
    for c in range(CMP_BLOCK):
        x = jnp.concatenate([pr[pl.ds(c, nb, stride=CMP_BLOCK), :, :] for pr in page_refs], axis=0)
        lhs = (x + pe_ref[c]).reshape(rows, hd).astype(jnp.bfloat16)
        acc = acc + jnp.dot(lhs, w_ref[c], preferred_element_type=jnp.float32)
    sub = lax.broadcasted_iota(jnp.int32, (rows, 1), 0) & (jk - 1)
    out = jnp.where(sub < NSA_KV_HEADS, acc[:, :hd], acc[:, hd:])
    o_ref[...] = out.reshape(SAMPLE_PG * nb, jk, hd)


def compress_paged(pool, page_table, cmp_pe, cmp_w):
    n_phys, page = pool.shape[:2]
    B, n_pages = page_table.shape
    jk, hd = 2 * NSA_KV_HEADS, NSA_HEAD_DIM
    nb = page // CMP_BLOCK
    assert n_pages % SAMPLE_PG == 0 and page % CMP_BLOCK == 0
    pool3 = pool.reshape(n_phys, page, jk, hd)
    w01 = jnp.concatenate([cmp_w[0], cmp_w[1]], axis=-1).astype(jnp.bfloat16)
    pe8 = jnp.repeat(jnp.swapaxes(cmp_pe, 0, 1), NSA_KV_HEADS, axis=1)
    page_spec = lambda i: pl.BlockSpec((None, page, jk, hd),
                                       lambda b, g, pt: (pt[b * n_pages + g * SAMPLE_PG + i], 0, 0, 0))
    return pl.pallas_call(
        functools.partial(_compress_paged_kernel, nb=nb),
        grid_spec=pltpu.PrefetchScalarGridSpec(
            num_scalar_prefetch=1, grid=(B, n_pages // SAMPLE_PG),
            in_specs=[page_spec(i) for i in range(SAMPLE_PG)]
            + [pl.BlockSpec((CMP_BLOCK, hd, 2 * hd), lambda b, g, pt: (0, 0, 0)),
               pl.BlockSpec((CMP_BLOCK, jk, hd), lambda b, g, pt: (0, 0, 0))],
            out_specs=pl.BlockSpec((None, SAMPLE_PG * nb, jk, hd), lambda b, g, pt: (b, g, 0, 0))),
        out_shape=jax.ShapeDtypeStruct((B, n_pages * nb, jk, hd), jnp.float32),
        compiler_params=pltpu.CompilerParams(dimension_semantics=("arbitrary",) * 2, vmem_limit_bytes=VMEM_LIMIT_BYTES),
        name="compress_paged",
    )(page_table.reshape(-1), *([pool3] * SAMPLE_PG), w01, pe8)


def _nsa_sample_kernel(pt_ref, *refs, past, page, n_steps, n_sb, half):
    del pt_ref
    f32, bf16 = jnp.float32, jnp.bfloat16
    KVH, G, hd = NSA_KV_HEADS, NSA_GROUP, NSA_HEAD_DIM
    q_refs = refs[0:KVH]
    gate_ref, ksn_ref, vsn_ref, kwn_ref, vwn_ref, kvc_ref, bc_ref = refs[KVH:KVH + 7]
    page_refs = refs[KVH + 7:KVH + 7 + SAMPLE_PG]
    es_ref, bs_ref, bn_ref, win_ref, bw_ref, o_ref, m_ref, l_ref, acc_ref, oc_ref, sel_ref = refs[KVH + 7 + SAMPLE_PG:]
    g_step = pl.program_id(1)
    R = G * T_PAD
    scale = NSA_HEAD_DIM ** -0.5
    nt = (((1,), (1,)), ((), ()))
    t_row = past + (lax.broadcasted_iota(jnp.int32, (R, 1), 0) & (T_PAD - 1))
    qs = [jnp.concatenate([q_refs[kh][:, g * hd:(g + 1) * hd] for g in range(G)], axis=0).astype(bf16) for kh in range(KVH)]

    @pl.when(g_step == 0)
    def _():
        m_ref[...] = jnp.full(m_ref.shape, NEG_BIG, f32)
        l_ref[...] = jnp.zeros(l_ref.shape, f32)
        acc_ref[...] = jnp.zeros(acc_ref.shape, f32)
        lane = lax.broadcasted_iota(jnp.int32, (1, 2 * half), 1)
        lo = lane < half
        blk_id = jnp.where(lo, 2 * lane, 2 * (lane - half) + 1)
        real = jnp.where(lo, lane, lane - half) < n_sb
        vis = real & (blk_id * CMP_BLOCK + (CMP_BLOCK - 1) <= t_row)
        tpos = past + lax.broadcasted_iota(jnp.int32, (T_PAD, 1), 0)
        cur = tpos // SEL_BLOCK
        bl = lax.broadcasted_iota(jnp.int32, (T_PAD, half), 1)
        force = (bl == 0) | (bl == cur) | (bl == cur - 1)
        for kh in range(KVH):
            lc = lax.dot_general(qs[kh], kvc_ref[kh].astype(bf16), nt, preferred_element_type=f32) * scale + bc_ref[kh]
            lc = jnp.where(vis, lc, -jnp.inf)
            mc = jnp.max(lc, axis=-1, keepdims=True)
            pc = jnp.exp(lc - jnp.where(mc > -jnp.inf, mc, 0.0))
            sc = jnp.sum(pc, axis=-1, keepdims=True)
            pc = pc / jnp.where(sc > 0, sc, 1.0)
            oc_ref[kh * R:(kh + 1) * R, :] = jnp.dot(pc.astype(bf16), kvc_ref[KVH + kh].astype(bf16),
                                                     preferred_element_type=f32)
            pg = pc[0:T_PAD]
            for g in range(1, G):
                pg = pg + pc[g * T_PAD:(g + 1) * T_PAD]
            imp = pg[:, :half] + pg[:, half:]
            score = jnp.where(force, FORCE_SCORE, imp)
            score = jnp.where((bl <= cur) & (bl < n_sb), score, -jnp.inf)
            rank = jnp.zeros((T_PAD, half), jnp.int32)
            for i in range(n_sb):
                si = score[:, i:i + 1]
                rank = rank + ((si > score) | ((si == score) & (i < bl))).astype(jnp.int32)
            selm = ((rank < N_SEL) & (bl < n_sb)).astype(bf16)
            sel_ref[kh * R:(kh + 1) * R, :] = jnp.concatenate([selm] * G, axis=0)

    last = g_step == n_steps - 1
    bsel = bs_ref[jnp.where(last, 1, 0)]
    okf = jnp.dot(sel_ref[...], es_ref[...], preferred_element_type=f32) > 0.5
    for kh in range(KVH):
        rs = slice(kh * R, (kh + 1) * R)
        k = jnp.concatenate([pr[pl.ds(kh, page, stride=2 * KVH), :] for pr in page_refs], axis=0).astype(bf16)
        v = jnp.concatenate([pr[pl.ds(KVH + kh, page, stride=2 * KVH), :] for pr in page_refs], axis=0).astype(bf16)
        s = lax.dot_general(qs[kh], k, nt, preferred_element_type=f32) * scale + bsel[rs]
        ok = okf[rs]
        s = jnp.where(ok, s, NEG_BIG)
        m_old = m_ref[rs]
        m_new = jnp.maximum(m_old, jnp.max(s, axis=-1, keepdims=True))
        p = jnp.where(ok, jnp.exp(s - m_new), 0.0)
        alpha = jnp.exp(m_old - m_new)
        l_ref[rs] = alpha * l_ref[rs] + jnp.sum(p, axis=-1, keepdims=True)
        acc_ref[rs] = alpha * acc_ref[rs] + jnp.dot(p.astype(bf16), v, preferred_element_type=f32)
        m_ref[rs] = m_new

    @pl.when(last)
    def _():
        gate = jax.nn.sigmoid(gate_ref[...])
        ncol = lax.broadcasted_iota(jnp.int32, (1, T_PAD), 1)
        wb = win_ref.shape[0] // (2 * KVH)
        wcol = lax.broadcasted_iota(jnp.int32, (1, wb), 1)
        for kh in range(KVH):
            rs = slice(kh * R, (kh + 1) * R)
            cs = slice(kh * hd, (kh + 1) * hd)
            s = lax.dot_general(qs[kh], ksn_ref[:, cs].astype(bf16), nt, preferred_element_type=f32) * scale + bn_ref[kh]
            ok = (past + ncol) <= t_row
            s = jnp.where(ok, s, NEG_BIG)
            m_old = m_ref[rs]
            m_new = jnp.maximum(m_old, jnp.max(s, axis=-1, keepdims=True))
            p = jnp.where(ok, jnp.exp(s - m_new), 0.0)
            alpha = jnp.exp(m_old - m_new)
            l_s = alpha * l_ref[rs] + jnp.sum(p, axis=-1, keepdims=True)
            osel = (alpha * acc_ref[rs] + jnp.dot(p.astype(bf16), vsn_ref[:, cs].astype(bf16),
                                                  preferred_element_type=f32)) / l_s
            s1 = lax.dot_general(qs[kh], win_ref[pl.ds(kh, wb, stride=2 * KVH), :].astype(bf16), nt,
                                 preferred_element_type=f32) * scale + bw_ref[kh]
            d1 = t_row - (past - wb + wcol)
            ok1 = (d1 >= 0) & (d1 <= WINDOW)
            s1 = jnp.where(ok1, s1, NEG_BIG)
            s2 = lax.dot_general(qs[kh], kwn_ref[:, cs].astype(bf16), nt, preferred_element_type=f32) * scale + bn_ref[kh]
            s2 = jnp.where(ok, s2, NEG_BIG)
            mw = jnp.maximum(jnp.max(s1, axis=-1, keepdims=True), jnp.max(s2, axis=-1, keepdims=True))
            p1 = jnp.where(ok1, jnp.exp(s1 - mw), 0.0)
            p2 = jnp.where(ok, jnp.exp(s2 - mw), 0.0)
            l_w = jnp.sum(p1, axis=-1, keepdims=True) + jnp.sum(p2, axis=-1, keepdims=True)
            owin = (jnp.dot(p1.astype(bf16), win_ref[pl.ds(KVH + kh, wb, stride=2 * KVH), :].astype(bf16),
                            preferred_element_type=f32)
                    + jnp.dot(p2.astype(bf16), vwn_ref[:, cs].astype(bf16), preferred_element_type=f32)) / l_w
            oc = oc_ref[rs]
            for g in range(G):
                r = slice(g * T_PAD, (g + 1) * T_PAD)
                c0 = kh * LANE + 3 * g
                o_ref[:, (kh * G + g) * hd:(kh * G + g + 1) * hd] = (
                    gate[:, c0:c0 + 1] * oc[r] + gate[:, c0 + 1:c0 + 2] * osel[r] + gate[:, c0 + 2:c0 + 3] * owin[r])


def nsa_sample(proj_pad, kvc_t, pool_sel, win_buf, page_table, rel_bias, B, T):
    import numpy as np
    KVH, G, hd = NSA_KV_HEADS, NSA_GROUP, NSA_HEAD_DIM
    n_phys, page = pool_sel.shape[:2]
    n_pages = page_table.shape[1]
    past = n_pages * page
    wb = win_buf.shape[1]
    half = kvc_t.shape[2] // 2
    n_sb = -(-(past + T) // SEL_BLOCK)
    n_steps = n_pages // SAMPLE_PG
    keys = SAMPLE_PG * page
    R = G * T_PAD
    assert n_pages % SAMPLE_PG == 0 and page % SEL_BLOCK == 0 and T <= T_PAD and n_sb <= half and past % SEL_BLOCK == 0
    bucket = bucket_table(past + T_PAD + 1)
    tab = rel_bias.astype(jnp.float32)
    qpos = past + np.arange(T_PAD)

    def bias_rows(dist):
        b = bias_lookup(tab, bucket[np.maximum(dist, 0)])
        return b.reshape(T_PAD, dist.shape[1], KVH, G).transpose(2, 3, 0, 1).reshape(KVH, R, dist.shape[1])

    lane = np.arange(2 * half)
    blk = np.where(lane < half, 2 * lane, 2 * (lane - half) + 1)
    bc = bias_rows(qpos[:, None] - (blk * CMP_BLOCK + CMP_BLOCK - 1)[None, :])
    far = bias_rows(np.full((T_PAD, keys), past, np.int64))
    assert bucket[page + 1] == NUM_BUCKETS - 1
    near = bias_rows(qpos[:, None] - ((n_steps - 1) * keys + np.arange(keys))[None, :])
    bs = jnp.stack([far, near], axis=0).reshape(2, KVH * R, keys)
    bn = bias_rows(qpos[:, None] - (past + np.arange(T_PAD))[None, :])
    bw = bias_rows(qpos[:, None] - (past - wb + np.arange(wb))[None, :])
    es = (np.arange(half)[None, :, None] == (np.arange(n_steps)[:, None, None] * (keys // SEL_BLOCK)
                                              + np.arange(keys)[None, None, :] // SEL_BLOCK))
    es = jnp.asarray(es, jnp.bfloat16)
    pool2 = pool_sel.reshape(n_phys, page * 2 * KVH, hd)
    win2 = win_buf.reshape(B, wb * 2 * KVH, hd)
    wq = G * hd
    c = lambda *idx: (lambda b, g, pt: idx)
    row_blk = lambda w, col: pl.BlockSpec((T_PAD, w), lambda b, g, pt: (b, col))
    in_specs = ([row_blk(wq, P_Q // wq + kh) for kh in range(KVH)]
                + [row_blk(KVH * LANE, P_GATES // (KVH * LANE)),
                   row_blk(KVH * hd, P_KVS // (KVH * hd)), row_blk(KVH * hd, P_KVS // (KVH * hd) + 1),
                   row_blk(KVH * hd, P_KVW // (KVH * hd)), row_blk(KVH * hd, P_KVW // (KVH * hd) + 1),
                   pl.BlockSpec((None, 2 * KVH, 2 * half, hd), lambda b, g, pt: (b, 0, 0, 0)),
                   pl.BlockSpec((KVH, R, 2 * half), c(0, 0, 0))]
                + [pl.BlockSpec((None, page * 2 * KVH, hd),
                                (lambda i: lambda b, g, pt: (pt[b * n_pages + g * SAMPLE_PG + i], 0, 0))(i))
                   for i in range(SAMPLE_PG)]
                + [pl.BlockSpec((None, half, keys), lambda b, g, pt: (g, 0, 0)),
                   pl.BlockSpec((2, KVH * R, keys), c(0, 0, 0)),
                   pl.BlockSpec((KVH, R, T_PAD), c(0, 0, 0)),
                   pl.BlockSpec((None, wb * 2 * KVH, hd), lambda b, g, pt: (b, 0, 0)),
                   pl.BlockSpec((KVH, R, wb), c(0, 0, 0))])
    f32 = jnp.float32
    return pl.pallas_call(
        functools.partial(_nsa_sample_kernel, past=past, page=page, n_steps=n_steps, n_sb=n_sb, half=half),
        grid_spec=pltpu.PrefetchScalarGridSpec(
            num_scalar_prefetch=1, grid=(B, n_steps), in_specs=in_specs,
            out_specs=pl.BlockSpec((T_PAD, NSA_WIDTH), lambda b, g, pt: (b, 0)),
            scratch_shapes=[pltpu.VMEM((KVH * R, 1), f32), pltpu.VMEM((KVH * R, 1), f32), pltpu.VMEM((KVH * R, hd), f32),
                            pltpu.VMEM((KVH * R, hd), f32), pltpu.VMEM((KVH * R, half), jnp.bfloat16)]),
        out_shape=jax.ShapeDtypeStruct((B * T_PAD, NSA_WIDTH), f32),
        compiler_params=pltpu.CompilerParams(dimension_semantics=("arbitrary",) * 2, vmem_limit_bytes=VMEM_LIMIT_BYTES),
        name="nsa_sample",
    )(page_table.reshape(-1), *([proj_pad] * (KVH + 5)), kvc_t, bc, *([pool2] * SAMPLE_PG), es, bs, bn, win2, bw)


RWKV_PAIR = 2 * RWKV_HEAD_DIM
HIGHEST = lax.Precision.HIGHEST


def _rwkv_kernel_one_pair(r_ref, k_ref, v_ref, xw_ref, xa_ref, xg_ref,
                 sr_ref, sk_ref, sv_ref, sxw_ref, sxa_ref, sxg_ref,
                 mr_ref, mk_ref, mv_ref, mxw_ref, mxa_ref, mxg_ref,
                 s0_ref, w0_ref, a0_ref, kk_ref, ka_ref, rk_ref, lnw_ref, lnb_ref,
                 wup_ref, aup_ref, gup_ref,
                 o_ref, sout_ref,
                 S_ref, cr_ref, ck_ref, cv_ref, cxw_ref, cxa_ref, cxg_ref, *, C, t_valid, n_chunks):
    f32, bf16 = jnp.float32, jnp.bfloat16
    hd, W = RWKV_HEAD_DIM, RWKV_PAIR
    ci = pl.program_id(2)
    lane = lax.broadcasted_iota(jnp.int32, (1, W), 1)
    h0 = lane < hd
    row = lax.broadcasted_iota(jnp.int32, (C, 1), 0)
    valid = (ci * C + row) < t_valid
    carries = ((cr_ref, sr_ref), (ck_ref, sk_ref), (cv_ref, sv_ref), (cxw_ref, sxw_ref), (cxa_ref, sxa_ref),
               (cxg_ref, sxg_ref))

    @pl.when(ci == 0)
    def _():
        S_ref[...] = jnp.zeros((W, W), f32)
        S_ref[0:hd, 0:hd] = s0_ref[0]
        S_ref[hd:W, hd:W] = s0_ref[1]
        for c_ref, s_ref in carries:
            c_ref[0:1, :] = s_ref[...]

    def shifted(x_ref, c_ref, mu_ref):
        x = x_ref[...]
        prev = jnp.where(row == 0, c_ref[0:1, :], pltpu.roll(x, 1, axis=0))
        c_ref[0:1, :] = x[C - 1:C, :]
        return x + (prev - x) * mu_ref[...]

    def seg_sum(x):
        s_lo = jnp.sum(jnp.where(h0, x, 0.0), axis=-1, keepdims=True)
        s_hi = jnp.sum(jnp.where(h0, 0.0, x), axis=-1, keepdims=True)
        return jnp.where(h0, s_lo, s_hi)

    def stack(x):
        return jnp.concatenate([jnp.where(h0, x, 0.0), jnp.where(h0, 0.0, x)], axis=0)

    mr = shifted(r_ref, cr_ref, mr_ref)
    mk = shifted(k_ref, ck_ref, mk_ref)
    mv = shifted(v_ref, cv_ref, mv_ref)
    mxw = shifted(xw_ref, cxw_ref, mxw_ref)
    mxa = shifted(xa_ref, cxa_ref, mxa_ref)
    mxg = shifted(xg_ref, cxg_ref, mxg_ref)

    z = w0_ref[...] + jnp.dot(jnp.tanh(mxw).astype(bf16), wup_ref[...].astype(bf16), preferred_element_type=f32)
    w_log = -jax.nn.softplus(-z) - 0.5
    lw = -jnp.exp(w_log)
    a = jax.nn.sigmoid(a0_ref[...] + jnp.dot(mxa.astype(bf16), aup_ref[...].astype(bf16), preferred_element_type=f32))
    g = jnp.dot(jax.nn.sigmoid(mxg).astype(bf16), gup_ref[...].astype(bf16), preferred_element_type=f32)
    kk = mk * kk_ref[...]
    kk = kk / jnp.maximum(jnp.sqrt(seg_sum(kk * kk)), 1e-12)
    kp = mk * (1.0 + (a - 1.0) * ka_ref[...])
    lw = jnp.where(valid, lw, 0.0)
    a_t = jnp.where(valid, -kk, 0.0)
    b_t = jnp.where(valid, kk * a, 0.0)
    k_t = jnp.where(valid, kp, 0.0)
    v_t = jnp.where(valid, mv, 0.0)

    ti = lax.broadcasted_iota(jnp.int32, (C, C), 0)
    tj = lax.broadcasted_iota(jnp.int32, (C, C), 1)
    cl = jnp.dot((tj <= ti).astype(f32), lw, precision=HIGHEST, preferred_element_type=f32)
    e_neg = jnp.exp(-cl)
    At = a_t * jnp.exp(cl - lw)
    Bt = b_t * e_neg
    Kt = k_t * e_neg
    Rt = mr * jnp.exp(cl)
    G = jnp.concatenate([stack(At), stack(Rt)], axis=0).astype(bf16)
    Z = jnp.concatenate([Bt, Bt, Kt, Kt], axis=0).astype(bf16)
    nt = (((1,), (1,)), ((), ()))
    M = lax.dot_general(G, Z, nt, preferred_element_type=f32)
    ri = lax.broadcasted_iota(jnp.int32, (2 * C, 2 * C), 0)
    cj = lax.broadcasted_iota(jnp.int32, (2 * C, 2 * C), 1)
    same = (ri >= C) == (cj >= C)
    strict = same & ((cj & (C - 1)) < (ri & (C - 1)))
    incl = same & ((cj & (C - 1)) <= (ri & (C - 1)))
    L = jnp.where(strict, M[0:2 * C, 0:2 * C], 0.0)
    AK = jnp.where(strict, M[0:2 * C, 2 * C:4 * C], 0.0)
    RB = jnp.where(incl, M[2 * C:4 * C, 0:2 * C], 0.0)
    RK = jnp.where(incl, M[2 * C:4 * C, 2 * C:4 * C], 0.0)
    S = S_ref[...]
    GS = lax.dot_general(G, S.astype(bf16), nt, preferred_element_type=f32)
    Vs = stack(v_t)
    U = GS[0:2 * C] + jnp.dot(AK.astype(bf16), Vs.astype(bf16), preferred_element_type=f32)
    P = L
    levels = C.bit_length() - 1
    for lvl in range(levels):
        U = U + jnp.dot(P, U, precision=HIGHEST, preferred_element_type=f32)
        if lvl + 1 < levels:
            P = jnp.dot(P, P, precision=HIGHEST, preferred_element_type=f32)
    X = jnp.concatenate([U, Vs], axis=0).astype(bf16)
    Y = GS[2 * C:4 * C] + jnp.dot(jnp.concatenate([RB, RK], axis=1).astype(bf16), X, preferred_element_type=f32)
    y = Y[0:C] + Y[C:2 * C]
    dS = lax.dot_general(X, Z, (((0,), (0,)), ((), ())), preferred_element_type=f32)
    vi = lax.broadcasted_iota(jnp.int32, (W, W), 0)
    kj = lax.broadcasted_iota(jnp.int32, (W, W), 1)
    S_new = jnp.where((vi < hd) == (kj < hd), S + dS, 0.0) * jnp.exp(cl[C - 1:C, :])
    S_ref[...] = S_new

    mean = seg_sum(y) * (1.0 / hd)
    yc = y - mean
    var = seg_sum(yc * yc) * (1.0 / hd)
    yn = yc * lax.rsqrt(var + LNX_EPS) * lnw_ref[...] + lnb_ref[...]
    bonus = seg_sum(mr * kp * rk_ref[...]) * mv
    o_ref[...] = (yn + bonus) * g

    @pl.when(ci == n_chunks - 1)
    def _():
        sout_ref[0] = S_new[0:hd, 0:hd]
        sout_ref[1] = S_new[hd:W, hd:W]


def rwkv_mix_one_pair(proj_p, shift_prev, wkv0, lp, B, T, t_valid, C):
    f32 = jnp.float32
    W, hd = RWKV_PAIR, RWKV_HEAD_DIM
    n_pairs = RWKV_WIDTH // W
    n_chunks = T // C
    assert T % C == 0 and C & (C - 1) == 0 and C >= 8

    def pad_vec(x):
        z = lambda n: jnp.zeros(x.shape[:-1] + (n,), x.dtype)
        return jnp.concatenate([x[..., :RWKV_SPLITS[2]], x[..., RWKV_SPLITS[2]:RWKV_SPLITS[3]], z(LANE - DECAY_LORA),
                                x[..., RWKV_SPLITS[3]:RWKV_SPLITS[4]], z(LANE - AAA_LORA), x[..., RWKV_SPLITS[4]:]], -1)

    shift_p = pad_vec(shift_prev.astype(f32))[:, None, :]
    mu_p = pad_vec(lp['rwkv_mu'].astype(f32))[None, :]
    pad_rows_to = lambda w: jnp.pad(w, ((0, LANE - w.shape[0]), (0, 0)))
    wup, aup = pad_rows_to(lp['rwkv_w_up']), pad_rows_to(lp['rwkv_a_up'])
    vec = lambda x: x.reshape(1, RWKV_WIDTH).astype(f32)
    nw = RWKV_WIDTH // W
    col_r, col_k, col_v = 0, nw, 2 * nw
    col_xw, col_xa, col_xg = P_XW // LANE, P_XA // LANE, P_XG // GATE_LORA
    row_spec = lambda w, cfn: pl.BlockSpec((C, w), lambda b, p, c: (b * n_chunks + c, cfn(p)))
    sh_spec = lambda w, cfn: pl.BlockSpec((None, 1, w), lambda b, p, c: (b, 0, cfn(p)))
    mu_spec = lambda w, cfn: pl.BlockSpec((1, w), lambda b, p, c: (0, cfn(p)))
    cols = [(W, lambda p: col_r + p), (W, lambda p: col_k + p), (W, lambda p: col_v + p),
            (LANE, lambda p: col_xw), (LANE, lambda p: col_xa), (GATE_LORA, lambda p: col_xg)]
    pvec = pl.BlockSpec((1, W), lambda b, p, c: (0, p))
    state_spec = pl.BlockSpec((None, 2, hd, hd), lambda b, p, c: (b, p, 0, 0))
    in_specs = ([row_spec(w, f) for w, f in cols] + [sh_spec(w, f) for w, f in cols] + [mu_spec(w, f) for w, f in cols]
                + [state_spec] + [pvec] * 7
                + [pl.BlockSpec((LANE, W), lambda b, p, c: (0, p)), pl.BlockSpec((LANE, W), lambda b, p, c: (0, p)),
                   pl.BlockSpec((GATE_LORA, W), lambda b, p, c: (0, p))])
    scratch = [pltpu.VMEM((W, W), f32)] + [pltpu.VMEM((8, w), f32) for w, _ in cols]
    o, s_fin = pl.pallas_call(
        functools.partial(_rwkv_kernel, C=C, t_valid=t_valid, n_chunks=n_chunks),
        grid=(B, n_pairs, n_chunks),
        in_specs=in_specs,
        out_specs=[pl.BlockSpec((C, W), lambda b, p, c: (b * n_chunks + c, p)), state_spec],
        out_shape=[jax.ShapeDtypeStruct((B * T, RWKV_WIDTH), f32),
                   jax.ShapeDtypeStruct((B, RWKV_HEADS, hd, hd), f32)],
        scratch_shapes=scratch,
        compiler_params=pltpu.CompilerParams(dimension_semantics=("arbitrary",) * 3, vmem_limit_bytes=VMEM_LIMIT_BYTES),
        name="rwkv_mix",
    )(*([proj_p] * 6), *([shift_p] * 6), *([mu_p] * 6), wkv0.astype(f32),
      vec(lp['rwkv_w0']), vec(lp['rwkv_a0']), vec(lp['rwkv_k_k']), vec(lp['rwkv_k_a']), vec(lp['rwkv_r_k']),
      vec(lp['rwkv_lnx_w']), vec(lp['rwkv_lnx_b']), wup, aup, lp['rwkv_g_up'])
    return o, s_fin


def _dot3(a, b):
    f32, bf16 = jnp.float32, jnp.bfloat16
    ah = a.astype(bf16)
    al = (a - ah.astype(f32)).astype(bf16)
    bh = b.astype(bf16)
    bl = (b - bh.astype(f32)).astype(bf16)
    d = lambda x, y: jnp.dot(x, y, preferred_element_type=f32)
    return d(ah, bh) + (d(ah, bl) + d(al, bh))


def _rwkv_kernel(r_ref, k_ref, v_ref, xw_ref, xa_ref, xg_ref,
                 sr_ref, sk_ref, sv_ref, sxw_ref, sxa_ref, sxg_ref,
                 mr_ref, mk_ref, mv_ref, mxw_ref, mxa_ref, mxg_ref,
                 s0_ref, w0_ref, a0_ref, kk_ref, ka_ref, rk_ref, lnw_ref, lnb_ref,
                 wup_ref, aup_ref, gup_ref,
                 o_ref, sout_ref,
                 S_ref, cr_ref, ck_ref, cv_ref, cxw_ref, cxa_ref, cxg_ref, *, C, NP, t_valid, n_chunks):
    f32, bf16 = jnp.float32, jnp.bfloat16
    hd, W = RWKV_HEAD_DIM, RWKV_PAIR
    ci = pl.program_id(2)
    lane = lax.broadcasted_iota(jnp.int32, (1, W), 1)
    h0 = lane < hd
    row = lax.broadcasted_iota(jnp.int32, (C, 1), 0)
    valid = (ci * C + row) < t_valid
    carries = ((cr_ref, sr_ref), (ck_ref, sk_ref), (cv_ref, sv_ref), (cxw_ref, sxw_ref), (cxa_ref, sxa_ref),
               (cxg_ref, sxg_ref))

    @pl.when(ci == 0)
    def _():
        S_ref[...] = jnp.zeros(S_ref.shape, f32)
        for p in range(NP):
            S_ref[p, 0:hd, 0:hd] = s0_ref[2 * p]
            S_ref[p, hd:W, hd:W] = s0_ref[2 * p + 1]
        for c_ref, s_ref in carries:
            c_ref[0:1, :] = s_ref[...]

    def shifted(x_ref, c_ref, mu_ref):
        x = x_ref[...]
        prev = jnp.where(row == 0, c_ref[0:1, :], pltpu.roll(x, 1, axis=0))
        c_ref[0:1, :] = x[C - 1:C, :]
        return x + (prev - x) * mu_ref[...]

    def seg_sum(x):
        s_lo = jnp.sum(jnp.where(h0, x, 0.0), axis=-1, keepdims=True)
        s_hi = jnp.sum(jnp.where(h0, 0.0, x), axis=-1, keepdims=True)
        return jnp.where(h0, s_lo, s_hi)

    def stack(x):
        return jnp.concatenate([jnp.where(h0, x, 0.0), jnp.where(h0, 0.0, x)], axis=0)

    mr_all = shifted(r_ref, cr_ref, mr_ref)
    mk_all = shifted(k_ref, ck_ref, mk_ref)
    mv_all = shifted(v_ref, cv_ref, mv_ref)
    mxw = shifted(xw_ref, cxw_ref, mxw_ref)
    mxa = shifted(xa_ref, cxa_ref, mxa_ref)
    mxg = shifted(xg_ref, cxg_ref, mxg_ref)

    z_all = w0_ref[...] + jnp.dot(jnp.tanh(mxw).astype(bf16), wup_ref[...].astype(bf16), preferred_element_type=f32)
    a_all = jax.nn.sigmoid(a0_ref[...] + jnp.dot(mxa.astype(bf16), aup_ref[...].astype(bf16), preferred_element_type=f32))
    g_all = jnp.dot(jax.nn.sigmoid(mxg).astype(bf16), gup_ref[...].astype(bf16), preferred_element_type=f32)

    ti = lax.broadcasted_iota(jnp.int32, (C, C), 0)
    tj = lax.broadcasted_iota(jnp.int32, (C, C), 1)
    tri = (tj <= ti).astype(f32)
    ri = lax.broadcasted_iota(jnp.int32, (2 * C, 2 * C), 0)
    cj = lax.broadcasted_iota(jnp.int32, (2 * C, 2 * C), 1)
    same = (ri >= C) == (cj >= C)
    strict = same & ((cj & (C - 1)) < (ri & (C - 1)))
    incl = same & ((cj & (C - 1)) <= (ri & (C - 1)))
    vi = lax.broadcasted_iota(jnp.int32, (W, W), 0)
    kj = lax.broadcasted_iota(jnp.int32, (W, W), 1)
    blockdiag = (vi < hd) == (kj < hd)
    nt = (((1,), (1,)), ((), ()))
    levels = C.bit_length() - 1

    stage = []
    for p in range(NP):
        ls = slice(p * W, (p + 1) * W)
        mr, mk, mv, a, g = mr_all[:, ls], mk_all[:, ls], mv_all[:, ls], a_all[:, ls], g_all[:, ls]
        w_log = -jax.nn.softplus(-z_all[:, ls]) - 0.5
        lw = -jnp.exp(w_log)
        kk = mk * kk_ref[:, ls]
        kk = kk / jnp.maximum(jnp.sqrt(seg_sum(kk * kk)), 1e-12)
        kp = mk * (1.0 + (a - 1.0) * ka_ref[:, ls])
        lw = jnp.where(valid, lw, 0.0)
        a_t = jnp.where(valid, -kk, 0.0)
        b_t = jnp.where(valid, kk * a, 0.0)
        k_t = jnp.where(valid, kp, 0.0)
        v_t = jnp.where(valid, mv, 0.0)

        cl = jnp.dot(tri, lw, precision=HIGHEST, preferred_element_type=f32)
        e_neg = jnp.exp(-cl)
        At = a_t * jnp.exp(cl - lw)
        Bt = b_t * e_neg
        Kt = k_t * e_neg
        Rt = mr * jnp.exp(cl)
        G = jnp.concatenate([stack(At), stack(Rt)], axis=0).astype(bf16)
        Z = jnp.concatenate([Bt, Bt, Kt, Kt], axis=0).astype(bf16)
        M = lax.dot_general(G, Z, nt, preferred_element_type=f32)
        L = jnp.where(strict, M[0:2 * C, 0:2 * C], 0.0)
        AK = jnp.where(strict, M[0:2 * C, 2 * C:4 * C], 0.0)
        RB = jnp.where(incl, M[2 * C:4 * C, 0:2 * C], 0.0)
        RK = jnp.where(incl, M[2 * C:4 * C, 2 * C:4 * C], 0.0)
        S = S_ref[p]
        GS = lax.dot_general(G, S.astype(bf16), nt, preferred_element_type=f32)
        Vs = stack(v_t)
        U = GS[0:2 * C] + jnp.dot(AK.astype(bf16), Vs.astype(bf16), preferred_element_type=f32)
        stage.append(dict(ls=ls, mr=mr, mv=mv, kp=kp, g=g, cl=cl, Z=Z, GS=GS, Vs=Vs, RB=RB, RK=RK, S=S, U=U, P=L))

    for lvl in range(levels):
        for st in stage:
            st['U'] = st['U'] + _dot3(st['P'], st['U'])
            if lvl + 1 < levels:
                st['P'] = _dot3(st['P'], st['P'])

    for p, st in enumerate(stage):
        ls, mr, mv, kp, g, cl, Z, GS, Vs, RB, RK, S, U = (st[n] for n in ('ls', 'mr', 'mv', 'kp', 'g', 'cl', 'Z', 'GS', 'Vs',
                                                                       'RB', 'RK', 'S', 'U'))
        X = jnp.concatenate([U, Vs], axis=0).astype(bf16)
        Y = GS[2 * C:4 * C] + jnp.dot(jnp.concatenate([RB, RK], axis=1).astype(bf16), X, preferred_element_type=f32)
        y = Y[0:C] + Y[C:2 * C]
        dS = lax.dot_general(X, Z, (((0,), (0,)), ((), ())), preferred_element_type=f32)
        S_new = jnp.where(blockdiag, S + dS, 0.0) * jnp.exp(cl[C - 1:C, :])
        S_ref[p] = S_new

        mean = seg_sum(y) * (1.0 / hd)
        yc = y - mean
        var = seg_sum(yc * yc) * (1.0 / hd)
        yn = yc * lax.rsqrt(var + LNX_EPS) * lnw_ref[:, ls] + lnb_ref[:, ls]
        bonus = seg_sum(mr * kp * rk_ref[:, ls]) * mv
        o_ref[:, ls] = (yn + bonus) * g

    @pl.when(ci == n_chunks - 1)
    def _():
        for p in range(NP):
            s_fin = S_ref[p]
            sout_ref[2 * p] = s_fin[0:hd, 0:hd]
            sout_ref[2 * p + 1] = s_fin[hd:W, hd:W]


def rwkv_mix(proj_p, shift_prev, wkv0, lp, B, T, t_valid, C, NP):
    f32 = jnp.float32
    hd = RWKV_HEAD_DIM
    W = NP * RWKV_PAIR
    n_steps = RWKV_WIDTH // W
    n_chunks = T // C
    assert T % C == 0 and C & (C - 1) == 0 and C >= 8 and RWKV_WIDTH % W == 0

    def pad_vec(x):
        z = lambda n: jnp.zeros(x.shape[:-1] + (n,), x.dtype)
        return jnp.concatenate([x[..., :RWKV_SPLITS[2]], x[..., RWKV_SPLITS[2]:RWKV_SPLITS[3]], z(LANE - DECAY_LORA),
                                x[..., RWKV_SPLITS[3]:RWKV_SPLITS[4]], z(LANE - AAA_LORA), x[..., RWKV_SPLITS[4]:]], -1)

    shift_p = pad_vec(shift_prev.astype(f32))[:, None, :]
    mu_p = pad_vec(lp['rwkv_mu'].astype(f32))[None, :]
    pad_rows_to = lambda w: jnp.pad(w, ((0, LANE - w.shape[0]), (0, 0)))
    wup, aup = pad_rows_to(lp['rwkv_w_up']), pad_rows_to(lp['rwkv_a_up'])
    vec = lambda x: x.reshape(1, RWKV_WIDTH).astype(f32)
    nw = RWKV_WIDTH // W
    col_r, col_k, col_v = 0, nw, 2 * nw
    col_xw, col_xa, col_xg = P_XW // LANE, P_XA // LANE, P_XG // GATE_LORA
    row_spec = lambda w, cfn: pl.BlockSpec((C, w), lambda b, p, c: (b * n_chunks + c, cfn(p)))
    sh_spec = lambda w, cfn: pl.BlockSpec((None, 1, w), lambda b, p, c: (b, 0, cfn(p)))
    mu_spec = lambda w, cfn: pl.BlockSpec((1, w), lambda b, p, c: (0, cfn(p)))
    cols = [(W, lambda p: col_r + p), (W, lambda p: col_k + p), (W, lambda p: col_v + p),
            (LANE, lambda p: col_xw), (LANE, lambda p: col_xa), (GATE_LORA, lambda p: col_xg)]
    pvec = pl.BlockSpec((1, W), lambda b, p, c: (0, p))
    state_spec = pl.BlockSpec((None, 2 * NP, hd, hd), lambda b, p, c: (b, p, 0, 0))
    in_specs = ([row_spec(w, f) for w, f in cols] + [sh_spec(w, f) for w, f in cols] + [mu_spec(w, f) for w, f in cols]
                + [state_spec] + [pvec] * 7
                + [pl.BlockSpec((LANE, W), lambda b, p, c: (0, p)), pl.BlockSpec((LANE, W), lambda b, p, c: (0, p)),
                   pl.BlockSpec((GATE_LORA, W), lambda b, p, c: (0, p))])
    scratch = [pltpu.VMEM((NP, RWKV_PAIR, RWKV_PAIR), f32)] + [pltpu.VMEM((8, w), f32) for w, _ in cols]
    o, s_fin = pl.pallas_call(
        functools.partial(_rwkv_kernel, C=C, NP=NP, t_valid=t_valid, n_chunks=n_chunks),
        grid=(B, n_steps, n_chunks),
        in_specs=in_specs,
        out_specs=[pl.BlockSpec((C, W), lambda b, p, c: (b * n_chunks + c, p)), state_spec],
        out_shape=[jax.ShapeDtypeStruct((B * T, RWKV_WIDTH), f32),
                   jax.ShapeDtypeStruct((B, RWKV_HEADS, hd, hd), f32)],
        scratch_shapes=scratch,
        compiler_params=pltpu.CompilerParams(dimension_semantics=("arbitrary",) * 3, vmem_limit_bytes=VMEM_LIMIT_BYTES),
        name="rwkv_mix",
    )(*([proj_p] * 6), *([shift_p] * 6), *([mu_p] * 6), wkv0.astype(f32),
      vec(lp['rwkv_w0']), vec(lp['rwkv_a0']), vec(lp['rwkv_k_k']), vec(lp['rwkv_k_a']), vec(lp['rwkv_r_k']),
      vec(lp['rwkv_lnx_w']), vec(lp['rwkv_lnx_b']), wup, aup, lp['rwkv_g_up'])
    return o, s_fin


def _expert_changed(be_ref, i):
    return jnp.logical_or(i == 0, be_ref[i] != be_ref[jnp.maximum(i - 1, 0)])


def _ffn_up_kernel(be_ref, nv_ref, x_ref, wg_ref, wu_ref, h_ref, wgb_ref, wub_ref):
    i = pl.program_id(0)
    live = i < nv_ref[0]

    @pl.when(live & _expert_changed(be_ref, i))
    def _():
        wgb_ref[...] = wg_ref[...].astype(jnp.bfloat16)
        wub_ref[...] = wu_ref[...].astype(jnp.bfloat16)

    @pl.when(live)
    def _():
        x = x_ref[...]
        hg = jnp.dot(x, wgb_ref[...], preferred_element_type=jnp.float32)
        hu = jnp.dot(x, wub_ref[...], preferred_element_type=jnp.float32)
        h_ref[...] = (jax.nn.silu(hg) * hu).astype(h_ref.dtype)

    @pl.when(i >= nv_ref[0])
    def _():
        h_ref[...] = jnp.zeros(h_ref.shape, h_ref.dtype)


def _ffn_down_kernel(be_ref, nv_ref, h_ref, wd_ref, rw_ref, o_ref, wdb_ref):
    i = pl.program_id(0)
    live = i < nv_ref[0]

    @pl.when(live & _expert_changed(be_ref, i))
    def _():
        wdb_ref[...] = wd_ref[...].astype(jnp.bfloat16)

    @pl.when(live)
    def _():
        o_ref[...] = jnp.dot(h_ref[...].astype(jnp.bfloat16), wdb_ref[...],
                             preferred_element_type=jnp.float32) * rw_ref[...]

    @pl.when(i >= nv_ref[0])
    def _():
        o_ref[...] = jnp.zeros(o_ref.shape, o_ref.dtype)


def expert_ffn(xs, w_gate, w_up, w_down, row_w, blk_exp, n_valid, blk):
    rows, d = xs.shape
    ff = w_gate.shape[-1]
    n_blocks = rows // blk
    params = pltpu.CompilerParams(dimension_semantics=("arbitrary",), vmem_limit_bytes=VMEM_LIMIT_BYTES)
    h = pl.pallas_call(
        _ffn_up_kernel,
        grid_spec=pltpu.PrefetchScalarGridSpec(
            num_scalar_prefetch=2, grid=(n_blocks,),
            in_specs=[pl.BlockSpec((blk, d), lambda i, be, nv: (i, 0)),
                      pl.BlockSpec((None, d, ff), lambda i, be, nv: (be[i], 0, 0)),
                      pl.BlockSpec((None, d, ff), lambda i, be, nv: (be[i], 0, 0))],
            out_specs=pl.BlockSpec((blk, ff), lambda i, be, nv: (i, 0)),
            scratch_shapes=[pltpu.VMEM((d, ff), jnp.bfloat16), pltpu.VMEM((d, ff), jnp.bfloat16)]),
        out_shape=jax.ShapeDtypeStruct((rows, ff), jnp.bfloat16),
        compiler_params=params, name="ffn_up",
    )(blk_exp, n_valid, xs, w_gate, w_up)
    return pl.pallas_call(
        _ffn_down_kernel,
        grid_spec=pltpu.PrefetchScalarGridSpec(
            num_scalar_prefetch=2, grid=(n_blocks,),
            in_specs=[pl.BlockSpec((blk, ff), lambda i, be, nv: (i, 0)),
                      pl.BlockSpec((None, ff, d), lambda i, be, nv: (be[i], 0, 0)),
                      pl.BlockSpec((blk, 1), lambda i, be, nv: (i, 0))],
            out_specs=pl.BlockSpec((blk, d), lambda i, be, nv: (i, 0)),
            scratch_shapes=[pltpu.VMEM((ff, d), jnp.bfloat16)]),
        out_shape=jax.ShapeDtypeStruct((rows, d), jnp.float32),
        compiler_params=params, name="ffn_down",
    )(blk_exp, n_valid, h, w_down, row_w)


def _router_kernel(x_ref, wt_ref, b_ref, e_ref, w_ref):
    f32 = jnp.float32
    tm = x_ref.shape[0]
    gs = N_EXPERTS // N_GROUPS
    logits = lax.dot_general(wt_ref[...].astype(jnp.bfloat16), x_ref[...].astype(jnp.bfloat16),
                             (((1,), (1,)), ((), ())), preferred_element_type=f32)
    scores = jax.nn.sigmoid(logits)
    biased = scores + b_ref[...]
    g3 = biased.reshape(N_GROUPS, gs, tm)
    it = lax.broadcasted_iota(jnp.int32, (N_GROUPS, gs, tm), 1)
    m1 = jnp.max(g3, axis=1, keepdims=True)
    i1 = jnp.min(jnp.where(g3 == m1, it, gs), axis=1, keepdims=True)
    m2 = jnp.max(jnp.where(it == i1, -jnp.inf, g3), axis=1, keepdims=True)
    gscore = (m1 + m2).reshape(N_GROUPS, tm)
    gi = lax.broadcasted_iota(jnp.int32, (N_GROUPS, tm), 0)
    grank = jnp.zeros((N_GROUPS, tm), jnp.int32)
    for j in range(N_GROUPS):
        sj = gscore[j:j + 1, :]
        grank = grank + ((sj > gscore) | ((sj == gscore) & (j < gi))).astype(jnp.int32)
    gmask = (grank < TOPK_GROUPS).reshape(N_GROUPS, 1, tm)
    masked = jnp.where(gmask, g3, -jnp.inf).reshape(N_EXPERTS, tm)
    ei = lax.broadcasted_iota(jnp.int32, (N_EXPERTS, tm), 0)
    idxs, wts = [], []
    for _ in range(TOP_K):
        m = jnp.max(masked, axis=0, keepdims=True)
        idx = jnp.min(jnp.where(masked == m, ei, N_EXPERTS), axis=0, keepdims=True)
        hit = ei == idx
        idxs.append(idx)
        wts.append(jnp.sum(jnp.where(hit, scores, 0.0), axis=0, keepdims=True))
        masked = jnp.where(hit, -jnp.inf, masked)
    w = jnp.concatenate(wts, axis=0)
    e_ref[...] = jnp.concatenate(idxs, axis=0)
    w_ref[...] = w / jnp.sum(w, axis=0, keepdims=True) * ROUTED_SCALE


def moe_route(x, router_w, router_bias):
    n, d = x.shape
    tm = min(512, n)
    assert n % tm == 0
    e, w = pl.pallas_call(
        _router_kernel,
        grid=(n // tm,),
        in_specs=[pl.BlockSpec((tm, d), lambda i: (i, 0)),
                  pl.BlockSpec((N_EXPERTS, d), lambda i: (0, 0)),
                  pl.BlockSpec((N_EXPERTS, 1), lambda i: (0, 0))],
        out_specs=[pl.BlockSpec((TOP_K, tm), lambda i: (0, i)), pl.BlockSpec((TOP_K, tm), lambda i: (0, i))],
        out_shape=[jax.ShapeDtypeStruct((TOP_K, n), jnp.int32), jax.ShapeDtypeStruct((TOP_K, n), jnp.float32)],
        compiler_params=pltpu.CompilerParams(dimension_semantics=("arbitrary",), vmem_limit_bytes=VMEM_LIMIT_BYTES),
        name="moe_route",
    )(x, router_w.T, router_bias.astype(jnp.float32).reshape(N_EXPERTS, 1))
    return e.T, w.T


def _moe_rank_kernel(e_ref, rank_ref, cnt_ref, carry_ref):
    f32, bf16 = jnp.float32, jnp.bfloat16
    ta = e_ref.shape[1]
    i = pl.program_id(0)

    @pl.when(i == 0)
    def _():
        carry_ref[...] = jnp.zeros(carry_ref.shape, f32)

    ei = lax.broadcasted_iota(jnp.int32, (N_EXPERTS, ta), 0)
    hot = ei == e_ref[...]
    r = lax.broadcasted_iota(jnp.int32, (ta, ta), 0)
    c = lax.broadcasted_iota(jnp.int32, (ta, ta), 1)
    before = jnp.dot(hot.astype(bf16), (r < c).astype(bf16), preferred_element_type=f32)
    carry = carry_ref[...]
    rank_ref[...] = jnp.sum(jnp.where(hot, before + carry, 0.0), axis=0, keepdims=True).astype(jnp.int32)
    carry = carry + jnp.sum(hot.astype(f32), axis=1, keepdims=True)
    carry_ref[...] = carry
    cnt_ref[...] = carry.astype(jnp.int32)


def moe_rank(flat_e):
    nk = flat_e.shape[0]
    ta = min(512, nk)
    assert nk % ta == 0 and nk < 2 ** 24
    rank, cnt = pl.pallas_call(
        _moe_rank_kernel,
        grid=(nk // ta,),
        in_specs=[pl.BlockSpec((1, ta), lambda i: (0, i))],
        out_specs=[pl.BlockSpec((1, ta), lambda i: (0, i)), pl.BlockSpec((N_EXPERTS, 1), lambda i: (0, 0))],
        out_shape=[jax.ShapeDtypeStruct((1, nk), jnp.int32), jax.ShapeDtypeStruct((N_EXPERTS, 1), jnp.int32)],
        scratch_shapes=[pltpu.VMEM((N_EXPERTS, 1), jnp.float32)],
        compiler_params=pltpu.CompilerParams(dimension_semantics=("arbitrary",), vmem_limit_bytes=VMEM_LIMIT_BYTES),
        name="moe_rank",
    )(flat_e.reshape(1, nk))
    return rank.reshape(nk), cnt.reshape(N_EXPERTS)


def moe_dispatch(h, lp, blk):
    B, T, D = h.shape
    x = h.reshape(B * T, D)
    n = x.shape[0]
    f32 = jnp.float32
    eidx, w = moe_route(x, lp['router_w'], lp['router_bias'])

    nk = n * TOP_K
    flat_e = eidx.reshape(nk)
    rank, counts = moe_rank(flat_e)
    padded = (counts + blk - 1) // blk * blk
    pad_end = jnp.cumsum(padded)
    dest = jnp.take(pad_end - padded, flat_e) + rank
    n_blocks = -(-(nk + N_EXPERTS * (blk - 1)) // blk)
    rows = n_blocks * blk
    row_aid = jnp.full((rows,), -1, jnp.int32).at[dest].set(jnp.arange(nk, dtype=jnp.int32))
    live = row_aid >= 0
    row_aid = jnp.maximum(row_aid, 0)
    row_tok = row_aid // TOP_K
    row_w = jnp.where(live, jnp.take(w.reshape(nk), row_aid), 0.0)
    blk_exp = jnp.minimum(jnp.searchsorted(pad_end, jnp.arange(n_blocks) * blk, side='right'),
                          N_EXPERTS - 1).astype(jnp.int32)
    n_valid = (pad_end[-1] // blk).astype(jnp.int32).reshape(1)
    pos = dest.reshape(n, TOP_K)

    xb = x.astype(jnp.bfloat16)
    return dict(xb=xb, xs=xb[row_tok], row_w=row_w[:, None], blk_exp=blk_exp, n_valid=n_valid, pos=pos, blk=blk,
                shape=(B, T, D))


def moe_experts(dp, lp):
    n = dp['xb'].shape[0]
    ys = expert_ffn(dp['xs'], lp['exp_w_gate'], lp['exp_w_up'], lp['exp_w_down'], dp['row_w'], dp['blk_exp'],
                    dp['n_valid'], dp['blk'])
    sblk = min(256, n)
    one = jnp.ones((n, 1), jnp.float32)
    shared = expert_ffn(dp['xb'], lp['sh_w_gate'][None], lp['sh_w_up'][None], lp['sh_w_down'][None], one,
                        jnp.zeros((n // sblk,), jnp.int32), jnp.full((1,), n // sblk, jnp.int32), sblk)
    return ys[dp['pos']], shared


def moe_combine(gathered, shared, shape):
    return (gathered.sum(axis=1) + shared).reshape(shape)


def moe_ffn_pallas(h, lp, blk):
    dp = moe_dispatch(h, lp, blk)
    gathered, shared = moe_experts(dp, lp)
    return moe_combine(gathered, shared, dp['shape'])


def pad_rows(a, length):
    return jnp.pad(a, [(0, 0), (0, length - a.shape[1])] + [(0, 0)] * (a.ndim - 2))


def rwkv7_mix(pr, shift_prev, wkv0, lp):
    B, T, _ = pr.shape
    f32 = jnp.float32
    prev = jnp.concatenate([shift_prev[:, None, :].astype(pr.dtype), pr[:, :-1]], axis=1)
    m = pr + (prev - pr) * lp['rwkv_mu']
    r, k, v, xw, xa, xg = jnp.split(m, RWKV_SPLITS, axis=-1)
    w_log = -jax.nn.softplus(-(lp['rwkv_w0'] + jnp.tanh(xw) @ lp['rwkv_w_up']).astype(f32)) - 0.5
    decay = jnp.exp(-jnp.exp(w_log))
    a = jax.nn.sigmoid((lp['rwkv_a0'] + xa @ lp['rwkv_a_up']).astype(f32))
    g = jax.nn.sigmoid(xg) @ lp['rwkv_g_up']

    def heads(t):
        return t.astype(f32).reshape(B, T, RWKV_HEADS, RWKV_HEAD_DIM)

    r, k, v, decay, a = heads(r), heads(k), heads(v), heads(decay), heads(a)
    kk = k * lp['rwkv_k_k'].astype(f32).reshape(RWKV_HEADS, RWKV_HEAD_DIM)
    kk = kk / jnp.maximum(jnp.linalg.norm(kk, axis=-1, keepdims=True), 1e-12)
    k = k * (1.0 + (a - 1.0) * lp['rwkv_k_a'].astype(f32).reshape(RWKV_HEADS, RWKV_HEAD_DIM))

    def step(S, inp):
        r_t, w_t, k_t, v_t, a_t, b_t = inp
        sa = jnp.einsum('bhvk,bhk->bhv', S, a_t)
        S = S * w_t[:, :, None, :] + sa[..., None] * b_t[:, :, None, :] + v_t[..., None] * k_t[:, :, None, :]
        return S, jnp.einsum('bhvk,bhk->bhv', S, r_t)

    xs = tuple(jnp.moveaxis(t, 1, 0) for t in (r, decay, k, v, -kk, kk * a))
    s_final, ys = lax.scan(step, wkv0.astype(f32), xs)
    y = jnp.moveaxis(ys, 0, 1)
    mu = jnp.mean(y, axis=-1, keepdims=True)
    var = jnp.mean(jnp.square(y - mu), axis=-1, keepdims=True)
    y = ((y - mu) * lax.rsqrt(var + LNX_EPS)).reshape(B, T, RWKV_WIDTH)
    y = y * lp['rwkv_lnx_w'].astype(f32) + lp['rwkv_lnx_b'].astype(f32)
    bonus = (jnp.sum(r * k * lp['rwkv_r_k'].astype(f32), axis=-1, keepdims=True) * v).reshape(B, T, RWKV_WIDTH)
    out = ((y + bonus) * g.astype(f32)).astype(pr.dtype)
    return out, pr[:, -1], s_final.astype(wkv0.dtype)


def compress_blocks(rows, pe, w):
    B, L = rows.shape[:2]
    blk = rows.reshape(B, L // CMP_BLOCK, CMP_BLOCK, 2, NSA_KV_HEADS, NSA_HEAD_DIM)
    blk = blk + jnp.swapaxes(pe, 0, 1)[None, None, :, :, None, :]
    return jnp.einsum('bncjkd,jcde->bnjke', blk, w)


def nsa_attend(q, q_pos, gates, kc, vc, c_end, n_sel_blocks, gather_sel, kw, vw, w_pos, rel_bias):
    B, Tq = q.shape[:2]
    f32 = jnp.float32
    scale = NSA_HEAD_DIM ** -0.5
    qg = q.reshape(B, Tq, NSA_KV_HEADS, NSA_GROUP, NSA_HEAD_DIM)
    bias_tab = rel_bias.astype(f32).reshape(NUM_BUCKETS, NSA_KV_HEADS, NSA_GROUP)

    dc = q_pos[:, None] - c_end[None, :]
    bias_c = bias_tab[rel_bucket(dc)].transpose(0, 2, 3, 1)
    lc = jnp.einsum('btkgd,bnkd->btkgn', qg, kc).astype(f32) * scale + bias_c
    pc = masked_softmax(lc, (dc >= 0)[:, None, None, :])
    oc = jnp.einsum('btkgn,bnkd->btkgd', pc.astype(vc.dtype), vc)

    imp = pc.sum(3).reshape(B, Tq, NSA_KV_HEADS, n_sel_blocks, SEL_BLOCK // CMP_BLOCK).sum(-1)
    blk = jnp.arange(n_sel_blocks)
    cur = (q_pos // SEL_BLOCK)[:, None]
    force = (blk == 0) | (blk == cur) | (blk == cur - 1)
    score = jnp.where(force[:, None, :], FORCE_SCORE, imp)
    score = jnp.where((blk <= cur)[:, None, :], score, -jnp.inf)
    _, idx = lax.top_k(jnp.moveaxis(score, 2, 1), min(N_SEL, n_sel_blocks))

    sel = gather_sel(idx)
    ks, vs = sel[..., 0, :], sel[..., 1, :]
    spos = idx[..., None] * SEL_BLOCK + jnp.arange(SEL_BLOCK)
    ds = q_pos[None, None, :, None, None] - spos
    hi = jnp.arange(NSA_KV_HEADS)[None, :, None, None, None]
    bias_s = jnp.moveaxis(bias_tab[rel_bucket(ds), hi], -1, 3)
    ls = jnp.einsum('btkgd,bktjsd->bktgjs', qg, ks).astype(f32) * scale + bias_s
    ms = (ds >= 0)[:, :, :, None]
    ps = masked_softmax(ls.reshape(ls.shape[:4] + (-1,)), ms.reshape(ms.shape[:4] + (-1,))).reshape(ls.shape)
    osel = jnp.einsum('bktgjs,bktjsd->btkgd', ps.astype(vs.dtype), vs)

    dw = q_pos[:, None] - w_pos[None, :]
    mw = (dw >= 0) & (dw <= WINDOW) & (w_pos >= 0)[None, :]
    bias_w = bias_tab[rel_bucket(dw)].transpose(0, 2, 3, 1)
    lw = jnp.einsum('btkgd,blkd->btkgl', qg, kw).astype(f32) * scale + bias_w
    pw = masked_softmax(lw, mw[:, None, None, :])
    ow = jnp.einsum('btkgl,blkd->btkgd', pw.astype(vw.dtype), vw)

    gt = gates.reshape(B, Tq, NSA_KV_HEADS, NSA_GROUP, 3)
    o = gt[..., 0:1] * oc + gt[..., 1:2] * osel + gt[..., 2:3] * ow
    return o.reshape(B, Tq, NSA_WIDTH)


def split_proj(h, lp):
    B, T, _ = h.shape
    proj = mm3(h, lp['w_in'])
    pr = proj[..., :RWKV_PROJ]
    pn = proj[..., RWKV_PROJ:]
    q = pn[..., :NSA_WIDTH].reshape(B, T, NSA_HEADS, NSA_HEAD_DIM)
    kv = pn[..., NSA_WIDTH:NSA_WIDTH + 3 * NSA_KV_WIDTH].reshape(B, T, 3, 2, NSA_KV_HEADS, NSA_HEAD_DIM)
    gates = jax.nn.sigmoid(pn[..., NSA_WIDTH + 3 * NSA_KV_WIDTH:].astype(jnp.float32))
    gates = gates.reshape(B, T, NSA_HEADS, 3).astype(h.dtype)
    return pr, q, kv[:, :, 0], kv[:, :, 1], kv[:, :, 2], gates


def merge_groups(o_rwkv, o_nsa, lp):
    o = jnp.concatenate([o_rwkv, rms_norm(o_nsa, lp['nsa_out_g'])], axis=-1).astype(jnp.bfloat16)
    return mm3(o, lp['w_out'])


def mixer_prompt(h, lp, rel_bias):
    B, T, d = h.shape
    proj_p = pallas_matmul(h.reshape(B * T, d), lp['w_in_p'])
    o_r, wkv = rwkv_mix(proj_p, jnp.zeros((B, RWKV_PROJ), h.dtype),
                        jnp.zeros((B, RWKV_HEADS, RWKV_HEAD_DIM, RWKV_HEAD_DIM), jnp.float32), lp, B, T, T, 64, 8)
    o_r = o_r.reshape(B, T, RWKV_WIDTH)
    shift_last = unpad_rwkv(proj_p.reshape(B, T, P_TOTAL)[:, -1])
    kvc = compress_prompt(proj_p, lp['cmp_pe'], lp['cmp_w'], B, T)
    o_n = nsa_prompt(proj_p, kvc, rel_bias, B, T).reshape(B, T, NSA_WIDTH)
    kv = proj_p[:, P_KVC:P_GATES].reshape(B, T, 3, 2, NSA_KV_HEADS, NSA_HEAD_DIM)
    kv_c, kv_s, kv_w = kv[:, :, 0], kv[:, :, 1], kv[:, :, 2]
    win_keep = min(WINDOW, T)
    return merge_groups(o_r, o_n, lp), (kv_c, kv_s, kv_w[:, T - win_keep:], wkv, shift_last)


def mixer_prompt_gather(h, lp, rel_bias):
    B, T, _ = h.shape
    pr, q, kv_c, kv_s, kv_w, gates = split_proj(h, lp)
    o_r, shift_last, wkv = rwkv7_mix(pr, jnp.zeros((B, RWKV_PROJ), h.dtype),
                                     jnp.zeros((B, RWKV_HEADS, RWKV_HEAD_DIM, RWKV_HEAD_DIM), jnp.float32), lp)
    lp_len = -(-T // SEL_BLOCK) * SEL_BLOCK
    comp = compress_blocks(pad_rows(kv_c, lp_len), lp['cmp_pe'], lp['cmp_w'])
    kc, vc = comp[:, :, 0], comp[:, :, 1]
    c_end = jnp.arange(lp_len // CMP_BLOCK) * CMP_BLOCK + (CMP_BLOCK - 1)
    n_sb = lp_len // SEL_BLOCK
    sel_blocks = pad_rows(kv_s, lp_len).reshape(B, n_sb, SEL_BLOCK, 2, NSA_KV_HEADS, NSA_HEAD_DIM)
    bi = jnp.arange(B)[:, None, None, None]
    hi = jnp.arange(NSA_KV_HEADS)[None, :, None, None]

    def gather_sel(idx):
        return sel_blocks[bi, idx, :, :, hi, :]

    kw_pad = jnp.pad(kv_w, ((0, 0), (WINDOW, 0), (0, 0), (0, 0), (0, 0)))

    def chunk(i):
        t0 = i * Q_CHUNK
        qc = lax.dynamic_slice_in_dim(q, t0, Q_CHUNK, axis=1)
        gc = lax.dynamic_slice_in_dim(gates, t0, Q_CHUNK, axis=1)
        wc = lax.dynamic_slice_in_dim(kw_pad, t0, WINDOW + Q_CHUNK, axis=1)
        q_pos = t0 + jnp.arange(Q_CHUNK)
        w_pos = t0 - WINDOW + jnp.arange(WINDOW + Q_CHUNK)
        return nsa_attend(qc, q_pos, gc, kc, vc, c_end, n_sb, gather_sel,
                          wc[:, :, 0], wc[:, :, 1], w_pos, rel_bias)

    o_n = lax.map(chunk, jnp.arange(T // Q_CHUNK))
    o_n = jnp.moveaxis(o_n, 0, 1).reshape(B, T, NSA_WIDTH)
    win_keep = min(WINDOW, T)
    return merge_groups(o_r, o_n, lp), (kv_c, kv_s, kv_w[:, T - win_keep:], wkv, shift_last)


def mixer_sample(h, lp, rel_bias, pool_cmp, pool_sel, win_buf, wkv0, shift0, page_table):
    B, T, _ = h.shape
    n_pages = page_table.shape[1]
    page = pool_cmp.shape[1]
    past = n_pages * page
    d = h.shape[-1]
    proj_p = pallas_matmul(h.reshape(B * T, d), lp['w_in_p']).reshape(B, T, P_TOTAL)
    import numpy as np
    kv = proj_p[..., P_KVC:P_GATES].reshape(B, T, 3, 2, NSA_KV_HEADS, NSA_HEAD_DIM)
    kv_c, kv_s, kv_w = kv[:, :, 0], kv[:, :, 1], kv[:, :, 2]
    proj_pad = jnp.pad(proj_p, ((0, 0), (0, T_PAD - T), (0, 0))).reshape(B * T_PAD, P_TOTAL)
    o_r, wkv = rwkv_mix(proj_pad, shift0, wkv0, lp, B, T_PAD, T, T_PAD, 4)
    o_r = o_r.reshape(B, T_PAD, RWKV_WIDTH)[:, :T]
    shift_last = unpad_rwkv(proj_p[:, -1])
    lp_len = -(-(past + T) // SEL_BLOCK) * SEL_BLOCK

    jk, hd = 2 * NSA_KV_HEADS, NSA_HEAD_DIM
    kvc_past = compress_paged(pool_cmp, page_table, lp['cmp_pe'], lp['cmp_w'])
    new_len = lp_len - past
    new_rows = pad_rows(proj_p[..., P_KVC:P_KVS], new_len).reshape(B * new_len, NSA_KV_WIDTH)
    kvc_new = compress_prompt(new_rows, lp['cmp_pe'], lp['cmp_w'], B, new_len, col0=0)
    nb_new = new_len // CMP_BLOCK
    assert nb_new <= 2
    kvc_new = kvc_new.reshape(B, jk, nb_new, hd).transpose(0, 2, 1, 3)
    kvc_all = jnp.concatenate([kvc_past, kvc_new], axis=1)
    nc = lp_len // CMP_BLOCK
    n_half = nc // 2
    half = -(-n_half // LANE) * LANE
    idx = np.zeros((2 * half,), np.int32)
    idx[:n_half] = np.arange(0, nc, 2)
    idx[half:half + n_half] = np.arange(1, nc, 2)
    kvc_t = kvc_all[:, idx].transpose(0, 2, 1, 3)
    o_n = nsa_sample(proj_pad, kvc_t, pool_sel, win_buf, page_table, rel_bias, B, T)
    o_n = o_n.reshape(B, T_PAD, NSA_WIDTH)[:, :T]
    new_win = jnp.concatenate([win_buf, kv_w], axis=1)[:, T:]
    return merge_groups(o_r, o_n, lp), (kv_c, kv_s, new_win, wkv, shift_last)


def moe_ffn(h, lp):
    B, T, D = h.shape
    x = h.reshape(B * T, D)
    n = x.shape[0]
    f32 = jnp.float32
    scores = jax.nn.sigmoid((x @ lp['router_w']).astype(f32))
    biased = scores + lp['router_bias'].astype(f32)
    grp = biased.reshape(n, N_GROUPS, N_EXPERTS // N_GROUPS)
    grp_score = lax.top_k(grp, 2)[0].sum(-1)
    _, gidx = lax.top_k(grp_score, TOPK_GROUPS)
    gmask = jax.nn.one_hot(gidx, N_GROUPS, dtype=f32).sum(1)
    emask = jnp.repeat(gmask, N_EXPERTS // N_GROUPS, axis=1) > 0
    _, eidx = lax.top_k(jnp.where(emask, biased, -jnp.inf), TOP_K)
    w = jnp.take_along_axis(scores, eidx, axis=1)
    w = w / jnp.sum(w, axis=-1, keepdims=True) * ROUTED_SCALE

    nk = n * TOP_K
    flat_e = eidx.reshape(nk)
    order = jnp.argsort(flat_e)
    se = flat_e[order]
    counts = jnp.bincount(flat_e, length=N_EXPERTS)
    padded = (counts + MOE_BLOCK - 1) // MOE_BLOCK * MOE_BLOCK
    pad_end = jnp.cumsum(padded)
    dest = (pad_end - padded)[se] + jnp.arange(nk) - (jnp.cumsum(counts) - counts)[se]
    n_blocks = -(-(nk + N_EXPERTS * (MOE_BLOCK - 1)) // MOE_BLOCK)
    rows = n_blocks * MOE_BLOCK
    row_tok = jnp.zeros((rows,), jnp.int32).at[dest].set((order // TOP_K).astype(jnp.int32))
    row_w = jnp.zeros((rows,), f32).at[dest].set(w.reshape(nk)[order])
    blk_exp = jnp.minimum(jnp.searchsorted(pad_end, jnp.arange(n_blocks) * MOE_BLOCK, side='right'),
                          N_EXPERTS - 1)

    def expert_block(args):
        tok, wt, e = args
        xb = x[tok]
        hb = jax.nn.silu(xb @ lp['exp_w_gate'][e]) * (xb @ lp['exp_w_up'][e])
        return (hb @ lp['exp_w_down'][e]).astype(f32) * wt[:, None]

    out = lax.map(expert_block, (row_tok.reshape(n_blocks, MOE_BLOCK), row_w.reshape(n_blocks, MOE_BLOCK), blk_exp))
    routed = jnp.zeros((n, D), f32).at[row_tok].add(out.reshape(rows, D))
    shared = (jax.nn.silu(x @ lp['sh_w_gate']) * (x @ lp['sh_w_up'])) @ lp['sh_w_down']
    return (routed + shared.astype(f32)).astype(h.dtype).reshape(B, T, D)


def trunk_layer(x, mod, lp, mixer):
    B, _, D = x.shape
    mod = mod.reshape(B, 6, 1, D)
    sh1, sc1, gt1, sh2, sc2, gt2 = (mod[:, i] for i in range(6))
    g = lp['norm_g']
    h = (rms_norm(x, g[0]) * (1 + sc1) + sh1).astype(jnp.bfloat16)
    o, st = mixer(h)
    x = x + gt1 * rms_norm(o, g[1])
    h = (rms_norm(x, g[2]) * (1 + sc2) + sh2).astype(jnp.bfloat16)
    dp = moe_dispatch(h, lp, 256 if x.shape[0] * x.shape[1] >= 4096 else 32)
    return x, gt2, dp, st


def trunk_finish(x, gt2, gathered, shared, dp, lp):
    return x + gt2 * rms_norm(moe_combine(gathered, shared, dp['shape']), lp['norm_g'][3])


def kernel(x_prompt, x_sample, c_prompt, c_sample, cache_cmp, cache_sel, state_win, state_wkv,
           state_shift, page_table, rel_bias, w_ada, b_ada, norm_g, w_in, w_out, rwkv_mu, rwkv_w0,
           rwkv_w_up, rwkv_a0, rwkv_a_up, rwkv_g_up, rwkv_k_k, rwkv_k_a, rwkv_r_k, rwkv_lnx_w,
           rwkv_lnx_b, cmp_pe, cmp_w, nsa_out_g, router_w, router_bias, exp_w_gate, exp_w_up,
           exp_w_down, sh_w_gate, sh_w_up, sh_w_down):
    l = 0
    lp = dict(w_ada=w_ada[l], b_ada=b_ada[l], norm_g=norm_g[l], w_in=w_in[l], w_out=w_out[l],
              rwkv_mu=rwkv_mu[l], rwkv_w0=rwkv_w0[l], rwkv_w_up=rwkv_w_up[l], rwkv_a0=rwkv_a0[l],
              rwkv_a_up=rwkv_a_up[l], rwkv_g_up=rwkv_g_up[l], rwkv_k_k=rwkv_k_k[l],
              rwkv_k_a=rwkv_k_a[l], rwkv_r_k=rwkv_r_k[l], rwkv_lnx_w=rwkv_lnx_w[l],
              rwkv_lnx_b=rwkv_lnx_b[l], cmp_pe=cmp_pe[l], cmp_w=cmp_w[l], nsa_out_g=nsa_out_g[l],
              router_w=router_w[l], router_bias=router_bias[l], exp_w_gate=exp_w_gate[l],
              exp_w_up=exp_w_up[l], exp_w_down=exp_w_down[l], sh_w_gate=sh_w_gate[l],
              sh_w_up=sh_w_up[l], sh_w_down=sh_w_down[l])
    lp['w_in_p'] = pad_w_in(w_in[l])
    nb_p = x_prompt.shape[0]
    mod = pallas_matmul(jax.nn.silu(jnp.concatenate([c_prompt, c_sample], axis=0)), lp['w_ada']) + lp['b_ada']
    xp, gtp, dpp, st_p = trunk_layer(x_prompt, mod[:nb_p], lp, lambda h: mixer_prompt(h, lp, rel_bias))
    xs_, gts, dps, st_s = trunk_layer(x_sample, mod[nb_p:], lp, lambda h: mixer_sample(
        h, lp, rel_bias, cache_cmp[l], cache_sel[l], state_win[l], state_wkv[l], state_shift[l], page_table))
    dpp['xs'], dps['xs'] = lax.optimization_barrier((dpp['xs'], dps['xs']))
    gath_p, shared_p = moe_experts(dpp, lp)
    gath_s, shared_s = moe_experts(dps, lp)
    ys = trunk_finish(xs_, gts, gath_s, shared_s, dps, lp)
    gath_p, ys = lax.optimization_barrier((gath_p, ys))
    yp = trunk_finish(xp, gtp, gath_p, shared_p, dpp, lp)
    p_cmp, p_sel, p_win, p_wkv, p_shift = [a[None] for a in st_p]
    s_cmp, s_sel, s_win, s_wkv, s_shift = [a[None] for a in st_s]
    return (yp, ys, p_cmp, p_sel, p_win, p_wkv, p_shift, s_cmp, s_sel, s_win, s_wkv, s_shift)
```

```python
import functools
import math

import jax
import jax.numpy as jnp
from jax import lax
from jax.experimental import pallas as pl
from jax.experimental.pallas import tpu as pltpu

D_MODEL = 4096
RWKV_WIDTH = D_MODEL // 2
NSA_WIDTH = D_MODEL - RWKV_WIDTH
RWKV_HEAD_DIM = 64
RWKV_HEADS = RWKV_WIDTH // RWKV_HEAD_DIM
DECAY_LORA = 96
AAA_LORA = 96
GATE_LORA = 256
RWKV_PROJ = 3 * RWKV_WIDTH + DECAY_LORA + AAA_LORA + GATE_LORA
RWKV_SPLITS = (RWKV_WIDTH, 2 * RWKV_WIDTH, 3 * RWKV_WIDTH, 3 * RWKV_WIDTH + DECAY_LORA,
               3 * RWKV_WIDTH + DECAY_LORA + AAA_LORA)
LNX_EPS = 64e-5
NSA_HEAD_DIM = 128
NSA_HEADS = NSA_WIDTH // NSA_HEAD_DIM
NSA_KV_HEADS = 4
NSA_GROUP = NSA_HEADS // NSA_KV_HEADS
CMP_BLOCK = 32
SEL_BLOCK = 64
N_SEL = 16
WINDOW = 512
Q_CHUNK = 32
FORCE_SCORE = 1e4
NSA_KV_WIDTH = 2 * NSA_KV_HEADS * NSA_HEAD_DIM
NUM_BUCKETS = 32
MAX_DISTANCE = 128
N_EXPERTS = 64
N_GROUPS = 8
TOPK_GROUPS = 4
TOP_K = 8
ROUTED_SCALE = 2.5
MOE_BLOCK = 128
RMS_EPS = 1e-6

VMEM_LIMIT_BYTES = 56 * 1024 * 1024


def _matmul_kernel(x_ref, w_ref, o_ref):
    o_ref[...] = jnp.dot(x_ref[...].astype(jnp.bfloat16), w_ref[...].astype(jnp.bfloat16),
                         preferred_element_type=jnp.float32)


def pallas_matmul(x, w, tn=512):
    m, k = x.shape
    n = w.shape[1]
    row_mult = 16 if x.dtype == jnp.bfloat16 else 8
    mp = -(-m // row_mult) * row_mult
    tm = min(1024 if x.dtype == jnp.bfloat16 else 512, mp)
    mp = -(-mp // tm) * tm
    np_ = -(-n // tn) * tn
    if mp != m:
        x = jnp.pad(x, ((0, mp - m), (0, 0)))
    if np_ != n:
        w = jnp.pad(w, ((0, 0), (0, np_ - n)))
    out = pl.pallas_call(
        _matmul_kernel,
        grid=(mp // tm, np_ // tn),
        in_specs=[pl.BlockSpec((tm, k), lambda i, j: (i, 0)),
                  pl.BlockSpec((k, tn), lambda i, j: (0, j))],
        out_specs=pl.BlockSpec((tm, tn), lambda i, j: (i, j)),
        out_shape=jax.ShapeDtypeStruct((mp, np_), jnp.float32),
        compiler_params=pltpu.CompilerParams(
            dimension_semantics=("arbitrary", "arbitrary"), vmem_limit_bytes=VMEM_LIMIT_BYTES),
        name="matmul",
    )(x, w)
    return out[:m, :n]


def mm3(h, w):
    b, t, d = h.shape
    return pallas_matmul(h.reshape(b * t, d), w).reshape(b, t, w.shape[1])


def rms_norm(x, g):
    xf = x.astype(jnp.float32)
    y = xf * lax.rsqrt(jnp.mean(xf * xf, axis=-1, keepdims=True) + RMS_EPS)
    return (y * g.astype(jnp.float32)).astype(x.dtype)


def rel_bucket(dist):
    max_exact = NUM_BUCKETS // 2
    n = jnp.maximum(dist, 0)
    nf = jnp.maximum(n, max_exact).astype(jnp.float32)
    large = max_exact + (jnp.log(nf / max_exact) / math.log(MAX_DISTANCE / max_exact)
                         * (NUM_BUCKETS - max_exact)).astype(jnp.int32)
    return jnp.where(n < max_exact, n, jnp.minimum(large, NUM_BUCKETS - 1))


def masked_softmax(logits, mask):
    logits = jnp.where(mask, logits.astype(jnp.float32), -jnp.inf)
    m = jnp.max(logits, axis=-1, keepdims=True)
    p = jnp.exp(logits - jnp.where(jnp.isfinite(m), m, 0.0))
    s = jnp.sum(p, axis=-1, keepdims=True)
    return p / jnp.where(s > 0, s, 1.0)


LANE = 128
P_XW = 3 * RWKV_WIDTH
P_XA = P_XW + LANE
P_XG = P_XA + LANE
P_Q = P_XG + GATE_LORA
P_KVC = P_Q + NSA_WIDTH
P_KVS = P_KVC + NSA_KV_WIDTH
P_KVW = P_KVS + NSA_KV_WIDTH
P_GATES = P_KVW + NSA_KV_WIDTH
P_TOTAL = P_GATES + NSA_KV_HEADS * LANE
N_GATE_COLS = 3 * NSA_GROUP


def pad_w_in(w_in):
    d = w_in.shape[0]
    z = lambda n: jnp.zeros((d, n), w_in.dtype)
    o_xw, o_xa, o_xg = RWKV_SPLITS[2], RWKV_SPLITS[3], RWKV_SPLITS[4]
    o_g = RWKV_PROJ + NSA_WIDTH + 3 * NSA_KV_WIDTH
    parts = [w_in[:, :o_xw], w_in[:, o_xw:o_xa], z(LANE - DECAY_LORA), w_in[:, o_xa:o_xg], z(LANE - AAA_LORA),
             w_in[:, o_xg:o_g]]
    for kh in range(NSA_KV_HEADS):
        parts += [w_in[:, o_g + kh * N_GATE_COLS:o_g + (kh + 1) * N_GATE_COLS], z(LANE - N_GATE_COLS)]
    return jnp.concatenate(parts, axis=1).astype(jnp.bfloat16)


def unpad_rwkv(proj_p):
    return jnp.concatenate([proj_p[..., :P_XW], proj_p[..., P_XW:P_XW + DECAY_LORA],
                            proj_p[..., P_XA:P_XA + AAA_LORA], proj_p[..., P_XG:P_XG + GATE_LORA]], axis=-1)


def bias_lookup(tab, idx):
    sel = jnp.asarray(idx.reshape(-1, 1), jnp.int32) == jnp.arange(tab.shape[0], dtype=jnp.int32)[None, :]
    out = jnp.dot(sel.astype(jnp.float32), tab, precision=lax.Precision.HIGHEST)
    return out.reshape(idx.shape + (tab.shape[1],))


def bucket_table(n):
    import numpy as np
    d = np.arange(n)
    max_exact = NUM_BUCKETS // 2
    nf = np.maximum(d, max_exact).astype(np.float64)
    large = max_exact + (np.log(nf / max_exact) / math.log(MAX_DISTANCE / max_exact)
                         * (NUM_BUCKETS - max_exact)).astype(np.int64)
    return np.where(d < max_exact, d, np.minimum(large, NUM_BUCKETS - 1)).astype(np.int32)


def _compress_kernel(x_ref, w_ref, pe_ref, o_ref, *, nblk):
    half = nblk // 2
    acc = jnp.zeros((nblk, NSA_HEAD_DIM), jnp.float32)
    for c in range(CMP_BLOCK):
        xe = x_ref[pl.ds(c, half, stride=2 * CMP_BLOCK), :]
        xo = x_ref[pl.ds(CMP_BLOCK + c, half, stride=2 * CMP_BLOCK), :]
        lhs = jnp.concatenate([xe, xo], axis=0) + pe_ref[c:c + 1, :]
        acc = acc + jnp.dot(lhs.astype(jnp.bfloat16), w_ref[c].astype(jnp.bfloat16),
                            preferred_element_type=jnp.float32)
    o_ref[...] = acc


def _compress_dense_kernel(x_ref, w_ref, pe_ref, o_ref, *, nblk):
    f32, bf16 = jnp.float32, jnp.bfloat16
    L = nblk * CMP_BLOCK
    half = nblk // 2
    rowc = lax.broadcasted_iota(jnp.int32, (L, 1), 0) & (CMP_BLOCK - 1)
    xb = (x_ref[...] + jnp.concatenate([pe_ref[...]] * nblk, axis=0)).astype(bf16)
    y = jnp.zeros((L, NSA_HEAD_DIM), f32)
    for c in range(CMP_BLOCK):
        y = y + jnp.dot(jnp.where(rowc == c, xb, jnp.zeros_like(xb)), w_ref[c].astype(bf16), preferred_element_type=f32)
    n_i = lax.broadcasted_iota(jnp.int32, (nblk, L), 0)
    r_i = lax.broadcasted_iota(jnp.int32, (nblk, L), 1)
    want = jnp.where(n_i < half, 2 * n_i, 2 * (n_i - half) + 1)
    bsum = ((r_i // CMP_BLOCK) == want).astype(bf16)
    yh = y.astype(bf16)
    ym = (y - yh.astype(f32)).astype(bf16)
    yl = (y - yh.astype(f32) - ym.astype(f32)).astype(bf16)
    d = lambda a: jnp.dot(bsum, a, preferred_element_type=f32)
    o_ref[...] = d(yh) + (d(ym) + d(yl))


def compress_prompt(proj_p, cmp_pe, cmp_w, B, T, col0=P_KVC):
    nblk = T // CMP_BLOCK
    hd, KVH = NSA_HEAD_DIM, NSA_KV_HEADS
    return pl.pallas_call(
        functools.partial(_compress_dense_kernel if nblk >= 16 else _compress_kernel, nblk=nblk),
        grid=(B, 2, KVH),
        in_specs=[pl.BlockSpec((T, hd), lambda b, j, kh: (b, col0 // hd + j * KVH + kh)),
                  pl.BlockSpec((None, CMP_BLOCK, hd, hd), lambda b, j, kh: (j, 0, 0, 0)),
                  pl.BlockSpec((None, CMP_BLOCK, hd), lambda b, j, kh: (j, 0, 0))],
        out_specs=pl.BlockSpec((None, None, None, nblk, hd), lambda b, j, kh: (b, j, kh, 0, 0)),
        out_shape=jax.ShapeDtypeStruct((B, 2, KVH, nblk, hd), jnp.float32),
        compiler_params=pltpu.CompilerParams(dimension_semantics=("arbitrary",) * 3, vmem_limit_bytes=VMEM_LIMIT_BYTES),
        name="compress_prompt",
    )(proj_p, cmp_w, cmp_pe)


NSA_TQ = 128
NEG_BIG = -1e30


def _nsa_prompt_kernel(q_ref, gate_ref, ks_ref, vs_ref, kw_ref, vw_ref, kc_ref, vc_ref, bt_ref, bc_ref, es_ref,
                       o_ref, sel_ref, qs_ref, m_ref, l_ref, acc_ref, *, n_sb):
    f32, bf16 = jnp.float32, jnp.bfloat16
    tq, G, hd = NSA_TQ, NSA_GROUP, NSA_HEAD_DIM
    qi = pl.program_id(2)
    t0 = qi * tq
    scale = NSA_HEAD_DIM ** -0.5
    qs_ref[...] = jnp.transpose(jnp.concatenate([q_ref[:, g * hd:(g + 1) * hd] for g in range(G)], axis=0)).astype(bf16)
    tlane = t0 + lax.broadcasted_iota(jnp.int32, (1, tq), 1)

    ncmp = kc_ref.shape[0]
    brow = lax.broadcasted_iota(jnp.int32, (ncmp, 1), 0)
    blk_id = jnp.where(brow < ncmp // 2, 2 * brow, 2 * (brow - ncmp // 2) + 1)
    vis = (blk_id * CMP_BLOCK + (CMP_BLOCK - 1)) <= tlane
    vis = jnp.concatenate([vis] * G, axis=1)
    lc = jnp.dot(kc_ref[...].astype(bf16), qs_ref[...], preferred_element_type=f32) * scale + bc_ref[...]
    lc = jnp.where(vis, lc, -jnp.inf)
    mc = jnp.max(lc, axis=0, keepdims=True)
    pc = jnp.exp(lc - jnp.where(mc > -jnp.inf, mc, 0.0))
    sc = jnp.sum(pc, axis=0, keepdims=True)
    pc = pc / jnp.where(sc > 0, sc, 1.0)
    oct = lax.dot_general(vc_ref[...].astype(bf16), pc.astype(bf16), (((0,), (0,)), ((), ())),
                          preferred_element_type=f32)

    pg = pc[:, 0:tq]
    for g in range(1, G):
        pg = pg + pc[:, g * tq:(g + 1) * tq]
    imp = pg[:n_sb] + pg[n_sb:]
    cur = tlane // SEL_BLOCK
    bl = lax.broadcasted_iota(jnp.int32, (n_sb, tq), 0)
    force = (bl == 0) | (bl == cur) | (bl == cur - 1)
    score = jnp.where(force, FORCE_SCORE, imp)
    score = jnp.where(bl <= cur, score, -jnp.inf)
    rank = jnp.zeros((n_sb, tq), jnp.int32)
    for i in range(n_sb):
        si = score[i:i + 1, :]
        ahead = (si > score) | ((si == score) & (i < bl))
        rank = rank + ahead.astype(jnp.int32)
    sel_ref[...] = (rank < N_SEL).astype(bf16)
    gate = jax.nn.sigmoid(gate_ref[...])
    for g in range(G):
        o_ref[:, g * hd:(g + 1) * hd] = gate[:, 3 * g:3 * g + 1] * jnp.transpose(oct[:, g * tq:(g + 1) * tq])

    kcol = lax.broadcasted_iota(jnp.int32, (tq, 1), 0)
    tlane = t0 + lax.broadcasted_iota(jnp.int32, (1, tq), 1)

    def attend(c, carry, k_ref, v_ref, selected):
        s0 = pl.multiple_of(c * tq, tq)
        dist = tlane - (s0 + kcol)
        if selected:
            ok = (jnp.dot(es_ref[c], sel_ref[...], preferred_element_type=f32) > 0.5) & (dist >= 0)
        else:
            ok = (dist >= 0) & (dist <= WINDOW)
        ok = jnp.concatenate([ok] * G, axis=1)
        k = k_ref[pl.ds(s0, tq), :].astype(bf16)
        vt = jnp.transpose(v_ref[pl.ds(s0, tq), :]).astype(bf16)
        s = jnp.dot(k, qs_ref[...], preferred_element_type=f32) * scale + bt_ref[jnp.minimum(qi - c, 2)]
        s = jnp.where(ok, s, NEG_BIG)
        m_old = m_ref[...]
        m_new = jnp.maximum(m_old, jnp.max(s, axis=0, keepdims=True))
        p = jnp.exp(s - m_new)
        alpha = jnp.exp(m_old - m_new)
        l_ref[...] = alpha * l_ref[...] + jnp.sum(p, axis=0, keepdims=True)
        acc_ref[...] = alpha * acc_ref[...] + jnp.dot(vt, p.astype(bf16), preferred_element_type=f32)
        m_ref[...] = m_new
        return carry

    def run_branch(lo, k_ref, v_ref, selected, gate_col):
        m_ref[...] = jnp.full(m_ref.shape, NEG_BIG, f32)
        l_ref[...] = jnp.zeros(l_ref.shape, f32)
        acc_ref[...] = jnp.zeros(acc_ref.shape, f32)
        lax.fori_loop(lo, qi + 1, functools.partial(attend, k_ref=k_ref, v_ref=v_ref, selected=selected), 0)
        ot = acc_ref[...] / l_ref[...]
        for g in range(G):
            o_ref[:, g * hd:(g + 1) * hd] += (gate[:, 3 * g + gate_col:3 * g + gate_col + 1]
                                              * jnp.transpose(ot[:, g * tq:(g + 1) * tq]))

    run_branch(0, ks_ref, vs_ref, True, 1)
    run_branch(jnp.maximum(qi - WINDOW // tq, 0), kw_ref, vw_ref, False, 2)


def nsa_prompt(proj_p, kvc, rel_bias, B, T):
    import numpy as np
    tq, G, hd, KVH = NSA_TQ, NSA_GROUP, NSA_HEAD_DIM, NSA_KV_HEADS
    nq = T // tq
    n_sb = T // SEL_BLOCK
    ncmp = T // CMP_BLOCK
    assert T % tq == 0 and T % SEL_BLOCK == 0 and ncmp == 2 * n_sb
    bucket = bucket_table(max(T, 3 * tq))
    tab = rel_bias.astype(jnp.float32)
    ii, jj = np.meshgrid(np.arange(tq), np.arange(tq), indexing="ij")
    didx = np.stack([bucket[np.maximum(ii - jj, 0)], bucket[tq + ii - jj], np.full((tq, tq), NUM_BUCKETS - 1)])
    assert bucket[tq + 1] == NUM_BUCKETS - 1
    bt = bias_lookup(tab, didx).reshape(3, tq, tq, KVH, G).transpose(3, 0, 2, 4, 1).reshape(KVH, 3, tq, G * tq)
    blk = np.concatenate([np.arange(0, ncmp, 2), np.arange(1, ncmp, 2)])
    dc = np.arange(T)[:, None] - (blk * CMP_BLOCK + CMP_BLOCK - 1)[None, :]
    bc = bias_lookup(tab, bucket[np.maximum(dc, 0)]).reshape(nq, tq, ncmp, KVH, G).transpose(3, 0, 2, 4, 1).reshape(KVH, nq, ncmp, G * tq)
    es = (np.arange(n_sb)[None, :, None] == (np.arange(nq)[:, None, None] * (tq // SEL_BLOCK)
                                              + np.arange(tq)[None, None, :] // SEL_BLOCK))
    es = jnp.asarray(np.swapaxes(es, 1, 2), jnp.bfloat16)
    col = lambda off: off // hd
    kv_spec = lambda off: pl.BlockSpec((T, hd), lambda b, kh, i: (b, col(off) + kh))
    return pl.pallas_call(
        functools.partial(_nsa_prompt_kernel, n_sb=n_sb),
        grid=(B, KVH, nq),
        in_specs=[pl.BlockSpec((tq, G * hd), lambda b, kh, i: (b * nq + i, P_Q // (G * hd) + kh)),
                  pl.BlockSpec((tq, LANE), lambda b, kh, i: (b * nq + i, P_GATES // LANE + kh)),
                  kv_spec(P_KVS), kv_spec(P_KVS + KVH * hd), kv_spec(P_KVW), kv_spec(P_KVW + KVH * hd),
                  pl.BlockSpec((None, None, None, ncmp, hd), lambda b, kh, i: (b, 0, kh, 0, 0)),
                  pl.BlockSpec((None, None, None, ncmp, hd), lambda b, kh, i: (b, 1, kh, 0, 0)),
                  pl.BlockSpec((None, 3, tq, G * tq), lambda b, kh, i: (kh, 0, 0, 0)),
                  pl.BlockSpec((None, None, ncmp, G * tq), lambda b, kh, i: (kh, i, 0, 0)),
                  pl.BlockSpec((nq, tq, n_sb), lambda b, kh, i: (0, 0, 0))],
        out_specs=pl.BlockSpec((tq, G * hd), lambda b, kh, i: (b * nq + i, kh)),
        out_shape=jax.ShapeDtypeStruct((B * T, NSA_WIDTH), jnp.float32),
        scratch_shapes=[pltpu.VMEM((n_sb, tq), jnp.bfloat16), pltpu.VMEM((hd, G * tq), jnp.bfloat16),
                        pltpu.VMEM((1, G * tq), jnp.float32), pltpu.VMEM((1, G * tq), jnp.float32),
                        pltpu.VMEM((hd, G * tq), jnp.float32)],
        compiler_params=pltpu.CompilerParams(dimension_semantics=("arbitrary",) * 3, vmem_limit_bytes=VMEM_LIMIT_BYTES),
        name="nsa_prompt",
    )(proj_p, proj_p, proj_p, proj_p, proj_p, proj_p, kvc, kvc, bt, bc, es)


SAMPLE_PG = 8
T_PAD = 8


def _compress_paged_kernel(pt_ref, *refs, nb):
    del pt_ref
    page_refs = refs[:SAMPLE_PG]
    w_ref, pe_ref, o_ref = refs[SAMPLE_PG:]
    jk, hd = 2 * NSA_KV_HEADS, NSA_HEAD_DIM
    rows = SAMPLE_PG * nb * jk
    acc = jnp.zeros((rows, 2 * hd), jnp.float32)
    for c in range(CMP_BLOCK):
        x = jnp.concatenate([pr[pl.ds(c, nb, stride=CMP_BLOCK), :, :] for pr in page_refs], axis=0)
        lhs = (x + pe_ref[c]).reshape(rows, hd).astype(jnp.bfloat16)
        acc = acc + jnp.dot(lhs, w_ref[c], preferred_element_type=jnp.float32)
    sub = lax.broadcasted_iota(jnp.int32, (rows, 1), 0) & (jk - 1)
    out = jnp.where(sub < NSA_KV_HEADS, acc[:, :hd], acc[:, hd:])
    o_ref[...] = out.reshape(SAMPLE_PG * nb, jk, hd)


def compress_paged(pool, page_table, cmp_pe, cmp_w):
    n_phys, page = pool.shape[:2]
    B, n_pages = page_table.shape
    jk, hd = 2 * NSA_KV_HEADS, NSA_HEAD_DIM
    nb = page // CMP_BLOCK
    assert n_pages % SAMPLE_PG == 0 and page % CMP_BLOCK == 0
    pool3 = pool.reshape(n_phys, page, jk, hd)
    w01 = jnp.concatenate([cmp_w[0], cmp_w[1]], axis=-1).astype(jnp.bfloat16)
    pe8 = jnp.repeat(jnp.swapaxes(cmp_pe, 0, 1), NSA_KV_HEADS, axis=1)
    page_spec = lambda i: pl.BlockSpec((None, page, jk, hd),
                                       lambda b, g, pt: (pt[b * n_pages + g * SAMPLE_PG + i], 0, 0, 0))
    return pl.pallas_call(
        functools.partial(_compress_paged_kernel, nb=nb),
        grid_spec=pltpu.PrefetchScalarGridSpec(
            num_scalar_prefetch=1, grid=(B, n_pages // SAMPLE_PG),
            in_specs=[page_spec(i) for i in range(SAMPLE_PG)]
            + [pl.BlockSpec((CMP_BLOCK, hd, 2 * hd), lambda b, g, pt: (0, 0, 0)),
               pl.BlockSpec((CMP_BLOCK, jk, hd), lambda b, g, pt: (0, 0, 0))],
            out_specs=pl.BlockSpec((None, SAMPLE_PG * nb, jk, hd), lambda b, g, pt: (b, g, 0, 0))),
        out_shape=jax.ShapeDtypeStruct((B, n_pages * nb, jk, hd), jnp.float32),
        compiler_params=pltpu.CompilerParams(dimension_semantics=("arbitrary",) * 2, vmem_limit_bytes=VMEM_LIMIT_BYTES),
        name="compress_paged",
    )(page_table.reshape(-1), *([pool3] * SAMPLE_PG), w01, pe8)


def _nsa_sample_kernel(pt_ref, *refs, past, page, n_steps, n_sb, half):
    del pt_ref
    f32, bf16 = jnp.float32, jnp.bfloat16
    KVH, G, hd = NSA_KV_HEADS, NSA_GROUP, NSA_HEAD_DIM
    q_refs = refs[0:KVH]
    gate_ref, ksn_ref, vsn_ref, kwn_ref, vwn_ref, kvc_ref, bc_ref = refs[KVH:KVH + 7]
    page_refs = refs[KVH + 7:KVH + 7 + SAMPLE_PG]
    es_ref, bs_ref, bn_ref, win_ref, bw_ref, o_ref, m_ref, l_ref, acc_ref, oc_ref, sel_ref = refs[KVH + 7 + SAMPLE_PG:]
    g_step = pl.program_id(1)
    R = G * T_PAD
    scale = NSA_HEAD_DIM ** -0.5
    nt = (((1,), (1,)), ((), ()))
    t_row = past + (lax.broadcasted_iota(jnp.int32, (R, 1), 0) & (T_PAD - 1))
    qs = [jnp.concatenate([q_refs[kh][:, g * hd:(g + 1) * hd] for g in range(G)], axis=0).astype(bf16) for kh in range(KVH)]

    @pl.when(g_step == 0)
    def _():
        m_ref[...] = jnp.full(m_ref.shape, NEG_BIG, f32)
        l_ref[...] = jnp.zeros(l_ref.shape, f32)
        acc_ref[...] = jnp.zeros(acc_ref.shape, f32)
        lane = lax.broadcasted_iota(jnp.int32, (1, 2 * half), 1)
        lo = lane < half
        blk_id = jnp.where(lo, 2 * lane, 2 * (lane - half) + 1)
        real = jnp.where(lo, lane, lane - half) < n_sb
        vis = real & (blk_id * CMP_BLOCK + (CMP_BLOCK - 1) <= t_row)
        tpos = past + lax.broadcasted_iota(jnp.int32, (T_PAD, 1), 0)
        cur = tpos // SEL_BLOCK
        bl = lax.broadcasted_iota(jnp.int32, (T_PAD, half), 1)
        force = (bl == 0) | (bl == cur) | (bl == cur - 1)
        for kh in range(KVH):
            lc = lax.dot_general(qs[kh], kvc_ref[kh].astype(bf16), nt, preferred_element_type=f32) * scale + bc_ref[kh]
            lc = jnp.where(vis, lc, -jnp.inf)
            mc = jnp.max(lc, axis=-1, keepdims=True)
            pc = jnp.exp(lc - jnp.where(mc > -jnp.inf, mc, 0.0))
            sc = jnp.sum(pc, axis=-1, keepdims=True)
            pc = pc / jnp.where(sc > 0, sc, 1.0)
            oc_ref[kh * R:(kh + 1) * R, :] = jnp.dot(pc.astype(bf16), kvc_ref[KVH + kh].astype(bf16),
                                                     preferred_element_type=f32)
            pg = pc[0:T_PAD]
            for g in range(1, G):
                pg = pg + pc[g * T_PAD:(g + 1) * T_PAD]
            imp = pg[:, :half] + pg[:, half:]
            score = jnp.where(force, FORCE_SCORE, imp)
            score = jnp.where((bl <= cur) & (bl < n_sb), score, -jnp.inf)
            rank = jnp.zeros((T_PAD, half), jnp.int32)
            for i in range(n_sb):
                si = score[:, i:i + 1]
                rank = rank + ((si > score) | ((si == score) & (i < bl))).astype(jnp.int32)
            selm = ((rank < N_SEL) & (bl < n_sb)).astype(bf16)
            sel_ref[kh * R:(kh + 1) * R, :] = jnp.concatenate([selm] * G, axis=0)

    last = g_step == n_steps - 1
    bsel = bs_ref[jnp.where(last, 1, 0)]
    okf = jnp.dot(sel_ref[...], es_ref[...], preferred_element_type=f32) > 0.5
    for kh in range(KVH):
        rs = slice(kh * R, (kh + 1) * R)
        k = jnp.concatenate([pr[pl.ds(kh, page, stride=2 * KVH), :] for pr in page_refs], axis=0).astype(bf16)
        v = jnp.concatenate([pr[pl.ds(KVH + kh, page, stride=2 * KVH), :] for pr in page_refs], axis=0).astype(bf16)
        s = lax.dot_general(qs[kh], k, nt, preferred_element_type=f32) * scale + bsel[rs]
        ok = okf[rs]
        s = jnp.where(ok, s, NEG_BIG)
        m_old = m_ref[rs]
        m_new = jnp.maximum(m_old, jnp.max(s, axis=-1, keepdims=True))
        p = jnp.where(ok, jnp.exp(s - m_new), 0.0)
        alpha = jnp.exp(m_old - m_new)
        l_ref[rs] = alpha * l_ref[rs] + jnp.sum(p, axis=-1, keepdims=True)
        acc_ref[rs] = alpha * acc_ref[rs] + jnp.dot(p.astype(bf16), v, preferred_element_type=f32)
        m_ref[rs] = m_new

    @pl.when(last)
    def _():
        gate = jax.nn.sigmoid(gate_ref[...])
        ncol = lax.broadcasted_iota(jnp.int32, (1, T_PAD), 1)
        wb = win_ref.shape[0] // (2 * KVH)
        wcol = lax.broadcasted_iota(jnp.int32, (1, wb), 1)
        for kh in range(KVH):
            rs = slice(kh * R, (kh + 1) * R)
            cs = slice(kh * hd, (kh + 1) * hd)
            s = lax.dot_general(qs[kh], ksn_ref[:, cs].astype(bf16), nt, preferred_element_type=f32) * scale + bn_ref[kh]
            ok = (past + ncol) <= t_row
            s = jnp.where(ok, s, NEG_BIG)
            m_old = m_ref[rs]
            m_new = jnp.maximum(m_old, jnp.max(s, axis=-1, keepdims=True))
            p = jnp.where(ok, jnp.exp(s - m_new), 0.0)
            alpha = jnp.exp(m_old - m_new)
            l_s = alpha * l_ref[rs] + jnp.sum(p, axis=-1, keepdims=True)
            osel = (alpha * acc_ref[rs] + jnp.dot(p.astype(bf16), vsn_ref[:, cs].astype(bf16),
                                                  preferred_element_type=f32)) / l_s
            s1 = lax.dot_general(qs[kh], win_ref[pl.ds(kh, wb, stride=2 * KVH), :].astype(bf16), nt,
                                 preferred_element_type=f32) * scale + bw_ref[kh]
            d1 = t_row - (past - wb + wcol)
            ok1 = (d1 >= 0) & (d1 <= WINDOW)
            s1 = jnp.where(ok1, s1, NEG_BIG)
            s2 = lax.dot_general(qs[kh], kwn_ref[:, cs].astype(bf16), nt, preferred_element_type=f32) * scale + bn_ref[kh]
            s2 = jnp.where(ok, s2, NEG_BIG)
            mw = jnp.maximum(jnp.max(s1, axis=-1, keepdims=True), jnp.max(s2, axis=-1, keepdims=True))
            p1 = jnp.where(ok1, jnp.exp(s1 - mw), 0.0)
            p2 = jnp.where(ok, jnp.exp(s2 - mw), 0.0)
            l_w = jnp.sum(p1, axis=-1, keepdims=True) + jnp.sum(p2, axis=-1, keepdims=True)
            owin = (jnp.dot(p1.astype(bf16), win_ref[pl.ds(KVH + kh, wb, stride=2 * KVH), :].astype(bf16),
                            preferred_element_type=f32)
                    + jnp.dot(p2.astype(bf16), vwn_ref[:, cs].astype(bf16), preferred_element_type=f32)) / l_w
            oc = oc_ref[rs]
            for g in range(G):
                r = slice(g * T_PAD, (g + 1) * T_PAD)
                c0 = kh * LANE + 3 * g
                o_ref[:, (kh * G + g) * hd:(kh * G + g + 1) * hd] = (
                    gate[:, c0:c0 + 1] * oc[r] + gate[:, c0 + 1:c0 + 2] * osel[r] + gate[:, c0 + 2:c0 + 3] * owin[r])


def nsa_sample(proj_pad, kvc_t, pool_sel, win_buf, page_table, rel_bias, B, T):
    import numpy as np
    KVH, G, hd = NSA_KV_HEADS, NSA_GROUP, NSA_HEAD_DIM
    n_phys, page = pool_sel.shape[:2]
    n_pages = page_table.shape[1]
    past = n_pages * page
    wb = win_buf.shape[1]
    half = kvc_t.shape[2] // 2
    n_sb = -(-(past + T) // SEL_BLOCK)
    n_steps = n_pages // SAMPLE_PG
    keys = SAMPLE_PG * page
    R = G * T_PAD
    assert n_pages % SAMPLE_PG == 0 and page % SEL_BLOCK == 0 and T <= T_PAD and n_sb <= half and past % SEL_BLOCK == 0
    bucket = bucket_table(past + T_PAD + 1)
    tab = rel_bias.astype(jnp.float32)
    qpos = past + np.arange(T_PAD)

    def bias_rows(dist):
        b = bias_lookup(tab, bucket[np.maximum(dist, 0)])
        return b.reshape(T_PAD, dist.shape[1], KVH, G).transpose(2, 3, 0, 1).reshape(KVH, R, dist.shape[1])

    lane = np.arange(2 * half)
    blk = np.where(lane < half, 2 * lane, 2 * (lane - half) + 1)
    bc = bias_rows(qpos[:, None] - (blk * CMP_BLOCK + CMP_BLOCK - 1)[None, :])
    far = bias_rows(np.full((T_PAD, keys), past, np.int64))
    assert bucket[page + 1] == NUM_BUCKETS - 1
    near = bias_rows(qpos[:, None] - ((n_steps - 1) * keys + np.arange(keys))[None, :])
    bs = jnp.stack([far, near], axis=0).reshape(2, KVH * R, keys)
    bn = bias_rows(qpos[:, None] - (past + np.arange(T_PAD))[None, :])
    bw = bias_rows(qpos[:, None] - (past - wb + np.arange(wb))[None, :])
    es = (np.arange(half)[None, :, None] == (np.arange(n_steps)[:, None, None] * (keys // SEL_BLOCK)
                                              + np.arange(keys)[None, None, :] // SEL_BLOCK))
    es = jnp.asarray(es, jnp.bfloat16)
    pool2 = pool_sel.reshape(n_phys, page * 2 * KVH, hd)
    win2 = win_buf.reshape(B, wb * 2 * KVH, hd)
    wq = G * hd
    c = lambda *idx: (lambda b, g, pt: idx)
    row_blk = lambda w, col: pl.BlockSpec((T_PAD, w), lambda b, g, pt: (b, col))
    in_specs = ([row_blk(wq, P_Q // wq + kh) for kh in range(KVH)]
                + [row_blk(KVH * LANE, P_GATES // (KVH * LANE)),
                   row_blk(KVH * hd, P_KVS // (KVH * hd)), row_blk(KVH * hd, P_KVS // (KVH * hd) + 1),
                   row_blk(KVH * hd, P_KVW // (KVH * hd)), row_blk(KVH * hd, P_KVW // (KVH * hd) + 1),
                   pl.BlockSpec((None, 2 * KVH, 2 * half, hd), lambda b, g, pt: (b, 0, 0, 0)),
                   pl.BlockSpec((KVH, R, 2 * half), c(0, 0, 0))]
                + [pl.BlockSpec((None, page * 2 * KVH, hd),
                                (lambda i: lambda b, g, pt: (pt[b * n_pages + g * SAMPLE_PG + i], 0, 0))(i))
                   for i in range(SAMPLE_PG)]
                + [pl.BlockSpec((None, half, keys), lambda b, g, pt: (g, 0, 0)),
                   pl.BlockSpec((2, KVH * R, keys), c(0, 0, 0)),
                   pl.BlockSpec((KVH, R, T_PAD), c(0, 0, 0)),
                   pl.BlockSpec((None, wb * 2 * KVH, hd), lambda b, g, pt: (b, 0, 0)),
                   pl.BlockSpec((KVH, R, wb), c(0, 0, 0))])
    f32 = jnp.float32
    return pl.pallas_call(
        functools.partial(_nsa_sample_kernel, past=past, page=page, n_steps=n_steps, n_sb=n_sb, half=half),
        grid_spec=pltpu.PrefetchScalarGridSpec(
            num_scalar_prefetch=1, grid=(B, n_steps), in_specs=in_specs,
            out_specs=pl.BlockSpec((T_PAD, NSA_WIDTH), lambda b, g, pt: (b, 0)),
            scratch_shapes=[pltpu.VMEM((KVH * R, 1), f32), pltpu.VMEM((KVH * R, 1), f32), pltpu.VMEM((KVH * R, hd), f32),
                            pltpu.VMEM((KVH * R, hd), f32), pltpu.VMEM((KVH * R, half), jnp.bfloat16)]),
        out_shape=jax.ShapeDtypeStruct((B * T_PAD, NSA_WIDTH), f32),
        compiler_params=pltpu.CompilerParams(dimension_semantics=("arbitrary",) * 2, vmem_limit_bytes=VMEM_LIMIT_BYTES),
        name="nsa_sample",
    )(page_table.reshape(-1), *([proj_pad] * (KVH + 5)), kvc_t, bc, *([pool2] * SAMPLE_PG), es, bs, bn, win2, bw)


RWKV_PAIR = 2 * RWKV_HEAD_DIM
HIGHEST = lax.Precision.HIGHEST


def _rwkv_kernel_one_pair(r_ref, k_ref, v_ref, xw_ref, xa_ref, xg_ref,
                 sr_ref, sk_ref, sv_ref, sxw_ref, sxa_ref, sxg_ref,
                 mr_ref, mk_ref, mv_ref, mxw_ref, mxa_ref, mxg_ref,
                 s0_ref, w0_ref, a0_ref, kk_ref, ka_ref, rk_ref, lnw_ref, lnb_ref,
                 wup_ref, aup_ref, gup_ref,
                 o_ref, sout_ref,
                 S_ref, cr_ref, ck_ref, cv_ref, cxw_ref, cxa_ref, cxg_ref, *, C, t_valid, n_chunks):
    f32, bf16 = jnp.float32, jnp.bfloat16
    hd, W = RWKV_HEAD_DIM, RWKV_PAIR
    ci = pl.program_id(2)
    lane = lax.broadcasted_iota(jnp.int32, (1, W), 1)
    h0 = lane < hd
    row = lax.broadcasted_iota(jnp.int32, (C, 1), 0)
    valid = (ci * C + row) < t_valid
    carries = ((cr_ref, sr_ref), (ck_ref, sk_ref), (cv_ref, sv_ref), (cxw_ref, sxw_ref), (cxa_ref, sxa_ref),
               (cxg_ref, sxg_ref))

    @pl.when(ci == 0)
    def _():
        S_ref[...] = jnp.zeros((W, W), f32)
        S_ref[0:hd, 0:hd] = s0_ref[0]
        S_ref[hd:W, hd:W] = s0_ref[1]
        for c_ref, s_ref in carries:
            c_ref[0:1, :] = s_ref[...]

    def shifted(x_ref, c_ref, mu_ref):
        x = x_ref[...]
        prev = jnp.where(row == 0, c_ref[0:1, :], pltpu.roll(x, 1, axis=0))
        c_ref[0:1, :] = x[C - 1:C, :]
        return x + (prev - x) * mu_ref[...]

    def seg_sum(x):
        s_lo = jnp.sum(jnp.where(h0, x, 0.0), axis=-1, keepdims=True)
        s_hi = jnp.sum(jnp.where(h0, 0.0, x), axis=-1, keepdims=True)
        return jnp.where(h0, s_lo, s_hi)

    def stack(x):
        return jnp.concatenate([jnp.where(h0, x, 0.0), jnp.where(h0, 0.0, x)], axis=0)

    mr = shifted(r_ref, cr_ref, mr_ref)
    mk = shifted(k_ref, ck_ref, mk_ref)
    mv = shifted(v_ref, cv_ref, mv_ref)
    mxw = shifted(xw_ref, cxw_ref, mxw_ref)
    mxa = shifted(xa_ref, cxa_ref, mxa_ref)
    mxg = shifted(xg_ref, cxg_ref, mxg_ref)

    z = w0_ref[...] + jnp.dot(jnp.tanh(mxw).astype(bf16), wup_ref[...].astype(bf16), preferred_element_type=f32)
    w_log = -jax.nn.softplus(-z) - 0.5
    lw = -jnp.exp(w_log)
    a = jax.nn.sigmoid(a0_ref[...] + jnp.dot(mxa.astype(bf16), aup_ref[...].astype(bf16), preferred_element_type=f32))
    g = jnp.dot(jax.nn.sigmoid(mxg).astype(bf16), gup_ref[...].astype(bf16), preferred_element_type=f32)
    kk = mk * kk_ref[...]
    kk = kk / jnp.maximum(jnp.sqrt(seg_sum(kk * kk)), 1e-12)
    kp = mk * (1.0 + (a - 1.0) * ka_ref[...])
    lw = jnp.where(valid, lw, 0.0)
    a_t = jnp.where(valid, -kk, 0.0)
    b_t = jnp.where(valid, kk * a, 0.0)
    k_t = jnp.where(valid, kp, 0.0)
    v_t = jnp.where(valid, mv, 0.0)

    ti = lax.broadcasted_iota(jnp.int32, (C, C), 0)
    tj = lax.broadcasted_iota(jnp.int32, (C, C), 1)
    cl = jnp.dot((tj <= ti).astype(f32), lw, precision=HIGHEST, preferred_element_type=f32)
    e_neg = jnp.exp(-cl)
    At = a_t * jnp.exp(cl - lw)
    Bt = b_t * e_neg
    Kt = k_t * e_neg
    Rt = mr * jnp.exp(cl)
    G = jnp.concatenate([stack(At), stack(Rt)], axis=0).astype(bf16)
    Z = jnp.concatenate([Bt, Bt, Kt, Kt], axis=0).astype(bf16)
    nt = (((1,), (1,)), ((), ()))
    M = lax.dot_general(G, Z, nt, preferred_element_type=f32)
    ri = lax.broadcasted_iota(jnp.int32, (2 * C, 2 * C), 0)
    cj = lax.broadcasted_iota(jnp.int32, (2 * C, 2 * C), 1)
    same = (ri >= C) == (cj >= C)
    strict = same & ((cj & (C - 1)) < (ri & (C - 1)))
    incl = same & ((cj & (C - 1)) <= (ri & (C - 1)))
    L = jnp.where(strict, M[0:2 * C, 0:2 * C], 0.0)
    AK = jnp.where(strict, M[0:2 * C, 2 * C:4 * C], 0.0)
    RB = jnp.where(incl, M[2 * C:4 * C, 0:2 * C], 0.0)
    RK = jnp.where(incl, M[2 * C:4 * C, 2 * C:4 * C], 0.0)
    S = S_ref[...]
    GS = lax.dot_general(G, S.astype(bf16), nt, preferred_element_type=f32)
    Vs = stack(v_t)
    U = GS[0:2 * C] + jnp.dot(AK.astype(bf16), Vs.astype(bf16), preferred_element_type=f32)
    P = L
    levels = C.bit_length() - 1
    for lvl in range(levels):
        U = U + jnp.dot(P, U, precision=HIGHEST, preferred_element_type=f32)
        if lvl + 1 < levels:
            P = jnp.dot(P, P, precision=HIGHEST, preferred_element_type=f32)
    X = jnp.concatenate([U, Vs], axis=0).astype(bf16)
    Y = GS[2 * C:4 * C] + jnp.dot(jnp.concatenate([RB, RK], axis=1).astype(bf16), X, preferred_element_type=f32)
    y = Y[0:C] + Y[C:2 * C]
    dS = lax.dot_general(X, Z, (((0,), (0,)), ((), ())), preferred_element_type=f32)
    vi = lax.broadcasted_iota(jnp.int32, (W, W), 0)
    kj = lax.broadcasted_iota(jnp.int32, (W, W), 1)
    S_new = jnp.where((vi < hd) == (kj < hd), S + dS, 0.0) * jnp.exp(cl[C - 1:C, :])
    S_ref[...] = S_new

    mean = seg_sum(y) * (1.0 / hd)
    yc = y - mean
    var = seg_sum(yc * yc) * (1.0 / hd)
    yn = yc * lax.rsqrt(var + LNX_EPS) * lnw_ref[...] + lnb_ref[...]
    bonus = seg_sum(mr * kp * rk_ref[...]) * mv
    o_ref[...] = (yn + bonus) * g

    @pl.when(ci == n_chunks - 1)
    def _():
        sout_ref[0] = S_new[0:hd, 0:hd]
        sout_ref[1] = S_new[hd:W, hd:W]


def rwkv_mix_one_pair(proj_p, shift_prev, wkv0, lp, B, T, t_valid, C):
    f32 = jnp.float32
    W, hd = RWKV_PAIR, RWKV_HEAD_DIM
    n_pairs = RWKV_WIDTH // W
    n_chunks = T // C
    assert T % C == 0 and C & (C - 1) == 0 and C >= 8

    def pad_vec(x):
        z = lambda n: jnp.zeros(x.shape[:-1] + (n,), x.dtype)
        return jnp.concatenate([x[..., :RWKV_SPLITS[2]], x[..., RWKV_SPLITS[2]:RWKV_SPLITS[3]], z(LANE - DECAY_LORA),
                                x[..., RWKV_SPLITS[3]:RWKV_SPLITS[4]], z(LANE - AAA_LORA), x[..., RWKV_SPLITS[4]:]], -1)

    shift_p = pad_vec(shift_prev.astype(f32))[:, None, :]
    mu_p = pad_vec(lp['rwkv_mu'].astype(f32))[None, :]
    pad_rows_to = lambda w: jnp.pad(w, ((0, LANE - w.shape[0]), (0, 0)))
    wup, aup = pad_rows_to(lp['rwkv_w_up']), pad_rows_to(lp['rwkv_a_up'])
    vec = lambda x: x.reshape(1, RWKV_WIDTH).astype(f32)
    nw = RWKV_WIDTH // W
    col_r, col_k, col_v = 0, nw, 2 * nw
    col_xw, col_xa, col_xg = P_XW // LANE, P_XA // LANE, P_XG // GATE_LORA
    row_spec = lambda w, cfn: pl.BlockSpec((C, w), lambda b, p, c: (b * n_chunks + c, cfn(p)))
    sh_spec = lambda w, cfn: pl.BlockSpec((None, 1, w), lambda b, p, c: (b, 0, cfn(p)))
    mu_spec = lambda w, cfn: pl.BlockSpec((1, w), lambda b, p, c: (0, cfn(p)))
    cols = [(W, lambda p: col_r + p), (W, lambda p: col_k + p), (W, lambda p: col_v + p),
            (LANE, lambda p: col_xw), (LANE, lambda p: col_xa), (GATE_LORA, lambda p: col_xg)]
    pvec = pl.BlockSpec((1, W), lambda b, p, c: (0, p))
    state_spec = pl.BlockSpec((None, 2, hd, hd), lambda b, p, c: (b, p, 0, 0))
    in_specs = ([row_spec(w, f) for w, f in cols] + [sh_spec(w, f) for w, f in cols] + [mu_spec(w, f) for w, f in cols]
                + [state_spec] + [pvec] * 7
                + [pl.BlockSpec((LANE, W), lambda b, p, c: (0, p)), pl.BlockSpec((LANE, W), lambda b, p, c: (0, p)),
                   pl.BlockSpec((GATE_LORA, W), lambda b, p, c: (0, p))])
    scratch = [pltpu.VMEM((W, W), f32)] + [pltpu.VMEM((8, w), f32) for w, _ in cols]
    o, s_fin = pl.pallas_call(
        functools.partial(_rwkv_kernel, C=C, t_valid=t_valid, n_chunks=n_chunks),
        grid=(B, n_pairs, n_chunks),
        in_specs=in_specs,
        out_specs=[pl.BlockSpec((C, W), lambda b, p, c: (b * n_chunks + c, p)), state_spec],
        out_shape=[jax.ShapeDtypeStruct((B * T, RWKV_WIDTH), f32),
                   jax.ShapeDtypeStruct((B, RWKV_HEADS, hd, hd), f32)],
        scratch_shapes=scratch,
        compiler_params=pltpu.CompilerParams(dimension_semantics=("arbitrary",) * 3, vmem_limit_bytes=VMEM_LIMIT_BYTES),
        name="rwkv_mix",
    )(*([proj_p] * 6), *([shift_p] * 6), *([mu_p] * 6), wkv0.astype(f32),
      vec(lp['rwkv_w0']), vec(lp['rwkv_a0']), vec(lp['rwkv_k_k']), vec(lp['rwkv_k_a']), vec(lp['rwkv_r_k']),
      vec(lp['rwkv_lnx_w']), vec(lp['rwkv_lnx_b']), wup, aup, lp['rwkv_g_up'])
    return o, s_fin


def _dot3(a, b):
    f32, bf16 = jnp.float32, jnp.bfloat16
    ah = a.astype(bf16)
    al = (a - ah.astype(f32)).astype(bf16)
    bh = b.astype(bf16)
    bl = (b - bh.astype(f32)).astype(bf16)
    d = lambda x, y: jnp.dot(x, y, preferred_element_type=f32)
    return d(ah, bh) + (d(ah, bl) + d(al, bh))


def _rwkv_kernel(r_ref, k_ref, v_ref, xw_ref, xa_ref, xg_ref,
                 sr_ref, sk_ref, sv_ref, sxw_ref, sxa_ref, sxg_ref,
                 mr_ref, mk_ref, mv_ref, mxw_ref, mxa_ref, mxg_ref,
                 s0_ref, w0_ref, a0_ref, kk_ref, ka_ref, rk_ref, lnw_ref, lnb_ref,
                 wup_ref, aup_ref, gup_ref,
                 o_ref, sout_ref,
                 S_ref, cr_ref, ck_ref, cv_ref, cxw_ref, cxa_ref, cxg_ref, *, C, NP, t_valid, n_chunks):
    f32, bf16 = jnp.float32, jnp.bfloat16
    hd, W = RWKV_HEAD_DIM, RWKV_PAIR
    ci = pl.program_id(2)
    lane = lax.broadcasted_iota(jnp.int32, (1, W), 1)
    h0 = lane < hd
    row = lax.broadcasted_iota(jnp.int32, (C, 1), 0)
    valid = (ci * C + row) < t_valid
    carries = ((cr_ref, sr_ref), (ck_ref, sk_ref), (cv_ref, sv_ref), (cxw_ref, sxw_ref), (cxa_ref, sxa_ref),
               (cxg_ref, sxg_ref))

    @pl.when(ci == 0)
    def _():
        S_ref[...] = jnp.zeros(S_ref.shape, f32)
        for p in range(NP):
            S_ref[p, 0:hd, 0:hd] = s0_ref[2 * p]
            S_ref[p, hd:W, hd:W] = s0_ref[2 * p + 1]
        for c_ref, s_ref in carries:
            c_ref[0:1, :] = s_ref[...]

    def shifted(x_ref, c_ref, mu_ref):
        x = x_ref[...]
        prev = jnp.where(row == 0, c_ref[0:1, :], pltpu.roll(x, 1, axis=0))
        c_ref[0:1, :] = x[C - 1:C, :]
        return x + (prev - x) * mu_ref[...]

    def seg_sum(x):
        s_lo = jnp.sum(jnp.where(h0, x, 0.0), axis=-1, keepdims=True)
        s_hi = jnp.sum(jnp.where(h0, 0.0, x), axis=-1, keepdims=True)
        return jnp.where(h0, s_lo, s_hi)

    def stack(x):
        return jnp.concatenate([jnp.where(h0, x, 0.0), jnp.where(h0, 0.0, x)], axis=0)

    mr_all = shifted(r_ref, cr_ref, mr_ref)
    mk_all = shifted(k_ref, ck_ref, mk_ref)
    mv_all = shifted(v_ref, cv_ref, mv_ref)
    mxw = shifted(xw_ref, cxw_ref, mxw_ref)
    mxa = shifted(xa_ref, cxa_ref, mxa_ref)
    mxg = shifted(xg_ref, cxg_ref, mxg_ref)

    z_all = w0_ref[...] + jnp.dot(jnp.tanh(mxw).astype(bf16), wup_ref[...].astype(bf16), preferred_element_type=f32)
    a_all = jax.nn.sigmoid(a0_ref[...] + jnp.dot(mxa.astype(bf16), aup_ref[...].astype(bf16), preferred_element_type=f32))
    g_all = jnp.dot(jax.nn.sigmoid(mxg).astype(bf16), gup_ref[...].astype(bf16), preferred_element_type=f32)

    ti = lax.broadcasted_iota(jnp.int32, (C, C), 0)
    tj = lax.broadcasted_iota(jnp.int32, (C, C), 1)
    tri = (tj <= ti).astype(f32)
    ri = lax.broadcasted_iota(jnp.int32, (2 * C, 2 * C), 0)
    cj = lax.broadcasted_iota(jnp.int32, (2 * C, 2 * C), 1)
    same = (ri >= C) == (cj >= C)
    strict = same & ((cj & (C - 1)) < (ri & (C - 1)))
    incl = same & ((cj & (C - 1)) <= (ri & (C - 1)))
    vi = lax.broadcasted_iota(jnp.int32, (W, W), 0)
    kj = lax.broadcasted_iota(jnp.int32, (W, W), 1)
    blockdiag = (vi < hd) == (kj < hd)
    nt = (((1,), (1,)), ((), ()))
    levels = C.bit_length() - 1

    stage = []
    for p in range(NP):
        ls = slice(p * W, (p + 1) * W)
        mr, mk, mv, a, g = mr_all[:, ls], mk_all[:, ls], mv_all[:, ls], a_all[:, ls], g_all[:, ls]
        w_log = -jax.nn.softplus(-z_all[:, ls]) - 0.5
        lw = -jnp.exp(w_log)
        kk = mk * kk_ref[:, ls]
        kk = kk / jnp.maximum(jnp.sqrt(seg_sum(kk * kk)), 1e-12)
        kp = mk * (1.0 + (a - 1.0) * ka_ref[:, ls])
        lw = jnp.where(valid, lw, 0.0)
        a_t = jnp.where(valid, -kk, 0.0)
        b_t = jnp.where(valid, kk * a, 0.0)
        k_t = jnp.where(valid, kp, 0.0)
        v_t = jnp.where(valid, mv, 0.0)

        cl = jnp.dot(tri, lw, precision=HIGHEST, preferred_element_type=f32)
        e_neg = jnp.exp(-cl)
        At = a_t * jnp.exp(cl - lw)
        Bt = b_t * e_neg
        Kt = k_t * e_neg
        Rt = mr * jnp.exp(cl)
        G = jnp.concatenate([stack(At), stack(Rt)], axis=0).astype(bf16)
        Z = jnp.concatenate([Bt, Bt, Kt, Kt], axis=0).astype(bf16)
        M = lax.dot_general(G, Z, nt, preferred_element_type=f32)
        L = jnp.where(strict, M[0:2 * C, 0:2 * C], 0.0)
        AK = jnp.where(strict, M[0:2 * C, 2 * C:4 * C], 0.0)
        RB = jnp.where(incl, M[2 * C:4 * C, 0:2 * C], 0.0)
        RK = jnp.where(incl, M[2 * C:4 * C, 2 * C:4 * C], 0.0)
        S = S_ref[p]
        GS = lax.dot_general(G, S.astype(bf16), nt, preferred_element_type=f32)
        Vs = stack(v_t)
        U = GS[0:2 * C] + jnp.dot(AK.astype(bf16), Vs.astype(bf16), preferred_element_type=f32)
        stage.append(dict(ls=ls, mr=mr, mv=mv, kp=kp, g=g, cl=cl, Z=Z, GS=GS, Vs=Vs, RB=RB, RK=RK, S=S, U=U, P=L))

    for lvl in range(levels):
        for st in stage:
            st['U'] = st['U'] + _dot3(st['P'], st['U'])
            if lvl + 1 < levels:
                st['P'] = _dot3(st['P'], st['P'])

    for p, st in enumerate(stage):
        ls, mr, mv, kp, g, cl, Z, GS, Vs, RB, RK, S, U = (st[n] for n in ('ls', 'mr', 'mv', 'kp', 'g', 'cl', 'Z', 'GS', 'Vs',
                                                                       'RB', 'RK', 'S', 'U'))
        X = jnp.concatenate([U, Vs], axis=0).astype(bf16)
        Y = GS[2 * C:4 * C] + jnp.dot(jnp.concatenate([RB, RK], axis=1).astype(bf16), X, preferred_element_type=f32)
        y = Y[0:C] + Y[C:2 * C]
        dS = lax.dot_general(X, Z, (((0,), (0,)), ((), ())), preferred_element_type=f32)
        S_new = jnp.where(blockdiag, S + dS, 0.0) * jnp.exp(cl[C - 1:C, :])
        S_ref[p] = S_new

        mean = seg_sum(y) * (1.0 / hd)
        yc = y - mean
        var = seg_sum(yc * yc) * (1.0 / hd)
        yn = yc * lax.rsqrt(var + LNX_EPS) * lnw_ref[:, ls] + lnb_ref[:, ls]
        bonus = seg_sum(mr * kp * rk_ref[:, ls]) * mv
        o_ref[:, ls] = (yn + bonus) * g

    @pl.when(ci == n_chunks - 1)
    def _():
        for p in range(NP):
            s_fin = S_ref[p]
            sout_ref[2 * p] = s_fin[0:hd, 0:hd]
            sout_ref[2 * p + 1] = s_fin[hd:W, hd:W]


def rwkv_mix(proj_p, shift_prev, wkv0, lp, B, T, t_valid, C, NP):
    f32 = jnp.float32
    hd = RWKV_HEAD_DIM
    W = NP * RWKV_PAIR
    n_steps = RWKV_WIDTH // W
    n_chunks = T // C
    assert T % C == 0 and C & (C - 1) == 0 and C >= 8 and RWKV_WIDTH % W == 0

    def pad_vec(x):
        z = lambda n: jnp.zeros(x.shape[:-1] + (n,), x.dtype)
        return jnp.concatenate([x[..., :RWKV_SPLITS[2]], x[..., RWKV_SPLITS[2]:RWKV_SPLITS[3]], z(LANE - DECAY_LORA),
                                x[..., RWKV_SPLITS[3]:RWKV_SPLITS[4]], z(LANE - AAA_LORA), x[..., RWKV_SPLITS[4]:]], -1)

    shift_p = pad_vec(shift_prev.astype(f32))[:, None, :]
    mu_p = pad_vec(lp['rwkv_mu'].astype(f32))[None, :]
    pad_rows_to = lambda w: jnp.pad(w, ((0, LANE - w.shape[0]), (0, 0)))
    wup, aup = pad_rows_to(lp['rwkv_w_up']), pad_rows_to(lp['rwkv_a_up'])
    vec = lambda x: x.reshape(1, RWKV_WIDTH).astype(f32)
    nw = RWKV_WIDTH // W
    col_r, col_k, col_v = 0, nw, 2 * nw
    col_xw, col_xa, col_xg = P_XW // LANE, P_XA // LANE, P_XG // GATE_LORA
    row_spec = lambda w, cfn: pl.BlockSpec((C, w), lambda b, p, c: (b * n_chunks + c, cfn(p)))
    sh_spec = lambda w, cfn: pl.BlockSpec((None, 1, w), lambda b, p, c: (b, 0, cfn(p)))
    mu_spec = lambda w, cfn: pl.BlockSpec((1, w), lambda b, p, c: (0, cfn(p)))
    cols = [(W, lambda p: col_r + p), (W, lambda p: col_k + p), (W, lambda p: col_v + p),
            (LANE, lambda p: col_xw), (LANE, lambda p: col_xa), (GATE_LORA, lambda p: col_xg)]
    pvec = pl.BlockSpec((1, W), lambda b, p, c: (0, p))
    state_spec = pl.BlockSpec((None, 2 * NP, hd, hd), lambda b, p, c: (b, p, 0, 0))
    in_specs = ([row_spec(w, f) for w, f in cols] + [sh_spec(w, f) for w, f in cols] + [mu_spec(w, f) for w, f in cols]
                + [state_spec] + [pvec] * 7
                + [pl.BlockSpec((LANE, W), lambda b, p, c: (0, p)), pl.BlockSpec((LANE, W), lambda b, p, c: (0, p)),
                   pl.BlockSpec((GATE_LORA, W), lambda b, p, c: (0, p))])
    scratch = [pltpu.VMEM((NP, RWKV_PAIR, RWKV_PAIR), f32)] + [pltpu.VMEM((8, w), f32) for w, _ in cols]
    o, s_fin = pl.pallas_call(
        functools.partial(_rwkv_kernel, C=C, NP=NP, t_valid=t_valid, n_chunks=n_chunks),
        grid=(B, n_steps, n_chunks),
        in_specs=in_specs,
        out_specs=[pl.BlockSpec((C, W), lambda b, p, c: (b * n_chunks + c, p)), state_spec],
        out_shape=[jax.ShapeDtypeStruct((B * T, RWKV_WIDTH), f32),
                   jax.ShapeDtypeStruct((B, RWKV_HEADS, hd, hd), f32)],
        scratch_shapes=scratch,
        compiler_params=pltpu.CompilerParams(dimension_semantics=("arbitrary",) * 3, vmem_limit_bytes=VMEM_LIMIT_BYTES),
        name="rwkv_mix",
    )(*([proj_p] * 6), *([shift_p] * 6), *([mu_p] * 6), wkv0.astype(f32),
      vec(lp['rwkv_w0']), vec(lp['rwkv_a0']), vec(lp['rwkv_k_k']), vec(lp['rwkv_k_a']), vec(lp['rwkv_r_k']),
      vec(lp['rwkv_lnx_w']), vec(lp['rwkv_lnx_b']), wup, aup, lp['rwkv_g_up'])
    return o, s_fin


def _expert_changed(be_ref, i):
    return jnp.logical_or(i == 0, be_ref[i] != be_ref[jnp.maximum(i - 1, 0)])


def _ffn_up_kernel(be_ref, nv_ref, x_ref, wg_ref, wu_ref, h_ref, wgb_ref, wub_ref):
    i = pl.program_id(0)
    live = i < nv_ref[0]

    @pl.when(live & _expert_changed(be_ref, i))
    def _():
        wgb_ref[...] = wg_ref[...].astype(jnp.bfloat16)
        wub_ref[...] = wu_ref[...].astype(jnp.bfloat16)

    @pl.when(live)
    def _():
        x = x_ref[...]
        hg = jnp.dot(x, wgb_ref[...], preferred_element_type=jnp.float32)
        hu = jnp.dot(x, wub_ref[...], preferred_element_type=jnp.float32)
        h_ref[...] = (jax.nn.silu(hg) * hu).astype(h_ref.dtype)

    @pl.when(i >= nv_ref[0])
    def _():
        h_ref[...] = jnp.zeros(h_ref.shape, h_ref.dtype)


def _ffn_down_kernel(be_ref, nv_ref, h_ref, wd_ref, rw_ref, o_ref, wdb_ref):
    i = pl.program_id(0)
    live = i < nv_ref[0]

    @pl.when(live & _expert_changed(be_ref, i))
    def _():
        wdb_ref[...] = wd_ref[...].astype(jnp.bfloat16)

    @pl.when(live)
    def _():
        o_ref[...] = jnp.dot(h_ref[...].astype(jnp.bfloat16), wdb_ref[...],
                             preferred_element_type=jnp.float32) * rw_ref[...]

    @pl.when(i >= nv_ref[0])
    def _():
        o_ref[...] = jnp.zeros(o_ref.shape, o_ref.dtype)


def expert_ffn(xs, w_gate, w_up, w_down, row_w, blk_exp, n_valid, blk):
    rows, d = xs.shape
    ff = w_gate.shape[-1]
    n_blocks = rows // blk
    params = pltpu.CompilerParams(dimension_semantics=("arbitrary",), vmem_limit_bytes=VMEM_LIMIT_BYTES)
    h = pl.pallas_call(
        _ffn_up_kernel,
        grid_spec=pltpu.PrefetchScalarGridSpec(
            num_scalar_prefetch=2, grid=(n_blocks,),
            in_specs=[pl.BlockSpec((blk, d), lambda i, be, nv: (i, 0)),
                      pl.BlockSpec((None, d, ff), lambda i, be, nv: (be[i], 0, 0)),
                      pl.BlockSpec((None, d, ff), lambda i, be, nv: (be[i], 0, 0))],
            out_specs=pl.BlockSpec((blk, ff), lambda i, be, nv: (i, 0)),
            scratch_shapes=[pltpu.VMEM((d, ff), jnp.bfloat16), pltpu.VMEM((d, ff), jnp.bfloat16)]),
        out_shape=jax.ShapeDtypeStruct((rows, ff), jnp.bfloat16),
        compiler_params=params, name="ffn_up",
    )(blk_exp, n_valid, xs, w_gate, w_up)
    return pl.pallas_call(
        _ffn_down_kernel,
        grid_spec=pltpu.PrefetchScalarGridSpec(
            num_scalar_prefetch=2, grid=(n_blocks,),
            in_specs=[pl.BlockSpec((blk, ff), lambda i, be, nv: (i, 0)),
                      pl.BlockSpec((None, ff, d), lambda i, be, nv: (be[i], 0, 0)),
                      pl.BlockSpec((blk, 1), lambda i, be, nv: (i, 0))],
            out_specs=pl.BlockSpec((blk, d), lambda i, be, nv: (i, 0)),
            scratch_shapes=[pltpu.VMEM((ff, d), jnp.bfloat16)]),
        out_shape=jax.ShapeDtypeStruct((rows, d), jnp.float32),
        compiler_params=params, name="ffn_down",
    )(blk_exp, n_valid, h, w_down, row_w)


def _router_kernel(x_ref, wt_ref, b_ref, e_ref, w_ref):
    f32 = jnp.float32
    tm = x_ref.shape[0]
    gs = N_EXPERTS // N_GROUPS
    logits = lax.dot_general(wt_ref[...].astype(jnp.bfloat16), x_ref[...].astype(jnp.bfloat16),
                             (((1,), (1,)), ((), ())), preferred_element_type=f32)
    scores = jax.nn.sigmoid(logits)
    biased = scores + b_ref[...]
    g3 = biased.reshape(N_GROUPS, gs, tm)
    it = lax.broadcasted_iota(jnp.int32, (N_GROUPS, gs, tm), 1)
    m1 = jnp.max(g3, axis=1, keepdims=True)
    i1 = jnp.min(jnp.where(g3 == m1, it, gs), axis=1, keepdims=True)
    m2 = jnp.max(jnp.where(it == i1, -jnp.inf, g3), axis=1, keepdims=True)
    gscore = (m1 + m2).reshape(N_GROUPS, tm)
    gi = lax.broadcasted_iota(jnp.int32, (N_GROUPS, tm), 0)
    grank = jnp.zeros((N_GROUPS, tm), jnp.int32)
    for j in range(N_GROUPS):
        sj = gscore[j:j + 1, :]
        grank = grank + ((sj > gscore) | ((sj == gscore) & (j < gi))).astype(jnp.int32)
    gmask = (grank < TOPK_GROUPS).reshape(N_GROUPS, 1, tm)
    masked = jnp.where(gmask, g3, -jnp.inf).reshape(N_EXPERTS, tm)
    ei = lax.broadcasted_iota(jnp.int32, (N_EXPERTS, tm), 0)
    idxs, wts = [], []
    for _ in range(TOP_K):
        m = jnp.max(masked, axis=0, keepdims=True)
        idx = jnp.min(jnp.where(masked == m, ei, N_EXPERTS), axis=0, keepdims=True)
        hit = ei == idx
        idxs.append(idx)
        wts.append(jnp.sum(jnp.where(hit, scores, 0.0), axis=0, keepdims=True))
        masked = jnp.where(hit, -jnp.inf, masked)
    w = jnp.concatenate(wts, axis=0)
    e_ref[...] = jnp.concatenate(idxs, axis=0)
    w_ref[...] = w / jnp.sum(w, axis=0, keepdims=True) * ROUTED_SCALE


def moe_route(x, router_w, router_bias):
    n, d = x.shape
    tm = min(512, n)
    assert n % tm == 0
    e, w = pl.pallas_call(
        _router_kernel,
        grid=(n // tm,),
        in_specs=[pl.BlockSpec((tm, d), lambda i: (i, 0)),
                  pl.BlockSpec((N_EXPERTS, d), lambda i: (0, 0)),
                  pl.BlockSpec((N_EXPERTS, 1), lambda i: (0, 0))],
        out_specs=[pl.BlockSpec((TOP_K, tm), lambda i: (0, i)), pl.BlockSpec((TOP_K, tm), lambda i: (0, i))],
        out_shape=[jax.ShapeDtypeStruct((TOP_K, n), jnp.int32), jax.ShapeDtypeStruct((TOP_K, n), jnp.float32)],
        compiler_params=pltpu.CompilerParams(dimension_semantics=("arbitrary",), vmem_limit_bytes=VMEM_LIMIT_BYTES),
        name="moe_route",
    )(x, router_w.T, router_bias.astype(jnp.float32).reshape(N_EXPERTS, 1))
    return e.T, w.T


def _moe_rank_kernel(e_ref, rank_ref, cnt_ref, carry_ref):
    f32, bf16 = jnp.float32, jnp.bfloat16
    ta = e_ref.shape[1]
    i = pl.program_id(0)

    @pl.when(i == 0)
    def _():
        carry_ref[...] = jnp.zeros(carry_ref.shape, f32)

    ei = lax.broadcasted_iota(jnp.int32, (N_EXPERTS, ta), 0)
    hot = ei == e_ref[...]
    r = lax.broadcasted_iota(jnp.int32, (ta, ta), 0)
    c = lax.broadcasted_iota(jnp.int32, (ta, ta), 1)
    before = jnp.dot(hot.astype(bf16), (r < c).astype(bf16), preferred_element_type=f32)
    carry = carry_ref[...]
    rank_ref[...] = jnp.sum(jnp.where(hot, before + carry, 0.0), axis=0, keepdims=True).astype(jnp.int32)
    carry = carry + jnp.sum(hot.astype(f32), axis=1, keepdims=True)
    carry_ref[...] = carry
    cnt_ref[...] = carry.astype(jnp.int32)


def moe_rank(flat_e):
    nk = flat_e.shape[0]
    ta = min(512, nk)
    assert nk % ta == 0 and nk < 2 ** 24
    rank, cnt = pl.pallas_call(
        _moe_rank_kernel,
        grid=(nk // ta,),
        in_specs=[pl.BlockSpec((1, ta), lambda i: (0, i))],
        out_specs=[pl.BlockSpec((1, ta), lambda i: (0, i)), pl.BlockSpec((N_EXPERTS, 1), lambda i: (0, 0))],
        out_shape=[jax.ShapeDtypeStruct((1, nk), jnp.int32), jax.ShapeDtypeStruct((N_EXPERTS, 1), jnp.int32)],
        scratch_shapes=[pltpu.VMEM((N_EXPERTS, 1), jnp.float32)],
        compiler_params=pltpu.CompilerParams(dimension_semantics=("arbitrary",), vmem_limit_bytes=VMEM_LIMIT_BYTES),
        name="moe_rank",
    )(flat_e.reshape(1, nk))
    return rank.reshape(nk), cnt.reshape(N_EXPERTS)


def moe_dispatch(h, lp, blk):
    B, T, D = h.shape
    x = h.reshape(B * T, D)
    n = x.shape[0]
    f32 = jnp.float32
    eidx, w = moe_route(x, lp['router_w'], lp['router_bias'])

    nk = n * TOP_K
    flat_e = eidx.reshape(nk)
    rank, counts = moe_rank(flat_e)
    padded = (counts + blk - 1) // blk * blk
    pad_end = jnp.cumsum(padded)
    dest = jnp.take(pad_end - padded, flat_e) + rank
    n_blocks = -(-(nk + N_EXPERTS * (blk - 1)) // blk)
    rows = n_blocks * blk
    row_aid = jnp.full((rows,), -1, jnp.int32).at[dest].set(jnp.arange(nk, dtype=jnp.int32))
    live = row_aid >= 0
    row_aid = jnp.maximum(row_aid, 0)
    row_tok = row_aid // TOP_K
    row_w = jnp.where(live, jnp.take(w.reshape(nk), row_aid), 0.0)
    starts = jnp.arange(n_blocks, dtype=pad_end.dtype) * blk
    blk_exp = jnp.minimum(jnp.sum(pad_end[None, :] <= starts[:, None], axis=1), N_EXPERTS - 1).astype(jnp.int32)
    n_valid = (pad_end[-1] // blk).astype(jnp.int32).reshape(1)
    pos = dest.reshape(n, TOP_K)

    xb = x.astype(jnp.bfloat16)
    return dict(xb=xb, xs=xb[row_tok], row_w=row_w[:, None], blk_exp=blk_exp, n_valid=n_valid, pos=pos, blk=blk,
                shape=(B, T, D))


def moe_experts(dp, lp):
    n = dp['xb'].shape[0]
    ys = expert_ffn(dp['xs'], lp['exp_w_gate'], lp['exp_w_up'], lp['exp_w_down'], dp['row_w'], dp['blk_exp'],
                    dp['n_valid'], dp['blk'])
    sblk = min(256, n)
    one = jnp.ones((n, 1), jnp.float32)
    shared = expert_ffn(dp['xb'], lp['sh_w_gate'][None], lp['sh_w_up'][None], lp['sh_w_down'][None], one,
                        jnp.zeros((n // sblk,), jnp.int32), jnp.full((1,), n // sblk, jnp.int32), sblk)
    return ys[dp['pos']], shared


def moe_combine(gathered, shared, shape):
    return (gathered.sum(axis=1) + shared).reshape(shape)


def moe_ffn_pallas(h, lp, blk):
    dp = moe_dispatch(h, lp, blk)
    gathered, shared = moe_experts(dp, lp)
    return moe_combine(gathered, shared, dp['shape'])


def pad_rows(a, length):
    return jnp.pad(a, [(0, 0), (0, length - a.shape[1])] + [(0, 0)] * (a.ndim - 2))


def rwkv7_mix(pr, shift_prev, wkv0, lp):
    B, T, _ = pr.shape
    f32 = jnp.float32
    prev = jnp.concatenate([shift_prev[:, None, :].astype(pr.dtype), pr[:, :-1]], axis=1)
    m = pr + (prev - pr) * lp['rwkv_mu']
    r, k, v, xw, xa, xg = jnp.split(m, RWKV_SPLITS, axis=-1)
    w_log = -jax.nn.softplus(-(lp['rwkv_w0'] + jnp.tanh(xw) @ lp['rwkv_w_up']).astype(f32)) - 0.5
    decay = jnp.exp(-jnp.exp(w_log))
    a = jax.nn.sigmoid((lp['rwkv_a0'] + xa @ lp['rwkv_a_up']).astype(f32))
    g = jax.nn.sigmoid(xg) @ lp['rwkv_g_up']

    def heads(t):
        return t.astype(f32).reshape(B, T, RWKV_HEADS, RWKV_HEAD_DIM)

    r, k, v, decay, a = heads(r), heads(k), heads(v), heads(decay), heads(a)
    kk = k * lp['rwkv_k_k'].astype(f32).reshape(RWKV_HEADS, RWKV_HEAD_DIM)
    kk = kk / jnp.maximum(jnp.linalg.norm(kk, axis=-1, keepdims=True), 1e-12)
    k = k * (1.0 + (a - 1.0) * lp['rwkv_k_a'].astype(f32).reshape(RWKV_HEADS, RWKV_HEAD_DIM))

    def step(S, inp):
        r_t, w_t, k_t, v_t, a_t, b_t = inp
        sa = jnp.einsum('bhvk,bhk->bhv', S, a_t)
        S = S * w_t[:, :, None, :] + sa[..., None] * b_t[:, :, None, :] + v_t[..., None] * k_t[:, :, None, :]
        return S, jnp.einsum('bhvk,bhk->bhv', S, r_t)

    xs = tuple(jnp.moveaxis(t, 1, 0) for t in (r, decay, k, v, -kk, kk * a))
    s_final, ys = lax.scan(step, wkv0.astype(f32), xs)
    y = jnp.moveaxis(ys, 0, 1)
    mu = jnp.mean(y, axis=-1, keepdims=True)
    var = jnp.mean(jnp.square(y - mu), axis=-1, keepdims=True)
    y = ((y - mu) * lax.rsqrt(var + LNX_EPS)).reshape(B, T, RWKV_WIDTH)
    y = y * lp['rwkv_lnx_w'].astype(f32) + lp['rwkv_lnx_b'].astype(f32)
    bonus = (jnp.sum(r * k * lp['rwkv_r_k'].astype(f32), axis=-1, keepdims=True) * v).reshape(B, T, RWKV_WIDTH)
    out = ((y + bonus) * g.astype(f32)).astype(pr.dtype)
    return out, pr[:, -1], s_final.astype(wkv0.dtype)


def compress_blocks(rows, pe, w):
    B, L = rows.shape[:2]
    blk = rows.reshape(B, L // CMP_BLOCK, CMP_BLOCK, 2, NSA_KV_HEADS, NSA_HEAD_DIM)
    blk = blk + jnp.swapaxes(pe, 0, 1)[None, None, :, :, None, :]
    return jnp.einsum('bncjkd,jcde->bnjke', blk, w)


def nsa_attend(q, q_pos, gates, kc, vc, c_end, n_sel_blocks, gather_sel, kw, vw, w_pos, rel_bias):
    B, Tq = q.shape[:2]
    f32 = jnp.float32
    scale = NSA_HEAD_DIM ** -0.5
    qg = q.reshape(B, Tq, NSA_KV_HEADS, NSA_GROUP, NSA_HEAD_DIM)
    bias_tab = rel_bias.astype(f32).reshape(NUM_BUCKETS, NSA_KV_HEADS, NSA_GROUP)

    dc = q_pos[:, None] - c_end[None, :]
    bias_c = bias_tab[rel_bucket(dc)].transpose(0, 2, 3, 1)
    lc = jnp.einsum('btkgd,bnkd->btkgn', qg, kc).astype(f32) * scale + bias_c
    pc = masked_softmax(lc, (dc >= 0)[:, None, None, :])
    oc = jnp.einsum('btkgn,bnkd->btkgd', pc.astype(vc.dtype), vc)

    imp = pc.sum(3).reshape(B, Tq, NSA_KV_HEADS, n_sel_blocks, SEL_BLOCK // CMP_BLOCK).sum(-1)
    blk = jnp.arange(n_sel_blocks)
    cur = (q_pos // SEL_BLOCK)[:, None]
    force = (blk == 0) | (blk == cur) | (blk == cur - 1)
    score = jnp.where(force[:, None, :], FORCE_SCORE, imp)
    score = jnp.where((blk <= cur)[:, None, :], score, -jnp.inf)
    _, idx = lax.top_k(jnp.moveaxis(score, 2, 1), min(N_SEL, n_sel_blocks))

    sel = gather_sel(idx)
    ks, vs = sel[..., 0, :], sel[..., 1, :]
    spos = idx[..., None] * SEL_BLOCK + jnp.arange(SEL_BLOCK)
    ds = q_pos[None, None, :, None, None] - spos
    hi = jnp.arange(NSA_KV_HEADS)[None, :, None, None, None]
    bias_s = jnp.moveaxis(bias_tab[rel_bucket(ds), hi], -1, 3)
    ls = jnp.einsum('btkgd,bktjsd->bktgjs', qg, ks).astype(f32) * scale + bias_s
    ms = (ds >= 0)[:, :, :, None]
    ps = masked_softmax(ls.reshape(ls.shape[:4] + (-1,)), ms.reshape(ms.shape[:4] + (-1,))).reshape(ls.shape)
    osel = jnp.einsum('bktgjs,bktjsd->btkgd', ps.astype(vs.dtype), vs)

    dw = q_pos[:, None] - w_pos[None, :]
    mw = (dw >= 0) & (dw <= WINDOW) & (w_pos >= 0)[None, :]
    bias_w = bias_tab[rel_bucket(dw)].transpose(0, 2, 3, 1)
    lw = jnp.einsum('btkgd,blkd->btkgl', qg, kw).astype(f32) * scale + bias_w
    pw = masked_softmax(lw, mw[:, None, None, :])
    ow = jnp.einsum('btkgl,blkd->btkgd', pw.astype(vw.dtype), vw)

    gt = gates.reshape(B, Tq, NSA_KV_HEADS, NSA_GROUP, 3)
    o = gt[..., 0:1] * oc + gt[..., 1:2] * osel + gt[..., 2:3] * ow
    return o.reshape(B, Tq, NSA_WIDTH)


def split_proj(h, lp):
    B, T, _ = h.shape
    proj = mm3(h, lp['w_in'])
    pr = proj[..., :RWKV_PROJ]
    pn = proj[..., RWKV_PROJ:]
    q = pn[..., :NSA_WIDTH].reshape(B, T, NSA_HEADS, NSA_HEAD_DIM)
    kv = pn[..., NSA_WIDTH:NSA_WIDTH + 3 * NSA_KV_WIDTH].reshape(B, T, 3, 2, NSA_KV_HEADS, NSA_HEAD_DIM)
    gates = jax.nn.sigmoid(pn[..., NSA_WIDTH + 3 * NSA_KV_WIDTH:].astype(jnp.float32))
    gates = gates.reshape(B, T, NSA_HEADS, 3).astype(h.dtype)
    return pr, q, kv[:, :, 0], kv[:, :, 1], kv[:, :, 2], gates


def merge_groups(o_rwkv, o_nsa, lp):
    o = jnp.concatenate([o_rwkv, rms_norm(o_nsa, lp['nsa_out_g'])], axis=-1).astype(jnp.bfloat16)
    return mm3(o, lp['w_out'])


def mixer_prompt(h, lp, rel_bias):
    B, T, d = h.shape
    proj_p = pallas_matmul(h.reshape(B * T, d), lp['w_in_p'])
    o_r, wkv = rwkv_mix(proj_p, jnp.zeros((B, RWKV_PROJ), h.dtype),
                        jnp.zeros((B, RWKV_HEADS, RWKV_HEAD_DIM, RWKV_HEAD_DIM), jnp.float32), lp, B, T, T, 64, 8)
    o_r = o_r.reshape(B, T, RWKV_WIDTH)
    shift_last = unpad_rwkv(proj_p.reshape(B, T, P_TOTAL)[:, -1])
    kvc = compress_prompt(proj_p, lp['cmp_pe'], lp['cmp_w'], B, T)
    o_n = nsa_prompt(proj_p, kvc, rel_bias, B, T).reshape(B, T, NSA_WIDTH)
    kv = proj_p[:, P_KVC:P_GATES].reshape(B, T, 3, 2, NSA_KV_HEADS, NSA_HEAD_DIM)
    kv_c, kv_s, kv_w = kv[:, :, 0], kv[:, :, 1], kv[:, :, 2]
    win_keep = min(WINDOW, T)
    return merge_groups(o_r, o_n, lp), (kv_c, kv_s, kv_w[:, T - win_keep:], wkv, shift_last)


def mixer_prompt_gather(h, lp, rel_bias):
    B, T, _ = h.shape
    pr, q, kv_c, kv_s, kv_w, gates = split_proj(h, lp)
    o_r, shift_last, wkv = rwkv7_mix(pr, jnp.zeros((B, RWKV_PROJ), h.dtype),
                                     jnp.zeros((B, RWKV_HEADS, RWKV_HEAD_DIM, RWKV_HEAD_DIM), jnp.float32), lp)
    lp_len = -(-T // SEL_BLOCK) * SEL_BLOCK
    comp = compress_blocks(pad_rows(kv_c, lp_len), lp['cmp_pe'], lp['cmp_w'])
    kc, vc = comp[:, :, 0], comp[:, :, 1]
    c_end = jnp.arange(lp_len // CMP_BLOCK) * CMP_BLOCK + (CMP_BLOCK - 1)
    n_sb = lp_len // SEL_BLOCK
    sel_blocks = pad_rows(kv_s, lp_len).reshape(B, n_sb, SEL_BLOCK, 2, NSA_KV_HEADS, NSA_HEAD_DIM)
    bi = jnp.arange(B)[:, None, None, None]
    hi = jnp.arange(NSA_KV_HEADS)[None, :, None, None]

    def gather_sel(idx):
        return sel_blocks[bi, idx, :, :, hi, :]

    kw_pad = jnp.pad(kv_w, ((0, 0), (WINDOW, 0), (0, 0), (0, 0), (0, 0)))

    def chunk(i):
        t0 = i * Q_CHUNK
        qc = lax.dynamic_slice_in_dim(q, t0, Q_CHUNK, axis=1)
        gc = lax.dynamic_slice_in_dim(gates, t0, Q_CHUNK, axis=1)
        wc = lax.dynamic_slice_in_dim(kw_pad, t0, WINDOW + Q_CHUNK, axis=1)
        q_pos = t0 + jnp.arange(Q_CHUNK)
        w_pos = t0 - WINDOW + jnp.arange(WINDOW + Q_CHUNK)
        return nsa_attend(qc, q_pos, gc, kc, vc, c_end, n_sb, gather_sel,
                          wc[:, :, 0], wc[:, :, 1], w_pos, rel_bias)

    o_n = lax.map(chunk, jnp.arange(T // Q_CHUNK))
    o_n = jnp.moveaxis(o_n, 0, 1).reshape(B, T, NSA_WIDTH)
    win_keep = min(WINDOW, T)
    return merge_groups(o_r, o_n, lp), (kv_c, kv_s, kv_w[:, T - win_keep:], wkv, shift_last)


def mixer_sample(h, lp, rel_bias, pool_cmp, pool_sel, win_buf, wkv0, shift0, page_table):
    B, T, _ = h.shape
    n_pages = page_table.shape[1]
    page = pool_cmp.shape[1]
    past = n_pages * page
    d = h.shape[-1]
    proj_p = pallas_matmul(h.reshape(B * T, d), lp['w_in_p']).reshape(B, T, P_TOTAL)
    import numpy as np
    kv = proj_p[..., P_KVC:P_GATES].reshape(B, T, 3, 2, NSA_KV_HEADS, NSA_HEAD_DIM)
    kv_c, kv_s, kv_w = kv[:, :, 0], kv[:, :, 1], kv[:, :, 2]
    proj_pad = jnp.pad(proj_p, ((0, 0), (0, T_PAD - T), (0, 0))).reshape(B * T_PAD, P_TOTAL)
    o_r, wkv = rwkv_mix(proj_pad, shift0, wkv0, lp, B, T_PAD, T, T_PAD, 4)
    o_r = o_r.reshape(B, T_PAD, RWKV_WIDTH)[:, :T]
    shift_last = unpad_rwkv(proj_p[:, -1])
    lp_len = -(-(past + T) // SEL_BLOCK) * SEL_BLOCK

    jk, hd = 2 * NSA_KV_HEADS, NSA_HEAD_DIM
    kvc_past = compress_paged(pool_cmp, page_table, lp['cmp_pe'], lp['cmp_w'])
    new_len = lp_len - past
    new_rows = pad_rows(proj_p[..., P_KVC:P_KVS], new_len).reshape(B * new_len, NSA_KV_WIDTH)
    kvc_new = compress_prompt(new_rows, lp['cmp_pe'], lp['cmp_w'], B, new_len, col0=0)
    nb_new = new_len // CMP_BLOCK
    assert nb_new <= 2
    kvc_new = kvc_new.reshape(B, jk, nb_new, hd).transpose(0, 2, 1, 3)
    kvc_all = jnp.concatenate([kvc_past, kvc_new], axis=1)
    nc = lp_len // CMP_BLOCK
    n_half = nc // 2
    half = -(-n_half // LANE) * LANE
    idx = np.zeros((2 * half,), np.int32)
    idx[:n_half] = np.arange(0, nc, 2)
    idx[half:half + n_half] = np.arange(1, nc, 2)
    kvc_t = kvc_all[:, idx].transpose(0, 2, 1, 3)
    o_n = nsa_sample(proj_pad, kvc_t, pool_sel, win_buf, page_table, rel_bias, B, T)
    o_n = o_n.reshape(B, T_PAD, NSA_WIDTH)[:, :T]
    new_win = jnp.concatenate([win_buf, kv_w], axis=1)[:, T:]
    return merge_groups(o_r, o_n, lp), (kv_c, kv_s, new_win, wkv, shift_last)


def moe_ffn(h, lp):
    B, T, D = h.shape
    x = h.reshape(B * T, D)
    n = x.shape[0]
    f32 = jnp.float32
    scores = jax.nn.sigmoid((x @ lp['router_w']).astype(f32))
    biased = scores + lp['router_bias'].astype(f32)
    grp = biased.reshape(n, N_GROUPS, N_EXPERTS // N_GROUPS)
    grp_score = lax.top_k(grp, 2)[0].sum(-1)
    _, gidx = lax.top_k(grp_score, TOPK_GROUPS)
    gmask = jax.nn.one_hot(gidx, N_GROUPS, dtype=f32).sum(1)
    emask = jnp.repeat(gmask, N_EXPERTS // N_GROUPS, axis=1) > 0
    _, eidx = lax.top_k(jnp.where(emask, biased, -jnp.inf), TOP_K)
    w = jnp.take_along_axis(scores, eidx, axis=1)
    w = w / jnp.sum(w, axis=-1, keepdims=True) * ROUTED_SCALE

    nk = n * TOP_K
    flat_e = eidx.reshape(nk)
    order = jnp.argsort(flat_e)
    se = flat_e[order]
    counts = jnp.bincount(flat_e, length=N_EXPERTS)
    padded = (counts + MOE_BLOCK - 1) // MOE_BLOCK * MOE_BLOCK
    pad_end = jnp.cumsum(padded)
    dest = (pad_end - padded)[se] + jnp.arange(nk) - (jnp.cumsum(counts) - counts)[se]
    n_blocks = -(-(nk + N_EXPERTS * (MOE_BLOCK - 1)) // MOE_BLOCK)
    rows = n_blocks * MOE_BLOCK
    row_tok = jnp.zeros((rows,), jnp.int32).at[dest].set((order // TOP_K).astype(jnp.int32))
    row_w = jnp.zeros((rows,), f32).at[dest].set(w.reshape(nk)[order])
    blk_exp = jnp.minimum(jnp.searchsorted(pad_end, jnp.arange(n_blocks) * MOE_BLOCK, side='right'),
                          N_EXPERTS - 1)

    def expert_block(args):
        tok, wt, e = args
        xb = x[tok]
        hb = jax.nn.silu(xb @ lp['exp_w_gate'][e]) * (xb @ lp['exp_w_up'][e])
        return (hb @ lp['exp_w_down'][e]).astype(f32) * wt[:, None]

    out = lax.map(expert_block, (row_tok.reshape(n_blocks, MOE_BLOCK), row_w.reshape(n_blocks, MOE_BLOCK), blk_exp))
    routed = jnp.zeros((n, D), f32).at[row_tok].add(out.reshape(rows, D))
    shared = (jax.nn.silu(x @ lp['sh_w_gate']) * (x @ lp['sh_w_up'])) @ lp['sh_w_down']
    return (routed + shared.astype(f32)).astype(h.dtype).reshape(B, T, D)


def trunk_layer(x, mod, lp, mixer):
    B, _, D = x.shape
    mod = mod.reshape(B, 6, 1, D)
    sh1, sc1, gt1, sh2, sc2, gt2 = (mod[:, i] for i in range(6))
    g = lp['norm_g']
    h = (rms_norm(x, g[0]) * (1 + sc1) + sh1).astype(jnp.bfloat16)
    o, st = mixer(h)
    x = x + gt1 * rms_norm(o, g[1])
    h = (rms_norm(x, g[2]) * (1 + sc2) + sh2).astype(jnp.bfloat16)
    dp = moe_dispatch(h, lp, 256 if x.shape[0] * x.shape[1] >= 4096 else 32)
    return x, gt2, dp, st


def trunk_finish(x, gt2, gathered, shared, dp, lp):
    return x + gt2 * rms_norm(moe_combine(gathered, shared, dp['shape']), lp['norm_g'][3])


def kernel(x_prompt, x_sample, c_prompt, c_sample, cache_cmp, cache_sel, state_win, state_wkv,
           state_shift, page_table, rel_bias, w_ada, b_ada, norm_g, w_in, w_out, rwkv_mu, rwkv_w0,
           rwkv_w_up, rwkv_a0, rwkv_a_up, rwkv_g_up, rwkv_k_k, rwkv_k_a, rwkv_r_k, rwkv_lnx_w,
           rwkv_lnx_b, cmp_pe, cmp_w, nsa_out_g, router_w, router_bias, exp_w_gate, exp_w_up,
           exp_w_down, sh_w_gate, sh_w_up, sh_w_down):
    l = 0
    lp = dict(w_ada=w_ada[l], b_ada=b_ada[l], norm_g=norm_g[l], w_in=w_in[l], w_out=w_out[l],
              rwkv_mu=rwkv_mu[l], rwkv_w0=rwkv_w0[l], rwkv_w_up=rwkv_w_up[l], rwkv_a0=rwkv_a0[l],
              rwkv_a_up=rwkv_a_up[l], rwkv_g_up=rwkv_g_up[l], rwkv_k_k=rwkv_k_k[l],
              rwkv_k_a=rwkv_k_a[l], rwkv_r_k=rwkv_r_k[l], rwkv_lnx_w=rwkv_lnx_w[l],
              rwkv_lnx_b=rwkv_lnx_b[l], cmp_pe=cmp_pe[l], cmp_w=cmp_w[l], nsa_out_g=nsa_out_g[l],
              router_w=router_w[l], router_bias=router_bias[l], exp_w_gate=exp_w_gate[l],
              exp_w_up=exp_w_up[l], exp_w_down=exp_w_down[l], sh_w_gate=sh_w_gate[l],
              sh_w_up=sh_w_up[l], sh_w_down=sh_w_down[l])
    lp['w_in_p'] = pad_w_in(w_in[l])
    nb_p = x_prompt.shape[0]
    mod = pallas_matmul(jax.nn.silu(jnp.concatenate([c_prompt, c_sample], axis=0)), lp['w_ada']) + lp['b_ada']
    xp, gtp, dpp, st_p = trunk_layer(x_prompt, mod[:nb_p], lp, lambda h: mixer_prompt(h, lp, rel_bias))
    xs_, gts, dps, st_s = trunk_layer(x_sample, mod[nb_p:], lp, lambda h: mixer_sample(
        h, lp, rel_bias, cache_cmp[l], cache_sel[l], state_win[l], state_wkv[l], state_shift[l], page_table))
    dpp['xs'], dps['xs'] = lax.optimization_barrier((dpp['xs'], dps['xs']))
    gath_p, shared_p = moe_experts(dpp, lp)
    gath_s, shared_s = moe_experts(dps, lp)
    ys = trunk_finish(xs_, gts, gath_s, shared_s, dps, lp)
    gath_p, ys = lax.optimization_barrier((gath_p, ys))
    yp = trunk_finish(xp, gtp, gath_p, shared_p, dpp, lp)
    p_cmp, p_sel, p_win, p_wkv, p_shift = [a[None] for a in st_p]
    s_cmp, s_sel, s_win, s_wkv, s_shift = [a[None] for a in st_s]
    return (yp, ys, p_cmp, p_sel, p_win, p_wkv, p_shift, s_cmp, s_sel, s_win, s_wkv, s_shift)
```

```python
import functools
import math

import jax
import jax.numpy as jnp
from jax import lax
from jax.experimental import pallas as pl
from jax.experimental.pallas import tpu as pltpu

D_MODEL = 4096
RWKV_WIDTH = D_MODEL // 2
NSA_WIDTH = D_MODEL - RWKV_WIDTH
RWKV_HEAD_DIM = 64
RWKV_HEADS = RWKV_WIDTH // RWKV_HEAD_DIM
DECAY_LORA = 96
AAA_LORA = 96
GATE_LORA = 256
RWKV_PROJ = 3 * RWKV_WIDTH + DECAY_LORA + AAA_LORA + GATE_LORA
RWKV_SPLITS = (RWKV_WIDTH, 2 * RWKV_WIDTH, 3 * RWKV_WIDTH, 3 * RWKV_WIDTH + DECAY_LORA,
               3 * RWKV_WIDTH + DECAY_LORA + AAA_LORA)
LNX_EPS = 64e-5
NSA_HEAD_DIM = 128
NSA_HEADS = NSA_WIDTH // NSA_HEAD_DIM
NSA_KV_HEADS = 4
NSA_GROUP = NSA_HEADS // NSA_KV_HEADS
CMP_BLOCK = 32
SEL_BLOCK = 64
N_SEL = 16
WINDOW = 512
Q_CHUNK = 32
FORCE_SCORE = 1e4
NSA_KV_WIDTH = 2 * NSA_KV_HEADS * NSA_HEAD_DIM
NUM_BUCKETS = 32
MAX_DISTANCE = 128
N_EXPERTS = 64
N_GROUPS = 8
TOPK_GROUPS = 4
TOP_K = 8
ROUTED_SCALE = 2.5
MOE_BLOCK = 128
RMS_EPS = 1e-6

VMEM_LIMIT_BYTES = 56 * 1024 * 1024


def _matmul_kernel(x_ref, w_ref, o_ref):
    o_ref[...] = jnp.dot(x_ref[...].astype(jnp.bfloat16), w_ref[...].astype(jnp.bfloat16),
                         preferred_element_type=jnp.float32)


def pallas_matmul(x, w, tn=512):
    m, k = x.shape
    n = w.shape[1]
    row_mult = 16 if x.dtype == jnp.bfloat16 else 8
    mp = -(-m // row_mult) * row_mult
    tm = min(1024 if x.dtype == jnp.bfloat16 else 512, mp)
    mp = -(-mp // tm) * tm
    np_ = -(-n // tn) * tn
    if mp != m:
        x = jnp.pad(x, ((0, mp - m), (0, 0)))
    if np_ != n:
        w = jnp.pad(w, ((0, 0), (0, np_ - n)))
    out = pl.pallas_call(
        _matmul_kernel,
        grid=(mp // tm, np_ // tn),
        in_specs=[pl.BlockSpec((tm, k), lambda i, j: (i, 0)),
                  pl.BlockSpec((k, tn), lambda i, j: (0, j))],
        out_specs=pl.BlockSpec((tm, tn), lambda i, j: (i, j)),
        out_shape=jax.ShapeDtypeStruct((mp, np_), jnp.float32),
        compiler_params=pltpu.CompilerParams(
            dimension_semantics=("arbitrary", "arbitrary"), vmem_limit_bytes=VMEM_LIMIT_BYTES),
        name="matmul",
    )(x, w)
    return out[:m, :n]


def mm3(h, w):
    b, t, d = h.shape
    return pallas_matmul(h.reshape(b * t, d), w).reshape(b, t, w.shape[1])


def rms_norm(x, g):
    xf = x.astype(jnp.float32)
    y = xf * lax.rsqrt(jnp.mean(xf * xf, axis=-1, keepdims=True) + RMS_EPS)
    return (y * g.astype(jnp.float32)).astype(x.dtype)


def rel_bucket(dist):
    max_exact = NUM_BUCKETS // 2
    n = jnp.maximum(dist, 0)
    nf = jnp.maximum(n, max_exact).astype(jnp.float32)
    large = max_exact + (jnp.log(nf / max_exact) / math.log(MAX_DISTANCE / max_exact)
                         * (NUM_BUCKETS - max_exact)).astype(jnp.int32)
    return jnp.where(n < max_exact, n, jnp.minimum(large, NUM_BUCKETS - 1))


def masked_softmax(logits, mask):
    logits = jnp.where(mask, logits.astype(jnp.float32), -jnp.inf)
    m = jnp.max(logits, axis=-1, keepdims=True)
    p = jnp.exp(logits - jnp.where(jnp.isfinite(m), m, 0.0))
    s = jnp.sum(p, axis=-1, keepdims=True)
    return p / jnp.where(s > 0, s, 1.0)


LANE = 128
P_XW = 3 * RWKV_WIDTH
P_XA = P_XW + LANE
P_XG = P_XA + LANE
P_Q = P_XG + GATE_LORA
P_KVC = P_Q + NSA_WIDTH
P_KVS = P_KVC + NSA_KV_WIDTH
P_KVW = P_KVS + NSA_KV_WIDTH
P_GATES = P_KVW + NSA_KV_WIDTH
P_TOTAL = P_GATES + NSA_KV_HEADS * LANE
N_GATE_COLS = 3 * NSA_GROUP


def pad_w_in(w_in):
    d = w_in.shape[0]
    z = lambda n: jnp.zeros((d, n), w_in.dtype)
    o_xw, o_xa, o_xg = RWKV_SPLITS[2], RWKV_SPLITS[3], RWKV_SPLITS[4]
    o_g = RWKV_PROJ + NSA_WIDTH + 3 * NSA_KV_WIDTH
    parts = [w_in[:, :o_xw], w_in[:, o_xw:o_xa], z(LANE - DECAY_LORA), w_in[:, o_xa:o_xg], z(LANE - AAA_LORA),
             w_in[:, o_xg:o_g]]
    for kh in range(NSA_KV_HEADS):
        parts += [w_in[:, o_g + kh * N_GATE_COLS:o_g + (kh + 1) * N_GATE_COLS], z(LANE - N_GATE_COLS)]
    return jnp.concatenate(parts, axis=1).astype(jnp.bfloat16)


def unpad_rwkv(proj_p):
    return jnp.concatenate([proj_p[..., :P_XW], proj_p[..., P_XW:P_XW + DECAY_LORA],
                            proj_p[..., P_XA:P_XA + AAA_LORA], proj_p[..., P_XG:P_XG + GATE_LORA]], axis=-1)


def bias_lookup(tab, idx):
    sel = jnp.asarray(idx.reshape(-1, 1), jnp.int32) == jnp.arange(tab.shape[0], dtype=jnp.int32)[None, :]
    out = jnp.dot(sel.astype(jnp.float32), tab, precision=lax.Precision.HIGHEST)
    return out.reshape(idx.shape + (tab.shape[1],))


def bucket_table(n):
    import numpy as np
    d = np.arange(n)
    max_exact = NUM_BUCKETS // 2
    nf = np.maximum(d, max_exact).astype(np.float64)
    large = max_exact + (np.log(nf / max_exact) / math.log(MAX_DISTANCE / max_exact)
                         * (NUM_BUCKETS - max_exact)).astype(np.int64)
    return np.where(d < max_exact, d, np.minimum(large, NUM_BUCKETS - 1)).astype(np.int32)


def _compress_kernel(x_ref, w_ref, pe_ref, o_ref, *, nblk):
    half = nblk // 2
    acc = jnp.zeros((nblk, NSA_HEAD_DIM), jnp.float32)
    for c in range(CMP_BLOCK):
        xe = x_ref[pl.ds(c, half, stride=2 * CMP_BLOCK), :]
        xo = x_ref[pl.ds(CMP_BLOCK + c, half, stride=2 * CMP_BLOCK), :]
        lhs = jnp.concatenate([xe, xo], axis=0) + pe_ref[c:c + 1, :]
        acc = acc + jnp.dot(lhs.astype(jnp.bfloat16), w_ref[c].astype(jnp.bfloat16),
                            preferred_element_type=jnp.float32)
    o_ref[...] = acc


def _compress_dense_kernel(x_ref, w_ref, pe_ref, o_ref, *, nblk):
    f32, bf16 = jnp.float32, jnp.bfloat16
    L = nblk * CMP_BLOCK
    half = nblk // 2
    rowc = lax.broadcasted_iota(jnp.int32, (L, 1), 0) & (CMP_BLOCK - 1)
    xb = (x_ref[...] + jnp.concatenate([pe_ref[...]] * nblk, axis=0)).astype(bf16)
    y = jnp.zeros((L, NSA_HEAD_DIM), f32)
    for c in range(CMP_BLOCK):
        y = y + jnp.dot(jnp.where(rowc == c, xb, jnp.zeros_like(xb)), w_ref[c].astype(bf16), preferred_element_type=f32)
    n_i = lax.broadcasted_iota(jnp.int32, (nblk, L), 0)
    r_i = lax.broadcasted_iota(jnp.int32, (nblk, L), 1)
    want = jnp.where(n_i < half, 2 * n_i, 2 * (n_i - half) + 1)
    bsum = ((r_i // CMP_BLOCK) == want).astype(bf16)
    yh = y.astype(bf16)
    ym = (y - yh.astype(f32)).astype(bf16)
    yl = (y - yh.astype(f32) - ym.astype(f32)).astype(bf16)
    d = lambda a: jnp.dot(bsum, a, preferred_element_type=f32)
    o_ref[...] = d(yh) + (d(ym) + d(yl))


def compress_prompt(proj_p, cmp_pe, cmp_w, B, T, col0=P_KVC):
    nblk = T // CMP_BLOCK
    hd, KVH = NSA_HEAD_DIM, NSA_KV_HEADS
    return pl.pallas_call(
        functools.partial(_compress_dense_kernel if nblk >= 16 else _compress_kernel, nblk=nblk),
        grid=(B, 2, KVH),
        in_specs=[pl.BlockSpec((T, hd), lambda b, j, kh: (b, col0 // hd + j * KVH + kh)),
                  pl.BlockSpec((None, CMP_BLOCK, hd, hd), lambda b, j, kh: (j, 0, 0, 0)),
                  pl.BlockSpec((None, CMP_BLOCK, hd), lambda b, j, kh: (j, 0, 0))],
        out_specs=pl.BlockSpec((None, None, None, nblk, hd), lambda b, j, kh: (b, j, kh, 0, 0)),
        out_shape=jax.ShapeDtypeStruct((B, 2, KVH, nblk, hd), jnp.float32),
        compiler_params=pltpu.CompilerParams(dimension_semantics=("arbitrary",) * 3, vmem_limit_bytes=VMEM_LIMIT_BYTES),
        name="compress_prompt",
    )(proj_p, cmp_w, cmp_pe)


NSA_TQ = 128
NEG_BIG = -1e30


def _nsa_prompt_kernel(q_ref, gate_ref, ks_ref, vs_ref, kw_ref, vw_ref, kc_ref, vc_ref, bt_ref, bc_ref, es_ref,
                       o_ref, sel_ref, qs_ref, m_ref, l_ref, acc_ref, *, n_sb):
    f32, bf16 = jnp.float32, jnp.bfloat16
    tq, G, hd = NSA_TQ, NSA_GROUP, NSA_HEAD_DIM
    qi = pl.program_id(2)
    t0 = qi * tq
    scale = NSA_HEAD_DIM ** -0.5
    qs_ref[...] = jnp.transpose(jnp.concatenate([q_ref[:, g * hd:(g + 1) * hd] for g in range(G)], axis=0)).astype(bf16)
    tlane = t0 + lax.broadcasted_iota(jnp.int32, (1, tq), 1)

    ncmp = kc_ref.shape[0]
    brow = lax.broadcasted_iota(jnp.int32, (ncmp, 1), 0)
    blk_id = jnp.where(brow < ncmp // 2, 2 * brow, 2 * (brow - ncmp // 2) + 1)
    vis = (blk_id * CMP_BLOCK + (CMP_BLOCK - 1)) <= tlane
    vis = jnp.concatenate([vis] * G, axis=1)
    lc = jnp.dot(kc_ref[...].astype(bf16), qs_ref[...], preferred_element_type=f32) * scale + bc_ref[...]
    lc = jnp.where(vis, lc, -jnp.inf)
    mc = jnp.max(lc, axis=0, keepdims=True)
    pc = jnp.exp(lc - jnp.where(mc > -jnp.inf, mc, 0.0))
    sc = jnp.sum(pc, axis=0, keepdims=True)
    pc = pc / jnp.where(sc > 0, sc, 1.0)
    oct = lax.dot_general(vc_ref[...].astype(bf16), pc.astype(bf16), (((0,), (0,)), ((), ())),
                          preferred_element_type=f32)

    pg = pc[:, 0:tq]
    for g in range(1, G):
        pg = pg + pc[:, g * tq:(g + 1) * tq]
    imp = pg[:n_sb] + pg[n_sb:]
    cur = tlane // SEL_BLOCK
    bl = lax.broadcasted_iota(jnp.int32, (n_sb, tq), 0)
    force = (bl == 0) | (bl == cur) | (bl == cur - 1)
    score = jnp.where(force, FORCE_SCORE, imp)
    score = jnp.where(bl <= cur, score, -jnp.inf)
    rank = jnp.zeros((n_sb, tq), jnp.int32)
    for i in range(n_sb):
        si = score[i:i + 1, :]
        ahead = (si > score) | ((si == score) & (i < bl))
        rank = rank + ahead.astype(jnp.int32)
    sel_ref[...] = (rank < N_SEL).astype(bf16)
    gate = jax.nn.sigmoid(gate_ref[...])
    for g in range(G):
        o_ref[:, g * hd:(g + 1) * hd] = gate[:, 3 * g:3 * g + 1] * jnp.transpose(oct[:, g * tq:(g + 1) * tq])

    kcol = lax.broadcasted_iota(jnp.int32, (tq, 1), 0)
    tlane = t0 + lax.broadcasted_iota(jnp.int32, (1, tq), 1)

    def attend(c, k_ref, v_ref, selected, w):
        s0 = pl.multiple_of(c * tq, tq)
        dist = tlane - (s0 + lax.broadcasted_iota(jnp.int32, (w * tq, 1), 0))
        if selected:
            es = jnp.concatenate([es_ref[c + j] for j in range(w)], axis=0)
            ok = (jnp.dot(es, sel_ref[...], preferred_element_type=f32) > 0.5) & (dist >= 0)
        else:
            ok = (dist >= 0) & (dist <= WINDOW)
        ok = jnp.concatenate([ok] * G, axis=1)
        k = k_ref[pl.ds(s0, w * tq), :].astype(bf16)
        vt = jnp.transpose(v_ref[pl.ds(s0, w * tq), :]).astype(bf16)
        bias = jnp.concatenate([bt_ref[jnp.minimum(qi - (c + j), 2)] for j in range(w)], axis=0)
        s = jnp.dot(k, qs_ref[...], preferred_element_type=f32) * scale + bias
        s = jnp.where(ok, s, NEG_BIG)
        m_old = m_ref[...]
        m_new = jnp.maximum(m_old, jnp.max(s, axis=0, keepdims=True))
        p = jnp.exp(s - m_new)
        alpha = jnp.exp(m_old - m_new)
        l_ref[...] = alpha * l_ref[...] + jnp.sum(p, axis=0, keepdims=True)
        acc_ref[...] = alpha * acc_ref[...] + jnp.dot(vt, p.astype(bf16), preferred_element_type=f32)
        m_ref[...] = m_new

    def run_branch(lo, k_ref, v_ref, selected, gate_col):
        m_ref[...] = jnp.full(m_ref.shape, NEG_BIG, f32)
        l_ref[...] = jnp.zeros(l_ref.shape, f32)
        acc_ref[...] = jnp.zeros(acc_ref.shape, f32)
        odd = (qi + 1 - lo) & 1

        @pl.when(odd == 1)
        def _():
            attend(lo, k_ref, v_ref, selected, 1)

        def pair(i, carry):
            attend(lo + odd + 2 * i, k_ref, v_ref, selected, 2)
            return carry

        lax.fori_loop(0, (qi + 1 - lo) // 2, pair, 0)
        ot = acc_ref[...] / l_ref[...]
        for g in range(G):
            o_ref[:, g * hd:(g + 1) * hd] += (gate[:, 3 * g + gate_col:3 * g + gate_col + 1]
                                              * jnp.transpose(ot[:, g * tq:(g + 1) * tq]))

    run_branch(0, ks_ref, vs_ref, True, 1)
    run_branch(jnp.maximum(qi - WINDOW // tq, 0), kw_ref, vw_ref, False, 2)


def nsa_prompt(proj_p, kvc, rel_bias, B, T):
    import numpy as np
    tq, G, hd, KVH = NSA_TQ, NSA_GROUP, NSA_HEAD_DIM, NSA_KV_HEADS
    nq = T // tq
    n_sb = T // SEL_BLOCK
    ncmp = T // CMP_BLOCK
    assert T % tq == 0 and T % SEL_BLOCK == 0 and ncmp == 2 * n_sb
    bucket = bucket_table(max(T, 3 * tq))
    tab = rel_bias.astype(jnp.float32)
    ii, jj = np.meshgrid(np.arange(tq), np.arange(tq), indexing="ij")
    didx = np.stack([bucket[np.maximum(ii - jj, 0)], bucket[tq + ii - jj], np.full((tq, tq), NUM_BUCKETS - 1)])
    assert bucket[tq + 1] == NUM_BUCKETS - 1
    bt = bias_lookup(tab, didx).reshape(3, tq, tq, KVH, G).transpose(3, 0, 2, 4, 1).reshape(KVH, 3, tq, G * tq)
    blk = np.concatenate([np.arange(0, ncmp, 2), np.arange(1, ncmp, 2)])
    dc = np.arange(T)[:, None] - (blk * CMP_BLOCK + CMP_BLOCK - 1)[None, :]
    bc = bias_lookup(tab, bucket[np.maximum(dc, 0)]).reshape(nq, tq, ncmp, KVH, G).transpose(3, 0, 2, 4, 1).reshape(KVH, nq, ncmp, G * tq)
    es = (np.arange(n_sb)[None, :, None] == (np.arange(nq)[:, None, None] * (tq // SEL_BLOCK)
                                              + np.arange(tq)[None, None, :] // SEL_BLOCK))
    es = jnp.asarray(np.swapaxes(es, 1, 2), jnp.bfloat16)
    col = lambda off: off // hd
    kv_spec = lambda off: pl.BlockSpec((T, hd), lambda b, kh, i: (b, col(off) + kh))
    return pl.pallas_call(
        functools.partial(_nsa_prompt_kernel, n_sb=n_sb),
        grid=(B, KVH, nq),
        in_specs=[pl.BlockSpec((tq, G * hd), lambda b, kh, i: (b * nq + i, P_Q // (G * hd) + kh)),
                  pl.BlockSpec((tq, LANE), lambda b, kh, i: (b * nq + i, P_GATES // LANE + kh)),
                  kv_spec(P_KVS), kv_spec(P_KVS + KVH * hd), kv_spec(P_KVW), kv_spec(P_KVW + KVH * hd),
                  pl.BlockSpec((None, None, None, ncmp, hd), lambda b, kh, i: (b, 0, kh, 0, 0)),
                  pl.BlockSpec((None, None, None, ncmp, hd), lambda b, kh, i: (b, 1, kh, 0, 0)),
                  pl.BlockSpec((None, 3, tq, G * tq), lambda b, kh, i: (kh, 0, 0, 0)),
                  pl.BlockSpec((None, None, ncmp, G * tq), lambda b, kh, i: (kh, i, 0, 0)),
                  pl.BlockSpec((nq, tq, n_sb), lambda b, kh, i: (0, 0, 0))],
        out_specs=pl.BlockSpec((tq, G * hd), lambda b, kh, i: (b * nq + i, kh)),
        out_shape=jax.ShapeDtypeStruct((B * T, NSA_WIDTH), jnp.float32),
        scratch_shapes=[pltpu.VMEM((n_sb, tq), jnp.bfloat16), pltpu.VMEM((hd, G * tq), jnp.bfloat16),
                        pltpu.VMEM((1, G * tq), jnp.float32), pltpu.VMEM((1, G * tq), jnp.float32),
                        pltpu.VMEM((hd, G * tq), jnp.float32)],
        compiler_params=pltpu.CompilerParams(dimension_semantics=("arbitrary",) * 3, vmem_limit_bytes=VMEM_LIMIT_BYTES),
        name="nsa_prompt",
    )(proj_p, proj_p, proj_p, proj_p, proj_p, proj_p, kvc, kvc, bt, bc, es)


SAMPLE_PG = 8
T_PAD = 8


def _compress_paged_kernel(pt_ref, *refs, nb):
    del pt_ref
    page_refs = refs[:SAMPLE_PG]
    w_ref, pe_ref, o_ref = refs[SAMPLE_PG:]
    jk, hd = 2 * NSA_KV_HEADS, NSA_HEAD_DIM
    rows = SAMPLE_PG * nb * jk
    acc = jnp.zeros((rows, 2 * hd), jnp.float32)
    for c in range(CMP_BLOCK):
        x = jnp.concatenate([pr[pl.ds(c, nb, stride=CMP_BLOCK), :, :] for pr in page_refs], axis=0)
        lhs = (x + pe_ref[c]).reshape(rows, hd).astype(jnp.bfloat16)
        acc = acc + jnp.dot(lhs, w_ref[c], preferred_element_type=jnp.float32)
    sub = lax.broadcasted_iota(jnp.int32, (rows, 1), 0) & (jk - 1)
    out = jnp.where(sub < NSA_KV_HEADS, acc[:, :hd], acc[:, hd:])
    o_ref[...] = out.reshape(SAMPLE_PG * nb, jk, hd)


def compress_paged(pool, page_table, cmp_pe, cmp_w):
    n_phys, page = pool.shape[:2]
    B, n_pages = page_table.shape
    jk, hd = 2 * NSA_KV_HEADS, NSA_HEAD_DIM
    nb = page // CMP_BLOCK
    assert n_pages % SAMPLE_PG == 0 and page % CMP_BLOCK == 0
    pool3 = pool.reshape(n_phys, page, jk, hd)
    w01 = jnp.concatenate([cmp_w[0], cmp_w[1]], axis=-1).astype(jnp.bfloat16)
    pe8 = jnp.repeat(jnp.swapaxes(cmp_pe, 0, 1), NSA_KV_HEADS, axis=1)
    page_spec = lambda i: pl.BlockSpec((None, page, jk, hd),
                                       lambda b, g, pt: (pt[b * n_pages + g * SAMPLE_PG + i], 0, 0, 0))
    return pl.pallas_call(
        functools.partial(_compress_paged_kernel, nb=nb),
        grid_spec=pltpu.PrefetchScalarGridSpec(
            num_scalar_prefetch=1, grid=(B, n_pages // SAMPLE_PG),
            in_specs=[page_spec(i) for i in range(SAMPLE_PG)]
            + [pl.BlockSpec((CMP_BLOCK, hd, 2 * hd), lambda b, g, pt: (0, 0, 0)),
               pl.BlockSpec((CMP_BLOCK, jk, hd), lambda b, g, pt: (0, 0, 0))],
            out_specs=pl.BlockSpec((None, SAMPLE_PG * nb, jk, hd), lambda b, g, pt: (b, g, 0, 0))),
        out_shape=jax.ShapeDtypeStruct((B, n_pages * nb, jk, hd), jnp.float32),
        compiler_params=pltpu.CompilerParams(dimension_semantics=("arbitrary",) * 2, vmem_limit_bytes=VMEM_LIMIT_BYTES),
        name="compress_paged",
    )(page_table.reshape(-1), *([pool3] * SAMPLE_PG), w01, pe8)


def _nsa_sample_kernel(pt_ref, *refs, past, page, n_steps, n_sb, half):
    del pt_ref
    f32, bf16 = jnp.float32, jnp.bfloat16
    KVH, G, hd = NSA_KV_HEADS, NSA_GROUP, NSA_HEAD_DIM
    q_refs = refs[0:KVH]
    gate_ref, ksn_ref, vsn_ref, kwn_ref, vwn_ref, kvc_ref, bc_ref = refs[KVH:KVH + 7]
    page_refs = refs[KVH + 7:KVH + 7 + SAMPLE_PG]
    es_ref, bs_ref, bn_ref, win_ref, bw_ref, o_ref, m_ref, l_ref, acc_ref, oc_ref, sel_ref = refs[KVH + 7 + SAMPLE_PG:]
    g_step = pl.program_id(1)
    R = G * T_PAD
    scale = NSA_HEAD_DIM ** -0.5
    nt = (((1,), (1,)), ((), ()))
    t_row = past + (lax.broadcasted_iota(jnp.int32, (R, 1), 0) & (T_PAD - 1))
    qs = [jnp.concatenate([q_refs[kh][:, g * hd:(g + 1) * hd] for g in range(G)], axis=0).astype(bf16) for kh in range(KVH)]

    @pl.when(g_step == 0)
    def _():
        m_ref[...] = jnp.full(m_ref.shape, NEG_BIG, f32)
        l_ref[...] = jnp.zeros(l_ref.shape, f32)
        acc_ref[...] = jnp.zeros(acc_ref.shape, f32)
        lane = lax.broadcasted_iota(jnp.int32, (1, 2 * half), 1)
        lo = lane < half
        blk_id = jnp.where(lo, 2 * lane, 2 * (lane - half) + 1)
        real = jnp.where(lo, lane, lane - half) < n_sb
        vis = real & (blk_id * CMP_BLOCK + (CMP_BLOCK - 1) <= t_row)
        tpos = past + lax.broadcasted_iota(jnp.int32, (T_PAD, 1), 0)
        cur = tpos // SEL_BLOCK
        bl = lax.broadcasted_iota(jnp.int32, (T_PAD, half), 1)
        force = (bl == 0) | (bl == cur) | (bl == cur - 1)
        for kh in range(KVH):
            lc = lax.dot_general(qs[kh], kvc_ref[kh].astype(bf16), nt, preferred_element_type=f32) * scale + bc_ref[kh]
            lc = jnp.where(vis, lc, -jnp.inf)
            mc = jnp.max(lc, axis=-1, keepdims=True)
            pc = jnp.exp(lc - jnp.where(mc > -jnp.inf, mc, 0.0))
            sc = jnp.sum(pc, axis=-1, keepdims=True)
            pc = pc / jnp.where(sc > 0, sc, 1.0)
            oc_ref[kh * R:(kh + 1) * R, :] = jnp.dot(pc.astype(bf16), kvc_ref[KVH + kh].astype(bf16),
                                                     preferred_element_type=f32)
            pg = pc[0:T_PAD]
            for g in range(1, G):
                pg = pg + pc[g * T_PAD:(g + 1) * T_PAD]
            imp = pg[:, :half] + pg[:, half:]
            score = jnp.where(force, FORCE_SCORE, imp)
            score = jnp.where((bl <= cur) & (bl < n_sb), score, -jnp.inf)
            rank = jnp.zeros((T_PAD, half), jnp.int32)
            for i in range(n_sb):
                si = score[:, i:i + 1]
                rank = rank + ((si > score) | ((si == score) & (i < bl))).astype(jnp.int32)
            selm = ((rank < N_SEL) & (bl < n_sb)).astype(bf16)
            sel_ref[kh * R:(kh + 1) * R, :] = jnp.concatenate([selm] * G, axis=0)

    last = g_step == n_steps - 1
    bsel = bs_ref[jnp.where(last, 1, 0)]
    okf = jnp.dot(sel_ref[...], es_ref[...], preferred_element_type=f32) > 0.5
    for kh in range(KVH):
        rs = slice(kh * R, (kh + 1) * R)
        k = jnp.concatenate([pr[pl.ds(kh, page, stride=2 * KVH), :] for pr in page_refs], axis=0).astype(bf16)
        v = jnp.concatenate([pr[pl.ds(KVH + kh, page, stride=2 * KVH), :] for pr in page_refs], axis=0).astype(bf16)
        s = lax.dot_general(qs[kh], k, nt, preferred_element_type=f32) * scale + bsel[rs]
        ok = okf[rs]
        s = jnp.where(ok, s, NEG_BIG)
        m_old = m_ref[rs]
        m_new = jnp.maximum(m_old, jnp.max(s, axis=-1, keepdims=True))
        p = jnp.where(ok, jnp.exp(s - m_new), 0.0)
        alpha = jnp.exp(m_old - m_new)
        l_ref[rs] = alpha * l_ref[rs] + jnp.sum(p, axis=-1, keepdims=True)
        acc_ref[rs] = alpha * acc_ref[rs] + jnp.dot(p.astype(bf16), v, preferred_element_type=f32)
        m_ref[rs] = m_new

    @pl.when(last)
    def _():
        gate = jax.nn.sigmoid(gate_ref[...])
        ncol = lax.broadcasted_iota(jnp.int32, (1, T_PAD), 1)
        wb = win_ref.shape[0] // (2 * KVH)
        wcol = lax.broadcasted_iota(jnp.int32, (1, wb), 1)
        for kh in range(KVH):
            rs = slice(kh * R, (kh + 1) * R)
            cs = slice(kh * hd, (kh + 1) * hd)
            s = lax.dot_general(qs[kh], ksn_ref[:, cs].astype(bf16), nt, preferred_element_type=f32) * scale + bn_ref[kh]
            ok = (past + ncol) <= t_row
            s = jnp.where(ok, s, NEG_BIG)
            m_old = m_ref[rs]
            m_new = jnp.maximum(m_old, jnp.max(s, axis=-1, keepdims=True))
            p = jnp.where(ok, jnp.exp(s - m_new), 0.0)
            alpha = jnp.exp(m_old - m_new)
            l_s = alpha * l_ref[rs] + jnp.sum(p, axis=-1, keepdims=True)
            osel = (alpha * acc_ref[rs] + jnp.dot(p.astype(bf16), vsn_ref[:, cs].astype(bf16),
                                                  preferred_element_type=f32)) / l_s
            s1 = lax.dot_general(qs[kh], win_ref[pl.ds(kh, wb, stride=2 * KVH), :].astype(bf16), nt,
                                 preferred_element_type=f32) * scale + bw_ref[kh]
            d1 = t_row - (past - wb + wcol)
            ok1 = (d1 >= 0) & (d1 <= WINDOW)
            s1 = jnp.where(ok1, s1, NEG_BIG)
            s2 = lax.dot_general(qs[kh], kwn_ref[:, cs].astype(bf16), nt, preferred_element_type=f32) * scale + bn_ref[kh]
            s2 = jnp.where(ok, s2, NEG_BIG)
            mw = jnp.maximum(jnp.max(s1, axis=-1, keepdims=True), jnp.max(s2, axis=-1, keepdims=True))
            p1 = jnp.where(ok1, jnp.exp(s1 - mw), 0.0)
            p2 = jnp.where(ok, jnp.exp(s2 - mw), 0.0)
            l_w = jnp.sum(p1, axis=-1, keepdims=True) + jnp.sum(p2, axis=-1, keepdims=True)
            owin = (jnp.dot(p1.astype(bf16), win_ref[pl.ds(KVH + kh, wb, stride=2 * KVH), :].astype(bf16),
                            preferred_element_type=f32)
                    + jnp.dot(p2.astype(bf16), vwn_ref[:, cs].astype(bf16), preferred_element_type=f32)) / l_w
            oc = oc_ref[rs]
            for g in range(G):
                r = slice(g * T_PAD, (g + 1) * T_PAD)
                c0 = kh * LANE + 3 * g
                o_ref[:, (kh * G + g) * hd:(kh * G + g + 1) * hd] = (
                    gate[:, c0:c0 + 1] * oc[r] + gate[:, c0 + 1:c0 + 2] * osel[r] + gate[:, c0 + 2:c0 + 3] * owin[r])


def nsa_sample(proj_pad, kvc_t, pool_sel, win_buf, page_table, rel_bias, B, T):
    import numpy as np
    KVH, G, hd = NSA_KV_HEADS, NSA_GROUP, NSA_HEAD_DIM
    n_phys, page = pool_sel.shape[:2]
    n_pages = page_table.shape[1]
    past = n_pages * page
    wb = win_buf.shape[1]
    half = kvc_t.shape[2] // 2
    n_sb = -(-(past + T) // SEL_BLOCK)
    n_steps = n_pages // SAMPLE_PG
    keys = SAMPLE_PG * page
    R = G * T_PAD
    assert n_pages % SAMPLE_PG == 0 and page % SEL_BLOCK == 0 and T <= T_PAD and n_sb <= half and past % SEL_BLOCK == 0
    bucket = bucket_table(past + T_PAD + 1)
    tab = rel_bias.astype(jnp.float32)
    qpos = past + np.arange(T_PAD)

    def bias_rows(dist):
        b = bias_lookup(tab, bucket[np.maximum(dist, 0)])
        return b.reshape(T_PAD, dist.shape[1], KVH, G).transpose(2, 3, 0, 1).reshape(KVH, R, dist.shape[1])

    lane = np.arange(2 * half)
    blk = np.where(lane < half, 2 * lane, 2 * (lane - half) + 1)
    bc = bias_rows(qpos[:, None] - (blk * CMP_BLOCK + CMP_BLOCK - 1)[None, :])
    far = bias_rows(np.full((T_PAD, keys), past, np.int64))
    assert bucket[page + 1] == NUM_BUCKETS - 1
    near = bias_rows(qpos[:, None] - ((n_steps - 1) * keys + np.arange(keys))[None, :])
    bs = jnp.stack([far, near], axis=0).reshape(2, KVH * R, keys)
    bn = bias_rows(qpos[:, None] - (past + np.arange(T_PAD))[None, :])
    bw = bias_rows(qpos[:, None] - (past - wb + np.arange(wb))[None, :])
    es = (np.arange(half)[None, :, None] == (np.arange(n_steps)[:, None, None] * (keys // SEL_BLOCK)
                                              + np.arange(keys)[None, None, :] // SEL_BLOCK))
    es = jnp.asarray(es, jnp.bfloat16)
    pool2 = pool_sel.reshape(n_phys, page * 2 * KVH, hd)
    win2 = win_buf.reshape(B, wb * 2 * KVH, hd)
    wq = G * hd
    c = lambda *idx: (lambda b, g, pt: idx)
    row_blk = lambda w, col: pl.BlockSpec((T_PAD, w), lambda b, g, pt: (b, col))
    in_specs = ([row_blk(wq, P_Q // wq + kh) for kh in range(KVH)]
                + [row_blk(KVH * LANE, P_GATES // (KVH * LANE)),
                   row_blk(KVH * hd, P_KVS // (KVH * hd)), row_blk(KVH * hd, P_KVS // (KVH * hd) + 1),
                   row_blk(KVH * hd, P_KVW // (KVH * hd)), row_blk(KVH * hd, P_KVW // (KVH * hd) + 1),
                   pl.BlockSpec((None, 2 * KVH, 2 * half, hd), lambda b, g, pt: (b, 0, 0, 0)),
                   pl.BlockSpec((KVH, R, 2 * half), c(0, 0, 0))]
                + [pl.BlockSpec((None, page * 2 * KVH, hd),
                                (lambda i: lambda b, g, pt: (pt[b * n_pages + g * SAMPLE_PG + i], 0, 0))(i))
                   for i in range(SAMPLE_PG)]
                + [pl.BlockSpec((None, half, keys), lambda b, g, pt: (g, 0, 0)),
                   pl.BlockSpec((2, KVH * R, keys), c(0, 0, 0)),
                   pl.BlockSpec((KVH, R, T_PAD), c(0, 0, 0)),
                   pl.BlockSpec((None, wb * 2 * KVH, hd), lambda b, g, pt: (b, 0, 0)),
                   pl.BlockSpec((KVH, R, wb), c(0, 0, 0))])
    f32 = jnp.float32
    return pl.pallas_call(
        functools.partial(_nsa_sample_kernel, past=past, page=page, n_steps=n_steps, n_sb=n_sb, half=half),
        grid_spec=pltpu.PrefetchScalarGridSpec(
            num_scalar_prefetch=1, grid=(B, n_steps), in_specs=in_specs,
            out_specs=pl.BlockSpec((T_PAD, NSA_WIDTH), lambda b, g, pt: (b, 0)),
            scratch_shapes=[pltpu.VMEM((KVH * R, 1), f32), pltpu.VMEM((KVH * R, 1), f32), pltpu.VMEM((KVH * R, hd), f32),
                            pltpu.VMEM((KVH * R, hd), f32), pltpu.VMEM((KVH * R, half), jnp.bfloat16)]),
        out_shape=jax.ShapeDtypeStruct((B * T_PAD, NSA_WIDTH), f32),
        compiler_params=pltpu.CompilerParams(dimension_semantics=("arbitrary",) * 2, vmem_limit_bytes=VMEM_LIMIT_BYTES),
        name="nsa_sample",
    )(page_table.reshape(-1), *([proj_pad] * (KVH + 5)), kvc_t, bc, *([pool2] * SAMPLE_PG), es, bs, bn, win2, bw)


RWKV_PAIR = 2 * RWKV_HEAD_DIM
HIGHEST = lax.Precision.HIGHEST


def _rwkv_kernel_one_pair(r_ref, k_ref, v_ref, xw_ref, xa_ref, xg_ref,
                 sr_ref, sk_ref, sv_ref, sxw_ref, sxa_ref, sxg_ref,
                 mr_ref, mk_ref, mv_ref, mxw_ref, mxa_ref, mxg_ref,
                 s0_ref, w0_ref, a0_ref, kk_ref, ka_ref, rk_ref, lnw_ref, lnb_ref,
                 wup_ref, aup_ref, gup_ref,
                 o_ref, sout_ref,
                 S_ref, cr_ref, ck_ref, cv_ref, cxw_ref, cxa_ref, cxg_ref, *, C, t_valid, n_chunks):
    f32, bf16 = jnp.float32, jnp.bfloat16
    hd, W = RWKV_HEAD_DIM, RWKV_PAIR
    ci = pl.program_id(2)
    lane = lax.broadcasted_iota(jnp.int32, (1, W), 1)
    h0 = lane < hd
    row = lax.broadcasted_iota(jnp.int32, (C, 1), 0)
    valid = (ci * C + row) < t_valid
    carries = ((cr_ref, sr_ref), (ck_ref, sk_ref), (cv_ref, sv_ref), (cxw_ref, sxw_ref), (cxa_ref, sxa_ref),
               (cxg_ref, sxg_ref))

    @pl.when(ci == 0)
    def _():
        S_ref[...] = jnp.zeros((W, W), f32)
        S_ref[0:hd, 0:hd] = s0_ref[0]
        S_ref[hd:W, hd:W] = s0_ref[1]
        for c_ref, s_ref in carries:
            c_ref[0:1, :] = s_ref[...]

    def shifted(x_ref, c_ref, mu_ref):
        x = x_ref[...]
        prev = jnp.where(row == 0, c_ref[0:1, :], pltpu.roll(x, 1, axis=0))
        c_ref[0:1, :] = x[C - 1:C, :]
        return x + (prev - x) * mu_ref[...]

    def seg_sum(x):
        s_lo = jnp.sum(jnp.where(h0, x, 0.0), axis=-1, keepdims=True)
        s_hi = jnp.sum(jnp.where(h0, 0.0, x), axis=-1, keepdims=True)
        return jnp.where(h0, s_lo, s_hi)

    def stack(x):
        return jnp.concatenate([jnp.where(h0, x, 0.0), jnp.where(h0, 0.0, x)], axis=0)

    mr = shifted(r_ref, cr_ref, mr_ref)
    mk = shifted(k_ref, ck_ref, mk_ref)
    mv = shifted(v_ref, cv_ref, mv_ref)
    mxw = shifted(xw_ref, cxw_ref, mxw_ref)
    mxa = shifted(xa_ref, cxa_ref, mxa_ref)
    mxg = shifted(xg_ref, cxg_ref, mxg_ref)

    z = w0_ref[...] + jnp.dot(jnp.tanh(mxw).astype(bf16), wup_ref[...].astype(bf16), preferred_element_type=f32)
    w_log = -jax.nn.softplus(-z) - 0.5
    lw = -jnp.exp(w_log)
    a = jax.nn.sigmoid(a0_ref[...] + jnp.dot(mxa.astype(bf16), aup_ref[...].astype(bf16), preferred_element_type=f32))
    g = jnp.dot(jax.nn.sigmoid(mxg).astype(bf16), gup_ref[...].astype(bf16), preferred_element_type=f32)
    kk = mk * kk_ref[...]
    kk = kk / jnp.maximum(jnp.sqrt(seg_sum(kk * kk)), 1e-12)
    kp = mk * (1.0 + (a - 1.0) * ka_ref[...])
    lw = jnp.where(valid, lw, 0.0)
    a_t = jnp.where(valid, -kk, 0.0)
    b_t = jnp.where(valid, kk * a, 0.0)
    k_t = jnp.where(valid, kp, 0.0)
    v_t = jnp.where(valid, mv, 0.0)

    ti = lax.broadcasted_iota(jnp.int32, (C, C), 0)
    tj = lax.broadcasted_iota(jnp.int32, (C, C), 1)
    cl = jnp.dot((tj <= ti).astype(f32), lw, precision=HIGHEST, preferred_element_type=f32)
    e_neg = jnp.exp(-cl)
    At = a_t * jnp.exp(cl - lw)
    Bt = b_t * e_neg
    Kt = k_t * e_neg
    Rt = mr * jnp.exp(cl)
    G = jnp.concatenate([stack(At), stack(Rt)], axis=0).astype(bf16)
    Z = jnp.concatenate([Bt, Bt, Kt, Kt], axis=0).astype(bf16)
    nt = (((1,), (1,)), ((), ()))
    M = lax.dot_general(G, Z, nt, preferred_element_type=f32)
    ri = lax.broadcasted_iota(jnp.int32, (2 * C, 2 * C), 0)
    cj = lax.broadcasted_iota(jnp.int32, (2 * C, 2 * C), 1)
    same = (ri >= C) == (cj >= C)
    strict = same & ((cj & (C - 1)) < (ri & (C - 1)))
    incl = same & ((cj & (C - 1)) <= (ri & (C - 1)))
    L = jnp.where(strict, M[0:2 * C, 0:2 * C], 0.0)
    AK = jnp.where(strict, M[0:2 * C, 2 * C:4 * C], 0.0)
    RB = jnp.where(incl, M[2 * C:4 * C, 0:2 * C], 0.0)
    RK = jnp.where(incl, M[2 * C:4 * C, 2 * C:4 * C], 0.0)
    S = S_ref[...]
    GS = lax.dot_general(G, S.astype(bf16), nt, preferred_element_type=f32)
    Vs = stack(v_t)
    U = GS[0:2 * C] + jnp.dot(AK.astype(bf16), Vs.astype(bf16), preferred_element_type=f32)
    P = L
    levels = C.bit_length() - 1
    for lvl in range(levels):
        U = U + jnp.dot(P, U, precision=HIGHEST, preferred_element_type=f32)
        if lvl + 1 < levels:
            P = jnp.dot(P, P, precision=HIGHEST, preferred_element_type=f32)
    X = jnp.concatenate([U, Vs], axis=0).astype(bf16)
    Y = GS[2 * C:4 * C] + jnp.dot(jnp.concatenate([RB, RK], axis=1).astype(bf16), X, preferred_element_type=f32)
    y = Y[0:C] + Y[C:2 * C]
    dS = lax.dot_general(X, Z, (((0,), (0,)), ((), ())), preferred_element_type=f32)
    vi = lax.broadcasted_iota(jnp.int32, (W, W), 0)
    kj = lax.broadcasted_iota(jnp.int32, (W, W), 1)
    S_new = jnp.where((vi < hd) == (kj < hd), S + dS, 0.0) * jnp.exp(cl[C - 1:C, :])
    S_ref[...] = S_new

    mean = seg_sum(y) * (1.0 / hd)
    yc = y - mean
    var = seg_sum(yc * yc) * (1.0 / hd)
    yn = yc * lax.rsqrt(var + LNX_EPS) * lnw_ref[...] + lnb_ref[...]
    bonus = seg_sum(mr * kp * rk_ref[...]) * mv
    o_ref[...] = (yn + bonus) * g

    @pl.when(ci == n_chunks - 1)
    def _():
        sout_ref[0] = S_new[0:hd, 0:hd]
        sout_ref[1] = S_new[hd:W, hd:W]


def rwkv_mix_one_pair(proj_p, shift_prev, wkv0, lp, B, T, t_valid, C):
    f32 = jnp.float32
    W, hd = RWKV_PAIR, RWKV_HEAD_DIM
    n_pairs = RWKV_WIDTH // W
    n_chunks = T // C
    assert T % C == 0 and C & (C - 1) == 0 and C >= 8

    def pad_vec(x):
        z = lambda n: jnp.zeros(x.shape[:-1] + (n,), x.dtype)
        return jnp.concatenate([x[..., :RWKV_SPLITS[2]], x[..., RWKV_SPLITS[2]:RWKV_SPLITS[3]], z(LANE - DECAY_LORA),
                                x[..., RWKV_SPLITS[3]:RWKV_SPLITS[4]], z(LANE - AAA_LORA), x[..., RWKV_SPLITS[4]:]], -1)

    shift_p = pad_vec(shift_prev.astype(f32))[:, None, :]
    mu_p = pad_vec(lp['rwkv_mu'].astype(f32))[None, :]
    pad_rows_to = lambda w: jnp.pad(w, ((0, LANE - w.shape[0]), (0, 0)))
    wup, aup = pad_rows_to(lp['rwkv_w_up']), pad_rows_to(lp['rwkv_a_up'])
    vec = lambda x: x.reshape(1, RWKV_WIDTH).astype(f32)
    nw = RWKV_WIDTH // W
    col_r, col_k, col_v = 0, nw, 2 * nw
    col_xw, col_xa, col_xg = P_XW // LANE, P_XA // LANE, P_XG // GATE_LORA
    row_spec = lambda w, cfn: pl.BlockSpec((C, w), lambda b, p, c: (b * n_chunks + c, cfn(p)))
    sh_spec = lambda w, cfn: pl.BlockSpec((None, 1, w), lambda b, p, c: (b, 0, cfn(p)))
    mu_spec = lambda w, cfn: pl.BlockSpec((1, w), lambda b, p, c: (0, cfn(p)))
    cols = [(W, lambda p: col_r + p), (W, lambda p: col_k + p), (W, lambda p: col_v + p),
            (LANE, lambda p: col_xw), (LANE, lambda p: col_xa), (GATE_LORA, lambda p: col_xg)]
    pvec = pl.BlockSpec((1, W), lambda b, p, c: (0, p))
    state_spec = pl.BlockSpec((None, 2, hd, hd), lambda b, p, c: (b, p, 0, 0))
    in_specs = ([row_spec(w, f) for w, f in cols] + [sh_spec(w, f) for w, f in cols] + [mu_spec(w, f) for w, f in cols]
                + [state_spec] + [pvec] * 7
                + [pl.BlockSpec((LANE, W), lambda b, p, c: (0, p)), pl.BlockSpec((LANE, W), lambda b, p, c: (0, p)),
                   pl.BlockSpec((GATE_LORA, W), lambda b, p, c: (0, p))])
    scratch = [pltpu.VMEM((W, W), f32)] + [pltpu.VMEM((8, w), f32) for w, _ in cols]
    o, s_fin = pl.pallas_call(
        functools.partial(_rwkv_kernel, C=C, t_valid=t_valid, n_chunks=n_chunks),
        grid=(B, n_pairs, n_chunks),
        in_specs=in_specs,
        out_specs=[pl.BlockSpec((C, W), lambda b, p, c: (b * n_chunks + c, p)), state_spec],
        out_shape=[jax.ShapeDtypeStruct((B * T, RWKV_WIDTH), f32),
                   jax.ShapeDtypeStruct((B, RWKV_HEADS, hd, hd), f32)],
        scratch_shapes=scratch,
        compiler_params=pltpu.CompilerParams(dimension_semantics=("arbitrary",) * 3, vmem_limit_bytes=VMEM_LIMIT_BYTES),
        name="rwkv_mix",
    )(*([proj_p] * 6), *([shift_p] * 6), *([mu_p] * 6), wkv0.astype(f32),
      vec(lp['rwkv_w0']), vec(lp['rwkv_a0']), vec(lp['rwkv_k_k']), vec(lp['rwkv_k_a']), vec(lp['rwkv_r_k']),
      vec(lp['rwkv_lnx_w']), vec(lp['rwkv_lnx_b']), wup, aup, lp['rwkv_g_up'])
    return o, s_fin


def _dot3(a, b):
    f32, bf16 = jnp.float32, jnp.bfloat16
    ah = a.astype(bf16)
    al = (a - ah.astype(f32)).astype(bf16)
    bh = b.astype(bf16)
    bl = (b - bh.astype(f32)).astype(bf16)
    d = lambda x, y: jnp.dot(x, y, preferred_element_type=f32)
    return d(ah, bh) + (d(ah, bl) + d(al, bh))


def _rwkv_kernel(r_ref, k_ref, v_ref, xw_ref, xa_ref, xg_ref,
                 sr_ref, sk_ref, sv_ref, sxw_ref, sxa_ref, sxg_ref,
                 mr_ref, mk_ref, mv_ref, mxw_ref, mxa_ref, mxg_ref,
                 s0_ref, w0_ref, a0_ref, kk_ref, ka_ref, rk_ref, lnw_ref, lnb_ref,
                 wup_ref, aup_ref, gup_ref,
                 o_ref, sout_ref,
                 S_ref, cr_ref, ck_ref, cv_ref, cxw_ref, cxa_ref, cxg_ref, *, C, NP, t_valid, n_chunks):
    f32, bf16 = jnp.float32, jnp.bfloat16
    hd, W = RWKV_HEAD_DIM, RWKV_PAIR
    ci = pl.program_id(2)
    lane = lax.broadcasted_iota(jnp.int32, (1, W), 1)
    h0 = lane < hd
    row = lax.broadcasted_iota(jnp.int32, (C, 1), 0)
    valid = (ci * C + row) < t_valid
    carries = ((cr_ref, sr_ref), (ck_ref, sk_ref), (cv_ref, sv_ref), (cxw_ref, sxw_ref), (cxa_ref, sxa_ref),
               (cxg_ref, sxg_ref))

    @pl.when(ci == 0)
    def _():
        S_ref[...] = jnp.zeros(S_ref.shape, f32)
        for p in range(NP):
            S_ref[p, 0:hd, 0:hd] = s0_ref[2 * p]
            S_ref[p, hd:W, hd:W] = s0_ref[2 * p + 1]
        for c_ref, s_ref in carries:
            c_ref[0:1, :] = s_ref[...]

    def shifted(x_ref, c_ref, mu_ref):
        x = x_ref[...]
        prev = jnp.where(row == 0, c_ref[0:1, :], pltpu.roll(x, 1, axis=0))
        c_ref[0:1, :] = x[C - 1:C, :]
        return x + (prev - x) * mu_ref[...]

    def seg_sum(x):
        s_lo = jnp.sum(jnp.where(h0, x, 0.0), axis=-1, keepdims=True)
        s_hi = jnp.sum(jnp.where(h0, 0.0, x), axis=-1, keepdims=True)
        return jnp.where(h0, s_lo, s_hi)

    def stack(x):
        return jnp.concatenate([jnp.where(h0, x, 0.0), jnp.where(h0, 0.0, x)], axis=0)

    mr_all = shifted(r_ref, cr_ref, mr_ref)
    mk_all = shifted(k_ref, ck_ref, mk_ref)
    mv_all = shifted(v_ref, cv_ref, mv_ref)
    mxw = shifted(xw_ref, cxw_ref, mxw_ref)
    mxa = shifted(xa_ref, cxa_ref, mxa_ref)
    mxg = shifted(xg_ref, cxg_ref, mxg_ref)

    z_all = w0_ref[...] + jnp.dot(jnp.tanh(mxw).astype(bf16), wup_ref[...].astype(bf16), preferred_element_type=f32)
    a_all = jax.nn.sigmoid(a0_ref[...] + jnp.dot(mxa.astype(bf16), aup_ref[...].astype(bf16), preferred_element_type=f32))
    g_all = jnp.dot(jax.nn.sigmoid(mxg).astype(bf16), gup_ref[...].astype(bf16), preferred_element_type=f32)

    ti = lax.broadcasted_iota(jnp.int32, (C, C), 0)
    tj = lax.broadcasted_iota(jnp.int32, (C, C), 1)
    tri = (tj <= ti).astype(f32)
    ri = lax.broadcasted_iota(jnp.int32, (2 * C, 2 * C), 0)
    cj = lax.broadcasted_iota(jnp.int32, (2 * C, 2 * C), 1)
    same = (ri >= C) == (cj >= C)
    strict = same & ((cj & (C - 1)) < (ri & (C - 1)))
    incl = same & ((cj & (C - 1)) <= (ri & (C - 1)))
    vi = lax.broadcasted_iota(jnp.int32, (W, W), 0)
    kj = lax.broadcasted_iota(jnp.int32, (W, W), 1)
    blockdiag = (vi < hd) == (kj < hd)
    nt = (((1,), (1,)), ((), ()))
    levels = C.bit_length() - 1

    stage = []
    for p in range(NP):
        ls = slice(p * W, (p + 1) * W)
        mr, mk, mv, a, g = mr_all[:, ls], mk_all[:, ls], mv_all[:, ls], a_all[:, ls], g_all[:, ls]
        w_log = -jax.nn.softplus(-z_all[:, ls]) - 0.5
        lw = -jnp.exp(w_log)
        kk = mk * kk_ref[:, ls]
        kk = kk / jnp.maximum(jnp.sqrt(seg_sum(kk * kk)), 1e-12)
        kp = mk * (1.0 + (a - 1.0) * ka_ref[:, ls])
        lw = jnp.where(valid, lw, 0.0)
        a_t = jnp.where(valid, -kk, 0.0)
        b_t = jnp.where(valid, kk * a, 0.0)
        k_t = jnp.where(valid, kp, 0.0)
        v_t = jnp.where(valid, mv, 0.0)

        cl = jnp.dot(tri, lw, precision=HIGHEST, preferred_element_type=f32)
        e_neg = jnp.exp(-cl)
        At = a_t * jnp.exp(cl - lw)
        Bt = b_t * e_neg
        Kt = k_t * e_neg
        Rt = mr * jnp.exp(cl)
        G = jnp.concatenate([stack(At), stack(Rt)], axis=0).astype(bf16)
        Z = jnp.concatenate([Bt, Bt, Kt, Kt], axis=0).astype(bf16)
        M = lax.dot_general(G, Z, nt, preferred_element_type=f32)
        L = jnp.where(strict, M[0:2 * C, 0:2 * C], 0.0)
        AK = jnp.where(strict, M[0:2 * C, 2 * C:4 * C], 0.0)
        RB = jnp.where(incl, M[2 * C:4 * C, 0:2 * C], 0.0)
        RK = jnp.where(incl, M[2 * C:4 * C, 2 * C:4 * C], 0.0)
        S = S_ref[p]
        GS = lax.dot_general(G, S.astype(bf16), nt, preferred_element_type=f32)
        Vs = stack(v_t)
        U = GS[0:2 * C] + jnp.dot(AK.astype(bf16), Vs.astype(bf16), preferred_element_type=f32)
        stage.append(dict(ls=ls, mr=mr, mv=mv, kp=kp, g=g, cl=cl, Z=Z, GS=GS, Vs=Vs, RB=RB, RK=RK, S=S, U=U, P=L))

    for lvl in range(levels):
        for st in stage:
            st['U'] = st['U'] + _dot3(st['P'], st['U'])
            if lvl + 1 < levels:
                st['P'] = _dot3(st['P'], st['P'])

    for p, st in enumerate(stage):
        ls, mr, mv, kp, g, cl, Z, GS, Vs, RB, RK, S, U = (st[n] for n in ('ls', 'mr', 'mv', 'kp', 'g', 'cl', 'Z', 'GS', 'Vs',
                                                                       'RB', 'RK', 'S', 'U'))
        X = jnp.concatenate([U, Vs], axis=0).astype(bf16)
        Y = GS[2 * C:4 * C] + jnp.dot(jnp.concatenate([RB, RK], axis=1).astype(bf16), X, preferred_element_type=f32)
        y = Y[0:C] + Y[C:2 * C]
        dS = lax.dot_general(X, Z, (((0,), (0,)), ((), ())), preferred_element_type=f32)
        S_new = jnp.where(blockdiag, S + dS, 0.0) * jnp.exp(cl[C - 1:C, :])
        S_ref[p] = S_new

        mean = seg_sum(y) * (1.0 / hd)
        yc = y - mean
        var = seg_sum(yc * yc) * (1.0 / hd)
        yn = yc * lax.rsqrt(var + LNX_EPS) * lnw_ref[:, ls] + lnb_ref[:, ls]
        bonus = seg_sum(mr * kp * rk_ref[:, ls]) * mv
        o_ref[:, ls] = (yn + bonus) * g

    @pl.when(ci == n_chunks - 1)
    def _():
        for p in range(NP):
            s_fin = S_ref[p]
            sout_ref[2 * p] = s_fin[0:hd, 0:hd]
            sout_ref[2 * p + 1] = s_fin[hd:W, hd:W]


def rwkv_mix(proj_p, shift_prev, wkv0, lp, B, T, t_valid, C, NP):
    f32 = jnp.float32
    hd = RWKV_HEAD_DIM
    W = NP * RWKV_PAIR
    n_steps = RWKV_WIDTH // W
    n_chunks = T // C
    assert T % C == 0 and C & (C - 1) == 0 and C >= 8 and RWKV_WIDTH % W == 0

    def pad_vec(x):
        z = lambda n: jnp.zeros(x.shape[:-1] + (n,), x.dtype)
        return jnp.concatenate([x[..., :RWKV_SPLITS[2]], x[..., RWKV_SPLITS[2]:RWKV_SPLITS[3]], z(LANE - DECAY_LORA),
                                x[..., RWKV_SPLITS[3]:RWKV_SPLITS[4]], z(LANE - AAA_LORA), x[..., RWKV_SPLITS[4]:]], -1)

    shift_p = pad_vec(shift_prev.astype(f32))[:, None, :]
    mu_p = pad_vec(lp['rwkv_mu'].astype(f32))[None, :]
    pad_rows_to = lambda w: jnp.pad(w, ((0, LANE - w.shape[0]), (0, 0)))
    wup, aup = pad_rows_to(lp['rwkv_w_up']), pad_rows_to(lp['rwkv_a_up'])
    vec = lambda x: x.reshape(1, RWKV_WIDTH).astype(f32)
    nw = RWKV_WIDTH // W
    col_r, col_k, col_v = 0, nw, 2 * nw
    col_xw, col_xa, col_xg = P_XW // LANE, P_XA // LANE, P_XG // GATE_LORA
    row_spec = lambda w, cfn: pl.BlockSpec((C, w), lambda b, p, c: (b * n_chunks + c, cfn(p)))
    sh_spec = lambda w, cfn: pl.BlockSpec((None, 1, w), lambda b, p, c: (b, 0, cfn(p)))
    mu_spec = lambda w, cfn: pl.BlockSpec((1, w), lambda b, p, c: (0, cfn(p)))
    cols = [(W, lambda p: col_r + p), (W, lambda p: col_k + p), (W, lambda p: col_v + p),
            (LANE, lambda p: col_xw), (LANE, lambda p: col_xa), (GATE_LORA, lambda p: col_xg)]
    pvec = pl.BlockSpec((1, W), lambda b, p, c: (0, p))
    state_spec = pl.BlockSpec((None, 2 * NP, hd, hd), lambda b, p, c: (b, p, 0, 0))
    in_specs = ([row_spec(w, f) for w, f in cols] + [sh_spec(w, f) for w, f in cols] + [mu_spec(w, f) for w, f in cols]
                + [state_spec] + [pvec] * 7
                + [pl.BlockSpec((LANE, W), lambda b, p, c: (0, p)), pl.BlockSpec((LANE, W), lambda b, p, c: (0, p)),
                   pl.BlockSpec((GATE_LORA, W), lambda b, p, c: (0, p))])
    scratch = [pltpu.VMEM((NP, RWKV_PAIR, RWKV_PAIR), f32)] + [pltpu.VMEM((8, w), f32) for w, _ in cols]
    o, s_fin = pl.pallas_call(
        functools.partial(_rwkv_kernel, C=C, NP=NP, t_valid=t_valid, n_chunks=n_chunks),
        grid=(B, n_steps, n_chunks),
        in_specs=in_specs,
        out_specs=[pl.BlockSpec((C, W), lambda b, p, c: (b * n_chunks + c, p)), state_spec],
        out_shape=[jax.ShapeDtypeStruct((B * T, RWKV_WIDTH), f32),
                   jax.ShapeDtypeStruct((B, RWKV_HEADS, hd, hd), f32)],
        scratch_shapes=scratch,
        compiler_params=pltpu.CompilerParams(dimension_semantics=("arbitrary",) * 3, vmem_limit_bytes=VMEM_LIMIT_BYTES),
        name="rwkv_mix",
    )(*([proj_p] * 6), *([shift_p] * 6), *([mu_p] * 6), wkv0.astype(f32),
      vec(lp['rwkv_w0']), vec(lp['rwkv_a0']), vec(lp['rwkv_k_k']), vec(lp['rwkv_k_a']), vec(lp['rwkv_r_k']),
      vec(lp['rwkv_lnx_w']), vec(lp['rwkv_lnx_b']), wup, aup, lp['rwkv_g_up'])
    return o, s_fin


def _expert_changed(be_ref, i):
    return jnp.logical_or(i == 0, be_ref[i] != be_ref[jnp.maximum(i - 1, 0)])


def _ffn_up_kernel(be_ref, nv_ref, x_ref, wg_ref, wu_ref, h_ref, wgb_ref, wub_ref):
    i = pl.program_id(0)
    live = i < nv_ref[0]

    @pl.when(live & _expert_changed(be_ref, i))
    def _():
        wgb_ref[...] = wg_ref[...].astype(jnp.bfloat16)
        wub_ref[...] = wu_ref[...].astype(jnp.bfloat16)

    @pl.when(live)
    def _():
        x = x_ref[...]
        hg = jnp.dot(x, wgb_ref[...], preferred_element_type=jnp.float32)
        hu = jnp.dot(x, wub_ref[...], preferred_element_type=jnp.float32)
        h_ref[...] = (jax.nn.silu(hg) * hu).astype(h_ref.dtype)

    @pl.when(i >= nv_ref[0])
    def _():
        h_ref[...] = jnp.zeros(h_ref.shape, h_ref.dtype)


def _ffn_down_kernel(be_ref, nv_ref, h_ref, wd_ref, rw_ref, o_ref, wdb_ref):
    i = pl.program_id(0)
    live = i < nv_ref[0]

    @pl.when(live & _expert_changed(be_ref, i))
    def _():
        wdb_ref[...] = wd_ref[...].astype(jnp.bfloat16)

    @pl.when(live)
    def _():
        o_ref[...] = jnp.dot(h_ref[...].astype(jnp.bfloat16), wdb_ref[...],
                             preferred_element_type=jnp.float32) * rw_ref[...]

    @pl.when(i >= nv_ref[0])
    def _():
        o_ref[...] = jnp.zeros(o_ref.shape, o_ref.dtype)


def expert_ffn(xs, w_gate, w_up, w_down, row_w, blk_exp, n_valid, blk):
    rows, d = xs.shape
    ff = w_gate.shape[-1]
    n_blocks = rows // blk
    params = pltpu.CompilerParams(dimension_semantics=("arbitrary",), vmem_limit_bytes=VMEM_LIMIT_BYTES)
    h = pl.pallas_call(
        _ffn_up_kernel,
        grid_spec=pltpu.PrefetchScalarGridSpec(
            num_scalar_prefetch=2, grid=(n_blocks,),
            in_specs=[pl.BlockSpec((blk, d), lambda i, be, nv: (i, 0)),
                      pl.BlockSpec((None, d, ff), lambda i, be, nv: (be[i], 0, 0)),
                      pl.BlockSpec((None, d, ff), lambda i, be, nv: (be[i], 0, 0))],
            out_specs=pl.BlockSpec((blk, ff), lambda i, be, nv: (i, 0)),
            scratch_shapes=[pltpu.VMEM((d, ff), jnp.bfloat16), pltpu.VMEM((d, ff), jnp.bfloat16)]),
        out_shape=jax.ShapeDtypeStruct((rows, ff), jnp.bfloat16),
        compiler_params=params, name="ffn_up",
    )(blk_exp, n_valid, xs, w_gate, w_up)
    return pl.pallas_call(
        _ffn_down_kernel,
        grid_spec=pltpu.PrefetchScalarGridSpec(
            num_scalar_prefetch=2, grid=(n_blocks,),
            in_specs=[pl.BlockSpec((blk, ff), lambda i, be, nv: (i, 0)),
                      pl.BlockSpec((None, ff, d), lambda i, be, nv: (be[i], 0, 0)),
                      pl.BlockSpec((blk, 1), lambda i, be, nv: (i, 0))],
            out_specs=pl.BlockSpec((blk, d), lambda i, be, nv: (i, 0)),
            scratch_shapes=[pltpu.VMEM((ff, d), jnp.bfloat16)]),
        out_shape=jax.ShapeDtypeStruct((rows, d), jnp.float32),
        compiler_params=params, name="ffn_down",
    )(blk_exp, n_valid, h, w_down, row_w)


def _router_kernel(x_ref, wt_ref, b_ref, e_ref, w_ref):
    f32 = jnp.float32
    tm = x_ref.shape[0]
    gs = N_EXPERTS // N_GROUPS
    logits = lax.dot_general(wt_ref[...].astype(jnp.bfloat16), x_ref[...].astype(jnp.bfloat16),
                             (((1,), (1,)), ((), ())), preferred_element_type=f32)
    scores = jax.nn.sigmoid(logits)
    biased = scores + b_ref[...]
    g3 = biased.reshape(N_GROUPS, gs, tm)
    it = lax.broadcasted_iota(jnp.int32, (N_GROUPS, gs, tm), 1)
    m1 = jnp.max(g3, axis=1, keepdims=True)
    i1 = jnp.min(jnp.where(g3 == m1, it, gs), axis=1, keepdims=True)
    m2 = jnp.max(jnp.where(it == i1, -jnp.inf, g3), axis=1, keepdims=True)
    gscore = (m1 + m2).reshape(N_GROUPS, tm)
    gi = lax.broadcasted_iota(jnp.int32, (N_GROUPS, tm), 0)
    grank = jnp.zeros((N_GROUPS, tm), jnp.int32)
    for j in range(N_GROUPS):
        sj = gscore[j:j + 1, :]
        grank = grank + ((sj > gscore) | ((sj == gscore) & (j < gi))).astype(jnp.int32)
    gmask = (grank < TOPK_GROUPS).reshape(N_GROUPS, 1, tm)
    masked = jnp.where(gmask, g3, -jnp.inf).reshape(N_EXPERTS, tm)
    ei = lax.broadcasted_iota(jnp.int32, (N_EXPERTS, tm), 0)
    idxs, wts = [], []
    for _ in range(TOP_K):
        m = jnp.max(masked, axis=0, keepdims=True)
        idx = jnp.min(jnp.where(masked == m, ei, N_EXPERTS), axis=0, keepdims=True)
        hit = ei == idx
        idxs.append(idx)
        wts.append(jnp.sum(jnp.where(hit, scores, 0.0), axis=0, keepdims=True))
        masked = jnp.where(hit, -jnp.inf, masked)
    w = jnp.concatenate(wts, axis=0)
    e_ref[...] = jnp.concatenate(idxs, axis=0)
    w_ref[...] = w / jnp.sum(w, axis=0, keepdims=True) * ROUTED_SCALE


def moe_route(x, router_w, router_bias):
    n, d = x.shape
    tm = min(512, n)
    assert n % tm == 0
    e, w = pl.pallas_call(
        _router_kernel,
        grid=(n // tm,),
        in_specs=[pl.BlockSpec((tm, d), lambda i: (i, 0)),
                  pl.BlockSpec((N_EXPERTS, d), lambda i: (0, 0)),
                  pl.BlockSpec((N_EXPERTS, 1), lambda i: (0, 0))],
        out_specs=[pl.BlockSpec((TOP_K, tm), lambda i: (0, i)), pl.BlockSpec((TOP_K, tm), lambda i: (0, i))],
        out_shape=[jax.ShapeDtypeStruct((TOP_K, n), jnp.int32), jax.ShapeDtypeStruct((TOP_K, n), jnp.float32)],
        compiler_params=pltpu.CompilerParams(dimension_semantics=("arbitrary",), vmem_limit_bytes=VMEM_LIMIT_BYTES),
        name="moe_route",
    )(x, router_w.T, router_bias.astype(jnp.float32).reshape(N_EXPERTS, 1))
    return e.T, w.T


def _moe_rank_kernel(e_ref, rank_ref, cnt_ref, carry_ref):
    f32, bf16 = jnp.float32, jnp.bfloat16
    ta = e_ref.shape[1]
    i = pl.program_id(0)

    @pl.when(i == 0)
    def _():
        carry_ref[...] = jnp.zeros(carry_ref.shape, f32)

    ei = lax.broadcasted_iota(jnp.int32, (N_EXPERTS, ta), 0)
    hot = ei == e_ref[...]
    r = lax.broadcasted_iota(jnp.int32, (ta, ta), 0)
    c = lax.broadcasted_iota(jnp.int32, (ta, ta), 1)
    before = jnp.dot(hot.astype(bf16), (r < c).astype(bf16), preferred_element_type=f32)
    carry = carry_ref[...]
    rank_ref[...] = jnp.sum(jnp.where(hot, before + carry, 0.0), axis=0, keepdims=True).astype(jnp.int32)
    carry = carry + jnp.sum(hot.astype(f32), axis=1, keepdims=True)
    carry_ref[...] = carry
    cnt_ref[...] = carry.astype(jnp.int32)


def moe_rank(flat_e):
    nk = flat_e.shape[0]
    ta = min(512, nk)
    assert nk % ta == 0 and nk < 2 ** 24
    rank, cnt = pl.pallas_call(
        _moe_rank_kernel,
        grid=(nk // ta,),
        in_specs=[pl.BlockSpec((1, ta), lambda i: (0, i))],
        out_specs=[pl.BlockSpec((1, ta), lambda i: (0, i)), pl.BlockSpec((N_EXPERTS, 1), lambda i: (0, 0))],
        out_shape=[jax.ShapeDtypeStruct((1, nk), jnp.int32), jax.ShapeDtypeStruct((N_EXPERTS, 1), jnp.int32)],
        scratch_shapes=[pltpu.VMEM((N_EXPERTS, 1), jnp.float32)],
        compiler_params=pltpu.CompilerParams(dimension_semantics=("arbitrary",), vmem_limit_bytes=VMEM_LIMIT_BYTES),
        name="moe_rank",
    )(flat_e.reshape(1, nk))
    return rank.reshape(nk), cnt.reshape(N_EXPERTS)


def moe_dispatch(h, lp, blk):
    B, T, D = h.shape
    x = h.reshape(B * T, D)
    n = x.shape[0]
    f32 = jnp.float32
    eidx, w = moe_route(x, lp['router_w'], lp['router_bias'])

    nk = n * TOP_K
    flat_e = eidx.reshape(nk)
    rank, counts = moe_rank(flat_e)
    padded = (counts + blk - 1) // blk * blk
    pad_end = jnp.cumsum(padded)
    dest = jnp.take(pad_end - padded, flat_e) + rank
    n_blocks = -(-(nk + N_EXPERTS * (blk - 1)) // blk)
    rows = n_blocks * blk
    row_aid = jnp.full((rows,), -1, jnp.int32).at[dest].set(jnp.arange(nk, dtype=jnp.int32))
    live = row_aid >= 0
    row_aid = jnp.maximum(row_aid, 0)
    row_tok = row_aid // TOP_K
    row_w = jnp.where(live, jnp.take(w.reshape(nk), row_aid), 0.0)
    starts = jnp.arange(n_blocks, dtype=pad_end.dtype) * blk
    blk_exp = jnp.minimum(jnp.sum(pad_end[None, :] <= starts[:, None], axis=1), N_EXPERTS - 1).astype(jnp.int32)
    n_valid = (pad_end[-1] // blk).astype(jnp.int32).reshape(1)
    pos = dest.reshape(n, TOP_K)

    xb = x.astype(jnp.bfloat16)
    return dict(xb=xb, xs=xb[row_tok], row_w=row_w[:, None], blk_exp=blk_exp, n_valid=n_valid, pos=pos, blk=blk,
                shape=(B, T, D))


def moe_experts(dp, lp):
    n = dp['xb'].shape[0]
    ys = expert_ffn(dp['xs'], lp['exp_w_gate'], lp['exp_w_up'], lp['exp_w_down'], dp['row_w'], dp['blk_exp'],
                    dp['n_valid'], dp['blk'])
    sblk = min(256, n)
    one = jnp.ones((n, 1), jnp.float32)
    shared = expert_ffn(dp['xb'], lp['sh_w_gate'][None], lp['sh_w_up'][None], lp['sh_w_down'][None], one,
                        jnp.zeros((n // sblk,), jnp.int32), jnp.full((1,), n // sblk, jnp.int32), sblk)
    return ys[dp['pos']], shared


def moe_combine(gathered, shared, shape):
    return (gathered.sum(axis=1) + shared).reshape(shape)


def moe_ffn_pallas(h, lp, blk):
    dp = moe_dispatch(h, lp, blk)
    gathered, shared = moe_experts(dp, lp)
    return moe_combine(gathered, shared, dp['shape'])


def pad_rows(a, length):
    return jnp.pad(a, [(0, 0), (0, length - a.shape[1])] + [(0, 0)] * (a.ndim - 2))


def rwkv7_mix(pr, shift_prev, wkv0, lp):
    B, T, _ = pr.shape
    f32 = jnp.float32
    prev = jnp.concatenate([shift_prev[:, None, :].astype(pr.dtype), pr[:, :-1]], axis=1)
    m = pr + (prev - pr) * lp['rwkv_mu']
    r, k, v, xw, xa, xg = jnp.split(m, RWKV_SPLITS, axis=-1)
    w_log = -jax.nn.softplus(-(lp['rwkv_w0'] + jnp.tanh(xw) @ lp['rwkv_w_up']).astype(f32)) - 0.5
    decay = jnp.exp(-jnp.exp(w_log))
    a = jax.nn.sigmoid((lp['rwkv_a0'] + xa @ lp['rwkv_a_up']).astype(f32))
    g = jax.nn.sigmoid(xg) @ lp['rwkv_g_up']

    def heads(t):
        return t.astype(f32).reshape(B, T, RWKV_HEADS, RWKV_HEAD_DIM)

    r, k, v, decay, a = heads(r), heads(k), heads(v), heads(decay), heads(a)
    kk = k * lp['rwkv_k_k'].astype(f32).reshape(RWKV_HEADS, RWKV_HEAD_DIM)
    kk = kk / jnp.maximum(jnp.linalg.norm(kk, axis=-1, keepdims=True), 1e-12)
    k = k * (1.0 + (a - 1.0) * lp['rwkv_k_a'].astype(f32).reshape(RWKV_HEADS, RWKV_HEAD_DIM))

    def step(S, inp):
        r_t, w_t, k_t, v_t, a_t, b_t = inp
        sa = jnp.einsum('bhvk,bhk->bhv', S, a_t)
        S = S * w_t[:, :, None, :] + sa[..., None] * b_t[:, :, None, :] + v_t[..., None] * k_t[:, :, None, :]
        return S, jnp.einsum('bhvk,bhk->bhv', S, r_t)

    xs = tuple(jnp.moveaxis(t, 1, 0) for t in (r, decay, k, v, -kk, kk * a))
    s_final, ys = lax.scan(step, wkv0.astype(f32), xs)
    y = jnp.moveaxis(ys, 0, 1)
    mu = jnp.mean(y, axis=-1, keepdims=True)
    var = jnp.mean(jnp.square(y - mu), axis=-1, keepdims=True)
    y = ((y - mu) * lax.rsqrt(var + LNX_EPS)).reshape(B, T, RWKV_WIDTH)
    y = y * lp['rwkv_lnx_w'].astype(f32) + lp['rwkv_lnx_b'].astype(f32)
    bonus = (jnp.sum(r * k * lp['rwkv_r_k'].astype(f32), axis=-1, keepdims=True) * v).reshape(B, T, RWKV_WIDTH)
    out = ((y + bonus) * g.astype(f32)).astype(pr.dtype)
    return out, pr[:, -1], s_final.astype(wkv0.dtype)


def compress_blocks(rows, pe, w):
    B, L = rows.shape[:2]
    blk = rows.reshape(B, L // CMP_BLOCK, CMP_BLOCK, 2, NSA_KV_HEADS, NSA_HEAD_DIM)
    blk = blk + jnp.swapaxes(pe, 0, 1)[None, None, :, :, None, :]
    return jnp.einsum('bncjkd,jcde->bnjke', blk, w)


def nsa_attend(q, q_pos, gates, kc, vc, c_end, n_sel_blocks, gather_sel, kw, vw, w_pos, rel_bias):
    B, Tq = q.shape[:2]
    f32 = jnp.float32
    scale = NSA_HEAD_DIM ** -0.5
    qg = q.reshape(B, Tq, NSA_KV_HEADS, NSA_GROUP, NSA_HEAD_DIM)
    bias_tab = rel_bias.astype(f32).reshape(NUM_BUCKETS, NSA_KV_HEADS, NSA_GROUP)

    dc = q_pos[:, None] - c_end[None, :]
    bias_c = bias_tab[rel_bucket(dc)].transpose(0, 2, 3, 1)
    lc = jnp.einsum('btkgd,bnkd->btkgn', qg, kc).astype(f32) * scale + bias_c
    pc = masked_softmax(lc, (dc >= 0)[:, None, None, :])
    oc = jnp.einsum('btkgn,bnkd->btkgd', pc.astype(vc.dtype), vc)

    imp = pc.sum(3).reshape(B, Tq, NSA_KV_HEADS, n_sel_blocks, SEL_BLOCK // CMP_BLOCK).sum(-1)
    blk = jnp.arange(n_sel_blocks)
    cur = (q_pos // SEL_BLOCK)[:, None]
    force = (blk == 0) | (blk == cur) | (blk == cur - 1)
    score = jnp.where(force[:, None, :], FORCE_SCORE, imp)
    score = jnp.where((blk <= cur)[:, None, :], score, -jnp.inf)
    _, idx = lax.top_k(jnp.moveaxis(score, 2, 1), min(N_SEL, n_sel_blocks))

    sel = gather_sel(idx)
    ks, vs = sel[..., 0, :], sel[..., 1, :]
    spos = idx[..., None] * SEL_BLOCK + jnp.arange(SEL_BLOCK)
    ds = q_pos[None, None, :, None, None] - spos
    hi = jnp.arange(NSA_KV_HEADS)[None, :, None, None, None]
    bias_s = jnp.moveaxis(bias_tab[rel_bucket(ds), hi], -1, 3)
    ls = jnp.einsum('btkgd,bktjsd->bktgjs', qg, ks).astype(f32) * scale + bias_s
    ms = (ds >= 0)[:, :, :, None]
    ps = masked_softmax(ls.reshape(ls.shape[:4] + (-1,)), ms.reshape(ms.shape[:4] + (-1,))).reshape(ls.shape)
    osel = jnp.einsum('bktgjs,bktjsd->btkgd', ps.astype(vs.dtype), vs)

    dw = q_pos[:, None] - w_pos[None, :]
    mw = (dw >= 0) & (dw <= WINDOW) & (w_pos >= 0)[None, :]
    bias_w = bias_tab[rel_bucket(dw)].transpose(0, 2, 3, 1)
    lw = jnp.einsum('btkgd,blkd->btkgl', qg, kw).astype(f32) * scale + bias_w
    pw = masked_softmax(lw, mw[:, None, None, :])
    ow = jnp.einsum('btkgl,blkd->btkgd', pw.astype(vw.dtype), vw)

    gt = gates.reshape(B, Tq, NSA_KV_HEADS, NSA_GROUP, 3)
    o = gt[..., 0:1] * oc + gt[..., 1:2] * osel + gt[..., 2:3] * ow
    return o.reshape(B, Tq, NSA_WIDTH)


def split_proj(h, lp):
    B, T, _ = h.shape
    proj = mm3(h, lp['w_in'])
    pr = proj[..., :RWKV_PROJ]
    pn = proj[..., RWKV_PROJ:]
    q = pn[..., :NSA_WIDTH].reshape(B, T, NSA_HEADS, NSA_HEAD_DIM)
    kv = pn[..., NSA_WIDTH:NSA_WIDTH + 3 * NSA_KV_WIDTH].reshape(B, T, 3, 2, NSA_KV_HEADS, NSA_HEAD_DIM)
    gates = jax.nn.sigmoid(pn[..., NSA_WIDTH + 3 * NSA_KV_WIDTH:].astype(jnp.float32))
    gates = gates.reshape(B, T, NSA_HEADS, 3).astype(h.dtype)
    return pr, q, kv[:, :, 0], kv[:, :, 1], kv[:, :, 2], gates


def merge_groups(o_rwkv, o_nsa, lp):
    o = jnp.concatenate([o_rwkv, rms_norm(o_nsa, lp['nsa_out_g'])], axis=-1).astype(jnp.bfloat16)
    return mm3(o, lp['w_out'])


def mixer_prompt(h, lp, rel_bias):
    B, T, d = h.shape
    proj_p = pallas_matmul(h.reshape(B * T, d), lp['w_in_p'])
    o_r, wkv = rwkv_mix(proj_p, jnp.zeros((B, RWKV_PROJ), h.dtype),
                        jnp.zeros((B, RWKV_HEADS, RWKV_HEAD_DIM, RWKV_HEAD_DIM), jnp.float32), lp, B, T, T, 64, 8)
    o_r = o_r.reshape(B, T, RWKV_WIDTH)
    shift_last = unpad_rwkv(proj_p.reshape(B, T, P_TOTAL)[:, -1])
    kvc = compress_prompt(proj_p, lp['cmp_pe'], lp['cmp_w'], B, T)
    o_n = nsa_prompt(proj_p, kvc, rel_bias, B, T).reshape(B, T, NSA_WIDTH)
    kv = proj_p[:, P_KVC:P_GATES].reshape(B, T, 3, 2, NSA_KV_HEADS, NSA_HEAD_DIM)
    kv_c, kv_s, kv_w = kv[:, :, 0], kv[:, :, 1], kv[:, :, 2]
    win_keep = min(WINDOW, T)
    return merge_groups(o_r, o_n, lp), (kv_c, kv_s, kv_w[:, T - win_keep:], wkv, shift_last)


def mixer_prompt_gather(h, lp, rel_bias):
    B, T, _ = h.shape
    pr, q, kv_c, kv_s, kv_w, gates = split_proj(h, lp)
    o_r, shift_last, wkv = rwkv7_mix(pr, jnp.zeros((B, RWKV_PROJ), h.dtype),
                                     jnp.zeros((B, RWKV_HEADS, RWKV_HEAD_DIM, RWKV_HEAD_DIM), jnp.float32), lp)
    lp_len = -(-T // SEL_BLOCK) * SEL_BLOCK
    comp = compress_blocks(pad_rows(kv_c, lp_len), lp['cmp_pe'], lp['cmp_w'])
    kc, vc = comp[:, :, 0], comp[:, :, 1]
    c_end = jnp.arange(lp_len // CMP_BLOCK) * CMP_BLOCK + (CMP_BLOCK - 1)
    n_sb = lp_len // SEL_BLOCK
    sel_blocks = pad_rows(kv_s, lp_len).reshape(B, n_sb, SEL_BLOCK, 2, NSA_KV_HEADS, NSA_HEAD_DIM)
    bi = jnp.arange(B)[:, None, None, None]
    hi = jnp.arange(NSA_KV_HEADS)[None, :, None, None]

    def gather_sel(idx):
        return sel_blocks[bi, idx, :, :, hi, :]

    kw_pad = jnp.pad(kv_w, ((0, 0), (WINDOW, 0), (0, 0), (0, 0), (0, 0)))

    def chunk(i):
        t0 = i * Q_CHUNK
        qc = lax.dynamic_slice_in_dim(q, t0, Q_CHUNK, axis=1)
        gc = lax.dynamic_slice_in_dim(gates, t0, Q_CHUNK, axis=1)
        wc = lax.dynamic_slice_in_dim(kw_pad, t0, WINDOW + Q_CHUNK, axis=1)
        q_pos = t0 + jnp.arange(Q_CHUNK)
        w_pos = t0 - WINDOW + jnp.arange(WINDOW + Q_CHUNK)
        return nsa_attend(qc, q_pos, gc, kc, vc, c_end, n_sb, gather_sel,
                          wc[:, :, 0], wc[:, :, 1], w_pos, rel_bias)

    o_n = lax.map(chunk, jnp.arange(T // Q_CHUNK))
    o_n = jnp.moveaxis(o_n, 0, 1).reshape(B, T, NSA_WIDTH)
    win_keep = min(WINDOW, T)
    return merge_groups(o_r, o_n, lp), (kv_c, kv_s, kv_w[:, T - win_keep:], wkv, shift_last)


def mixer_sample(h, lp, rel_bias, pool_cmp, pool_sel, win_buf, wkv0, shift0, page_table):
    B, T, _ = h.shape
    n_pages = page_table.shape[1]
    page = pool_cmp.shape[1]
    past = n_pages * page
    d = h.shape[-1]
    proj_p = pallas_matmul(h.reshape(B * T, d), lp['w_in_p']).reshape(B, T, P_TOTAL)
    import numpy as np
    kv = proj_p[..., P_KVC:P_GATES].reshape(B, T, 3, 2, NSA_KV_HEADS, NSA_HEAD_DIM)
    kv_c, kv_s, kv_w = kv[:, :, 0], kv[:, :, 1], kv[:, :, 2]
    proj_pad = jnp.pad(proj_p, ((0, 0), (0, T_PAD - T), (0, 0))).reshape(B * T_PAD, P_TOTAL)
    o_r, wkv = rwkv_mix(proj_pad, shift0, wkv0, lp, B, T_PAD, T, T_PAD, 4)
    o_r = o_r.reshape(B, T_PAD, RWKV_WIDTH)[:, :T]
    shift_last = unpad_rwkv(proj_p[:, -1])
    lp_len = -(-(past + T) // SEL_BLOCK) * SEL_BLOCK

    jk, hd = 2 * NSA_KV_HEADS, NSA_HEAD_DIM
    kvc_past = compress_paged(pool_cmp, page_table, lp['cmp_pe'], lp['cmp_w'])
    new_len = lp_len - past
    new_rows = pad_rows(proj_p[..., P_KVC:P_KVS], new_len).reshape(B * new_len, NSA_KV_WIDTH)
    kvc_new = compress_prompt(new_rows, lp['cmp_pe'], lp['cmp_w'], B, new_len, col0=0)
    nb_new = new_len // CMP_BLOCK
    assert nb_new <= 2
    kvc_new = kvc_new.reshape(B, jk, nb_new, hd).transpose(0, 2, 1, 3)
    kvc_all = jnp.concatenate([kvc_past, kvc_new], axis=1)
    nc = lp_len // CMP_BLOCK
    n_half = nc // 2
    half = -(-n_half // LANE) * LANE
    idx = np.zeros((2 * half,), np.int32)
    idx[:n_half] = np.arange(0, nc, 2)
    idx[half:half + n_half] = np.arange(1, nc, 2)
    kvc_t = kvc_all[:, idx].transpose(0, 2, 1, 3)
    o_n = nsa_sample(proj_pad, kvc_t, pool_sel, win_buf, page_table, rel_bias, B, T)
    o_n = o_n.reshape(B, T_PAD, NSA_WIDTH)[:, :T]
    new_win = jnp.concatenate([win_buf, kv_w], axis=1)[:, T:]
    return merge_groups(o_r, o_n, lp), (kv_c, kv_s, new_win, wkv, shift_last)


def moe_ffn(h, lp):
    B, T, D = h.shape
    x = h.reshape(B * T, D)
    n = x.shape[0]
    f32 = jnp.float32
    scores = jax.nn.sigmoid((x @ lp['router_w']).astype(f32))
    biased = scores + lp['router_bias'].astype(f32)
    grp = biased.reshape(n, N_GROUPS, N_EXPERTS // N_GROUPS)
    grp_score = lax.top_k(grp, 2)[0].sum(-1)
    _, gidx = lax.top_k(grp_score, TOPK_GROUPS)
    gmask = jax.nn.one_hot(gidx, N_GROUPS, dtype=f32).sum(1)
    emask = jnp.repeat(gmask, N_EXPERTS // N_GROUPS, axis=1) > 0
    _, eidx = lax.top_k(jnp.where(emask, biased, -jnp.inf), TOP_K)
    w = jnp.take_along_axis(scores, eidx, axis=1)
    w = w / jnp.sum(w, axis=-1, keepdims=True) * ROUTED_SCALE

    nk = n * TOP_K
    flat_e = eidx.reshape(nk)
    order = jnp.argsort(flat_e)
    se = flat_e[order]
    counts = jnp.bincount(flat_e, length=N_EXPERTS)
    padded = (counts + MOE_BLOCK - 1) // MOE_BLOCK * MOE_BLOCK
    pad_end = jnp.cumsum(padded)
    dest = (pad_end - padded)[se] + jnp.arange(nk) - (jnp.cumsum(counts) - counts)[se]
    n_blocks = -(-(nk + N_EXPERTS * (MOE_BLOCK - 1)) // MOE_BLOCK)
    rows = n_blocks * MOE_BLOCK
    row_tok = jnp.zeros((rows,), jnp.int32).at[dest].set((order // TOP_K).astype(jnp.int32))
    row_w = jnp.zeros((rows,), f32).at[dest].set(w.reshape(nk)[order])
    blk_exp = jnp.minimum(jnp.searchsorted(pad_end, jnp.arange(n_blocks) * MOE_BLOCK, side='right'),
                          N_EXPERTS - 1)

    def expert_block(args):
        tok, wt, e = args
        xb = x[tok]
        hb = jax.nn.silu(xb @ lp['exp_w_gate'][e]) * (xb @ lp['exp_w_up'][e])
        return (hb @ lp['exp_w_down'][e]).astype(f32) * wt[:, None]

    out = lax.map(expert_block, (row_tok.reshape(n_blocks, MOE_BLOCK), row_w.reshape(n_blocks, MOE_BLOCK), blk_exp))
    routed = jnp.zeros((n, D), f32).at[row_tok].add(out.reshape(rows, D))
    shared = (jax.nn.silu(x @ lp['sh_w_gate']) * (x @ lp['sh_w_up'])) @ lp['sh_w_down']
    return (routed + shared.astype(f32)).astype(h.dtype).reshape(B, T, D)


def trunk_layer(x, mod, lp, mixer):
    B, _, D = x.shape
    mod = mod.reshape(B, 6, 1, D)
    sh1, sc1, gt1, sh2, sc2, gt2 = (mod[:, i] for i in range(6))
    g = lp['norm_g']
    h = (rms_norm(x, g[0]) * (1 + sc1) + sh1).astype(jnp.bfloat16)
    o, st = mixer(h)
    x = x + gt1 * rms_norm(o, g[1])
    h = (rms_norm(x, g[2]) * (1 + sc2) + sh2).astype(jnp.bfloat16)
    dp = moe_dispatch(h, lp, 256 if x.shape[0] * x.shape[1] >= 4096 else 32)
    return x, gt2, dp, st


def trunk_finish(x, gt2, gathered, shared, dp, lp):
    return x + gt2 * rms_norm(moe_combine(gathered, shared, dp['shape']), lp['norm_g'][3])


def kernel(x_prompt, x_sample, c_prompt, c_sample, cache_cmp, cache_sel, state_win, state_wkv,
           state_shift, page_table, rel_bias, w_ada, b_ada, norm_g, w_in, w_out, rwkv_mu, rwkv_w0,
           rwkv_w_up, rwkv_a0, rwkv_a_up, rwkv_g_up, rwkv_k_k, rwkv_k_a, rwkv_r_k, rwkv_lnx_w,
           rwkv_lnx_b, cmp_pe, cmp_w, nsa_out_g, router_w, router_bias, exp_w_gate, exp_w_up,
           exp_w_down, sh_w_gate, sh_w_up, sh_w_down):
    l = 0
    lp = dict(w_ada=w_ada[l], b_ada=b_ada[l], norm_g=norm_g[l], w_in=w_in[l], w_out=w_out[l],
              rwkv_mu=rwkv_mu[l], rwkv_w0=rwkv_w0[l], rwkv_w_up=rwkv_w_up[l], rwkv_a0=rwkv_a0[l],
              rwkv_a_up=rwkv_a_up[l], rwkv_g_up=rwkv_g_up[l], rwkv_k_k=rwkv_k_k[l],
              rwkv_k_a=rwkv_k_a[l], rwkv_r_k=rwkv_r_k[l], rwkv_lnx_w=rwkv_lnx_w[l],
              rwkv_lnx_b=rwkv_lnx_b[l], cmp_pe=cmp_pe[l], cmp_w=cmp_w[l], nsa_out_g=nsa_out_g[l],
              router_w=router_w[l], router_bias=router_bias[l], exp_w_gate=exp_w_gate[l],
              exp_w_up=exp_w_up[l], exp_w_down=exp_w_down[l], sh_w_gate=sh_w_gate[l],
              sh_w_up=sh_w_up[l], sh_w_down=sh_w_down[l])
    lp['w_in_p'] = pad_w_in(w_in[l])
    nb_p = x_prompt.shape[0]
    mod = pallas_matmul(jax.nn.silu(jnp.concatenate([c_prompt, c_sample], axis=0)), lp['w_ada']) + lp['b_ada']
    xp, gtp, dpp, st_p = trunk_layer(x_prompt, mod[:nb_p], lp, lambda h: mixer_prompt(h, lp, rel_bias))
    xs_, gts, dps, st_s = trunk_layer(x_sample, mod[nb_p:], lp, lambda h: mixer_sample(
        h, lp, rel_bias, cache_cmp[l], cache_sel[l], state_win[l], state_wkv[l], state_shift[l], page_table))
    dpp['xs'], dps['xs'] = lax.optimization_barrier((dpp['xs'], dps['xs']))
    gath_p, shared_p = moe_experts(dpp, lp)
    gath_s, shared_s = moe_experts(dps, lp)
    ys = trunk_finish(xs_, gts, gath_s, shared_s, dps, lp)
    gath_p, ys = lax.optimization_barrier((gath_p, ys))
    yp = trunk_finish(xp, gtp, gath_p, shared_p, dpp, lp)
    p_cmp, p_sel, p_win, p_wkv, p_shift = [a[None] for a in st_p]
    s_cmp, s_sel, s_win, s_wkv, s_shift = [a[None] for a in st_s]
    return (yp, ys, p_cmp, p_sel, p_win, p_wkv, p_shift, s_cmp, s_sel, s_win, s_wkv, s_shift)
```
